```python
import math
import jax, jax.numpy as jnp
from jax import lax
import numpy as np

D_MODEL = 2048
BATCH = 8
SEQ = 2048
DEPTH = 4

GRID_W = 64
CTX_LEN = 256

GLA_HEADS = 4
GLA_DK = 128
GLA_DV = 256
GLA_QK = GLA_HEADS * GLA_DK
GLA_V = GLA_HEADS * GLA_DV
GLA_GATE_RANK = 16
GLA_GATE_TAU = 16.0
GLA_CHUNK = 64
S5_WIDTH = 768
S5_GROUP = 16
S5_GROUPS = S5_WIDTH // S5_GROUP
S5_STATE = 64
ATTN_Q_HEADS = 8
ATTN_KV_HEADS = 2
ATTN_HEAD_DIM = 128
ATTN_Q = ATTN_Q_HEADS * ATTN_HEAD_DIM
ATTN_KV = ATTN_KV_HEADS * ATTN_HEAD_DIM
ATTN_BLOCK = 128
ROPE_THETA = 10000.0
D_FF = ((8 * D_MODEL + 3 * 256 - 1) // (3 * 256)) * 256
IN_SPLITS = (GLA_QK, GLA_QK, GLA_V, GLA_V, GLA_GATE_RANK, S5_WIDTH, ATTN_Q, ATTN_KV, ATTN_KV, 3 * D_MODEL)
IN_WIDTH = 2 * GLA_QK + 2 * GLA_V + GLA_GATE_RANK + S5_WIDTH + ATTN_Q + 2 * ATTN_KV + 3 * D_MODEL
DN_ALPHA = (2 * DEPTH) ** 0.25
DN_BETA = (8 * DEPTH) ** -0.25
EPS = 1e-6

kernel_name = "hybrid_gla_s5_gqa_prefix_dit"

F32 = jnp.float32


def layer_norm(x, w, b):
    xf = x.astype(F32)
    mu = jnp.mean(xf, -1, keepdims=True)
    var = jnp.mean(jnp.square(xf - mu), -1, keepdims=True)
    return ((xf - mu) * lax.rsqrt(var + EPS) * w + b).astype(x.dtype)


def rms_norm(x, w):
    xf = x.astype(F32)
    return (xf * lax.rsqrt(jnp.mean(xf * xf, -1, keepdims=True) + EPS) * w).astype(x.dtype)


def modulate(h, shift, scale):
    return h * (1 + scale) + shift


def post_norm(x, y, w, b):
    return layer_norm(DN_ALPHA * x + y, w, b)


def axial_rope(n_rows):
    rows = jnp.repeat(jnp.arange(n_rows), GRID_W).astype(F32)
    cols = jnp.tile(jnp.arange(GRID_W), n_rows).astype(F32)
    n_freq = ATTN_HEAD_DIM // 4
    inv = ROPE_THETA ** (-jnp.arange(n_freq, dtype=F32) / n_freq)
    ang = jnp.concatenate([rows[:, None] * inv, cols[:, None] * inv], -1)
    return jnp.cos(ang)[:, None, :], jnp.sin(ang)[:, None, :]


def apply_rope(x, cos, sin):
    xf = x.astype(F32).reshape(x.shape[:-1] + (x.shape[-1] // 2, 2))
    x0, x1 = xf[..., 0], xf[..., 1]
    out = jnp.stack([x0 * cos - x1 * sin, x0 * sin + x1 * cos], -1)
    return out.reshape(x.shape).astype(x.dtype)


def attend(q, k, v):
    s = jnp.einsum('bqhgd,bkhd->bhgqk', q, k, preferred_element_type=F32) * (ATTN_HEAD_DIM ** -0.5)
    p = jax.nn.softmax(s, axis=-1).astype(v.dtype)
    return jnp.einsum('bhgqk,bkhd->bqhgd', p, v)


def gqa_blocks(q, k, v):
    bsz, t = q.shape[:2]
    grp = ATTN_Q_HEADS // ATTN_KV_HEADS
    qb = q.reshape(bsz, t // ATTN_BLOCK, ATTN_BLOCK, ATTN_KV_HEADS, grp, ATTN_HEAD_DIM).swapaxes(0, 1)
    ob = lax.map(lambda qq: attend(qq, k, v), qb)
    return ob.swapaxes(0, 1).reshape(bsz, t, ATTN_Q)


def gla_chunked(q, k, v, log_a, s0):
    bsz, t, nh, _ = q.shape
    dv = v.shape[-1]
    n = t // GLA_CHUNK
    mask = jnp.tril(jnp.ones((GLA_CHUNK, GLA_CHUNK), bool))

    def to_chunks(z):
        return z.reshape(bsz, n, GLA_CHUNK, nh, z.shape[-1]).swapaxes(0, 1)

    def step(s, inp):
        qc, kc, vc, ac = inp
        b = jnp.cumsum(ac, axis=1)
        qe = qc * jnp.exp(b)
        ke = kc * jnp.exp(-b)
        att = jnp.where(mask, jnp.einsum('bihd,bjhd->bhij', qe, ke), 0.0)
        o = jnp.einsum('bhij,bjhv->bihv', att, vc) + jnp.einsum('bihd,bhdv->bihv', qe, s)
        b_last = b[:, -1]
        kd = kc * jnp.exp(b_last[:, None] - b)
        s = jnp.exp(b_last)[..., None] * s + jnp.einsum('bjhd,bjhv->bhdv', kd, vc)
        return s, o

    s_fin, o = lax.scan(step, s0, (to_chunks(q), to_chunks(k), to_chunks(v), to_chunks(log_a)))
    return o.swapaxes(0, 1).reshape(bsz, t, nh, dv), s_fin


def gla_bidirectional(q, k, v, glr, w_gate, b_gate, n_ctx):
    q, k, v = q.astype(F32), k.astype(F32), v.astype(F32)
    bsz, n_tok, nh, dk = q.shape
    dv = v.shape[-1]
    outs = []
    for d in range(2):
        log_a = jax.nn.log_sigmoid((glr @ w_gate[d] + b_gate[d]).astype(F32)) / GLA_GATE_TAU
        log_a = log_a.reshape(bsz, n_tok, nh, dk)
        ctx_in = [z[:, :n_ctx] for z in (q, k, v, log_a)]
        lat_in = [z[:, n_ctx:] for z in (q, k, v, log_a)]
        if d == 1:
            ctx_in = [jnp.flip(z, 1) for z in ctx_in]
            lat_in = [jnp.flip(z, 1) for z in lat_in]
        s0 = jnp.zeros((bsz, nh, dk, dv), F32)
        o_c, s_c = gla_chunked(*ctx_in, s0)
        o_l, _ = gla_chunked(*lat_in, s_c)
        if d == 1:
            o_c, o_l = jnp.flip(o_c, 1), jnp.flip(o_l, 1)
        outs.append(jnp.concatenate([o_c, o_l], 1))
    return outs[0] + outs[1]


def s5_discretise(lam_re, lam_im, log_dt):
    lam_re, lam_im = lam_re.astype(F32), lam_im.astype(F32)
    dt = jnp.exp(log_dt.astype(F32))[:, None]
    mag = jnp.exp(lam_re * dt)
    a_re, a_im = mag * jnp.cos(lam_im * dt), mag * jnp.sin(lam_im * dt)
    den = lam_re * lam_re + lam_im * lam_im
    nr, ni = a_re - 1, a_im
    k_re = (nr * lam_re + ni * lam_im) / den
    k_im = (ni * lam_re - nr * lam_im) / den
    return a_re, a_im, k_re, k_im


def s5_scan(bu_re, bu_im, a_re, a_im, s0_re, s0_im, reverse):
    idx = -1 if reverse else 0
    bu_re = bu_re.at[:, idx].add(a_re * s0_re - a_im * s0_im)
    bu_im = bu_im.at[:, idx].add(a_re * s0_im + a_im * s0_re)
    t = bu_re.shape[1]
    ar = jnp.broadcast_to(a_re, (1, t) + a_re.shape)
    ai = jnp.broadcast_to(a_im, (1, t) + a_im.shape)

    def combine(e1, e2):
        a1r, a1i, b1r, b1i = e1
        a2r, a2i, b2r, b2i = e2
        return (a2r * a1r - a2i * a1i, a2r * a1i + a2i * a1r,
                a2r * b1r - a2i * b1i + b2r, a2r * b1i + a2i * b1r + b2i)

    _, _, xr, xi = lax.associative_scan(combine, (ar, ai, bu_re, bu_im), axis=1, reverse=reverse)
    return xr, xi


def s5_bidirectional(u, n_ctx, lam_re, lam_im, log_dt, b_re, b_im, c_re, c_im, d_skip):
    bsz, n_tok, _ = u.shape
    uf = u.astype(F32).reshape(bsz, n_tok, S5_GROUPS, S5_GROUP)
    bu_re = jnp.einsum('btgc,gpc->btgp', uf, b_re.astype(F32))
    bu_im = jnp.einsum('btgc,gpc->btgp', uf, b_im.astype(F32))
    y = uf * d_skip.astype(F32).reshape(S5_GROUPS, S5_GROUP)
    zeros = jnp.zeros((bsz, S5_GROUPS, S5_STATE), F32)
    for d in range(2):
        rev = d == 1
        a_re, a_im, k_re, k_im = s5_discretise(lam_re[d], lam_im[d], log_dt[d])
        xr = k_re * bu_re - k_im * bu_im
        xi = k_re * bu_im + k_im * bu_re
        cr, ci = s5_scan(xr[:, :n_ctx], xi[:, :n_ctx], a_re, a_im, zeros, zeros, rev)
        end = 0 if rev else -1
        lr, li = s5_scan(xr[:, n_ctx:], xi[:, n_ctx:], a_re, a_im, cr[:, end], ci[:, end], rev)
        sr = jnp.concatenate([cr, lr], 1)
        si = jnp.concatenate([ci, li], 1)
        y = y + jnp.einsum('btgp,gcp->btgc', sr, c_re[d].astype(F32)) \
              - jnp.einsum('btgp,gcp->btgc', si, c_im[d].astype(F32))
    return y.reshape(bsz, n_tok, S5_WIDTH)


def token_mixers(h, n_ctx, keep_ctx, cos, sin, w_in, w_gla_gate, b_gla_gate, gla_norm_w,
                 s5_lam_re, s5_lam_im, s5_log_dt, s5_b_re, s5_b_im, s5_c_re, s5_c_im, s5_d,
                 w_s5_glu, q_norm_w, k_norm_w, w_proj_gla, w_proj_s5, w_proj_attn, w_out):
    bsz, n_tok, _ = h.shape
    cuts = [int(s) for s in np.cumsum(IN_SPLITS)[:-1]]
    gq, gk, gv, gr, glr, su, aq, ak, av, bg = jnp.split(h @ w_in, cuts, axis=-1)
    lo = 0 if keep_ctx else n_ctx

    def heads(z, nh):
        return z.reshape(bsz, n_tok, nh, -1)

    o = gla_bidirectional(heads(gq, GLA_HEADS) * (GLA_DK ** -0.5), heads(gk, GLA_HEADS),
                          heads(gv, GLA_HEADS), glr, w_gla_gate, b_gla_gate, n_ctx)[:, lo:]
    mu = jnp.mean(o, -1, keepdims=True)
    var = jnp.mean(jnp.square(o - mu), -1, keepdims=True)
    o = (o - mu) * lax.rsqrt(var + EPS) * gla_norm_w.astype(F32).reshape(GLA_HEADS, GLA_DV)
    o_gla = o.reshape(bsz, n_tok - lo, GLA_V).astype(h.dtype) * jax.nn.silu(gr[:, lo:])

    y = s5_bidirectional(su, n_ctx, s5_lam_re, s5_lam_im, s5_log_dt, s5_b_re, s5_b_im,
                         s5_c_re, s5_c_im, s5_d)[:, lo:].astype(h.dtype)
    y = jax.nn.gelu(y)
    o_s5 = y * jax.nn.sigmoid(y @ w_s5_glu)

    q = rms_norm(heads(aq, ATTN_Q_HEADS), q_norm_w)
    k = rms_norm(heads(ak, ATTN_KV_HEADS), k_norm_w)
    v = heads(av, ATTN_KV_HEADS)
    q_lat = apply_rope(q[:, n_ctx:], cos, sin)
    k_lat = apply_rope(k[:, n_ctx:], cos, sin)
    k_all = jnp.concatenate([k_lat, k[:, :n_ctx]], 1)
    v_all = jnp.concatenate([v[:, n_ctx:], v[:, :n_ctx]], 1)
    o_attn = gqa_blocks(q_lat, k_all, v_all)
    if keep_ctx:
        grp = ATTN_Q_HEADS // ATTN_KV_HEADS
        q_c = q[:, :n_ctx].reshape(bsz, n_ctx, ATTN_KV_HEADS, grp, ATTN_HEAD_DIM)
        o_c = attend(q_c, k[:, :n_ctx], v[:, :n_ctx]).reshape(bsz, n_ctx, ATTN_Q)
        o_attn = jnp.concatenate([o_c, o_attn], 1)

    g_a, g_b, g_c = jnp.split(jax.nn.sigmoid(bg[:, lo:]), 3, axis=-1)
    merged = g_a * (o_gla @ w_proj_gla) + g_b * (o_s5 @ w_proj_s5) + g_c * (o_attn @ w_proj_attn)
    return merged @ w_out


def swiglu(h, w_ffn_in, w_ffn_out):
    a, b = jnp.split(h @ w_ffn_in, 2, axis=-1)
    return (jax.nn.silu(a) * b) @ w_ffn_out


def _fwd_setup_inputs(seed: int = 0) -> dict:
    key = jax.random.key(seed)
    keys = iter(jax.random.split(key, 40))

    def nrm(shape, std):
        return std * jax.random.normal(next(keys), shape, F32)

    L, D, G, P = DEPTH, D_MODEL, S5_GROUPS, S5_STATE
    lam_im = jnp.pi * jnp.arange(P, dtype=F32) + nrm((L, 2, G, P), 0.01)
    return {
        "x": nrm((BATCH, SEQ, D), 1.0),
        "c": nrm((BATCH, D), 1.0),
        "ctx": nrm((BATCH, CTX_LEN, D), 1.0),
        "c_ctx": nrm((D,), 1.0),
        "w_ada": nrm((L, D, 6 * D), 0.5 * D ** -0.5),
        "b_ada": nrm((L, 6 * D), 0.02),
        "w_in": nrm((L, D, IN_WIDTH), D ** -0.5),
        "w_gla_gate": nrm((L, 2, GLA_GATE_RANK, GLA_QK), GLA_GATE_RANK ** -0.5),
        "b_gla_gate": nrm((L, 2, GLA_QK), 0.1),
        "gla_norm_w": 1.0 + nrm((L, GLA_V), 0.02),
        "s5_lam_re": -0.5 + nrm((L, 2, G, P), 0.01),
        "s5_lam_im": lam_im,
        "s5_log_dt": jax.random.uniform(next(keys), (L, 2, G), F32, math.log(1e-3), math.log(1e-1)),
        "s5_b_re": nrm((L, G, P, S5_GROUP), (2 * S5_GROUP) ** -0.5),
        "s5_b_im": nrm((L, G, P, S5_GROUP), (2 * S5_GROUP) ** -0.5),
        "s5_c_re": nrm((L, 2, G, S5_GROUP, P), 0.5),
        "s5_c_im": nrm((L, 2, G, S5_GROUP, P), 0.5),
        "s5_d": nrm((L, S5_WIDTH), 1.0),
        "w_s5_glu": nrm((L, S5_WIDTH, S5_WIDTH), S5_WIDTH ** -0.5),
        "q_norm_w": 1.0 + nrm((L, ATTN_HEAD_DIM), 0.02),
        "k_norm_w": 1.0 + nrm((L, ATTN_HEAD_DIM), 0.02),
        "w_proj_gla": nrm((L, GLA_V, D), GLA_V ** -0.5),
        "w_proj_s5": nrm((L, S5_WIDTH, D), S5_WIDTH ** -0.5),
        "w_proj_attn": nrm((L, ATTN_Q, D), ATTN_Q ** -0.5),
        "w_out": nrm((L, D, D), DN_BETA * D ** -0.5),
        "ln1_w": 1.0 + nrm((L, D), 0.02),
        "ln1_b": nrm((L, D), 0.02),
        "ln2_w": 1.0 + nrm((L, D), 0.02),
        "ln2_b": nrm((L, D), 0.02),
        "w_ffn_in": nrm((L, D, 2 * D_FF), D ** -0.5),
        "w_ffn_out": nrm((L, D_FF, D), DN_BETA * D_FF ** -0.5),
    }


def _fwd_reference(x, c, ctx, c_ctx, w_ada, b_ada, w_in, w_gla_gate, b_gla_gate, gla_norm_w,
              s5_lam_re, s5_lam_im, s5_log_dt, s5_b_re, s5_b_im, s5_c_re, s5_c_im, s5_d,
              w_s5_glu, q_norm_w, k_norm_w, w_proj_gla, w_proj_s5, w_proj_attn, w_out,
              ln1_w, ln1_b, ln2_w, ln2_b, w_ffn_in, w_ffn_out):
    n_ctx = ctx.shape[1]
    n_lat = x.shape[1]
    rows = n_lat // GRID_W
    cos, sin = axial_rope(rows)
    xc = ctx
    silu_c = jax.nn.silu(c)
    silu_cc = jax.nn.silu(c_ctx)
    for l in range(DEPTH):
        keep_ctx = l < DEPTH - 1
        mod = (silu_c @ w_ada[l] + b_ada[l])[:, None, :]
        mod_c = silu_cc @ w_ada[l] + b_ada[l]
        sh1, sc1, g1, sh2, sc2, g2 = jnp.split(mod, 6, axis=-1)
        csh1, csc1, cg1, csh2, csc2, cg2 = jnp.split(mod_c, 6, axis=-1)
        h = jnp.concatenate([modulate(xc, csh1, csc1), modulate(x, sh1, sc1)], axis=1)
        mix = token_mixers(h, n_ctx, keep_ctx, cos, sin, w_in[l], w_gla_gate[l], b_gla_gate[l],
                           gla_norm_w[l], s5_lam_re[l], s5_lam_im[l], s5_log_dt[l], s5_b_re[l],
                           s5_b_im[l], s5_c_re[l], s5_c_im[l], s5_d[l], w_s5_glu[l], q_norm_w[l],
                           k_norm_w[l], w_proj_gla[l], w_proj_s5[l], w_proj_attn[l], w_out[l])
        mix_lat = mix[:, mix.shape[1] - n_lat:]
        x = post_norm(x, g1 * mix_lat, ln1_w[l], ln1_b[l])
        x = post_norm(x, g2 * swiglu(modulate(x, sh2, sc2), w_ffn_in[l], w_ffn_out[l]), ln2_w[l], ln2_b[l])
        if keep_ctx:
            xc = post_norm(xc, cg1 * mix[:, :n_ctx], ln1_w[l], ln1_b[l])
            xc = post_norm(xc, cg2 * swiglu(modulate(xc, csh2, csc2), w_ffn_in[l], w_ffn_out[l]),
                           ln2_w[l], ln2_b[l])
    return x


import jax as _jax
import jax.numpy as _jnp

TWIN_FORMAT = 'train_step'
FWD_PARAMS = ['x', 'c', 'ctx', 'c_ctx', 'w_ada', 'b_ada', 'w_in', 'w_gla_gate', 'b_gla_gate', 'gla_norm_w', 's5_lam_re', 's5_lam_im', 's5_log_dt', 's5_b_re', 's5_b_im', 's5_c_re', 's5_c_im', 's5_d', 'w_s5_glu', 'q_norm_w', 'k_norm_w', 'w_proj_gla', 'w_proj_s5', 'w_proj_attn', 'w_out', 'ln1_w', 'ln1_b', 'ln2_w', 'ln2_b', 'w_ffn_in', 'w_ffn_out']
TWIN_WEIGHTS = ['c_ctx', 'w_ada', 'b_ada', 'w_in', 'w_gla_gate', 'b_gla_gate', 'gla_norm_w', 's5_lam_re', 's5_lam_im', 's5_log_dt', 's5_b_re', 's5_b_im', 's5_c_re', 's5_c_im', 's5_d', 'w_s5_glu', 'q_norm_w', 'k_norm_w', 'w_proj_gla', 'w_proj_s5', 'w_proj_attn', 'w_out', 'ln1_w', 'ln1_b', 'ln2_w', 'ln2_b', 'w_ffn_in', 'w_ffn_out']
TWIN_DIFF_INPUT = 'x'
TWIN_INPUTS = ['x', 'c', 'ctx', 'c_ctx', 'w_ada', 'b_ada', 'w_in', 'w_gla_gate', 'b_gla_gate', 'gla_norm_w', 's5_lam_re', 's5_lam_im', 's5_log_dt', 's5_b_re', 's5_b_im', 's5_c_re', 's5_c_im', 's5_d', 'w_s5_glu', 'q_norm_w', 'k_norm_w', 'w_proj_gla', 'w_proj_s5', 'w_proj_attn', 'w_out', 'ln1_w', 'ln1_b', 'ln2_w', 'ln2_b', 'w_ffn_in', 'w_ffn_out', 'loss_target', 'm_c_ctx', 'm_w_ada', 'm_b_ada', 'm_w_in', 'm_w_gla_gate', 'm_b_gla_gate', 'm_gla_norm_w', 'm_s5_lam_re', 'm_s5_lam_im', 'm_s5_log_dt', 'm_s5_b_re', 'm_s5_b_im', 'm_s5_c_re', 'm_s5_c_im', 'm_s5_d', 'm_w_s5_glu', 'm_q_norm_w', 'm_k_norm_w', 'm_w_proj_gla', 'm_w_proj_s5', 'm_w_proj_attn', 'm_w_out', 'm_ln1_w', 'm_ln1_b', 'm_ln2_w', 'm_ln2_b', 'm_w_ffn_in', 'm_w_ffn_out', 'v_c_ctx', 'v_w_ada', 'v_b_ada', 'v_w_in', 'v_w_gla_gate', 'v_b_gla_gate', 'v_gla_norm_w', 'v_s5_lam_re', 'v_s5_lam_im', 'v_s5_log_dt', 'v_s5_b_re', 'v_s5_b_im', 'v_s5_c_re', 'v_s5_c_im', 'v_s5_d', 'v_w_s5_glu', 'v_q_norm_w', 'v_k_norm_w', 'v_w_proj_gla', 'v_w_proj_s5', 'v_w_proj_attn', 'v_w_out', 'v_ln1_w', 'v_ln1_b', 'v_ln2_w', 'v_ln2_b', 'v_w_ffn_in', 'v_w_ffn_out']
TWIN_OUTPUTS = ['loss', 'grad_x', 'grad_c_ctx', 'grad_w_ada', 'grad_b_ada', 'grad_w_in', 'grad_w_gla_gate', 'grad_b_gla_gate', 'grad_gla_norm_w', 'grad_s5_lam_re', 'grad_s5_lam_im', 'grad_s5_log_dt', 'grad_s5_b_re', 'grad_s5_b_im', 'grad_s5_c_re', 'grad_s5_c_im', 'grad_s5_d', 'grad_w_s5_glu', 'grad_q_norm_w', 'grad_k_norm_w', 'grad_w_proj_gla', 'grad_w_proj_s5', 'grad_w_proj_attn', 'grad_w_out', 'grad_ln1_w', 'grad_ln1_b', 'grad_ln2_w', 'grad_ln2_b', 'grad_w_ffn_in', 'grad_w_ffn_out', 'delta_c_ctx', 'delta_w_ada', 'delta_b_ada', 'delta_w_in', 'delta_w_gla_gate', 'delta_b_gla_gate', 'delta_gla_norm_w', 'delta_s5_lam_re', 'delta_s5_lam_im', 'delta_s5_log_dt', 'delta_s5_b_re', 'delta_s5_b_im', 'delta_s5_c_re', 'delta_s5_c_im', 'delta_s5_d', 'delta_w_s5_glu', 'delta_q_norm_w', 'delta_k_norm_w', 'delta_w_proj_gla', 'delta_w_proj_s5', 'delta_w_proj_attn', 'delta_w_out', 'delta_ln1_w', 'delta_ln1_b', 'delta_ln2_w', 'delta_ln2_b', 'delta_w_ffn_in', 'delta_w_ffn_out', 'new_m_c_ctx', 'new_m_w_ada', 'new_m_b_ada', 'new_m_w_in', 'new_m_w_gla_gate', 'new_m_b_gla_gate', 'new_m_gla_norm_w', 'new_m_s5_lam_re', 'new_m_s5_lam_im', 'new_m_s5_log_dt', 'new_m_s5_b_re', 'new_m_s5_b_im', 'new_m_s5_c_re', 'new_m_s5_c_im', 'new_m_s5_d', 'new_m_w_s5_glu', 'new_m_q_norm_w', 'new_m_k_norm_w', 'new_m_w_proj_gla', 'new_m_w_proj_s5', 'new_m_w_proj_attn', 'new_m_w_out', 'new_m_ln1_w', 'new_m_ln1_b', 'new_m_ln2_w', 'new_m_ln2_b', 'new_m_w_ffn_in', 'new_m_w_ffn_out', 'new_v_c_ctx', 'new_v_w_ada', 'new_v_b_ada', 'new_v_w_in', 'new_v_w_gla_gate', 'new_v_b_gla_gate', 'new_v_gla_norm_w', 'new_v_s5_lam_re', 'new_v_s5_lam_im', 'new_v_s5_log_dt', 'new_v_s5_b_re', 'new_v_s5_b_im', 'new_v_s5_c_re', 'new_v_s5_c_im', 'new_v_s5_d', 'new_v_w_s5_glu', 'new_v_q_norm_w', 'new_v_k_norm_w', 'new_v_w_proj_gla', 'new_v_w_proj_s5', 'new_v_w_proj_attn', 'new_v_w_out', 'new_v_ln1_w', 'new_v_ln1_b', 'new_v_ln2_w', 'new_v_ln2_b', 'new_v_w_ffn_in', 'new_v_w_ffn_out']
TWIN_LEAF_KINDS = {'loss': 'loss', 'grad_x': 'grad_x', 'grad_c_ctx': 'grad_w', 'grad_w_ada': 'grad_w', 'grad_b_ada': 'grad_w', 'grad_w_in': 'grad_w', 'grad_w_gla_gate': 'grad_w', 'grad_b_gla_gate': 'grad_w', 'grad_gla_norm_w': 'grad_w', 'grad_s5_lam_re': 'grad_w', 'grad_s5_lam_im': 'grad_w', 'grad_s5_log_dt': 'grad_w', 'grad_s5_b_re': 'grad_w', 'grad_s5_b_im': 'grad_w', 'grad_s5_c_re': 'grad_w', 'grad_s5_c_im': 'grad_w', 'grad_s5_d': 'grad_w', 'grad_w_s5_glu': 'grad_w', 'grad_q_norm_w': 'grad_w', 'grad_k_norm_w': 'grad_w', 'grad_w_proj_gla': 'grad_w', 'grad_w_proj_s5': 'grad_w', 'grad_w_proj_attn': 'grad_w', 'grad_w_out': 'grad_w', 'grad_ln1_w': 'grad_w', 'grad_ln1_b': 'grad_w', 'grad_ln2_w': 'grad_w', 'grad_ln2_b': 'grad_w', 'grad_w_ffn_in': 'grad_w', 'grad_w_ffn_out': 'grad_w', 'delta_c_ctx': 'delta_w', 'delta_w_ada': 'delta_w', 'delta_b_ada': 'delta_w', 'delta_w_in': 'delta_w', 'delta_w_gla_gate': 'delta_w', 'delta_b_gla_gate': 'delta_w', 'delta_gla_norm_w': 'delta_w', 'delta_s5_lam_re': 'delta_w', 'delta_s5_lam_im': 'delta_w', 'delta_s5_log_dt': 'delta_w', 'delta_s5_b_re': 'delta_w', 'delta_s5_b_im': 'delta_w', 'delta_s5_c_re': 'delta_w', 'delta_s5_c_im': 'delta_w', 'delta_s5_d': 'delta_w', 'delta_w_s5_glu': 'delta_w', 'delta_q_norm_w': 'delta_w', 'delta_k_norm_w': 'delta_w', 'delta_w_proj_gla': 'delta_w', 'delta_w_proj_s5': 'delta_w', 'delta_w_proj_attn': 'delta_w', 'delta_w_out': 'delta_w', 'delta_ln1_w': 'delta_w', 'delta_ln1_b': 'delta_w', 'delta_ln2_w': 'delta_w', 'delta_ln2_b': 'delta_w', 'delta_w_ffn_in': 'delta_w', 'delta_w_ffn_out': 'delta_w', 'new_m_c_ctx': 'new_m', 'new_m_w_ada': 'new_m', 'new_m_b_ada': 'new_m', 'new_m_w_in': 'new_m', 'new_m_w_gla_gate': 'new_m', 'new_m_b_gla_gate': 'new_m', 'new_m_gla_norm_w': 'new_m', 'new_m_s5_lam_re': 'new_m', 'new_m_s5_lam_im': 'new_m', 'new_m_s5_log_dt': 'new_m', 'new_m_s5_b_re': 'new_m', 'new_m_s5_b_im': 'new_m', 'new_m_s5_c_re': 'new_m', 'new_m_s5_c_im': 'new_m', 'new_m_s5_d': 'new_m', 'new_m_w_s5_glu': 'new_m', 'new_m_q_norm_w': 'new_m', 'new_m_k_norm_w': 'new_m', 'new_m_w_proj_gla': 'new_m', 'new_m_w_proj_s5': 'new_m', 'new_m_w_proj_attn': 'new_m', 'new_m_w_out': 'new_m', 'new_m_ln1_w': 'new_m', 'new_m_ln1_b': 'new_m', 'new_m_ln2_w': 'new_m', 'new_m_ln2_b': 'new_m', 'new_m_w_ffn_in': 'new_m', 'new_m_w_ffn_out': 'new_m', 'new_v_c_ctx': 'new_v', 'new_v_w_ada': 'new_v', 'new_v_b_ada': 'new_v', 'new_v_w_in': 'new_v', 'new_v_w_gla_gate': 'new_v', 'new_v_b_gla_gate': 'new_v', 'new_v_gla_norm_w': 'new_v', 'new_v_s5_lam_re': 'new_v', 'new_v_s5_lam_im': 'new_v', 'new_v_s5_log_dt': 'new_v', 'new_v_s5_b_re': 'new_v', 'new_v_s5_b_im': 'new_v', 'new_v_s5_c_re': 'new_v', 'new_v_s5_c_im': 'new_v', 'new_v_s5_d': 'new_v', 'new_v_w_s5_glu': 'new_v', 'new_v_q_norm_w': 'new_v', 'new_v_k_norm_w': 'new_v', 'new_v_w_proj_gla': 'new_v', 'new_v_w_proj_s5': 'new_v', 'new_v_w_proj_attn': 'new_v', 'new_v_w_out': 'new_v', 'new_v_ln1_w': 'new_v', 'new_v_ln1_b': 'new_v', 'new_v_ln2_w': 'new_v', 'new_v_ln2_b': 'new_v', 'new_v_w_ffn_in': 'new_v', 'new_v_w_ffn_out': 'new_v'}


def _forward(args):
    return _fwd_reference(*[args[k] for k in FWD_PARAMS])


def _output_shape():
    out = _jax.eval_shape(lambda: _forward(_fwd_setup_inputs(0)))
    return out.shape, out.dtype

N_MICROBATCH = 1
ADAM_LR = 0.001
ADAM_B1 = 0.9
ADAM_B2 = 0.999
ADAM_EPS = 1e-08
ADAM_WD = 0.01
ADAM_STEP = 10
PER_EXAMPLE_BATCH_AXIS = {'x': 0, 'c': 0, 'ctx': 0, 'loss_target': 0}
SHARED_INPUTS = []
_WEIGHT_DTYPES = {'c_ctx': _jnp.float32, 'w_ada': _jnp.float32, 'b_ada': _jnp.float32, 'w_in': _jnp.float32, 'w_gla_gate': _jnp.float32, 'b_gla_gate': _jnp.float32, 'gla_norm_w': _jnp.float32, 's5_lam_re': _jnp.float32, 's5_lam_im': _jnp.float32, 's5_log_dt': _jnp.float32, 's5_b_re': _jnp.float32, 's5_b_im': _jnp.float32, 's5_c_re': _jnp.float32, 's5_c_im': _jnp.float32, 's5_d': _jnp.float32, 'w_s5_glu': _jnp.float32, 'q_norm_w': _jnp.float32, 'k_norm_w': _jnp.float32, 'w_proj_gla': _jnp.float32, 'w_proj_s5': _jnp.float32, 'w_proj_attn': _jnp.float32, 'w_out': _jnp.float32, 'ln1_w': _jnp.float32, 'ln1_b': _jnp.float32, 'ln2_w': _jnp.float32, 'ln2_b': _jnp.float32, 'w_ffn_in': _jnp.float32, 'w_ffn_out': _jnp.float32}
MOMENT_SCALE = {'c_ctx': 1.023397e-03, 'w_ada': 4.033949e-03, 'b_ada': 6.892700e-03, 'w_in': 1.665448e-03, 'w_gla_gate': 4.141150e-04, 'b_gla_gate': 1.170404e-03, 'gla_norm_w': 2.371090e-03, 's5_lam_re': 6.659025e-04, 's5_lam_im': 6.484749e-04, 's5_log_dt': 2.940333e-01, 's5_b_re': 6.012183e-04, 's5_b_im': 6.066840e-04, 's5_c_re': 1.442637e-04, 's5_c_im': 1.471454e-04, 's5_d': 1.899282e-03, 'w_s5_glu': 7.399691e-04, 'q_norm_w': 8.221920e-04, 'k_norm_w': 8.197068e-04, 'w_proj_gla': 1.671721e-03, 'w_proj_s5': 1.309681e-03, 'w_proj_attn': 8.838980e-04, 'w_out': 5.433285e-03, 'ln1_w': 2.831670e-01, 'ln1_b': 1.433284e-01, 'ln2_w': 4.027971e+00, 'ln2_b': 2.454164e-01, 'w_ffn_in': 2.039273e-03, 'w_ffn_out': 7.931145e-03}


def _to_microbatches(a, axis):
    t = _jnp.moveaxis(a, axis, 0)
    t = t.reshape((N_MICROBATCH, t.shape[0] // N_MICROBATCH) + t.shape[1:])
    return _jnp.moveaxis(t, 1, axis + 1)


def setup_inputs(seed: int = 0) -> dict:
    inp = _fwd_setup_inputs(seed)
    key = _jax.random.fold_in(_jax.random.key(seed), 7919)
    shape, _ = _output_shape()
    out = dict(inp)
    out["loss_target"] = _jax.random.normal(_jax.random.fold_in(key, 0), shape, _jnp.float32)
    for i, name in enumerate(TWIN_WEIGHTS):
        w = inp[name].astype(_jnp.float32)
        if MOMENT_SCALE is None:
            s = _jnp.sqrt(_jnp.mean(_jnp.square(w)) + 1e-30)
        else:
            s = MOMENT_SCALE[name]
        km, kv = _jax.random.split(_jax.random.fold_in(key, i + 1))
        out[name] = w
        out["m_" + name] = s * _jax.random.normal(km, w.shape, _jnp.float32)
        out["v_" + name] = (s * s) * _jax.random.uniform(kv, w.shape, _jnp.float32, 0.5, 1.5)
    if N_MICROBATCH > 1:
        for name, axis in PER_EXAMPLE_BATCH_AXIS.items():
            out[name] = _to_microbatches(out[name], axis)
    return {'x': out['x'], 'c': out['c'], 'ctx': out['ctx'], 'c_ctx': out['c_ctx'], 'w_ada': out['w_ada'], 'b_ada': out['b_ada'], 'w_in': out['w_in'], 'w_gla_gate': out['w_gla_gate'], 'b_gla_gate': out['b_gla_gate'], 'gla_norm_w': out['gla_norm_w'], 's5_lam_re': out['s5_lam_re'], 's5_lam_im': out['s5_lam_im'], 's5_log_dt': out['s5_log_dt'], 's5_b_re': out['s5_b_re'], 's5_b_im': out['s5_b_im'], 's5_c_re': out['s5_c_re'], 's5_c_im': out['s5_c_im'], 's5_d': out['s5_d'], 'w_s5_glu': out['w_s5_glu'], 'q_norm_w': out['q_norm_w'], 'k_norm_w': out['k_norm_w'], 'w_proj_gla': out['w_proj_gla'], 'w_proj_s5': out['w_proj_s5'], 'w_proj_attn': out['w_proj_attn'], 'w_out': out['w_out'], 'ln1_w': out['ln1_w'], 'ln1_b': out['ln1_b'], 'ln2_w': out['ln2_w'], 'ln2_b': out['ln2_b'], 'w_ffn_in': out['w_ffn_in'], 'w_ffn_out': out['w_ffn_out'], 'loss_target': out['loss_target'], 'm_c_ctx': out['m_c_ctx'], 'm_w_ada': out['m_w_ada'], 'm_b_ada': out['m_b_ada'], 'm_w_in': out['m_w_in'], 'm_w_gla_gate': out['m_w_gla_gate'], 'm_b_gla_gate': out['m_b_gla_gate'], 'm_gla_norm_w': out['m_gla_norm_w'], 'm_s5_lam_re': out['m_s5_lam_re'], 'm_s5_lam_im': out['m_s5_lam_im'], 'm_s5_log_dt': out['m_s5_log_dt'], 'm_s5_b_re': out['m_s5_b_re'], 'm_s5_b_im': out['m_s5_b_im'], 'm_s5_c_re': out['m_s5_c_re'], 'm_s5_c_im': out['m_s5_c_im'], 'm_s5_d': out['m_s5_d'], 'm_w_s5_glu': out['m_w_s5_glu'], 'm_q_norm_w': out['m_q_norm_w'], 'm_k_norm_w': out['m_k_norm_w'], 'm_w_proj_gla': out['m_w_proj_gla'], 'm_w_proj_s5': out['m_w_proj_s5'], 'm_w_proj_attn': out['m_w_proj_attn'], 'm_w_out': out['m_w_out'], 'm_ln1_w': out['m_ln1_w'], 'm_ln1_b': out['m_ln1_b'], 'm_ln2_w': out['m_ln2_w'], 'm_ln2_b': out['m_ln2_b'], 'm_w_ffn_in': out['m_w_ffn_in'], 'm_w_ffn_out': out['m_w_ffn_out'], 'v_c_ctx': out['v_c_ctx'], 'v_w_ada': out['v_w_ada'], 'v_b_ada': out['v_b_ada'], 'v_w_in': out['v_w_in'], 'v_w_gla_gate': out['v_w_gla_gate'], 'v_b_gla_gate': out['v_b_gla_gate'], 'v_gla_norm_w': out['v_gla_norm_w'], 'v_s5_lam_re': out['v_s5_lam_re'], 'v_s5_lam_im': out['v_s5_lam_im'], 'v_s5_log_dt': out['v_s5_log_dt'], 'v_s5_b_re': out['v_s5_b_re'], 'v_s5_b_im': out['v_s5_b_im'], 'v_s5_c_re': out['v_s5_c_re'], 'v_s5_c_im': out['v_s5_c_im'], 'v_s5_d': out['v_s5_d'], 'v_w_s5_glu': out['v_w_s5_glu'], 'v_q_norm_w': out['v_q_norm_w'], 'v_k_norm_w': out['v_k_norm_w'], 'v_w_proj_gla': out['v_w_proj_gla'], 'v_w_proj_s5': out['v_w_proj_s5'], 'v_w_proj_attn': out['v_w_proj_attn'], 'v_w_out': out['v_w_out'], 'v_ln1_w': out['v_ln1_w'], 'v_ln1_b': out['v_ln1_b'], 'v_ln2_w': out['v_ln2_w'], 'v_ln2_b': out['v_ln2_b'], 'v_w_ffn_in': out['v_w_ffn_in'], 'v_w_ffn_out': out['v_w_ffn_out']}


def _loss(weights, diff, rest, loss_target):
    with _jax.named_scope("forward"):
        args = {**rest, TWIN_DIFF_INPUT: diff, **{k: w.astype(_WEIGHT_DTYPES[k]) for k, w in weights.items()}}
        y = _forward(args)
    with _jax.named_scope("loss_head"):
        err = _jnp.square(y.astype(_jnp.float32) - loss_target)
        return 0.5 * _jnp.sum(_jnp.mean(err, axis=-1)) if err.ndim else 0.5 * err


def _adamw(w, g, m, v):
    m = ADAM_B1 * m + (1.0 - ADAM_B1) * g
    v = ADAM_B2 * v + (1.0 - ADAM_B2) * _jnp.square(g)
    m_hat = m / (1.0 - ADAM_B1 ** ADAM_STEP)
    v_hat = v / (1.0 - ADAM_B2 ** ADAM_STEP)
    delta = -ADAM_LR * (m_hat / (_jnp.sqrt(v_hat) + ADAM_EPS) + ADAM_WD * w)
    return delta, m, v


def reference(x, c, ctx, c_ctx, w_ada, b_ada, w_in, w_gla_gate, b_gla_gate, gla_norm_w, s5_lam_re, s5_lam_im, s5_log_dt, s5_b_re, s5_b_im, s5_c_re, s5_c_im, s5_d, w_s5_glu, q_norm_w, k_norm_w, w_proj_gla, w_proj_s5, w_proj_attn, w_out, ln1_w, ln1_b, ln2_w, ln2_b, w_ffn_in, w_ffn_out, loss_target, m_c_ctx, m_w_ada, m_b_ada, m_w_in, m_w_gla_gate, m_b_gla_gate, m_gla_norm_w, m_s5_lam_re, m_s5_lam_im, m_s5_log_dt, m_s5_b_re, m_s5_b_im, m_s5_c_re, m_s5_c_im, m_s5_d, m_w_s5_glu, m_q_norm_w, m_k_norm_w, m_w_proj_gla, m_w_proj_s5, m_w_proj_attn, m_w_out, m_ln1_w, m_ln1_b, m_ln2_w, m_ln2_b, m_w_ffn_in, m_w_ffn_out, v_c_ctx, v_w_ada, v_b_ada, v_w_in, v_w_gla_gate, v_b_gla_gate, v_gla_norm_w, v_s5_lam_re, v_s5_lam_im, v_s5_log_dt, v_s5_b_re, v_s5_b_im, v_s5_c_re, v_s5_c_im, v_s5_d, v_w_s5_glu, v_q_norm_w, v_k_norm_w, v_w_proj_gla, v_w_proj_s5, v_w_proj_attn, v_w_out, v_ln1_w, v_ln1_b, v_ln2_w, v_ln2_b, v_w_ffn_in, v_w_ffn_out):
    given = dict(x=x, c=c, ctx=ctx, c_ctx=c_ctx, w_ada=w_ada, b_ada=b_ada, w_in=w_in, w_gla_gate=w_gla_gate, b_gla_gate=b_gla_gate, gla_norm_w=gla_norm_w, s5_lam_re=s5_lam_re, s5_lam_im=s5_lam_im, s5_log_dt=s5_log_dt, s5_b_re=s5_b_re, s5_b_im=s5_b_im, s5_c_re=s5_c_re, s5_c_im=s5_c_im, s5_d=s5_d, w_s5_glu=w_s5_glu, q_norm_w=q_norm_w, k_norm_w=k_norm_w, w_proj_gla=w_proj_gla, w_proj_s5=w_proj_s5, w_proj_attn=w_proj_attn, w_out=w_out, ln1_w=ln1_w, ln1_b=ln1_b, ln2_w=ln2_w, ln2_b=ln2_b, w_ffn_in=w_ffn_in, w_ffn_out=w_ffn_out, loss_target=loss_target, m_c_ctx=m_c_ctx, m_w_ada=m_w_ada, m_b_ada=m_b_ada, m_w_in=m_w_in, m_w_gla_gate=m_w_gla_gate, m_b_gla_gate=m_b_gla_gate, m_gla_norm_w=m_gla_norm_w, m_s5_lam_re=m_s5_lam_re, m_s5_lam_im=m_s5_lam_im, m_s5_log_dt=m_s5_log_dt, m_s5_b_re=m_s5_b_re, m_s5_b_im=m_s5_b_im, m_s5_c_re=m_s5_c_re, m_s5_c_im=m_s5_c_im, m_s5_d=m_s5_d, m_w_s5_glu=m_w_s5_glu, m_q_norm_w=m_q_norm_w, m_k_norm_w=m_k_norm_w, m_w_proj_gla=m_w_proj_gla, m_w_proj_s5=m_w_proj_s5, m_w_proj_attn=m_w_proj_attn, m_w_out=m_w_out, m_ln1_w=m_ln1_w, m_ln1_b=m_ln1_b, m_ln2_w=m_ln2_w, m_ln2_b=m_ln2_b, m_w_ffn_in=m_w_ffn_in, m_w_ffn_out=m_w_ffn_out, v_c_ctx=v_c_ctx, v_w_ada=v_w_ada, v_b_ada=v_b_ada, v_w_in=v_w_in, v_w_gla_gate=v_w_gla_gate, v_b_gla_gate=v_b_gla_gate, v_gla_norm_w=v_gla_norm_w, v_s5_lam_re=v_s5_lam_re, v_s5_lam_im=v_s5_lam_im, v_s5_log_dt=v_s5_log_dt, v_s5_b_re=v_s5_b_re, v_s5_b_im=v_s5_b_im, v_s5_c_re=v_s5_c_re, v_s5_c_im=v_s5_c_im, v_s5_d=v_s5_d, v_w_s5_glu=v_w_s5_glu, v_q_norm_w=v_q_norm_w, v_k_norm_w=v_k_norm_w, v_w_proj_gla=v_w_proj_gla, v_w_proj_s5=v_w_proj_s5, v_w_proj_attn=v_w_proj_attn, v_w_out=v_w_out, v_ln1_w=v_ln1_w, v_ln1_b=v_ln1_b, v_ln2_w=v_ln2_w, v_ln2_b=v_ln2_b, v_w_ffn_in=v_w_ffn_in, v_w_ffn_out=v_w_ffn_out)
    weights = {n: given[n] for n in TWIN_WEIGHTS}
    shared = {n: given[n] for n in SHARED_INPUTS}
    per_example = {n: given[n] for n in ['x', 'c', 'ctx']}
    grad_fn = _jax.value_and_grad(_loss, argnums=(0, 1))

    def one_microbatch(ex, loss_target):
        ex = dict(ex)
        diff = ex.pop(TWIN_DIFF_INPUT)
        return grad_fn(weights, diff, {**shared, **ex}, loss_target)

    if N_MICROBATCH == 1:
        loss, (grad_w, grad_x) = one_microbatch(per_example, given["loss_target"])
    else:
        def body(carry, xs):
            loss_sum, grad_sum = carry
            l_k, (gw_k, gx_k) = one_microbatch(xs[0], xs[1])
            with _jax.named_scope("update"):
                return (loss_sum + l_k, _jax.tree.map(_jnp.add, grad_sum, gw_k)), gx_k

        init = (_jnp.zeros((), _jnp.float32), _jax.tree.map(_jnp.zeros_like, weights))
        (loss, grad_w), grad_x = _jax.lax.scan(body, init, (per_example, given["loss_target"]))
    with _jax.named_scope("update"):
        delta_w, new_m, new_v = {}, {}, {}
        for n in TWIN_WEIGHTS:
            delta_w[n], new_m[n], new_v[n] = _adamw(weights[n], grad_w[n], given["m_" + n], given["v_" + n])
    return (loss, grad_x, *[grad_w[n] for n in TWIN_WEIGHTS], *[delta_w[n] for n in TWIN_WEIGHTS],
            *[new_m[n] for n in TWIN_WEIGHTS], *[new_v[n] for n in TWIN_WEIGHTS])
```

```python
import functools

import jax
import jax.numpy as jnp
from jax import lax
from jax.experimental import pallas as pl
from jax.experimental.pallas import tpu as pltpu

F32 = jnp.float32
BF16 = jnp.bfloat16

GRID_W = 64
GLA_HEADS = 4
GLA_DK = 128
GLA_DV = 256
GLA_GATE_RANK = 16
GLA_GATE_TAU = 16.0
GLA_CHUNK = 64
S5_GROUP = 16
S5_STATE = 64
ATTN_Q_HEADS = 8
ATTN_KV_HEADS = 2
ATTN_HEAD_DIM = 128
ROPE_THETA = 10000.0
DEPTH = 4
DN_ALPHA = (2 * DEPTH) ** 0.25
EPS = 1e-6
ADAM_LR = 0.001
ADAM_B1 = 0.9
ADAM_B2 = 0.999
ADAM_EPS = 1e-08
ADAM_WD = 0.01
ADAM_STEP = 10

LANE = 128
SUBLANE = 8
VMEM_LIMIT = 56 * 1024 * 1024
ADA_ROWS = 16
S5_CHUNK = 128
S5_BLOCK_GROUPS = 8
ROW_TILE = 256
COL_TILE = 512
PACK_ROWS = 512
N_CHIPS = 4


def _params(sem, **kw):
    return pltpu.CompilerParams(dimension_semantics=sem, vmem_limit_bytes=VMEM_LIMIT, **kw)


def _tile(n, target, base):
    if n <= target:
        return n
    best = None
    for t in range(base, target + 1, base):
        if n % t == 0:
            best = t
    assert best is not None, (n, target, base)
    return best


_DIMS = {"nn": ((1,), (0,)), "nt": ((1,), (1,)), "tn": ((0,), (0,))}


def _dg(a, b, mode):
    return lax.dot_general(a.astype(BF16), b.astype(BF16), (_DIMS[mode], ((), ())), preferred_element_type=F32)


@functools.partial(jax.custom_vjp, nondiff_argnums=(2,))
def bdot(a, b, mode):
    return _dg(a, b, mode)


def _bdot_fwd(a, b, mode):
    return _dg(a, b, mode), (a, b)


def _bdot_bwd(mode, res, g):
    a, b = res
    if mode == "nn":
        return bdot(g, b, "nt").astype(a.dtype), bdot(a, g, "tn").astype(b.dtype)
    if mode == "nt":
        return bdot(g, b, "nn").astype(a.dtype), bdot(g, a, "tn").astype(b.dtype)
    return bdot(b, g, "nt").astype(a.dtype), bdot(a, g, "nn").astype(b.dtype)


bdot.defvjp(_bdot_fwd, _bdot_bwd)


def _mm_tiles(M, Kb, Nb):
    return _tile(M, 1024, SUBLANE), _tile(Kb, 1024, LANE), _tile(Nb, 1024, LANE)


def _mm_body(mode, last):
    def body(p_ref, q_ref, o_ref, acc):
        k = pl.program_id(3)

        @pl.when(k == 0)
        def _():
            acc[...] = jnp.zeros_like(acc)

        acc[...] += _dg(p_ref[...], q_ref[...], mode)

        @pl.when(k == last)
        def _():
            o_ref[...] = acc[...]

    return body


def _mm_nn(name, a, w):
    M = a.shape[0]
    B, Kb, Nb = w.shape
    tm, tk, tn = _mm_tiles(M, Kb, Nb)
    nk, nn = Kb // tk, Nb // tn
    return pl.pallas_call(
        _mm_body("nn", nk - 1), name=name, grid=(B, M // tm, nn, nk),
        in_specs=[pl.BlockSpec((tm, tk), lambda b, i, j, k: (i, b * nk + k)),
                  pl.BlockSpec((None, tk, tn), lambda b, i, j, k: (b, k, j))],
        out_specs=pl.BlockSpec((tm, tn), lambda b, i, j, k: (i, b * nn + j)),
        out_shape=jax.ShapeDtypeStruct((M, B * Nb), F32),
        scratch_shapes=[pltpu.VMEM((tm, tn), F32)],
        compiler_params=_params(("arbitrary",) * 4))(a, w)


def _mm_nt(name, g, w):
    M = g.shape[0]
    B, Kb, Nb = w.shape
    tm, tk, tn = _mm_tiles(M, Kb, Nb)
    nk, nn = Kb // tk, Nb // tn
    return pl.pallas_call(
        _mm_body("nt", nn - 1), name=name, grid=(B, M // tm, nk, nn),
        in_specs=[pl.BlockSpec((tm, tn), lambda b, i, k, n: (i, b * nn + n)),
                  pl.BlockSpec((None, tk, tn), lambda b, i, k, n: (b, k, n))],
        out_specs=pl.BlockSpec((tm, tk), lambda b, i, k, n: (i, b * nk + k)),
        out_shape=jax.ShapeDtypeStruct((M, B * Kb), F32),
        scratch_shapes=[pltpu.VMEM((tm, tk), F32)],
        compiler_params=_params(("arbitrary",) * 4))(g, w)


def _mm_tn(name, a, g, B):
    M = a.shape[0]
    Kb, Nb = a.shape[1] // B, g.shape[1] // B
    tm, tk, tn = _mm_tiles(M, Kb, Nb)
    nk, nn = Kb // tk, Nb // tn
    return pl.pallas_call(
        _mm_body("tn", M // tm - 1), name=name, grid=(B, nk, nn, M // tm),
        in_specs=[pl.BlockSpec((tm, tk), lambda b, k, j, m: (m, b * nk + k)),
                  pl.BlockSpec((tm, tn), lambda b, k, j, m: (m, b * nn + j))],
        out_specs=pl.BlockSpec((None, tk, tn), lambda b, k, j, m: (b, k, j)),
        out_shape=jax.ShapeDtypeStruct((B, Kb, Nb), F32),
        scratch_shapes=[pltpu.VMEM((tk, tn), F32)],
        compiler_params=_params(("arbitrary",) * 4))(a, g)


def mm(name, a, w):
    w3 = w if w.ndim == 3 else w[None]

    @jax.custom_vjp
    def op(a, w3):
        return _mm_nn(name + "_fwd", a, w3.astype(BF16))

    def fwd(a, w3):
        wb = w3.astype(BF16)
        return _mm_nn(name + "_fwd", a, wb), (a, wb)

    def bwd(res, g):
        a, wb = res
        return _mm_nt(name + "_dx", g, wb), _mm_tn(name + "_dw", a, g, wb.shape[0])

    op.defvjp(fwd, bwd)
    return op(a, w3)


def _spec_shape(spec, G, T):
    k = spec[0]
    if k == "row":
        return (T, spec[1])
    if k == "rowg":
        return (T, (G // spec[2]) * spec[1])
    if k == "bc":
        return (spec[1], spec[2])
    return (spec[1], (G // spec[3]) * spec[2])


def _spec_block(spec, tm, rmap):
    k = spec[0]
    if k == "row":
        return pl.BlockSpec((tm, spec[1]), lambda g, r: (rmap(r), 0))
    if k == "rowg":
        d = spec[2]
        return pl.BlockSpec((tm, spec[1]), lambda g, r: (rmap(r), g // d))
    if k == "bc":
        return pl.BlockSpec((spec[1], spec[2]), lambda g, r: (0, 0))
    d = spec[3]
    return pl.BlockSpec((spec[1], spec[2]), lambda g, r: (0, g // d))


def block_op(name, f, in_specs, out_specs, G, T, tm, diff, carry=(), row0=False):
    n_in, n_out, n_c = len(in_specs), len(out_specs), len(carry)
    n_steps = T // tm
    assert T % tm == 0
    diff_idx = [i for i in range(n_in) if diff[i]]
    for i in diff_idx:
        assert in_specs[i][0] != "row" or G == 1
        assert in_specs[i][0] != "rowg" or in_specs[i][2] == 1
    out_shapes = [jax.ShapeDtypeStruct(_spec_shape(s, G, T), F32) for s in out_specs]
    save_shapes = [jax.ShapeDtypeStruct((n_steps, a, G * b), F32) for a, b in carry]
    sem = ("arbitrary", "arbitrary")

    def call_f(r_idx, cvals, vals):
        args = list(vals)
        if n_c:
            args = [tuple(cvals)] + args
        if row0:
            args = [r_idx * tm] + args
        return f(*args)

    def fwd_body(*refs):
        in_refs = refs[:n_in]
        out_refs = refs[n_in:n_in + n_out]
        save_refs = refs[n_in + n_out:n_in + n_out + n_c]
        c_refs = refs[n_in + n_out + n_c:]
        r = pl.program_id(1)
        if n_c:
            @pl.when(r == 0)
            def _():
                for c in c_refs:
                    c[...] = jnp.zeros_like(c)

            cvals = [c[...] for c in c_refs]
            for s, v in zip(save_refs, cvals):
                s[...] = v
            new_c, outs = call_f(r, cvals, [x[...] for x in in_refs])
            for c, v in zip(c_refs, new_c):
                c[...] = v
        else:
            outs = call_f(r, (), [x[...] for x in in_refs])
        for o, v in zip(out_refs, outs):
            o[...] = v.astype(F32)

    def fwd_call(*arrays):
        ident = lambda r: r
        res = pl.pallas_call(
            fwd_body, name=name + "_fwd", grid=(G, n_steps),
            in_specs=[_spec_block(s, tm, ident) for s in in_specs],
            out_specs=[_spec_block(s, tm, ident) for s in out_specs]
            + [pl.BlockSpec((None, a, b), lambda g, r: (r, 0, g)) for a, b in carry],
            out_shape=out_shapes + save_shapes,
            scratch_shapes=[pltpu.VMEM((a, b), F32) for a, b in carry],
            compiler_params=_params(sem))(*arrays)
        return tuple(res)

    def bwd_body(*refs):
        in_refs = refs[:n_in]
        save_refs = refs[n_in:n_in + n_c]
        ct_refs = refs[n_in + n_c:n_in + n_c + n_out]
        g_refs = refs[n_in + n_c + n_out:n_in + n_c + n_out + len(diff_idx)]
        dc_refs = refs[n_in + n_c + n_out + len(diff_idx):]
        g = pl.program_id(0)
        r = pl.program_id(1)
        vals = [x[...] for x in in_refs]
        if n_c:
            @pl.when(r == 0)
            def _():
                for d in dc_refs:
                    d[...] = jnp.zeros_like(d)

        def fun(cvals, dvals):
            full = list(vals)
            for i, v in zip(diff_idx, dvals):
                full[i] = v
            return call_f(n_steps - 1 - r if n_c else r, cvals, full)

        _, vjp = jax.vjp(fun, tuple(s[...] for s in save_refs), tuple(vals[i] for i in diff_idx))
        cts = tuple(c[...] for c in ct_refs)
        if n_c:
            cts = (tuple(d[...] for d in dc_refs), cts)
        dcin, dvals = vjp(cts)
        for d, v in zip(dc_refs, dcin):
            d[...] = v
        for gref, i, v in zip(g_refs, diff_idx, dvals):
            spec = in_specs[i]
            if spec[0] in ("row", "rowg"):
                gref[...] = v
            else:
                first = (r == 0) & ((g == 0) if spec[0] == "bc" else (g % spec[3] == 0))

                @pl.when(first)
                def _(gref=gref, v=v):
                    gref[...] = v

                @pl.when(jnp.logical_not(first))
                def _(gref=gref, v=v):
                    gref[...] += v

    def bwd_call(arrays, saved, cts):
        rmap = (lambda r: n_steps - 1 - r) if n_c else (lambda r: r)
        res = pl.pallas_call(
            bwd_body, name=name + "_bwd", grid=(G, n_steps),
            in_specs=[_spec_block(s, tm, rmap) for s in in_specs]
            + [pl.BlockSpec((None, a, b), lambda g, r: (rmap(r), 0, g)) for a, b in carry]
            + [_spec_block(s, tm, rmap) for s in out_specs],
            out_specs=[_spec_block(in_specs[i], tm, rmap) for i in diff_idx],
            out_shape=[jax.ShapeDtypeStruct(_spec_shape(in_specs[i], G, T), F32) for i in diff_idx],
            scratch_shapes=[pltpu.VMEM((a, b), F32) for a, b in carry],
            compiler_params=_params(sem))(*arrays, *saved, *cts)
        return tuple(res)

    @jax.custom_vjp
    def op(*arrays):
        return fwd_call(*arrays)[:n_out]

    def op_fwd(*arrays):
        res = fwd_call(*arrays)
        return res[:n_out], (arrays, res[n_out:])

    def op_bwd(res, cts):
        arrays, saved = res
        grads = bwd_call(arrays, saved, cts)
        out = [jnp.zeros_like(a) for a in arrays]
        for i, gval in zip(diff_idx, grads):
            out[i] = gval
        return tuple(out)

    op.defvjp(op_fwd, op_bwd)
    return op


def _rows(n, m):
    return lax.broadcasted_iota(jnp.int32, (n, m), 0)


def _ctx_select(row0, tm, n_ctx, v_lat, v_ctx):
    if n_ctx == 0:
        return v_lat
    is_ctx = (row0 + _rows(tm, 1)) < n_ctx
    return jnp.where(is_ctx, v_ctx, v_lat)


def _silu(x):
    return x * jax.nn.sigmoid(x)


def _f_modulate(tm, n_ctx):
    def f(row0, x, sh_l, sh_c, sc_l, sc_c):
        sh = _ctx_select(row0, tm, n_ctx, sh_l, sh_c)
        sc = _ctx_select(row0, tm, n_ctx, sc_l, sc_c)
        return (x * (1 + sc) + sh,)
    return f


def _f_postnorm(tm, n_ctx):
    def f(row0, x, y, g_l, g_c, w, b):
        z = DN_ALPHA * x + _ctx_select(row0, tm, n_ctx, g_l, g_c) * y
        mu = jnp.mean(z, -1, keepdims=True)
        zc = z - mu
        var = jnp.mean(zc * zc, -1, keepdims=True)
        return (zc * lax.rsqrt(var + EPS) * w + b,)
    return f


def _log_sigmoid(x):
    return -(jnp.maximum(-x, 0.0) + jnp.log1p(jnp.exp(-jnp.abs(x))))


def _f_gla_prep(glr, wg0, wg1, b0, b1):
    return (_log_sigmoid(bdot(glr, wg0, "nn") + b0) / GLA_GATE_TAU,
            _log_sigmoid(bdot(glr, wg1, "nn") + b1) / GLA_GATE_TAU)


def _f_gla_step(carry, q, k, v, la):
    (st,) = carry
    n = q.shape[0]
    tri = _rows(n, n) >= lax.broadcasted_iota(jnp.int32, (n, n), 1)
    b = jnp.dot(tri.astype(F32), la, precision=lax.Precision.HIGHEST)
    qe = q * (GLA_DK ** -0.5) * jnp.exp(b)
    ke = k * jnp.exp(-b)
    att = jnp.where(tri, bdot(qe, ke, "nt"), 0.0)
    o = bdot(att, v, "nn") + bdot(qe, st, "nt")
    b_last = jnp.sum(jnp.where(_rows(n, 1) == n - 1, b, 0.0), axis=0, keepdims=True)
    kd = k * jnp.exp(b_last - b)
    st = st * jnp.exp(b_last) + bdot(v, kd, "tn")
    return (st,), (o,)


def _f_gla_norm(o0, o1, gr, w):
    o = o0 + o1
    mu = jnp.mean(o, -1, keepdims=True)
    oc = o - mu
    var = jnp.mean(oc * oc, -1, keepdims=True)
    return (oc * lax.rsqrt(var + EPS) * w * _silu(gr),)


@functools.partial(jax.custom_vjp, nondiff_argnums=(1, 2))
def _shift_rows(x, d, up):
    n = x.shape[0]
    rows = _rows(n, 1)
    if up:
        return jnp.where(rows < n - d, pltpu.roll(x, n - d, 0), 0.0)
    return jnp.where(rows >= d, pltpu.roll(x, d, 0), 0.0)


def _shift_fwd(x, d, up):
    return _shift_rows(x, d, up), None


def _shift_bwd(d, up, _, g):
    return (_shift_rows(g, d, not up),)


_shift_rows.defvjp(_shift_fwd, _shift_bwd)


def _f_s5_step(carry, bur, bui, lam_re, lam_im, log_dt):
    cr, ci = carry
    n = bur.shape[0]
    dt = jnp.exp(log_dt)
    mag = jnp.exp(lam_re * dt)
    ar, ai = mag * jnp.cos(lam_im * dt), mag * jnp.sin(lam_im * dt)
    den = lam_re * lam_re + lam_im * lam_im
    nr, ni = ar - 1, ai
    kr = (nr * lam_re + ni * lam_im) / den
    ki = (ni * lam_re - nr * lam_im) / den
    first = _rows(n, 1) == 0
    xr = kr * bur - ki * bui + jnp.where(first, ar * cr - ai * ci, 0.0)
    xi = kr * bui + ki * bur + jnp.where(first, ar * ci + ai * cr, 0.0)
    pr, pi = ar, ai
    d = 1
    while d < n:
        sr, si = _shift_rows(xr, d, False), _shift_rows(xi, d, False)
        xr, xi = xr + pr * sr - pi * si, xi + pr * si + pi * sr
        pr, pi = pr * pr - pi * pi, 2 * pr * pi
        d *= 2
    last = _rows(n, 1) == n - 1
    cr = jnp.sum(jnp.where(last, xr, 0.0), axis=0, keepdims=True)
    ci = jnp.sum(jnp.where(last, xi, 0.0), axis=0, keepdims=True)
    return (cr, ci), (xr, xi)


def _f_s5_post(su, dskip, y0r, y0i, y1r, y1i):
    return (jax.nn.gelu(su * dskip + y0r - y0i + y1r - y1i),)


def _f_s5_glu(y, t):
    return (y * jax.nn.sigmoid(t),)


def _swap_pairs(x):
    lane = lax.broadcasted_iota(jnp.int32, x.shape, 1)
    return jnp.where(lane % 2 == 0, pltpu.roll(x, x.shape[1] - 1, 1), pltpu.roll(x, 1, 1))


@jax.custom_vjp
def _rope(x, cos2, sin2):
    return x * cos2 + _swap_pairs(x) * sin2


def _rope_fwd(x, cos2, sin2):
    return _rope(x, cos2, sin2), (cos2, sin2)


def _rope_bwd(res, g):
    cos2, sin2 = res
    return g * cos2 + _swap_pairs(g * sin2), jnp.zeros_like(cos2), jnp.zeros_like(sin2)


_rope.defvjp(_rope_fwd, _rope_bwd)


def _f_qk_norm_rope(x, cos2, sin2, w):
    xn = x * lax.rsqrt(jnp.mean(x * x, -1, keepdims=True) + EPS) * w
    return (_rope(xn, cos2, sin2),)


def _f_attn(q, k, v):
    s = bdot(q, k, "nt") * (ATTN_HEAD_DIM ** -0.5)
    e = jnp.exp(s - jnp.max(s, -1, keepdims=True))
    p = e / jnp.sum(e, -1, keepdims=True)
    return (bdot(p, v, "nn"),)


def _f_merge(ga, gb, gc, pa, pb, pc):
    return (jax.nn.sigmoid(ga) * pa + jax.nn.sigmoid(gb) * pb + jax.nn.sigmoid(gc) * pc,)


def _f_swiglu(a, b):
    return (_silu(a) * b,)


def _f_silu(x):
    return (_silu(x),)


def _f_add_bias(x, b):
    return (x + b,)


def _seg_flip(z, n_ctx):
    return jnp.concatenate([jnp.flip(z[:n_ctx], 0), jnp.flip(z[n_ctx:], 0)], 0)


def modulate(name, x, sh, sc, n_ctx):
    T, D = x.shape
    tm = _tile(T, ROW_TILE, SUBLANE)
    cw = _tile(D, COL_TILE, LANE)
    col, vec = ("rowg", cw, 1), ("bcg", 1, cw, 1)
    op = block_op(name, _f_modulate(tm, n_ctx), [col, vec, vec, vec, vec], [col], D // cw, T, tm,
                  [True] * 5, row0=True)
    return op(x, sh[0], sh[1], sc[0], sc[1])[0]


def postnorm(name, x, y, g, w, b, n_ctx):
    T, D = x.shape
    tm = _tile(T, ROW_TILE, SUBLANE)
    vec = ("bc", 1, D)
    op = block_op(name, _f_postnorm(tm, n_ctx), [("row", D), ("row", D), vec, vec, vec, vec], [("row", D)], 1, T,
                  tm, [True] * 6, row0=True)
    return op(x, y, g[0], g[1], w, b)[0]


def rowwise(name, f, arrays, n_out=1):
    T, w = arrays[0].shape
    tm = _tile(T, ROW_TILE, SUBLANE)
    cw = _tile(w, COL_TILE, LANE)
    col = ("rowg", cw, 1)
    op = block_op(name, f, [col] * len(arrays), [col] * n_out, w // cw, T, tm, [True] * len(arrays))
    return op(*arrays)


def gla_prep(name, glr, wg, bg):
    T = glr.shape[0]
    qk = wg.shape[-1]
    tm = _tile(T, ROW_TILE, SUBLANE)
    op = block_op(name, _f_gla_prep, [("row", LANE), ("bc", LANE, qk), ("bc", LANE, qk), ("bc", 1, qk), ("bc", 1, qk)],
                  [("row", qk), ("row", qk)], 1, T, tm, [True] * 5)
    return op(glr, wg[0], wg[1], bg[0], bg[1])


def gla_scan(name, q, k, v, la):
    T = q.shape[0]
    op = block_op(name, _f_gla_step, [("rowg", GLA_DK, 1), ("rowg", GLA_DK, 1), ("rowg", GLA_DV, 1), ("rowg", GLA_DK, 1)],
                  [("rowg", GLA_DV, 1)], GLA_HEADS, T, GLA_CHUNK, [True] * 4, carry=[(GLA_DV, GLA_DK)])
    return op(q, k, v, la)[0]


def gla_norm(name, o0, o1, gr, w):
    T = o0.shape[0]
    tm = _tile(T, ROW_TILE, SUBLANE)
    hd = ("rowg", GLA_DV, 1)
    op = block_op(name, _f_gla_norm, [hd, hd, hd, ("bcg", 1, GLA_DV, 1)], [hd], GLA_HEADS, T, tm, [True] * 4)
    return op(o0, o1, gr, w)[0]


def s5_scan(name, bur, bui, lam_re, lam_im, log_dt):
    T, S = bur.shape
    cols = _tile(S, 768, LANE)
    G = S // cols
    col, par = ("rowg", cols, 1), ("bcg", 1, cols, 1)
    op = block_op(name, _f_s5_step, [col, col, par, par, par], [col, col], G, T, S5_CHUNK, [True] * 5,
                  carry=[(1, cols), (1, cols)])
    return op(bur, bui, lam_re, lam_im, log_dt)


def qk_norm_rope(name, x, cos2, sin2, w):
    T = x.shape[0]
    G = x.shape[1] // ATTN_HEAD_DIM
    tm = _tile(T, ROW_TILE, SUBLANE)
    hd = ("rowg", ATTN_HEAD_DIM, 1)
    op = block_op(name, _f_qk_norm_rope, [hd, ("row", ATTN_HEAD_DIM), ("row", ATTN_HEAD_DIM), ("bc", 1, ATTN_HEAD_DIM)],
                  [hd], G, T, tm, [True, False, False, True])
    return op(x, cos2, sin2, w)[0]


def attention(name, q, k, v):
    T, Tk = q.shape[0], k.shape[0]
    tm = _tile(T, ROW_TILE, SUBLANE)
    grp = ATTN_Q_HEADS // ATTN_KV_HEADS
    kv = ("bcg", Tk, ATTN_HEAD_DIM, grp)
    op = block_op(name, _f_attn, [("rowg", ATTN_HEAD_DIM, 1), kv, kv], [("rowg", ATTN_HEAD_DIM, 1)], ATTN_Q_HEADS, T,
                  tm, [True] * 3)
    return op(q, k, v)[0]


def sq_loss(name, y, t):
    T, D = y.shape
    tm = _tile(T, ROW_TILE, SUBLANE)

    def fwd_body(y_ref, t_ref, o_ref):
        e = y_ref[...] - t_ref[...]
        part = jnp.sum(jnp.sum(e * e, -1, keepdims=True), 0, keepdims=True) * (0.5 / D)

        @pl.when(pl.program_id(0) == 0)
        def _():
            o_ref[...] = jnp.zeros_like(o_ref)

        o_ref[...] += part * jnp.ones((1, LANE), F32)

    def bwd_body(y_ref, t_ref, g_ref, o_ref):
        o_ref[...] = (y_ref[...] - t_ref[...]) * (g_ref[:, 0:1] / D)

    row = pl.BlockSpec((tm, D), lambda r: (r, 0))
    one = pl.BlockSpec((1, LANE), lambda r: (0, 0))

    def fwd_call(y, t):
        return pl.pallas_call(fwd_body, name=name + "_fwd", grid=(T // tm,), in_specs=[row, row], out_specs=one,
                              out_shape=jax.ShapeDtypeStruct((1, LANE), F32), compiler_params=_params(("arbitrary",)))(y, t)

    @jax.custom_vjp
    def op(y, t):
        return fwd_call(y, t)[0, 0]

    def op_fwd(y, t):
        return fwd_call(y, t)[0, 0], (y, t)

    def op_bwd(res, g):
        y, t = res
        gy = pl.pallas_call(bwd_body, name=name + "_bwd", grid=(T // tm,), in_specs=[row, row, one], out_specs=row,
                            out_shape=jax.ShapeDtypeStruct((T, D), F32), compiler_params=_params(("arbitrary",)))(
                                y, t, jnp.full((1, LANE), g, F32))
        return gy, jnp.zeros_like(t)

    op.defvjp(op_fwd, op_bwd)
    return op(y, t)


def _rope_tables(n_ctx, n_lat):
    n_rows = n_lat // GRID_W
    rows = jnp.repeat(jnp.arange(n_rows), GRID_W).astype(F32)
    cols = jnp.tile(jnp.arange(GRID_W), n_rows).astype(F32)
    n_freq = ATTN_HEAD_DIM // 4
    inv = ROPE_THETA ** (-jnp.arange(n_freq, dtype=F32) / n_freq)
    ang = jnp.concatenate([rows[:, None] * inv, cols[:, None] * inv], -1)
    cos2 = jnp.repeat(jnp.cos(ang), 2, axis=-1)
    sin2 = jnp.stack([-jnp.sin(ang), jnp.sin(ang)], -1).reshape(n_lat, ATTN_HEAD_DIM)
    cos2 = jnp.concatenate([jnp.ones((n_ctx, ATTN_HEAD_DIM), F32), cos2], 0)
    sin2 = jnp.concatenate([jnp.zeros((n_ctx, ATTN_HEAD_DIM), F32), sin2], 0)
    return cos2, sin2


def _in_layout(D, s5_width):
    qk, gv = GLA_HEADS * GLA_DK, GLA_HEADS * GLA_DV
    aq, akv = ATTN_Q_HEADS * ATTN_HEAD_DIM, ATTN_KV_HEADS * ATTN_HEAD_DIM
    widths = [("gq", qk), ("gk", qk), ("gv", gv), ("gr", gv), ("glr", LANE), ("su", s5_width), ("aq", aq),
              ("ak", akv), ("av", akv), ("ga", D), ("gb", D), ("gc", D)]
    off, out = 0, {}
    for n, w in widths:
        out[n] = (off, w)
        off += w
    return out, off, 2 * qk + 2 * gv


def _s5_in_blocks(b):
    G, P, C = b.shape
    nb, bg = G // S5_BLOCK_GROUPS, S5_BLOCK_GROUPS
    t = b.reshape(nb, bg, P, C).transpose(0, 1, 3, 2)
    return jnp.einsum("bgcp,gh->bgchp", t, jnp.eye(bg, dtype=F32)).reshape(nb, bg * C, bg * P)


def _s5_out_blocks(c):
    G, C, P = c.shape
    nb, bg = G // S5_BLOCK_GROUPS, S5_BLOCK_GROUPS
    t = c.reshape(nb, bg, C, P).transpose(0, 1, 3, 2)
    return jnp.einsum("bgpc,gh->bgphc", t, jnp.eye(bg, dtype=F32)).reshape(nb, bg * P, bg * C)


def _layer(keep_ctx, xa, n_ctx, mod, p, cos2, sin2):
    T, D = xa.shape
    S = p["s5_d"].shape[-1]
    lay, width, gate_at = _in_layout(D, S)
    lo = 0 if keep_ctx else n_ctx
    ctx_rows = n_ctx if keep_ctx else 0

    h = modulate("modulate1", xa, mod["sh1"], mod["sc1"], n_ctx)
    w_in = p["w_in"]
    w_in = jnp.concatenate([w_in[:, :gate_at + GLA_GATE_RANK], jnp.zeros((D, LANE - GLA_GATE_RANK), F32),
                            w_in[:, gate_at + GLA_GATE_RANK:]], 1)
    z = mm("in_proj", h, w_in)
    zz = {n: z[:, o:o + w] for n, (o, w) in lay.items()}

    wg = jnp.pad(p["w_gla_gate"], ((0, 0), (0, LANE - GLA_GATE_RANK), (0, 0)))
    la0, la1 = gla_prep("gla_prep", zz["glr"], wg, p["b_gla_gate"][:, None, :])
    o0 = gla_scan("gla_scan", zz["gq"], zz["gk"], zz["gv"], la0)
    flip = lambda t: _seg_flip(t, n_ctx)
    o1 = flip(gla_scan("gla_scan", flip(zz["gq"]), flip(zz["gk"]), flip(zz["gv"]), flip(la1)))
    o_gla = gla_norm("gla_norm", o0[lo:], o1[lo:], zz["gr"][lo:], p["gla_norm_w"][None, :])

    su = zz["su"]
    bur = mm("s5_in_re", su, _s5_in_blocks(p["s5_b_re"]))
    bui = mm("s5_in_im", su, _s5_in_blocks(p["s5_b_im"]))
    ys = []
    for d in range(2):
        row = lambda t: t.reshape(1, -1)
        ldt = jnp.repeat(p["s5_log_dt"][d], S5_STATE)
        ir, ii = (bur, bui) if d == 0 else (flip(bur), flip(bui))
        sr, si = s5_scan("s5_scan", ir, ii, row(p["s5_lam_re"][d]), row(p["s5_lam_im"][d]), row(ldt))
        if d == 1:
            sr, si = flip(sr), flip(si)
        ys.append(mm("s5_out_re", sr[lo:], _s5_out_blocks(p["s5_c_re"][d])))
        ys.append(mm("s5_out_im", si[lo:], _s5_out_blocks(p["s5_c_im"][d])))
    T2 = T - lo
    tm = _tile(T2, ROW_TILE, SUBLANE)
    post = block_op("s5_post", _f_s5_post, [("row", S), ("bc", 1, S)] + [("row", S)] * 4, [("row", S)], 1, T2, tm,
                    [True] * 6)
    yg = post(su[lo:], p["s5_d"][None, :], *ys)[0]
    o_s5 = rowwise("s5_glu", _f_s5_glu, [yg, mm("s5_glu_proj", yg, p["w_s5_glu"])])[0]

    qn = qk_norm_rope("q_norm_rope", zz["aq"], cos2, sin2, p["q_norm_w"][None, :])
    kn = qk_norm_rope("k_norm_rope", zz["ak"], cos2, sin2, p["k_norm_w"][None, :])
    o_attn = attention("attn_lat", qn[n_ctx:], kn, zz["av"])
    if keep_ctx:
        o_c = attention("attn_ctx", qn[:n_ctx], kn[:n_ctx], zz["av"][:n_ctx])
        o_attn = jnp.concatenate([o_c, o_attn], 0)

    merged = rowwise("merge", _f_merge, [zz["ga"][lo:], zz["gb"][lo:], zz["gc"][lo:],
                                         mm("proj_gla", o_gla, p["w_proj_gla"]),
                                         mm("proj_s5", o_s5, p["w_proj_s5"]),
                                         mm("proj_attn", o_attn, p["w_proj_attn"])])[0]
    mix = mm("out_proj", merged, p["w_out"])
    x1 = postnorm("postnorm1", xa[lo:], mix, mod["g1"], p["ln1_w"][None, :], p["ln1_b"][None, :], ctx_rows)
    h2 = modulate("modulate2", x1, mod["sh2"], mod["sc2"], ctx_rows)
    u = mm("ffn_in", h2, p["w_ffn_in"])
    F = u.shape[1] // 2
    act = rowwise("swiglu", _f_swiglu, [u[:, :F], u[:, F:]])[0]
    f = mm("ffn_out", act, p["w_ffn_out"])
    return postnorm("postnorm2", x1, f, mod["g2"], p["ln2_w"][None, :], p["ln2_b"][None, :], ctx_rows)


def local_loss(x, c, ctx, target, w):
    n_lat, D = x.shape
    n_ctx = ctx.shape[0]
    cos2, sin2 = _rope_tables(n_ctx, n_lat)
    cc = jnp.concatenate([c[None, :], w["c_ctx"][None, :], jnp.zeros((ADA_ROWS - 2, D), F32)], 0)
    silu_cc = rowwise("silu_cond", _f_silu, [cc])[0]
    xa = jnp.concatenate([ctx, x], 0)
    depth = w["w_in"].shape[0]
    for l in range(depth):
        p = {n: v[l] for n, v in w.items() if n != "c_ctx"}
        m = mm("ada_proj", silu_cc, p["w_ada"])
        m = block_op("ada_bias", _f_add_bias, [("row", 6 * D), ("bc", 1, 6 * D)], [("row", 6 * D)], 1, ADA_ROWS,
                     ADA_ROWS, [True, True])(m, p["b_ada"][None, :])[0]
        names = ["sh1", "sc1", "g1", "sh2", "sc2", "g2"]
        mod = {n: (m[0:1, i * D:(i + 1) * D], m[1:2, i * D:(i + 1) * D]) for i, n in enumerate(names)}
        xa = _layer(l < depth - 1, xa, n_ctx, mod, p, cos2, sin2)
    return sq_loss("loss", xa, target)


MESH = pl.DeviceIdType.MESH
ANY = pl.BlockSpec(memory_space=pl.ANY)


def _place():
    x, y, c = lax.axis_index("x"), lax.axis_index("y"), lax.axis_index("c")
    chips = [(1 - x, y), (x, 1 - y), (1 - x, 1 - y)]
    return x, y, c, chips


def _rcopy(src, dst, ssem, rsem, to):
    return pltpu.make_async_remote_copy(src_ref=src, dst_ref=dst, send_sem=ssem, recv_sem=rsem, device_id=to,
                                        device_id_type=MESH)


def gather_shards(shards):
    n = len(shards)
    L = shards[0].shape[0]
    half = L // 2

    def body(*refs):
        src, dst = refs[:n], refs[n:2 * n]
        loc, isend, irecv, dsend, drecv = refs[2 * n:]
        x, y, c, chips = _place()
        j = 2 * x + y
        mine, other = pl.ds(c * half, half), pl.ds((1 - c) * half, half)
        local = [pltpu.make_async_copy(src[i], dst[i].at[:, j], loc.at[i]) for i in range(n)]
        for cp in local:
            cp.start()
        sends = [_rcopy(src[i].at[mine], dst[i].at[mine, j], isend.at[i, r], irecv.at[i, r], (kx, ky, c))
                 for i in range(n) for r, (kx, ky) in enumerate(chips)]
        for cp in sends:
            cp.start()
        passed = []
        for i in range(n):
            for r, (kx, ky) in enumerate(chips):
                part = dst[i].at[mine, 2 * kx + ky]
                _rcopy(part, part, isend.at[i, r], irecv.at[i, r], (kx, ky, c)).wait_recv()
                fw = _rcopy(part, part, dsend.at[i, r], drecv.at[i, r], (x, y, 1 - c))
                fw.start()
                passed.append(fw)
        for i in range(n):
            for r, (kx, ky) in enumerate(chips):
                part = dst[i].at[other, 2 * kx + ky]
                _rcopy(part, part, dsend.at[i, r], drecv.at[i, r], (x, y, 1 - c)).wait_recv()
        for cp in sends + passed:
            cp.wait_send()
        for cp in local:
            cp.wait()

    out_shape = [jax.ShapeDtypeStruct((L, N_CHIPS) + s.shape[1:], s.dtype) for s in shards]
    sems = [pltpu.SemaphoreType.DMA((n,))] + [pltpu.SemaphoreType.DMA((n, 3))] * 4
    return pl.pallas_call(body, name="gather_shards", in_specs=[ANY] * n, out_specs=[ANY] * n, out_shape=out_shape,
                          scratch_shapes=sems, compiler_params=pltpu.CompilerParams(has_side_effects=True))(*shards)


def swap_halves(grads):
    n = len(grads)
    half = grads[0].shape[0] // 2

    def body(*refs):
        src, dst = refs[:n], refs[n:2 * n]
        ssem, rsem = refs[2 * n:]
        x, y, c, _ = _place()
        other = pl.ds((1 - c) * half, half)
        cps = [_rcopy(src[i].at[other], dst[i], ssem.at[i], rsem.at[i], (x, y, 1 - c)) for i in range(n)]
        for cp in cps:
            cp.start()
        for cp in cps:
            cp.wait()

    out_shape = [jax.ShapeDtypeStruct((half,) + g.shape[1:], g.dtype) for g in grads]
    return pl.pallas_call(body, name="swap_halves", in_specs=[ANY] * n, out_specs=[ANY] * n, out_shape=out_shape,
                          scratch_shapes=[pltpu.SemaphoreType.DMA((n,))] * 2)(*grads)


def scatter_shards(parts):
    n = len(parts)
    half = parts[0].shape[0]

    def body(*refs):
        src, dst = refs[:n], refs[n:2 * n]
        ssem, rsem = refs[2 * n:]
        x, y, c, chips = _place()
        cps = [_rcopy(src[i].at[:, 2 * kx + ky], dst[i].at[r], ssem.at[i, r], rsem.at[i, r], (kx, ky, c))
               for i in range(n) for r, (kx, ky) in enumerate(chips)]
        for cp in cps:
            cp.start()
        for cp in cps:
            cp.wait()

    out_shape = [jax.ShapeDtypeStruct((3, half) + p.shape[2:], p.dtype) for p in parts]
    return pl.pallas_call(body, name="scatter_shards", in_specs=[ANY] * n, out_specs=[ANY] * n, out_shape=out_shape,
                          scratch_shapes=[pltpu.SemaphoreType.DMA((n, 3))] * 2)(*parts)


def join_halves(finals, L):
    n = len(finals)
    half = L // 2

    def body(*refs):
        src, dst = refs[:n], refs[n:2 * n]
        loc, ssem, rsem = refs[2 * n:]
        x, y, c, _ = _place()
        mine = pl.ds(c * half, half)
        local = [pltpu.make_async_copy(src[i], dst[i].at[mine], loc.at[i]) for i in range(n)]
        cps = [_rcopy(src[i], dst[i].at[mine], ssem.at[i], rsem.at[i], (x, y, 1 - c)) for i in range(n)]
        for cp in local + cps:
            cp.start()
        for cp in cps:
            cp.wait()
        for cp in local:
            cp.wait()

    out_shape = [jax.ShapeDtypeStruct((L,) + f.shape[1:], f.dtype) for f in finals]
    return pl.pallas_call(body, name="join_halves", in_specs=[ANY] * n, out_specs=[ANY] * n, out_shape=out_shape,
                          scratch_shapes=[pltpu.SemaphoreType.DMA((n,))] * 3)(*finals)


def gather_blocks(v):
    def body(v_ref, out_ref, send_sems, recv_sems, local_sem):
        x, y, c, chips = _place()
        me, sibling = (x, y, c), (x, y, 1 - c)

        def blk(px, py, pc):
            return out_ref.at[4 * px + 2 * py + pc]

        def copy(k, block, to, src=None):
            return _rcopy(blk(*block) if src is None else src, blk(*block), send_sems.at[k], recv_sems.at[k], to)

        own = pltpu.make_async_copy(v_ref, blk(*me), local_sem)
        own.start()
        first = [copy(0, me, sibling, src=v_ref)]
        first += [copy(1 + r, me, (*chip, c), src=v_ref) for r, chip in enumerate(chips)]
        for cp in first:
            cp.start()
        passed = [copy(4 + r, (*chip, c), sibling) for r, chip in enumerate(chips)]
        for r, chip in enumerate(chips):
            copy(1 + r, (*chip, c), me).wait_recv()
            passed[r].start()
        copy(0, sibling, me).wait_recv()
        for r, chip in enumerate(chips):
            copy(4 + r, (*chip, 1 - c), me).wait_recv()
        for cp in first + passed:
            cp.wait_send()
        own.wait()

    return pl.pallas_call(body, name="gather_blocks", in_specs=[ANY], out_specs=ANY,
                          out_shape=jax.ShapeDtypeStruct((8,) + v.shape, v.dtype),
                          scratch_shapes=[pltpu.SemaphoreType.DMA((7,)), pltpu.SemaphoreType.DMA((7,)),
                                          pltpu.SemaphoreType.DMA])(v)


STREAM_BLOCK = 256 * 1024


def _stream_rows(rows, cols):
    return _tile(rows, max(SUBLANE, STREAM_BLOCK // cols // SUBLANE * SUBLANE), SUBLANE)


def _view3(a, lead):
    shape = a.shape[:lead] + (-1, a.shape[-1])
    return a.reshape(shape)


def sum_parts(name, terms, n, c):
    rows, cols = terms[0][0].shape[-2:]
    tr = _stream_rows(rows, cols)

    def spec(arr, slot, by_core):
        if slot is not None:
            return pl.BlockSpec((None, None, tr, cols), lambda l, r, c_ref: (slot, l, r, 0))
        if by_core:
            return pl.BlockSpec((None, tr, cols), lambda l, r, c_ref: (c_ref[0] * n + l, r, 0))
        return pl.BlockSpec((None, tr, cols), lambda l, r, c_ref: (l, r, 0))

    def body(c_ref, *refs):
        acc = refs[0][...]
        for t in refs[1:-1]:
            acc = acc + t[...]
        refs[-1][...] = acc

    grid_spec = pltpu.PrefetchScalarGridSpec(
        num_scalar_prefetch=1, grid=(n, rows // tr), in_specs=[spec(*t) for t in terms],
        out_specs=pl.BlockSpec((None, tr, cols), lambda l, r, c_ref: (l, r, 0)))
    return pl.pallas_call(body, name=name, grid_spec=grid_spec, out_shape=jax.ShapeDtypeStruct((n, rows, cols), F32),
                          compiler_params=_params(("arbitrary", "arbitrary")))(
                              jnp.reshape(c, (1,)).astype(jnp.int32), *[t[0] for t in terms])


def adamw(name, w, g, m, v):
    n, rows, cols = w.shape
    tr = _stream_rows(rows, cols)

    def body(w_ref, g_ref, m_ref, v_ref, d_ref, nm_ref, nv_ref):
        gv = g_ref[...]
        nm = ADAM_B1 * m_ref[...] + (1.0 - ADAM_B1) * gv
        nv = ADAM_B2 * v_ref[...] + (1.0 - ADAM_B2) * (gv * gv)
        m_hat = nm / (1.0 - ADAM_B1 ** ADAM_STEP)
        v_hat = nv / (1.0 - ADAM_B2 ** ADAM_STEP)
        d_ref[...] = -ADAM_LR * (m_hat / (jnp.sqrt(v_hat) + ADAM_EPS) + ADAM_WD * w_ref[...])
        nm_ref[...] = nm
        nv_ref[...] = nv

    blk = pl.BlockSpec((None, tr, cols), lambda l, r: (l, r, 0))
    shp = jax.ShapeDtypeStruct(w.shape, F32)
    return pl.pallas_call(body, name=name, grid=(n, rows // tr), in_specs=[blk] * 4, out_specs=[blk] * 3,
                          out_shape=[shp] * 3, compiler_params=_params(("arbitrary", "arbitrary")))(w, g, m, v)


COL_SHARDED = ("w_ada", "w_in", "w_proj_gla", "w_proj_s5", "w_proj_attn", "w_ffn_in", "w_gla_gate", "b_gla_gate")
ROW_SHARDED = ("w_s5_glu", "w_out", "w_ffn_out")
SHARDED = COL_SHARDED + ROW_SHARDED
WEIGHTS = ("c_ctx", "w_ada", "b_ada", "w_in", "w_gla_gate", "b_gla_gate", "gla_norm_w", "s5_lam_re", "s5_lam_im",
           "s5_log_dt", "s5_b_re", "s5_b_im", "s5_c_re", "s5_c_im", "s5_d", "w_s5_glu", "q_norm_w", "k_norm_w",
           "w_proj_gla", "w_proj_s5", "w_proj_attn", "w_out", "ln1_w", "ln1_b", "ln2_w", "ln2_b", "w_ffn_in",
           "w_ffn_out")
REPLICATED = tuple(n for n in WEIGHTS if n not in SHARDED)


def _full_weight(name, g):
    if name in ROW_SHARDED:
        return g.reshape((g.shape[0], -1) + g.shape[3:])
    nd = g.ndim
    perm = (0,) + tuple(range(2, nd - 1)) + (1, nd - 1)
    t = g.transpose(perm)
    return t.reshape(t.shape[:-2] + (-1,))


def _pack(arrays):
    flat = jnp.concatenate([a.reshape(-1) for a in arrays])
    pad = (-flat.shape[0]) % (PACK_ROWS * LANE)
    return jnp.pad(flat, (0, pad)).reshape(-1, LANE)


def _unpack(packed, like):
    flat, out, off = packed.reshape(-1), [], 0
    for a in like:
        out.append(flat[off:off + a.size].reshape(a.shape))
        off += a.size
    return out


def kernel(x, c, ctx, c_ctx, w_ada, b_ada, w_in, w_gla_gate, b_gla_gate, gla_norm_w, s5_lam_re, s5_lam_im, s5_log_dt, s5_b_re, s5_b_im, s5_c_re, s5_c_im, s5_d, w_s5_glu, q_norm_w, k_norm_w, w_proj_gla, w_proj_s5, w_proj_attn, w_out, ln1_w, ln1_b, ln2_w, ln2_b, w_ffn_in, w_ffn_out, loss_target, m_c_ctx, m_w_ada, m_b_ada, m_w_in, m_w_gla_gate, m_b_gla_gate, m_gla_norm_w, m_s5_lam_re, m_s5_lam_im, m_s5_log_dt, m_s5_b_re, m_s5_b_im, m_s5_c_re, m_s5_c_im, m_s5_d, m_w_s5_glu, m_q_norm_w, m_k_norm_w, m_w_proj_gla, m_w_proj_s5, m_w_proj_attn, m_w_out, m_ln1_w, m_ln1_b, m_ln2_w, m_ln2_b, m_w_ffn_in, m_w_ffn_out, v_c_ctx, v_w_ada, v_b_ada, v_w_in, v_w_gla_gate, v_b_gla_gate, v_gla_norm_w, v_s5_lam_re, v_s5_lam_im, v_s5_log_dt, v_s5_b_re, v_s5_b_im, v_s5_c_re, v_s5_c_im, v_s5_d, v_w_s5_glu, v_q_norm_w, v_k_norm_w, v_w_proj_gla, v_w_proj_s5, v_w_proj_attn, v_w_out, v_ln1_w, v_ln1_b, v_ln2_w, v_ln2_b, v_w_ffn_in, v_w_ffn_out):
    args = dict(locals())
    w = {n: args[n] for n in WEIGHTS}
    m = {n: args["m_" + n] for n in WEIGHTS}
    v = {n: args["v_" + n] for n in WEIGHTS}
    core = lax.axis_index("c")
    L = w_in.shape[0]
    half = L // 2

    gathered = gather_shards([w[n].astype(BF16) for n in SHARDED])
    gathered = {n: g.astype(F32) for n, g in zip(SHARDED, gathered)}
    replicated = {n: w[n] for n in REPLICATED}

    def loss_fn(x1, gathered, replicated):
        full = {n: _full_weight(n, g) for n, g in gathered.items()}
        return local_loss(x1, c[0], ctx[0], loss_target[0], {**full, **replicated})

    loss, (gx, g_sh, g_rep) = jax.value_and_grad(loss_fn, argnums=(0, 1, 2))(x[0], gathered, replicated)
    loss = lax.psum(loss, ("x", "y", "c"))

    parts = [g_sh[n] for n in SHARDED]
    theirs = swap_halves(parts)
    chip_sums = [sum_parts("sum_cores", [(_view3(p, 1), None, True), (_view3(t, 1), None, False)], half, core)
                 .reshape((half,) + p.shape[1:]) for p, t in zip(parts, theirs)]
    recv = scatter_shards(chip_sums)
    place = 2 * lax.axis_index("x") + lax.axis_index("y")
    finals = []
    for n, s, r in zip(SHARDED, chip_sums, recv):
        shard_shape = s.shape[2:]
        own = lax.dynamic_index_in_dim(s, place, axis=1, keepdims=False)
        r3 = _view3(r, 2)
        terms = [(_view3(own, 1), None, False)] + [(r3, k, False) for k in range(3)]
        finals.append(sum_parts("sum_chips", terms, half, core).reshape((half,) + shard_shape))
    grads = dict(zip(SHARDED, join_halves(finals, L)))

    rep_parts = [g_rep[n] for n in REPLICATED]
    blocks = gather_blocks(_pack(rep_parts))
    total = sum_parts("sum_devices", [(blocks[:, None], k, False) for k in range(8)], 1, core)[0]
    grads.update(dict(zip(REPLICATED, _unpack(total, rep_parts))))

    delta, new_m, new_v = {}, {}, {}
    for n in SHARDED:
        d3, m3, v3 = adamw("adamw", _view3(w[n], 1), _view3(grads[n], 1), _view3(m[n], 1), _view3(v[n], 1))
        delta[n], new_m[n], new_v[n] = (t.reshape(w[n].shape) for t in (d3, m3, v3))
    packed = [_pack([d[n] for n in REPLICATED])[None] for d in (w, m, v)]
    d3, m3, v3 = adamw("adamw_replicated", packed[0], total[None], packed[1], packed[2])
    like = [w[n] for n in REPLICATED]
    for dst, src in ((delta, d3), (new_m, m3), (new_v, v3)):
        dst.update(dict(zip(REPLICATED, _unpack(src[0], like))))

    return (loss, gx[None], *[grads[n] for n in WEIGHTS], *[delta[n] for n in WEIGHTS],
            *[new_m[n] for n in WEIGHTS], *[new_v[n] for n in WEIGHTS])
```

```python
import functools

import jax
import jax.numpy as jnp
from jax import lax
from jax.experimental import pallas as pl
from jax.experimental.pallas import tpu as pltpu

F32 = jnp.float32
BF16 = jnp.bfloat16

GRID_W = 64
GLA_HEADS = 4
GLA_DK = 128
GLA_DV = 256
GLA_GATE_RANK = 16
GLA_GATE_TAU = 16.0
GLA_CHUNK = 64
S5_GROUP = 16
S5_STATE = 64
ATTN_Q_HEADS = 8
ATTN_KV_HEADS = 2
ATTN_HEAD_DIM = 128
ROPE_THETA = 10000.0
DEPTH = 4
DN_ALPHA = (2 * DEPTH) ** 0.25
EPS = 1e-6
ADAM_LR = 0.001
ADAM_B1 = 0.9
ADAM_B2 = 0.999
ADAM_EPS = 1e-08
ADAM_WD = 0.01
ADAM_STEP = 10

LANE = 128
SUBLANE = 8
VMEM_LIMIT = 56 * 1024 * 1024
ADA_ROWS = 16
S5_CHUNK = 128
S5_BLOCK_GROUPS = 8
ROW_TILE = 256
COL_TILE = 512
PACK_ROWS = 512
N_CHIPS = 4


def _params(sem, **kw):
    return pltpu.CompilerParams(dimension_semantics=sem, vmem_limit_bytes=VMEM_LIMIT, **kw)


def _tile(n, target, base):
    if n <= target:
        return n
    best = None
    for t in range(base, target + 1, base):
        if n % t == 0:
            best = t
    assert best is not None, (n, target, base)
    return best


_DIMS = {"nn": ((1,), (0,)), "nt": ((1,), (1,)), "tn": ((0,), (0,))}


def _dg(a, b, mode):
    return lax.dot_general(a.astype(BF16), b.astype(BF16), (_DIMS[mode], ((), ())), preferred_element_type=F32)


@functools.partial(jax.custom_vjp, nondiff_argnums=(2,))
def bdot(a, b, mode):
    return _dg(a, b, mode)


def _bdot_fwd(a, b, mode):
    return _dg(a, b, mode), (a, b)


def _bdot_bwd(mode, res, g):
    a, b = res
    if mode == "nn":
        return bdot(g, b, "nt").astype(a.dtype), bdot(a, g, "tn").astype(b.dtype)
    if mode == "nt":
        return bdot(g, b, "nn").astype(a.dtype), bdot(g, a, "tn").astype(b.dtype)
    return bdot(b, g, "nt").astype(a.dtype), bdot(a, g, "nn").astype(b.dtype)


bdot.defvjp(_bdot_fwd, _bdot_bwd)


def _mm_tiles(M, Kb, Nb):
    return _tile(M, 1024, SUBLANE), _tile(Kb, 1024, LANE), _tile(Nb, 1024, LANE)


def _mm_body(mode, last):
    def body(p_ref, q_ref, o_ref, acc):
        k = pl.program_id(3)

        @pl.when(k == 0)
        def _():
            acc[...] = jnp.zeros_like(acc)

        acc[...] += _dg(p_ref[...], q_ref[...], mode)

        @pl.when(k == last)
        def _():
            o_ref[...] = acc[...]

    return body


def _mm_nn(name, a, w):
    M = a.shape[0]
    B, Kb, Nb = w.shape
    tm, tk, tn = _mm_tiles(M, Kb, Nb)
    nk, nn = Kb // tk, Nb // tn
    return pl.pallas_call(
        _mm_body("nn", nk - 1), name=name, grid=(B, M // tm, nn, nk),
        in_specs=[pl.BlockSpec((tm, tk), lambda b, i, j, k: (i, b * nk + k)),
                  pl.BlockSpec((None, tk, tn), lambda b, i, j, k: (b, k, j))],
        out_specs=pl.BlockSpec((tm, tn), lambda b, i, j, k: (i, b * nn + j)),
        out_shape=jax.ShapeDtypeStruct((M, B * Nb), F32),
        scratch_shapes=[pltpu.VMEM((tm, tn), F32)],
        compiler_params=_params(("arbitrary",) * 4))(a, w)


def _mm_nt(name, g, w):
    M = g.shape[0]
    B, Kb, Nb = w.shape
    tm, tk, tn = _mm_tiles(M, Kb, Nb)
    nk, nn = Kb // tk, Nb // tn
    return pl.pallas_call(
        _mm_body("nt", nn - 1), name=name, grid=(B, M // tm, nk, nn),
        in_specs=[pl.BlockSpec((tm, tn), lambda b, i, k, n: (i, b * nn + n)),
                  pl.BlockSpec((None, tk, tn), lambda b, i, k, n: (b, k, n))],
        out_specs=pl.BlockSpec((tm, tk), lambda b, i, k, n: (i, b * nk + k)),
        out_shape=jax.ShapeDtypeStruct((M, B * Kb), F32),
        scratch_shapes=[pltpu.VMEM((tm, tk), F32)],
        compiler_params=_params(("arbitrary",) * 4))(g, w)


def _mm_tn(name, a, g, B):
    M = a.shape[0]
    Kb, Nb = a.shape[1] // B, g.shape[1] // B
    tm, tk, tn = _mm_tiles(M, Kb, Nb)
    nk, nn = Kb // tk, Nb // tn
    return pl.pallas_call(
        _mm_body("tn", M // tm - 1), name=name, grid=(B, nk, nn, M // tm),
        in_specs=[pl.BlockSpec((tm, tk), lambda b, k, j, m: (m, b * nk + k)),
                  pl.BlockSpec((tm, tn), lambda b, k, j, m: (m, b * nn + j))],
        out_specs=pl.BlockSpec((None, tk, tn), lambda b, k, j, m: (b, k, j)),
        out_shape=jax.ShapeDtypeStruct((B, Kb, Nb), F32),
        scratch_shapes=[pltpu.VMEM((tk, tn), F32)],
        compiler_params=_params(("arbitrary",) * 4))(a, g)


def mm(name, a, w):
    w3 = w if w.ndim == 3 else w[None]

    @jax.custom_vjp
    def op(a, w3):
        return _mm_nn(name + "_fwd", a, w3.astype(BF16))

    def fwd(a, w3):
        wb = w3.astype(BF16)
        return _mm_nn(name + "_fwd", a, wb), (a, wb)

    def bwd(res, g):
        a, wb = res
        return _mm_nt(name + "_dx", g, wb), _mm_tn(name + "_dw", a, g, wb.shape[0])

    op.defvjp(fwd, bwd)
    return op(a, w3)


def _spec_shape(spec, G, T):
    k = spec[0]
    if k == "row":
        return (T, spec[1])
    if k == "rowg":
        return (T, (G // spec[2]) * spec[1])
    if k == "bc":
        return (spec[1], spec[2])
    return (spec[1], (G // spec[3]) * spec[2])


def _spec_block(spec, tm, rmap):
    k = spec[0]
    if k == "row":
        return pl.BlockSpec((tm, spec[1]), lambda g, r: (rmap(r), 0))
    if k == "rowg":
        d = spec[2]
        return pl.BlockSpec((tm, spec[1]), lambda g, r: (rmap(r), g // d))
    if k == "bc":
        return pl.BlockSpec((spec[1], spec[2]), lambda g, r: (0, 0))
    d = spec[3]
    return pl.BlockSpec((spec[1], spec[2]), lambda g, r: (0, g // d))


def block_op(name, f, in_specs, out_specs, G, T, tm, diff, carry=(), row0=False, order=None):
    n_in, n_out, n_c = len(in_specs), len(out_specs), len(carry)
    n_steps = T // tm
    assert T % tm == 0
    order = order or (lambda s: s)
    diff_idx = [i for i in range(n_in) if diff[i]]
    for i in diff_idx:
        assert in_specs[i][0] != "row" or G == 1
        assert in_specs[i][0] != "rowg" or in_specs[i][2] == 1
    out_shapes = [jax.ShapeDtypeStruct(_spec_shape(s, G, T), F32) for s in out_specs]
    save_shapes = [jax.ShapeDtypeStruct((n_steps, a, G * b), F32) for a, b in carry]
    sem = ("arbitrary", "arbitrary")

    def call_f(r_idx, cvals, vals):
        args = list(vals)
        if n_c:
            args = [tuple(cvals)] + args
        if row0:
            args = [r_idx * tm] + args
        return f(*args)

    def fwd_body(*refs):
        in_refs = refs[:n_in]
        out_refs = refs[n_in:n_in + n_out]
        save_refs = refs[n_in + n_out:n_in + n_out + n_c]
        c_refs = refs[n_in + n_out + n_c:]
        r = pl.program_id(1)
        if n_c:
            @pl.when(r == 0)
            def _():
                for c in c_refs:
                    c[...] = jnp.zeros_like(c)

            cvals = [c[...] for c in c_refs]
            for s, v in zip(save_refs, cvals):
                s[...] = v
            new_c, outs = call_f(r, cvals, [x[...] for x in in_refs])
            for c, v in zip(c_refs, new_c):
                c[...] = v
        else:
            outs = call_f(r, (), [x[...] for x in in_refs])
        for o, v in zip(out_refs, outs):
            o[...] = v.astype(F32)

    def fwd_call(*arrays):
        res = pl.pallas_call(
            fwd_body, name=name + "_fwd", grid=(G, n_steps),
            in_specs=[_spec_block(s, tm, order) for s in in_specs],
            out_specs=[_spec_block(s, tm, order) for s in out_specs]
            + [pl.BlockSpec((None, a, b), lambda g, r: (r, 0, g)) for a, b in carry],
            out_shape=out_shapes + save_shapes,
            scratch_shapes=[pltpu.VMEM((a, b), F32) for a, b in carry],
            compiler_params=_params(sem))(*arrays)
        return tuple(res)

    def bwd_body(*refs):
        in_refs = refs[:n_in]
        save_refs = refs[n_in:n_in + n_c]
        ct_refs = refs[n_in + n_c:n_in + n_c + n_out]
        g_refs = refs[n_in + n_c + n_out:n_in + n_c + n_out + len(diff_idx)]
        dc_refs = refs[n_in + n_c + n_out + len(diff_idx):]
        g = pl.program_id(0)
        r = pl.program_id(1)
        vals = [x[...] for x in in_refs]
        if n_c:
            @pl.when(r == 0)
            def _():
                for d in dc_refs:
                    d[...] = jnp.zeros_like(d)

        def fun(cvals, dvals):
            full = list(vals)
            for i, v in zip(diff_idx, dvals):
                full[i] = v
            return call_f(n_steps - 1 - r if n_c else r, cvals, full)

        _, vjp = jax.vjp(fun, tuple(s[...] for s in save_refs), tuple(vals[i] for i in diff_idx))
        cts = tuple(c[...] for c in ct_refs)
        if n_c:
            cts = (tuple(d[...] for d in dc_refs), cts)
        dcin, dvals = vjp(cts)
        for d, v in zip(dc_refs, dcin):
            d[...] = v
        for gref, i, v in zip(g_refs, diff_idx, dvals):
            spec = in_specs[i]
            if spec[0] in ("row", "rowg"):
                gref[...] = v
            else:
                first = (r == 0) & ((g == 0) if spec[0] == "bc" else (g % spec[3] == 0))

                @pl.when(first)
                def _(gref=gref, v=v):
                    gref[...] = v

                @pl.when(jnp.logical_not(first))
                def _(gref=gref, v=v):
                    gref[...] += v

    def bwd_call(arrays, saved, cts):
        rmap = (lambda r: order(n_steps - 1 - r)) if n_c else (lambda r: r)
        res = pl.pallas_call(
            bwd_body, name=name + "_bwd", grid=(G, n_steps),
            in_specs=[_spec_block(s, tm, rmap) for s in in_specs]
            + [pl.BlockSpec((None, a, b), lambda g, r: (n_steps - 1 - r, 0, g)) for a, b in carry]
            + [_spec_block(s, tm, rmap) for s in out_specs],
            out_specs=[_spec_block(in_specs[i], tm, rmap) for i in diff_idx],
            out_shape=[jax.ShapeDtypeStruct(_spec_shape(in_specs[i], G, T), F32) for i in diff_idx],
            scratch_shapes=[pltpu.VMEM((a, b), F32) for a, b in carry],
            compiler_params=_params(sem))(*arrays, *saved, *cts)
        return tuple(res)

    @jax.custom_vjp
    def op(*arrays):
        return fwd_call(*arrays)[:n_out]

    def op_fwd(*arrays):
        res = fwd_call(*arrays)
        return res[:n_out], (arrays, res[n_out:])

    def op_bwd(res, cts):
        arrays, saved = res
        grads = bwd_call(arrays, saved, cts)
        out = [jnp.zeros_like(a) for a in arrays]
        for i, gval in zip(diff_idx, grads):
            out[i] = gval
        return tuple(out)

    op.defvjp(op_fwd, op_bwd)
    return op


def _rows(n, m):
    return lax.broadcasted_iota(jnp.int32, (n, m), 0)


def _ctx_select(row0, tm, n_ctx, v_lat, v_ctx):
    if n_ctx == 0:
        return v_lat
    is_ctx = (row0 + _rows(tm, 1)) < n_ctx
    return jnp.where(is_ctx, v_ctx, v_lat)


def _silu(x):
    return x * jax.nn.sigmoid(x)


def _f_modulate(tm, n_ctx):
    def f(row0, x, sh_l, sh_c, sc_l, sc_c):
        sh = _ctx_select(row0, tm, n_ctx, sh_l, sh_c)
        sc = _ctx_select(row0, tm, n_ctx, sc_l, sc_c)
        return (x * (1 + sc) + sh,)
    return f


def _f_postnorm(tm, n_ctx):
    def f(row0, x, y, g_l, g_c, w, b):
        z = DN_ALPHA * x + _ctx_select(row0, tm, n_ctx, g_l, g_c) * y
        mu = jnp.mean(z, -1, keepdims=True)
        zc = z - mu
        var = jnp.mean(zc * zc, -1, keepdims=True)
        return (zc * lax.rsqrt(var + EPS) * w + b,)
    return f


def _log_sigmoid(x):
    return -(jnp.maximum(-x, 0.0) + jnp.log1p(jnp.exp(-jnp.abs(x))))


def _f_gla_prep(glr, wg0, wg1, b0, b1):
    return (_log_sigmoid(bdot(glr, wg0, "nn") + b0) / GLA_GATE_TAU,
            _log_sigmoid(bdot(glr, wg1, "nn") + b1) / GLA_GATE_TAU)


def _f_gla_step(rev):
    def f(carry, q, k, v, la):
        (st,) = carry
        n = q.shape[0]
        cols = lax.broadcasted_iota(jnp.int32, (n, n), 1)
        tri = (_rows(n, n) <= cols) if rev else (_rows(n, n) >= cols)
        b = jnp.dot(tri.astype(F32), la, precision=lax.Precision.HIGHEST)
        qe = q * (GLA_DK ** -0.5) * jnp.exp(b)
        ke = k * jnp.exp(-b)
        att = jnp.where(tri, bdot(qe, ke, "nt"), 0.0)
        o = bdot(att, v, "nn") + bdot(qe, st, "nt")
        end = 0 if rev else n - 1
        b_last = jnp.sum(jnp.where(_rows(n, 1) == end, b, 0.0), axis=0, keepdims=True)
        kd = k * jnp.exp(b_last - b)
        st = st * jnp.exp(b_last) + bdot(v, kd, "tn")
        return (st,), (o,)
    return f


def _f_gla_norm(o0, o1, gr, w):
    o = o0 + o1
    mu = jnp.mean(o, -1, keepdims=True)
    oc = o - mu
    var = jnp.mean(oc * oc, -1, keepdims=True)
    return (oc * lax.rsqrt(var + EPS) * w * _silu(gr),)


@functools.partial(jax.custom_vjp, nondiff_argnums=(1, 2))
def _shift_rows(x, d, up):
    n = x.shape[0]
    rows = _rows(n, 1)
    if up:
        return jnp.where(rows < n - d, pltpu.roll(x, n - d, 0), 0.0)
    return jnp.where(rows >= d, pltpu.roll(x, d, 0), 0.0)


def _shift_fwd(x, d, up):
    return _shift_rows(x, d, up), None


def _shift_bwd(d, up, _, g):
    return (_shift_rows(g, d, not up),)


_shift_rows.defvjp(_shift_fwd, _shift_bwd)


def _f_s5_step(rev):
    def f(carry, bur, bui, lam_re, lam_im, log_dt):
        cr, ci = carry
        n = bur.shape[0]
        dt = jnp.exp(log_dt)
        mag = jnp.exp(lam_re * dt)
        ar, ai = mag * jnp.cos(lam_im * dt), mag * jnp.sin(lam_im * dt)
        den = lam_re * lam_re + lam_im * lam_im
        nr, ni = ar - 1, ai
        kr = (nr * lam_re + ni * lam_im) / den
        ki = (ni * lam_re - nr * lam_im) / den
        first = _rows(n, 1) == (n - 1 if rev else 0)
        xr = kr * bur - ki * bui + jnp.where(first, ar * cr - ai * ci, 0.0)
        xi = kr * bui + ki * bur + jnp.where(first, ar * ci + ai * cr, 0.0)
        pr, pi = ar, ai
        d = 1
        while d < n:
            sr, si = _shift_rows(xr, d, rev), _shift_rows(xi, d, rev)
            xr, xi = xr + pr * sr - pi * si, xi + pr * si + pi * sr
            pr, pi = pr * pr - pi * pi, 2 * pr * pi
            d *= 2
        last = _rows(n, 1) == (0 if rev else n - 1)
        cr = jnp.sum(jnp.where(last, xr, 0.0), axis=0, keepdims=True)
        ci = jnp.sum(jnp.where(last, xi, 0.0), axis=0, keepdims=True)
        return (cr, ci), (xr, xi)
    return f


def _f_s5_post(su, dskip, y0r, y0i, y1r, y1i):
    return (jax.nn.gelu(su * dskip + y0r - y0i + y1r - y1i),)


def _f_s5_glu(y, t):
    return (y * jax.nn.sigmoid(t),)


def _swap_pairs(x):
    lane = lax.broadcasted_iota(jnp.int32, x.shape, 1)
    return jnp.where(lane % 2 == 0, pltpu.roll(x, x.shape[1] - 1, 1), pltpu.roll(x, 1, 1))


@jax.custom_vjp
def _rope(x, cos2, sin2):
    return x * cos2 + _swap_pairs(x) * sin2


def _rope_fwd(x, cos2, sin2):
    return _rope(x, cos2, sin2), (cos2, sin2)


def _rope_bwd(res, g):
    cos2, sin2 = res
    return g * cos2 + _swap_pairs(g * sin2), jnp.zeros_like(cos2), jnp.zeros_like(sin2)


_rope.defvjp(_rope_fwd, _rope_bwd)


def _f_qk_norm_rope(x, cos2, sin2, w):
    xn = x * lax.rsqrt(jnp.mean(x * x, -1, keepdims=True) + EPS) * w
    return (_rope(xn, cos2, sin2),)


def _f_attn(q, k, v):
    s = bdot(q, k, "nt") * (ATTN_HEAD_DIM ** -0.5)
    e = jnp.exp(s - jnp.max(s, -1, keepdims=True))
    p = e / jnp.sum(e, -1, keepdims=True)
    return (bdot(p, v, "nn"),)


def _f_merge(ga, gb, gc, pa, pb, pc):
    return (jax.nn.sigmoid(ga) * pa + jax.nn.sigmoid(gb) * pb + jax.nn.sigmoid(gc) * pc,)


def _f_swiglu(a, b):
    return (_silu(a) * b,)


def _f_silu(x):
    return (_silu(x),)


def _f_add_bias(x, b):
    return (x + b,)


def _scan_order(rev, n_ctx, T, tm):
    if not rev:
        return None
    nc, n = n_ctx // tm, T // tm
    return lambda s: jnp.where(s < nc, nc - 1 - s, n - 1 - (s - nc))


def modulate(name, x, sh, sc, n_ctx):
    T, D = x.shape
    tm = _tile(T, ROW_TILE, SUBLANE)
    cw = _tile(D, COL_TILE, LANE)
    col, vec = ("rowg", cw, 1), ("bcg", 1, cw, 1)
    op = block_op(name, _f_modulate(tm, n_ctx), [col, vec, vec, vec, vec], [col], D // cw, T, tm,
                  [True] * 5, row0=True)
    return op(x, sh[0], sh[1], sc[0], sc[1])[0]


def postnorm(name, x, y, g, w, b, n_ctx):
    T, D = x.shape
    tm = _tile(T, ROW_TILE, SUBLANE)
    vec = ("bc", 1, D)
    op = block_op(name, _f_postnorm(tm, n_ctx), [("row", D), ("row", D), vec, vec, vec, vec], [("row", D)], 1, T,
                  tm, [True] * 6, row0=True)
    return op(x, y, g[0], g[1], w, b)[0]


def rowwise(name, f, arrays, n_out=1):
    T, w = arrays[0].shape
    tm = _tile(T, ROW_TILE, SUBLANE)
    cw = _tile(w, COL_TILE, LANE)
    col = ("rowg", cw, 1)
    op = block_op(name, f, [col] * len(arrays), [col] * n_out, w // cw, T, tm, [True] * len(arrays))
    return op(*arrays)


def gla_prep(name, glr, wg, bg):
    T = glr.shape[0]
    qk = wg.shape[-1]
    tm = _tile(T, ROW_TILE, SUBLANE)
    op = block_op(name, _f_gla_prep, [("row", LANE), ("bc", LANE, qk), ("bc", LANE, qk), ("bc", 1, qk), ("bc", 1, qk)],
                  [("row", qk), ("row", qk)], 1, T, tm, [True] * 5)
    return op(glr, wg[0], wg[1], bg[0], bg[1])


def gla_scan(name, q, k, v, la, rev, n_ctx):
    T = q.shape[0]
    op = block_op(name, _f_gla_step(rev), [("rowg", GLA_DK, 1), ("rowg", GLA_DK, 1), ("rowg", GLA_DV, 1), ("rowg", GLA_DK, 1)],
                  [("rowg", GLA_DV, 1)], GLA_HEADS, T, GLA_CHUNK, [True] * 4, carry=[(GLA_DV, GLA_DK)],
                  order=_scan_order(rev, n_ctx, T, GLA_CHUNK))
    return op(q, k, v, la)[0]


def gla_norm(name, o0, o1, gr, w):
    T = o0.shape[0]
    tm = _tile(T, ROW_TILE, SUBLANE)
    hd = ("rowg", GLA_DV, 1)
    op = block_op(name, _f_gla_norm, [hd, hd, hd, ("bcg", 1, GLA_DV, 1)], [hd], GLA_HEADS, T, tm, [True] * 4)
    return op(o0, o1, gr, w)[0]


def s5_scan(name, bur, bui, lam_re, lam_im, log_dt, rev, n_ctx):
    T, S = bur.shape
    cols = _tile(S, 768, LANE)
    G = S // cols
    col, par = ("rowg", cols, 1), ("bcg", 1, cols, 1)
    op = block_op(name, _f_s5_step(rev), [col, col, par, par, par], [col, col], G, T, S5_CHUNK, [True] * 5,
                  carry=[(1, cols), (1, cols)], order=_scan_order(rev, n_ctx, T, S5_CHUNK))
    return op(bur, bui, lam_re, lam_im, log_dt)


def qk_norm_rope(name, x, cos2, sin2, w):
    T = x.shape[0]
    G = x.shape[1] // ATTN_HEAD_DIM
    tm = _tile(T, ROW_TILE, SUBLANE)
    hd = ("rowg", ATTN_HEAD_DIM, 1)
    op = block_op(name, _f_qk_norm_rope, [hd, ("row", ATTN_HEAD_DIM), ("row", ATTN_HEAD_DIM), ("bc", 1, ATTN_HEAD_DIM)],
                  [hd], G, T, tm, [True, False, False, True])
    return op(x, cos2, sin2, w)[0]


def attention(name, q, k, v):
    T, Tk = q.shape[0], k.shape[0]
    tm = _tile(T, ROW_TILE, SUBLANE)
    grp = ATTN_Q_HEADS // ATTN_KV_HEADS
    kv = ("bcg", Tk, ATTN_HEAD_DIM, grp)
    op = block_op(name, _f_attn, [("rowg", ATTN_HEAD_DIM, 1), kv, kv], [("rowg", ATTN_HEAD_DIM, 1)], ATTN_Q_HEADS, T,
                  tm, [True] * 3)
    return op(q, k, v)[0]


def sq_loss(name, y, t):
    T, D = y.shape
    tm = _tile(T, ROW_TILE, SUBLANE)

    def fwd_body(y_ref, t_ref, o_ref):
        e = y_ref[...] - t_ref[...]
        part = jnp.sum(jnp.sum(e * e, -1, keepdims=True), 0, keepdims=True) * (0.5 / D)

        @pl.when(pl.program_id(0) == 0)
        def _():
            o_ref[...] = jnp.zeros_like(o_ref)

        o_ref[...] += part * jnp.ones((1, LANE), F32)

    def bwd_body(y_ref, t_ref, g_ref, o_ref):
        o_ref[...] = (y_ref[...] - t_ref[...]) * (g_ref[:, 0:1] / D)

    row = pl.BlockSpec((tm, D), lambda r: (r, 0))
    one = pl.BlockSpec((1, LANE), lambda r: (0, 0))

    def fwd_call(y, t):
        return pl.pallas_call(fwd_body, name=name + "_fwd", grid=(T // tm,), in_specs=[row, row], out_specs=one,
                              out_shape=jax.ShapeDtypeStruct((1, LANE), F32), compiler_params=_params(("arbitrary",)))(y, t)

    @jax.custom_vjp
    def op(y, t):
        return fwd_call(y, t)[0, 0]

    def op_fwd(y, t):
        return fwd_call(y, t)[0, 0], (y, t)

    def op_bwd(res, g):
        y, t = res
        gy = pl.pallas_call(bwd_body, name=name + "_bwd", grid=(T // tm,), in_specs=[row, row, one], out_specs=row,
                            out_shape=jax.ShapeDtypeStruct((T, D), F32), compiler_params=_params(("arbitrary",)))(
                                y, t, jnp.full((1, LANE), g, F32))
        return gy, jnp.zeros_like(t)

    op.defvjp(op_fwd, op_bwd)
    return op(y, t)


def _rope_tables(n_ctx, n_lat):
    n_rows = n_lat // GRID_W
    rows = jnp.repeat(jnp.arange(n_rows), GRID_W).astype(F32)
    cols = jnp.tile(jnp.arange(GRID_W), n_rows).astype(F32)
    n_freq = ATTN_HEAD_DIM // 4
    inv = ROPE_THETA ** (-jnp.arange(n_freq, dtype=F32) / n_freq)
    ang = jnp.concatenate([rows[:, None] * inv, cols[:, None] * inv], -1)
    cos2 = jnp.repeat(jnp.cos(ang), 2, axis=-1)
    sin2 = jnp.stack([-jnp.sin(ang), jnp.sin(ang)], -1).reshape(n_lat, ATTN_HEAD_DIM)
    cos2 = jnp.concatenate([jnp.ones((n_ctx, ATTN_HEAD_DIM), F32), cos2], 0)
    sin2 = jnp.concatenate([jnp.zeros((n_ctx, ATTN_HEAD_DIM), F32), sin2], 0)
    return cos2, sin2


def _in_layout(D, s5_width):
    qk, gv = GLA_HEADS * GLA_DK, GLA_HEADS * GLA_DV
    aq, akv = ATTN_Q_HEADS * ATTN_HEAD_DIM, ATTN_KV_HEADS * ATTN_HEAD_DIM
    widths = [("gq", qk), ("gk", qk), ("gv", gv), ("gr", gv), ("glr", LANE), ("su", s5_width), ("aq", aq),
              ("ak", akv), ("av", akv), ("ga", D), ("gb", D), ("gc", D)]
    off, out = 0, {}
    for n, w in widths:
        out[n] = (off, w)
        off += w
    return out, off, 2 * qk + 2 * gv


def _s5_in_blocks(b):
    G, P, C = b.shape
    nb, bg = G // S5_BLOCK_GROUPS, S5_BLOCK_GROUPS
    t = b.reshape(nb, bg, P, C).transpose(0, 1, 3, 2)
    return jnp.einsum("bgcp,gh->bgchp", t, jnp.eye(bg, dtype=F32)).reshape(nb, bg * C, bg * P)


def _s5_out_blocks(c):
    G, C, P = c.shape
    nb, bg = G // S5_BLOCK_GROUPS, S5_BLOCK_GROUPS
    t = c.reshape(nb, bg, C, P).transpose(0, 1, 3, 2)
    return jnp.einsum("bgpc,gh->bgphc", t, jnp.eye(bg, dtype=F32)).reshape(nb, bg * P, bg * C)


def _layer(keep_ctx, xa, n_ctx, mod, p, cos2, sin2):
    T, D = xa.shape
    S = p["s5_d"].shape[-1]
    lay, width, gate_at = _in_layout(D, S)
    lo = 0 if keep_ctx else n_ctx
    ctx_rows = n_ctx if keep_ctx else 0

    h = modulate("modulate1", xa, mod["sh1"], mod["sc1"], n_ctx)
    w_in = p["w_in"]
    w_in = jnp.concatenate([w_in[:, :gate_at + GLA_GATE_RANK], jnp.zeros((D, LANE - GLA_GATE_RANK), F32),
                            w_in[:, gate_at + GLA_GATE_RANK:]], 1)
    z = mm("in_proj", h, w_in)
    zz = {n: z[:, o:o + w] for n, (o, w) in lay.items()}

    wg = jnp.pad(p["w_gla_gate"], ((0, 0), (0, LANE - GLA_GATE_RANK), (0, 0)))
    la0, la1 = gla_prep("gla_prep", zz["glr"], wg, p["b_gla_gate"][:, None, :])
    o0 = gla_scan("gla_scan", zz["gq"], zz["gk"], zz["gv"], la0, False, n_ctx)
    o1 = gla_scan("gla_scan_rev", zz["gq"], zz["gk"], zz["gv"], la1, True, n_ctx)
    o_gla = gla_norm("gla_norm", o0[lo:], o1[lo:], zz["gr"][lo:], p["gla_norm_w"][None, :])

    su = zz["su"]
    bur = mm("s5_in_re", su, _s5_in_blocks(p["s5_b_re"]))
    bui = mm("s5_in_im", su, _s5_in_blocks(p["s5_b_im"]))
    ys = []
    for d in range(2):
        row = lambda t: t.reshape(1, -1)
        ldt = jnp.repeat(p["s5_log_dt"][d], S5_STATE)
        sr, si = s5_scan("s5_scan_rev" if d else "s5_scan", bur, bui, row(p["s5_lam_re"][d]),
                         row(p["s5_lam_im"][d]), row(ldt), d == 1, n_ctx)
        ys.append(mm("s5_out_re", sr[lo:], _s5_out_blocks(p["s5_c_re"][d])))
        ys.append(mm("s5_out_im", si[lo:], _s5_out_blocks(p["s5_c_im"][d])))
    T2 = T - lo
    tm = _tile(T2, ROW_TILE, SUBLANE)
    post = block_op("s5_post", _f_s5_post, [("row", S), ("bc", 1, S)] + [("row", S)] * 4, [("row", S)], 1, T2, tm,
                    [True] * 6)
    yg = post(su[lo:], p["s5_d"][None, :], *ys)[0]
    o_s5 = rowwise("s5_glu", _f_s5_glu, [yg, mm("s5_glu_proj", yg, p["w_s5_glu"])])[0]

    qn = qk_norm_rope("q_norm_rope", zz["aq"], cos2, sin2, p["q_norm_w"][None, :])
    kn = qk_norm_rope("k_norm_rope", zz["ak"], cos2, sin2, p["k_norm_w"][None, :])
    o_attn = attention("attn_lat", qn[n_ctx:], kn, zz["av"])
    if keep_ctx:
        o_c = attention("attn_ctx", qn[:n_ctx], kn[:n_ctx], zz["av"][:n_ctx])
        o_attn = jnp.concatenate([o_c, o_attn], 0)

    merged = rowwise("merge", _f_merge, [zz["ga"][lo:], zz["gb"][lo:], zz["gc"][lo:],
                                         mm("proj_gla", o_gla, p["w_proj_gla"]),
                                         mm("proj_s5", o_s5, p["w_proj_s5"]),
                                         mm("proj_attn", o_attn, p["w_proj_attn"])])[0]
    mix = mm("out_proj", merged, p["w_out"])
    x1 = postnorm("postnorm1", xa[lo:], mix, mod["g1"], p["ln1_w"][None, :], p["ln1_b"][None, :], ctx_rows)
    h2 = modulate("modulate2", x1, mod["sh2"], mod["sc2"], ctx_rows)
    u = mm("ffn_in", h2, p["w_ffn_in"])
    F = u.shape[1] // 2
    act = rowwise("swiglu", _f_swiglu, [u[:, :F], u[:, F:]])[0]
    f = mm("ffn_out", act, p["w_ffn_out"])
    return postnorm("postnorm2", x1, f, mod["g2"], p["ln2_w"][None, :], p["ln2_b"][None, :], ctx_rows)


def local_loss(x, c, ctx, target, w):
    n_lat, D = x.shape
    n_ctx = ctx.shape[0]
    cos2, sin2 = _rope_tables(n_ctx, n_lat)
    cc = jnp.concatenate([c[None, :], w["c_ctx"][None, :], jnp.zeros((ADA_ROWS - 2, D), F32)], 0)
    silu_cc = rowwise("silu_cond", _f_silu, [cc])[0]
    xa = jnp.concatenate([ctx, x], 0)
    depth = w["w_in"].shape[0]
    for l in range(depth):
        p = {n: v[l] for n, v in w.items() if n != "c_ctx"}
        m = mm("ada_proj", silu_cc, p["w_ada"])
        m = block_op("ada_bias", _f_add_bias, [("row", 6 * D), ("bc", 1, 6 * D)], [("row", 6 * D)], 1, ADA_ROWS,
                     ADA_ROWS, [True, True])(m, p["b_ada"][None, :])[0]
        names = ["sh1", "sc1", "g1", "sh2", "sc2", "g2"]
        mod = {n: (m[0:1, i * D:(i + 1) * D], m[1:2, i * D:(i + 1) * D]) for i, n in enumerate(names)}
        xa = _layer(l < depth - 1, xa, n_ctx, mod, p, cos2, sin2)
    return sq_loss("loss", xa, target)


MESH = pl.DeviceIdType.MESH
ANY = pl.BlockSpec(memory_space=pl.ANY)


def _place():
    x, y, c = lax.axis_index("x"), lax.axis_index("y"), lax.axis_index("c")
    chips = [(1 - x, y), (x, 1 - y), (1 - x, 1 - y)]
    return x, y, c, chips


def _rcopy(src, dst, ssem, rsem, to):
    return pltpu.make_async_remote_copy(src_ref=src, dst_ref=dst, send_sem=ssem, recv_sem=rsem, device_id=to,
                                        device_id_type=MESH)


def gather_shards(bufs):
    n = len(bufs)
    L = bufs[0].shape[0]
    half = L // 2

    def body(*refs):
        dst = refs[n:2 * n]
        isend, irecv, dsend, drecv = refs[2 * n:]
        x, y, c, chips = _place()
        j = 2 * x + y
        mine, other = pl.ds(c * half, half), pl.ds((1 - c) * half, half)
        sends = [_rcopy(dst[i].at[mine, j], dst[i].at[mine, j], isend.at[i, r], irecv.at[i, r], (kx, ky, c))
                 for i in range(n) for r, (kx, ky) in enumerate(chips)]
        for cp in sends:
            cp.start()
        passed = []
        for i in range(n):
            for r, (kx, ky) in enumerate(chips):
                part = dst[i].at[mine, 2 * kx + ky]
                _rcopy(part, part, isend.at[i, r], irecv.at[i, r], (kx, ky, c)).wait_recv()
                fw = _rcopy(part, part, dsend.at[i, r], drecv.at[i, r], (x, y, 1 - c))
                fw.start()
                passed.append(fw)
        for i in range(n):
            for r, (kx, ky) in enumerate(chips):
                part = dst[i].at[other, 2 * kx + ky]
                _rcopy(part, part, dsend.at[i, r], drecv.at[i, r], (x, y, 1 - c)).wait_recv()
        for cp in sends + passed:
            cp.wait_send()

    out_shape = [jax.ShapeDtypeStruct(b.shape, b.dtype) for b in bufs]
    return pl.pallas_call(body, name="gather_shards", in_specs=[ANY] * n, out_specs=[ANY] * n, out_shape=out_shape,
                          scratch_shapes=[pltpu.SemaphoreType.DMA((n, 3))] * 4,
                          input_output_aliases={i: i for i in range(n)})(*bufs)


def swap_halves(grads):
    n = len(grads)
    half = grads[0].shape[0] // 2

    def body(*refs):
        src, dst = refs[:n], refs[n:2 * n]
        ssem, rsem = refs[2 * n:]
        x, y, c, _ = _place()
        other = pl.ds((1 - c) * half, half)
        cps = [_rcopy(src[i].at[other], dst[i], ssem.at[i], rsem.at[i], (x, y, 1 - c)) for i in range(n)]
        for cp in cps:
            cp.start()
        for cp in cps:
            cp.wait()

    out_shape = [jax.ShapeDtypeStruct((half,) + g.shape[1:], g.dtype) for g in grads]
    return pl.pallas_call(body, name="swap_halves", in_specs=[ANY] * n, out_specs=[ANY] * n, out_shape=out_shape,
                          scratch_shapes=[pltpu.SemaphoreType.DMA((n,))] * 2)(*grads)


def scatter_shards(parts):
    n = len(parts)
    half = parts[0].shape[0]

    def body(*refs):
        src, dst = refs[:n], refs[n:2 * n]
        ssem, rsem = refs[2 * n:]
        x, y, c, chips = _place()
        cps = [_rcopy(src[i].at[:, 2 * kx + ky], dst[i].at[r], ssem.at[i, r], rsem.at[i, r], (kx, ky, c))
               for i in range(n) for r, (kx, ky) in enumerate(chips)]
        for cp in cps:
            cp.start()
        for cp in cps:
            cp.wait()

    out_shape = [jax.ShapeDtypeStruct((3, half) + p.shape[2:], p.dtype) for p in parts]
    return pl.pallas_call(body, name="scatter_shards", in_specs=[ANY] * n, out_specs=[ANY] * n, out_shape=out_shape,
                          scratch_shapes=[pltpu.SemaphoreType.DMA((n, 3))] * 2)(*parts)


def join_halves(bufs):
    n = len(bufs)
    half = bufs[0].shape[0] // 2

    def body(*refs):
        dst = refs[n:2 * n]
        ssem, rsem = refs[2 * n:]
        x, y, c, _ = _place()
        mine = pl.ds(c * half, half)
        cps = [_rcopy(dst[i].at[mine], dst[i].at[mine], ssem.at[i], rsem.at[i], (x, y, 1 - c)) for i in range(n)]
        for cp in cps:
            cp.start()
        for cp in cps:
            cp.wait()

    out_shape = [jax.ShapeDtypeStruct(b.shape, b.dtype) for b in bufs]
    return pl.pallas_call(body, name="join_halves", in_specs=[ANY] * n, out_specs=[ANY] * n, out_shape=out_shape,
                          scratch_shapes=[pltpu.SemaphoreType.DMA((n,))] * 2,
                          input_output_aliases={i: i for i in range(n)})(*bufs)


def gather_blocks(v):
    def body(v_ref, out_ref, send_sems, recv_sems, local_sem):
        x, y, c, chips = _place()
        me, sibling = (x, y, c), (x, y, 1 - c)

        def blk(px, py, pc):
            return out_ref.at[4 * px + 2 * py + pc]

        def copy(k, block, to, src=None):
            return _rcopy(blk(*block) if src is None else src, blk(*block), send_sems.at[k], recv_sems.at[k], to)

        own = pltpu.make_async_copy(v_ref, blk(*me), local_sem)
        own.start()
        first = [copy(0, me, sibling, src=v_ref)]
        first += [copy(1 + r, me, (*chip, c), src=v_ref) for r, chip in enumerate(chips)]
        for cp in first:
            cp.start()
        passed = [copy(4 + r, (*chip, c), sibling) for r, chip in enumerate(chips)]
        for r, chip in enumerate(chips):
            copy(1 + r, (*chip, c), me).wait_recv()
            passed[r].start()
        copy(0, sibling, me).wait_recv()
        for r, chip in enumerate(chips):
            copy(4 + r, (*chip, 1 - c), me).wait_recv()
        for cp in first + passed:
            cp.wait_send()
        own.wait()

    return pl.pallas_call(body, name="gather_blocks", in_specs=[ANY], out_specs=ANY,
                          out_shape=jax.ShapeDtypeStruct((8,) + v.shape, v.dtype),
                          scratch_shapes=[pltpu.SemaphoreType.DMA((7,)), pltpu.SemaphoreType.DMA((7,)),
                                          pltpu.SemaphoreType.DMA])(v)


STREAM_BLOCK = 256 * 1024


def _stream_rows(rows, cols):
    base = 2 * SUBLANE if rows % (2 * SUBLANE) == 0 else SUBLANE
    return _tile(rows, max(base, STREAM_BLOCK // cols // base * base), base)


def _view3(a, lead):
    shape = a.shape[:lead] + (-1, a.shape[-1])
    return a.reshape(shape)


def sum_parts(name, terms, out, grid, where):
    def body(w_ref, *refs):
        acc = refs[0][...].astype(F32)
        for t in refs[1:-1]:
            acc = acc + t[...].astype(F32)
        refs[-1][...] = acc.astype(refs[-1].dtype)

    grid_spec = pltpu.PrefetchScalarGridSpec(
        num_scalar_prefetch=1, grid=grid, in_specs=[pl.BlockSpec(b, f) for _, b, f in terms],
        out_specs=pl.BlockSpec(out[2], out[3]))
    return pl.pallas_call(body, name=name, grid_spec=grid_spec, out_shape=jax.ShapeDtypeStruct(out[0], out[1]),
                          compiler_params=_params(("arbitrary",) * len(grid)))(where, *[t[0] for t in terms])


def cast_place(w, where):
    w3 = _view3(w, 1)
    L, rows, cols = w3.shape
    tr = _stream_rows(rows, cols)

    def body(w_ref, src, dst):
        dst[...] = src[...].astype(BF16)

    grid_spec = pltpu.PrefetchScalarGridSpec(
        num_scalar_prefetch=1, grid=(L, rows // tr),
        in_specs=[pl.BlockSpec((None, tr, cols), lambda l, r, wh: (l, r, 0))],
        out_specs=pl.BlockSpec((None, None, tr, cols), lambda l, r, wh: (l, wh[1], r, 0)))
    out = pl.pallas_call(body, name="cast_place", grid_spec=grid_spec,
                         out_shape=jax.ShapeDtypeStruct((L, N_CHIPS, rows, cols), BF16),
                         compiler_params=_params(("arbitrary", "arbitrary")))(where, w3)
    return out.reshape((L, N_CHIPS) + w.shape[1:])


def adamw(name, w, g, m, v):
    n, rows, cols = w.shape
    tr = _stream_rows(rows, cols)

    def body(w_ref, g_ref, m_ref, v_ref, d_ref, nm_ref, nv_ref):
        gv = g_ref[...]
        nm = ADAM_B1 * m_ref[...] + (1.0 - ADAM_B1) * gv
        nv = ADAM_B2 * v_ref[...] + (1.0 - ADAM_B2) * (gv * gv)
        m_hat = nm / (1.0 - ADAM_B1 ** ADAM_STEP)
        v_hat = nv / (1.0 - ADAM_B2 ** ADAM_STEP)
        d_ref[...] = -ADAM_LR * (m_hat / (jnp.sqrt(v_hat) + ADAM_EPS) + ADAM_WD * w_ref[...])
        nm_ref[...] = nm
        nv_ref[...] = nv

    blk = pl.BlockSpec((None, tr, cols), lambda l, r: (l, r, 0))
    shp = jax.ShapeDtypeStruct(w.shape, F32)
    return pl.pallas_call(body, name=name, grid=(n, rows // tr), in_specs=[blk] * 4, out_specs=[blk] * 3,
                          out_shape=[shp] * 3, compiler_params=_params(("arbitrary", "arbitrary")))(w, g, m, v)


COL_SHARDED = ("w_ada", "w_in", "w_proj_gla", "w_proj_s5", "w_proj_attn", "w_ffn_in", "w_gla_gate", "b_gla_gate")
ROW_SHARDED = ("w_s5_glu", "w_out", "w_ffn_out")
SHARDED = COL_SHARDED + ROW_SHARDED
WEIGHTS = ("c_ctx", "w_ada", "b_ada", "w_in", "w_gla_gate", "b_gla_gate", "gla_norm_w", "s5_lam_re", "s5_lam_im",
           "s5_log_dt", "s5_b_re", "s5_b_im", "s5_c_re", "s5_c_im", "s5_d", "w_s5_glu", "q_norm_w", "k_norm_w",
           "w_proj_gla", "w_proj_s5", "w_proj_attn", "w_out", "ln1_w", "ln1_b", "ln2_w", "ln2_b", "w_ffn_in",
           "w_ffn_out")
REPLICATED = tuple(n for n in WEIGHTS if n not in SHARDED)


def _full_weight(name, g):
    if name in ROW_SHARDED:
        return g.reshape((g.shape[0], -1) + g.shape[3:])
    nd = g.ndim
    perm = (0,) + tuple(range(2, nd - 1)) + (1, nd - 1)
    t = g.transpose(perm)
    return t.reshape(t.shape[:-2] + (-1,))


def _pack(arrays):
    flat = jnp.concatenate([a.reshape(-1) for a in arrays])
    pad = (-flat.shape[0]) % (PACK_ROWS * LANE)
    return jnp.pad(flat, (0, pad)).reshape(-1, LANE)


def _unpack(packed, like):
    flat, out, off = packed.reshape(-1), [], 0
    for a in like:
        out.append(flat[off:off + a.size].reshape(a.shape))
        off += a.size
    return out


def kernel(x, c, ctx, c_ctx, w_ada, b_ada, w_in, w_gla_gate, b_gla_gate, gla_norm_w, s5_lam_re, s5_lam_im, s5_log_dt, s5_b_re, s5_b_im, s5_c_re, s5_c_im, s5_d, w_s5_glu, q_norm_w, k_norm_w, w_proj_gla, w_proj_s5, w_proj_attn, w_out, ln1_w, ln1_b, ln2_w, ln2_b, w_ffn_in, w_ffn_out, loss_target, m_c_ctx, m_w_ada, m_b_ada, m_w_in, m_w_gla_gate, m_b_gla_gate, m_gla_norm_w, m_s5_lam_re, m_s5_lam_im, m_s5_log_dt, m_s5_b_re, m_s5_b_im, m_s5_c_re, m_s5_c_im, m_s5_d, m_w_s5_glu, m_q_norm_w, m_k_norm_w, m_w_proj_gla, m_w_proj_s5, m_w_proj_attn, m_w_out, m_ln1_w, m_ln1_b, m_ln2_w, m_ln2_b, m_w_ffn_in, m_w_ffn_out, v_c_ctx, v_w_ada, v_b_ada, v_w_in, v_w_gla_gate, v_b_gla_gate, v_gla_norm_w, v_s5_lam_re, v_s5_lam_im, v_s5_log_dt, v_s5_b_re, v_s5_b_im, v_s5_c_re, v_s5_c_im, v_s5_d, v_w_s5_glu, v_q_norm_w, v_k_norm_w, v_w_proj_gla, v_w_proj_s5, v_w_proj_attn, v_w_out, v_ln1_w, v_ln1_b, v_ln2_w, v_ln2_b, v_w_ffn_in, v_w_ffn_out):
    args = dict(locals())
    w = {n: args[n] for n in WEIGHTS}
    m = {n: args["m_" + n] for n in WEIGHTS}
    v = {n: args["v_" + n] for n in WEIGHTS}
    L = w_in.shape[0]
    half = L // 2
    core = lax.axis_index("c").astype(jnp.int32)
    place = (2 * lax.axis_index("x") + lax.axis_index("y")).astype(jnp.int32)
    zero = jnp.zeros((), jnp.int32)
    where, by_core, by_place = jnp.stack([core, place]), jnp.stack([core, zero]), jnp.stack([zero, place])

    gathered = gather_shards([cast_place(w[n], by_place) for n in SHARDED])
    gathered = {n: g.astype(F32) for n, g in zip(SHARDED, gathered)}
    replicated = {n: w[n] for n in REPLICATED}

    def loss_fn(x1, gathered, replicated):
        full = {n: _full_weight(n, g) for n, g in gathered.items()}
        return local_loss(x1, c[0], ctx[0], loss_target[0], {**full, **replicated})

    loss, (gx, g_sh, g_rep) = jax.value_and_grad(loss_fn, argnums=(0, 1, 2))(x[0], gathered, replicated)
    loss = lax.psum(loss, ("x", "y", "c"))

    parts = [_view3(g_sh[n], 2) for n in SHARDED]
    theirs = swap_halves(parts)
    chip_sums, blocks4 = [], []
    for p, t in zip(parts, theirs):
        rows, cols = p.shape[2:]
        tr = _stream_rows(rows, cols)
        blk = (None, None, tr, cols)
        blocks4.append(blk)
        chip_sums.append(sum_parts(
            "sum_cores", [(p, blk, lambda l, s, r, wh: (wh[0] * half + l, s, r, 0)),
                          (t, blk, lambda l, s, r, wh: (l, s, r, 0))],
            ((half,) + p.shape[1:], BF16, blk, lambda l, s, r, wh: (l, s, r, 0)), (half, N_CHIPS, rows // tr), by_core))
    recv = scatter_shards(chip_sums)
    finals = []
    for p, t, rcv, blk in zip(parts, theirs, recv, blocks4):
        rows, cols = p.shape[2:]
        terms = [(p, blk, lambda l, r, wh: (wh[0] * half + l, wh[1], r, 0)),
                 (t, blk, lambda l, r, wh: (l, wh[1], r, 0))]
        terms += [(rcv, blk, functools.partial(lambda k, l, r, wh: (k, l, r, 0), k)) for k in range(3)]
        finals.append(sum_parts("sum_chips", terms, ((L, rows, cols), F32, blk[1:],
                                                     lambda l, r, wh: (wh[0] * half + l, r, 0)),
                                (half, rows // blk[2]), where))
    grads = {n: g.reshape(w[n].shape) for n, g in zip(SHARDED, join_halves(finals))}

    rep_parts = [g_rep[n] for n in REPLICATED]
    blocks = gather_blocks(_pack(rep_parts))
    prow = (None, PACK_ROWS, LANE)
    total = sum_parts("sum_devices",
                      [(blocks, prow, functools.partial(lambda k, r, wh: (k, r, 0), k)) for k in range(8)],
                      (blocks.shape[1:], F32, prow[1:], lambda r, wh: (r, 0)), (blocks.shape[1] // PACK_ROWS,),
                      jnp.zeros((2,), jnp.int32))
    grads.update(dict(zip(REPLICATED, _unpack(total, rep_parts))))

    delta, new_m, new_v = {}, {}, {}
    for n in SHARDED:
        d3, m3, v3 = adamw("adamw", _view3(w[n], 1), _view3(grads[n], 1), _view3(m[n], 1), _view3(v[n], 1))
        delta[n], new_m[n], new_v[n] = (t.reshape(w[n].shape) for t in (d3, m3, v3))
    packed = [_pack([d[n] for n in REPLICATED])[None] for d in (w, m, v)]
    d3, m3, v3 = adamw("adamw_replicated", packed[0], total[None], packed[1], packed[2])
    like = [w[n] for n in REPLICATED]
    for dst, src in ((delta, d3), (new_m, m3), (new_v, v3)):
        dst.update(dict(zip(REPLICATED, _unpack(src[0], like))))

    return (loss, gx[None], *[grads[n] for n in WEIGHTS], *[delta[n] for n in WEIGHTS],
            *[new_m[n] for n in WEIGHTS], *[new_v[n] for n in WEIGHTS])
```

```python
import functools

import jax
import jax.numpy as jnp
from jax import lax
from jax.experimental import pallas as pl
from jax.experimental.pallas import tpu as pltpu

F32 = jnp.float32
BF16 = jnp.bfloat16

GRID_W = 64
GLA_HEADS = 4
GLA_DK = 128
GLA_DV = 256
GLA_GATE_RANK = 16
GLA_GATE_TAU = 16.0
GLA_CHUNK = 64
S5_GROUP = 16
S5_STATE = 64
ATTN_Q_HEADS = 8
ATTN_KV_HEADS = 2
ATTN_HEAD_DIM = 128
ROPE_THETA = 10000.0
DEPTH = 4
DN_ALPHA = (2 * DEPTH) ** 0.25
EPS = 1e-6
ADAM_LR = 0.001
ADAM_B1 = 0.9
ADAM_B2 = 0.999
ADAM_EPS = 1e-08
ADAM_WD = 0.01
ADAM_STEP = 10

LANE = 128
SUBLANE = 8
VMEM_LIMIT = 56 * 1024 * 1024
ADA_ROWS = 16
S5_CHUNK = 128
S5_BLOCK_GROUPS = 8
ROW_TILE = 256
COL_TILE = 512
PACK_ROWS = 512
N_CHIPS = 4


def _params(sem, **kw):
    return pltpu.CompilerParams(dimension_semantics=sem, vmem_limit_bytes=VMEM_LIMIT, **kw)


def _tile(n, target, base):
    if n <= target:
        return n
    best = None
    for t in range(base, target + 1, base):
        if n % t == 0:
            best = t
    assert best is not None, (n, target, base)
    return best


_DIMS = {"nn": ((1,), (0,)), "nt": ((1,), (1,)), "tn": ((0,), (0,))}


def _dg(a, b, mode):
    return lax.dot_general(a.astype(BF16), b.astype(BF16), (_DIMS[mode], ((), ())), preferred_element_type=F32)


@functools.partial(jax.custom_vjp, nondiff_argnums=(2,))
def bdot(a, b, mode):
    return _dg(a, b, mode)


def _bdot_fwd(a, b, mode):
    return _dg(a, b, mode), (a, b)


def _bdot_bwd(mode, res, g):
    a, b = res
    if mode == "nn":
        return bdot(g, b, "nt").astype(a.dtype), bdot(a, g, "tn").astype(b.dtype)
    if mode == "nt":
        return bdot(g, b, "nn").astype(a.dtype), bdot(g, a, "tn").astype(b.dtype)
    return bdot(b, g, "nt").astype(a.dtype), bdot(a, g, "nn").astype(b.dtype)


bdot.defvjp(_bdot_fwd, _bdot_bwd)


MM_TILES = {"nn": (1152, 1024, 1664), "nt": (1152, 1024, 1664), "tn": (256, 2048, 1024)}


def _mm_tiles(mode, M, Kb, Nb):
    tm, tk, tn = MM_TILES[mode]
    return _tile(M, tm, SUBLANE), _tile(Kb, tk, LANE), _tile(Nb, tn, LANE)


def _mm_body(mode, last):
    def body(p_ref, q_ref, o_ref, acc):
        k = pl.program_id(3)

        @pl.when(k == 0)
        def _():
            acc[...] = jnp.zeros_like(acc)

        acc[...] += _dg(p_ref[...], q_ref[...], mode)

        @pl.when(k == last)
        def _():
            o_ref[...] = acc[...]

    return body


def _mm_nn(name, a, w):
    M = a.shape[0]
    B, Kb, Nb = w.shape
    tm, tk, tn = _mm_tiles("nn", M, Kb, Nb)
    nk, nn = Kb // tk, Nb // tn
    return pl.pallas_call(
        _mm_body("nn", nk - 1), name=name, grid=(B, M // tm, nn, nk),
        in_specs=[pl.BlockSpec((tm, tk), lambda b, i, j, k: (i, b * nk + k)),
                  pl.BlockSpec((None, tk, tn), lambda b, i, j, k: (b, k, j))],
        out_specs=pl.BlockSpec((tm, tn), lambda b, i, j, k: (i, b * nn + j)),
        out_shape=jax.ShapeDtypeStruct((M, B * Nb), F32),
        scratch_shapes=[pltpu.VMEM((tm, tn), F32)],
        compiler_params=_params(("arbitrary",) * 4))(a, w)


def _mm_nt(name, g, w):
    M = g.shape[0]
    B, Kb, Nb = w.shape
    tm, tk, tn = _mm_tiles("nt", M, Kb, Nb)
    nk, nn = Kb // tk, Nb // tn
    return pl.pallas_call(
        _mm_body("nt", nn - 1), name=name, grid=(B, M // tm, nk, nn),
        in_specs=[pl.BlockSpec((tm, tn), lambda b, i, k, n: (i, b * nn + n)),
                  pl.BlockSpec((None, tk, tn), lambda b, i, k, n: (b, k, n))],
        out_specs=pl.BlockSpec((tm, tk), lambda b, i, k, n: (i, b * nk + k)),
        out_shape=jax.ShapeDtypeStruct((M, B * Kb), F32),
        scratch_shapes=[pltpu.VMEM((tm, tk), F32)],
        compiler_params=_params(("arbitrary",) * 4))(g, w)


def _mm_tn(name, a, g, B):
    M = a.shape[0]
    Kb, Nb = a.shape[1] // B, g.shape[1] // B
    tm, tk, tn = _mm_tiles("tn", M, Kb, Nb)
    nk, nn = Kb // tk, Nb // tn
    return pl.pallas_call(
        _mm_body("tn", M // tm - 1), name=name, grid=(B, nk, nn, M // tm),
        in_specs=[pl.BlockSpec((tm, tk), lambda b, k, j, m: (m, b * nk + k)),
                  pl.BlockSpec((tm, tn), lambda b, k, j, m: (m, b * nn + j))],
        out_specs=pl.BlockSpec((None, tk, tn), lambda b, k, j, m: (b, k, j)),
        out_shape=jax.ShapeDtypeStruct((B, Kb, Nb), F32),
        scratch_shapes=[pltpu.VMEM((tk, tn), F32)],
        compiler_params=_params(("arbitrary",) * 4))(a, g)


def mm(name, a, w):
    w3 = w if w.ndim == 3 else w[None]

    @jax.custom_vjp
    def op(a, w3):
        return _mm_nn(name + "_fwd", a, w3.astype(BF16))

    def fwd(a, w3):
        wb = w3.astype(BF16)
        return _mm_nn(name + "_fwd", a, wb), (a, wb)

    def bwd(res, g):
        a, wb = res
        return _mm_nt(name + "_dx", g, wb), _mm_tn(name + "_dw", a, g, wb.shape[0])

    op.defvjp(fwd, bwd)
    return op(a, w3)


def _spec_shape(spec, G, T):
    k = spec[0]
    if k == "row":
        return (T, spec[1])
    if k == "rowg":
        return (T, (G // spec[2]) * spec[1])
    if k == "bc":
        return (spec[1], spec[2])
    return (spec[1], (G // spec[3]) * spec[2])


def _spec_block(spec, tm, rmap):
    k = spec[0]
    if k == "row":
        return pl.BlockSpec((tm, spec[1]), lambda g, r: (rmap(r), 0))
    if k == "rowg":
        d = spec[2]
        return pl.BlockSpec((tm, spec[1]), lambda g, r: (rmap(r), g // d))
    if k == "bc":
        return pl.BlockSpec((spec[1], spec[2]), lambda g, r: (0, 0))
    d = spec[3]
    return pl.BlockSpec((spec[1], spec[2]), lambda g, r: (0, g // d))


def block_op(name, f, in_specs, out_specs, G, T, tm, diff, carry=(), row0=False, order=None):
    n_in, n_out, n_c = len(in_specs), len(out_specs), len(carry)
    n_steps = T // tm
    assert T % tm == 0
    order = order or (lambda s: s)
    diff_idx = [i for i in range(n_in) if diff[i]]
    for i in diff_idx:
        assert in_specs[i][0] != "row" or G == 1
        assert in_specs[i][0] != "rowg" or in_specs[i][2] == 1
    out_shapes = [jax.ShapeDtypeStruct(_spec_shape(s, G, T), F32) for s in out_specs]
    save_shapes = [jax.ShapeDtypeStruct((n_steps, a, G * b), F32) for a, b in carry]
    sem = ("arbitrary", "arbitrary")

    def call_f(r_idx, cvals, vals):
        args = list(vals)
        if n_c:
            args = [tuple(cvals)] + args
        if row0:
            args = [r_idx * tm] + args
        return f(*args)

    def fwd_body(*refs):
        in_refs = refs[:n_in]
        out_refs = refs[n_in:n_in + n_out]
        save_refs = refs[n_in + n_out:n_in + n_out + n_c]
        c_refs = refs[n_in + n_out + n_c:]
        r = pl.program_id(1)
        if n_c:
            @pl.when(r == 0)
            def _():
                for c in c_refs:
                    c[...] = jnp.zeros_like(c)

            cvals = [c[...] for c in c_refs]
            for s, v in zip(save_refs, cvals):
                s[...] = v
            new_c, outs = call_f(r, cvals, [x[...] for x in in_refs])
            for c, v in zip(c_refs, new_c):
                c[...] = v
        else:
            outs = call_f(r, (), [x[...] for x in in_refs])
        for o, v in zip(out_refs, outs):
            o[...] = v.astype(F32)

    def fwd_call(*arrays):
        res = pl.pallas_call(
            fwd_body, name=name + "_fwd", grid=(G, n_steps),
            in_specs=[_spec_block(s, tm, order) for s in in_specs],
            out_specs=[_spec_block(s, tm, order) for s in out_specs]
            + [pl.BlockSpec((None, a, b), lambda g, r: (r, 0, g)) for a, b in carry],
            out_shape=out_shapes + save_shapes,
            scratch_shapes=[pltpu.VMEM((a, b), F32) for a, b in carry],
            compiler_params=_params(sem))(*arrays)
        return tuple(res)

    def bwd_body(*refs):
        in_refs = refs[:n_in]
        save_refs = refs[n_in:n_in + n_c]
        ct_refs = refs[n_in + n_c:n_in + n_c + n_out]
        g_refs = refs[n_in + n_c + n_out:n_in + n_c + n_out + len(diff_idx)]
        dc_refs = refs[n_in + n_c + n_out + len(diff_idx):]
        g = pl.program_id(0)
        r = pl.program_id(1)
        vals = [x[...] for x in in_refs]
        if n_c:
            @pl.when(r == 0)
            def _():
                for d in dc_refs:
                    d[...] = jnp.zeros_like(d)

        def fun(cvals, dvals):
            full = list(vals)
            for i, v in zip(diff_idx, dvals):
                full[i] = v
            return call_f(n_steps - 1 - r if n_c else r, cvals, full)

        _, vjp = jax.vjp(fun, tuple(s[...] for s in save_refs), tuple(vals[i] for i in diff_idx))
        cts = tuple(c[...] for c in ct_refs)
        if n_c:
            cts = (tuple(d[...] for d in dc_refs), cts)
        dcin, dvals = vjp(cts)
        for d, v in zip(dc_refs, dcin):
            d[...] = v
        for gref, i, v in zip(g_refs, diff_idx, dvals):
            spec = in_specs[i]
            if spec[0] in ("row", "rowg"):
                gref[...] = v
            else:
                first = (r == 0) & ((g == 0) if spec[0] == "bc" else (g % spec[3] == 0))

                @pl.when(first)
                def _(gref=gref, v=v):
                    gref[...] = v

                @pl.when(jnp.logical_not(first))
                def _(gref=gref, v=v):
                    gref[...] += v

    def bwd_call(arrays, saved, cts):
        rmap = (lambda r: order(n_steps - 1 - r)) if n_c else (lambda r: r)
        res = pl.pallas_call(
            bwd_body, name=name + "_bwd", grid=(G, n_steps),
            in_specs=[_spec_block(s, tm, rmap) for s in in_specs]
            + [pl.BlockSpec((None, a, b), lambda g, r: (n_steps - 1 - r, 0, g)) for a, b in carry]
            + [_spec_block(s, tm, rmap) for s in out_specs],
            out_specs=[_spec_block(in_specs[i], tm, rmap) for i in diff_idx],
            out_shape=[jax.ShapeDtypeStruct(_spec_shape(in_specs[i], G, T), F32) for i in diff_idx],
            scratch_shapes=[pltpu.VMEM((a, b), F32) for a, b in carry],
            compiler_params=_params(sem))(*arrays, *saved, *cts)
        return tuple(res)

    @jax.custom_vjp
    def op(*arrays):
        return fwd_call(*arrays)[:n_out]

    def op_fwd(*arrays):
        res = fwd_call(*arrays)
        return res[:n_out], (arrays, res[n_out:])

    def op_bwd(res, cts):
        arrays, saved = res
        grads = bwd_call(arrays, saved, cts)
        out = [jnp.zeros_like(a) for a in arrays]
        for i, gval in zip(diff_idx, grads):
            out[i] = gval
        return tuple(out)

    op.defvjp(op_fwd, op_bwd)
    return op


def _rows(n, m):
    return lax.broadcasted_iota(jnp.int32, (n, m), 0)


def _ctx_select(row0, tm, n_ctx, v_lat, v_ctx):
    if n_ctx == 0:
        return v_lat
    is_ctx = (row0 + _rows(tm, 1)) < n_ctx
    return jnp.where(is_ctx, v_ctx, v_lat)


def _silu(x):
    return x * jax.nn.sigmoid(x)


def _f_modulate(tm, n_ctx):
    def f(row0, x, sh_l, sh_c, sc_l, sc_c):
        sh = _ctx_select(row0, tm, n_ctx, sh_l, sh_c)
        sc = _ctx_select(row0, tm, n_ctx, sc_l, sc_c)
        return (x * (1 + sc) + sh,)
    return f


def _f_postnorm(tm, n_ctx):
    def f(row0, x, y, g_l, g_c, w, b):
        z = DN_ALPHA * x + _ctx_select(row0, tm, n_ctx, g_l, g_c) * y
        mu = jnp.mean(z, -1, keepdims=True)
        zc = z - mu
        var = jnp.mean(zc * zc, -1, keepdims=True)
        return (zc * lax.rsqrt(var + EPS) * w + b,)
    return f


def _log_sigmoid(x):
    return -(jnp.maximum(-x, 0.0) + jnp.log1p(jnp.exp(-jnp.abs(x))))


def _f_gla_prep(glr, wg0, wg1, b0, b1):
    return (_log_sigmoid(bdot(glr, wg0, "nn") + b0) / GLA_GATE_TAU,
            _log_sigmoid(bdot(glr, wg1, "nn") + b1) / GLA_GATE_TAU)


def _f_gla_step(rev):
    def f(carry, q, k, v, la):
        (st,) = carry
        n = q.shape[0]
        cols = lax.broadcasted_iota(jnp.int32, (n, n), 1)
        tri = (_rows(n, n) <= cols) if rev else (_rows(n, n) >= cols)
        b = jnp.dot(tri.astype(F32), la, precision=lax.Precision.HIGHEST)
        qe = q * (GLA_DK ** -0.5) * jnp.exp(b)
        ke = k * jnp.exp(-b)
        att = jnp.where(tri, bdot(qe, ke, "nt"), 0.0)
        o = bdot(att, v, "nn") + bdot(qe, st, "nt")
        end = 0 if rev else n - 1
        b_last = jnp.sum(jnp.where(_rows(n, 1) == end, b, 0.0), axis=0, keepdims=True)
        kd = k * jnp.exp(b_last - b)
        st = st * jnp.exp(b_last) + bdot(v, kd, "tn")
        return (st,), (o,)
    return f


def _f_gla_norm(o0, o1, gr, w):
    o = o0 + o1
    mu = jnp.mean(o, -1, keepdims=True)
    oc = o - mu
    var = jnp.mean(oc * oc, -1, keepdims=True)
    return (oc * lax.rsqrt(var + EPS) * w * _silu(gr),)


@functools.partial(jax.custom_vjp, nondiff_argnums=(1, 2))
def _shift_rows(x, d, up):
    n = x.shape[0]
    rows = _rows(n, 1)
    if up:
        return jnp.where(rows < n - d, pltpu.roll(x, n - d, 0), 0.0)
    return jnp.where(rows >= d, pltpu.roll(x, d, 0), 0.0)


def _shift_fwd(x, d, up):
    return _shift_rows(x, d, up), None


def _shift_bwd(d, up, _, g):
    return (_shift_rows(g, d, not up),)


_shift_rows.defvjp(_shift_fwd, _shift_bwd)


def _f_s5_step(rev):
    def f(carry, bur, bui, lam_re, lam_im, log_dt):
        cr, ci = carry
        n = bur.shape[0]
        dt = jnp.exp(log_dt)
        mag = jnp.exp(lam_re * dt)
        ar, ai = mag * jnp.cos(lam_im * dt), mag * jnp.sin(lam_im * dt)
        den = lam_re * lam_re + lam_im * lam_im
        nr, ni = ar - 1, ai
        kr = (nr * lam_re + ni * lam_im) / den
        ki = (ni * lam_re - nr * lam_im) / den
        first = _rows(n, 1) == (n - 1 if rev else 0)
        xr = kr * bur - ki * bui + jnp.where(first, ar * cr - ai * ci, 0.0)
        xi = kr * bui + ki * bur + jnp.where(first, ar * ci + ai * cr, 0.0)
        pr, pi = ar, ai
        d = 1
        while d < n:
            sr, si = _shift_rows(xr, d, rev), _shift_rows(xi, d, rev)
            xr, xi = xr + pr * sr - pi * si, xi + pr * si + pi * sr
            pr, pi = pr * pr - pi * pi, 2 * pr * pi
            d *= 2
        last = _rows(n, 1) == (0 if rev else n - 1)
        cr = jnp.sum(jnp.where(last, xr, 0.0), axis=0, keepdims=True)
        ci = jnp.sum(jnp.where(last, xi, 0.0), axis=0, keepdims=True)
        return (cr, ci), (xr, xi)
    return f


def _f_s5_post(su, dskip, y0r, y0i, y1r, y1i):
    return (jax.nn.gelu(su * dskip + y0r - y0i + y1r - y1i),)


def _f_s5_glu(y, t):
    return (y * jax.nn.sigmoid(t),)


def _swap_pairs(x):
    lane = lax.broadcasted_iota(jnp.int32, x.shape, 1)
    return jnp.where(lane % 2 == 0, pltpu.roll(x, x.shape[1] - 1, 1), pltpu.roll(x, 1, 1))


@jax.custom_vjp
def _rope(x, cos2, sin2):
    return x * cos2 + _swap_pairs(x) * sin2


def _rope_fwd(x, cos2, sin2):
    return _rope(x, cos2, sin2), (cos2, sin2)


def _rope_bwd(res, g):
    cos2, sin2 = res
    return g * cos2 + _swap_pairs(g * sin2), jnp.zeros_like(cos2), jnp.zeros_like(sin2)


_rope.defvjp(_rope_fwd, _rope_bwd)


def _f_qk_norm_rope(x, cos2, sin2, w):
    xn = x * lax.rsqrt(jnp.mean(x * x, -1, keepdims=True) + EPS) * w
    return (_rope(xn, cos2, sin2),)


def _f_attn(q, k, v):
    s = bdot(q, k, "nt") * (ATTN_HEAD_DIM ** -0.5)
    e = jnp.exp(s - jnp.max(s, -1, keepdims=True))
    p = e / jnp.sum(e, -1, keepdims=True)
    return (bdot(p, v, "nn"),)


def _f_merge(ga, gb, gc, pa, pb, pc):
    return (jax.nn.sigmoid(ga) * pa + jax.nn.sigmoid(gb) * pb + jax.nn.sigmoid(gc) * pc,)


def _f_swiglu(a, b):
    return (_silu(a) * b,)


def _f_silu(x):
    return (_silu(x),)


def _f_add_bias(x, b):
    return (x + b,)


def _scan_order(rev, n_ctx, T, tm):
    if not rev:
        return None
    nc, n = n_ctx // tm, T // tm
    return lambda s: jnp.where(s < nc, nc - 1 - s, n - 1 - (s - nc))


def modulate(name, x, sh, sc, n_ctx):
    T, D = x.shape
    tm = _tile(T, ROW_TILE, SUBLANE)
    cw = _tile(D, COL_TILE, LANE)
    col, vec = ("rowg", cw, 1), ("bcg", 1, cw, 1)
    op = block_op(name, _f_modulate(tm, n_ctx), [col, vec, vec, vec, vec], [col], D // cw, T, tm,
                  [True] * 5, row0=True)
    return op(x, sh[0], sh[1], sc[0], sc[1])[0]


def postnorm(name, x, y, g, w, b, n_ctx):
    T, D = x.shape
    tm = _tile(T, ROW_TILE, SUBLANE)
    vec = ("bc", 1, D)
    op = block_op(name, _f_postnorm(tm, n_ctx), [("row", D), ("row", D), vec, vec, vec, vec], [("row", D)], 1, T,
                  tm, [True] * 6, row0=True)
    return op(x, y, g[0], g[1], w, b)[0]


def rowwise(name, f, arrays, n_out=1):
    T, w = arrays[0].shape
    tm = _tile(T, ROW_TILE, SUBLANE)
    cw = _tile(w, COL_TILE, LANE)
    col = ("rowg", cw, 1)
    op = block_op(name, f, [col] * len(arrays), [col] * n_out, w // cw, T, tm, [True] * len(arrays))
    return op(*arrays)


def gla_prep(name, glr, wg, bg):
    T = glr.shape[0]
    qk = wg.shape[-1]
    tm = _tile(T, ROW_TILE, SUBLANE)
    op = block_op(name, _f_gla_prep, [("row", LANE), ("bc", LANE, qk), ("bc", LANE, qk), ("bc", 1, qk), ("bc", 1, qk)],
                  [("row", qk), ("row", qk)], 1, T, tm, [True] * 5)
    return op(glr, wg[0], wg[1], bg[0], bg[1])


def gla_scan(name, q, k, v, la, rev, n_ctx):
    T = q.shape[0]
    op = block_op(name, _f_gla_step(rev), [("rowg", GLA_DK, 1), ("rowg", GLA_DK, 1), ("rowg", GLA_DV, 1), ("rowg", GLA_DK, 1)],
                  [("rowg", GLA_DV, 1)], GLA_HEADS, T, GLA_CHUNK, [True] * 4, carry=[(GLA_DV, GLA_DK)],
                  order=_scan_order(rev, n_ctx, T, GLA_CHUNK))
    return op(q, k, v, la)[0]


def gla_norm(name, o0, o1, gr, w):
    T = o0.shape[0]
    tm = _tile(T, ROW_TILE, SUBLANE)
    hd = ("rowg", GLA_DV, 1)
    op = block_op(name, _f_gla_norm, [hd, hd, hd, ("bcg", 1, GLA_DV, 1)], [hd], GLA_HEADS, T, tm, [True] * 4)
    return op(o0, o1, gr, w)[0]


def s5_scan(name, bur, bui, lam_re, lam_im, log_dt, rev, n_ctx):
    T, S = bur.shape
    cols = _tile(S, 768, LANE)
    G = S // cols
    col, par = ("rowg", cols, 1), ("bcg", 1, cols, 1)
    op = block_op(name, _f_s5_step(rev), [col, col, par, par, par], [col, col], G, T, S5_CHUNK, [True] * 5,
                  carry=[(1, cols), (1, cols)], order=_scan_order(rev, n_ctx, T, S5_CHUNK))
    return op(bur, bui, lam_re, lam_im, log_dt)


def qk_norm_rope(name, x, cos2, sin2, w):
    T = x.shape[0]
    G = x.shape[1] // ATTN_HEAD_DIM
    tm = _tile(T, ROW_TILE, SUBLANE)
    hd = ("rowg", ATTN_HEAD_DIM, 1)
    op = block_op(name, _f_qk_norm_rope, [hd, ("row", ATTN_HEAD_DIM), ("row", ATTN_HEAD_DIM), ("bc", 1, ATTN_HEAD_DIM)],
                  [hd], G, T, tm, [True, False, False, True])
    return op(x, cos2, sin2, w)[0]


def attention(name, q, k, v):
    T, Tk = q.shape[0], k.shape[0]
    tm = _tile(T, ROW_TILE, SUBLANE)
    grp = ATTN_Q_HEADS // ATTN_KV_HEADS
    kv = ("bcg", Tk, ATTN_HEAD_DIM, grp)
    op = block_op(name, _f_attn, [("rowg", ATTN_HEAD_DIM, 1), kv, kv], [("rowg", ATTN_HEAD_DIM, 1)], ATTN_Q_HEADS, T,
                  tm, [True] * 3)
    return op(q, k, v)[0]


def sq_loss(name, y, t):
    T, D = y.shape
    tm = _tile(T, ROW_TILE, SUBLANE)

    def fwd_body(y_ref, t_ref, o_ref):
        e = y_ref[...] - t_ref[...]
        part = jnp.sum(jnp.sum(e * e, -1, keepdims=True), 0, keepdims=True) * (0.5 / D)

        @pl.when(pl.program_id(0) == 0)
        def _():
            o_ref[...] = jnp.zeros_like(o_ref)

        o_ref[...] += part * jnp.ones((1, LANE), F32)

    def bwd_body(y_ref, t_ref, g_ref, o_ref):
        o_ref[...] = (y_ref[...] - t_ref[...]) * (g_ref[:, 0:1] / D)

    row = pl.BlockSpec((tm, D), lambda r: (r, 0))
    one = pl.BlockSpec((1, LANE), lambda r: (0, 0))

    def fwd_call(y, t):
        return pl.pallas_call(fwd_body, name=name + "_fwd", grid=(T // tm,), in_specs=[row, row], out_specs=one,
                              out_shape=jax.ShapeDtypeStruct((1, LANE), F32), compiler_params=_params(("arbitrary",)))(y, t)

    @jax.custom_vjp
    def op(y, t):
        return fwd_call(y, t)[0, 0]

    def op_fwd(y, t):
        return fwd_call(y, t)[0, 0], (y, t)

    def op_bwd(res, g):
        y, t = res
        gy = pl.pallas_call(bwd_body, name=name + "_bwd", grid=(T // tm,), in_specs=[row, row, one], out_specs=row,
                            out_shape=jax.ShapeDtypeStruct((T, D), F32), compiler_params=_params(("arbitrary",)))(
                                y, t, jnp.full((1, LANE), g, F32))
        return gy, jnp.zeros_like(t)

    op.defvjp(op_fwd, op_bwd)
    return op(y, t)


def _rope_tables(n_ctx, n_lat):
    n_rows = n_lat // GRID_W
    rows = jnp.repeat(jnp.arange(n_rows), GRID_W).astype(F32)
    cols = jnp.tile(jnp.arange(GRID_W), n_rows).astype(F32)
    n_freq = ATTN_HEAD_DIM // 4
    inv = ROPE_THETA ** (-jnp.arange(n_freq, dtype=F32) / n_freq)
    ang = jnp.concatenate([rows[:, None] * inv, cols[:, None] * inv], -1)
    cos2 = jnp.repeat(jnp.cos(ang), 2, axis=-1)
    sin2 = jnp.stack([-jnp.sin(ang), jnp.sin(ang)], -1).reshape(n_lat, ATTN_HEAD_DIM)
    cos2 = jnp.concatenate([jnp.ones((n_ctx, ATTN_HEAD_DIM), F32), cos2], 0)
    sin2 = jnp.concatenate([jnp.zeros((n_ctx, ATTN_HEAD_DIM), F32), sin2], 0)
    return cos2, sin2


def _in_layout(D, s5_width):
    qk, gv = GLA_HEADS * GLA_DK, GLA_HEADS * GLA_DV
    aq, akv = ATTN_Q_HEADS * ATTN_HEAD_DIM, ATTN_KV_HEADS * ATTN_HEAD_DIM
    widths = [("gq", qk), ("gk", qk), ("gv", gv), ("gr", gv), ("glr", LANE), ("su", s5_width), ("aq", aq),
              ("ak", akv), ("av", akv), ("ga", D), ("gb", D), ("gc", D)]
    off, out = 0, {}
    for n, w in widths:
        out[n] = (off, w)
        off += w
    return out, off, 2 * qk + 2 * gv


def _s5_in_blocks(b):
    G, P, C = b.shape
    nb, bg = G // S5_BLOCK_GROUPS, S5_BLOCK_GROUPS
    t = b.reshape(nb, bg, P, C).transpose(0, 1, 3, 2)
    return jnp.einsum("bgcp,gh->bgchp", t, jnp.eye(bg, dtype=F32)).reshape(nb, bg * C, bg * P)


def _s5_out_blocks(c):
    G, C, P = c.shape
    nb, bg = G // S5_BLOCK_GROUPS, S5_BLOCK_GROUPS
    t = c.reshape(nb, bg, C, P).transpose(0, 1, 3, 2)
    return jnp.einsum("bgpc,gh->bgphc", t, jnp.eye(bg, dtype=F32)).reshape(nb, bg * P, bg * C)


def _layer(keep_ctx, xa, n_ctx, mod, p, cos2, sin2):
    T, D = xa.shape
    S = p["s5_d"].shape[-1]
    lay, width, gate_at = _in_layout(D, S)
    lo = 0 if keep_ctx else n_ctx
    ctx_rows = n_ctx if keep_ctx else 0

    h = modulate("modulate1", xa, mod["sh1"], mod["sc1"], n_ctx)
    w_in = p["w_in"]
    w_in = jnp.concatenate([w_in[:, :gate_at + GLA_GATE_RANK], jnp.zeros((D, LANE - GLA_GATE_RANK), F32),
                            w_in[:, gate_at + GLA_GATE_RANK:]], 1)
    z = mm("in_proj", h, w_in)
    zz = {n: z[:, o:o + w] for n, (o, w) in lay.items()}

    wg = jnp.pad(p["w_gla_gate"], ((0, 0), (0, LANE - GLA_GATE_RANK), (0, 0)))
    la0, la1 = gla_prep("gla_prep", zz["glr"], wg, p["b_gla_gate"][:, None, :])
    o0 = gla_scan("gla_scan", zz["gq"], zz["gk"], zz["gv"], la0, False, n_ctx)
    o1 = gla_scan("gla_scan_rev", zz["gq"], zz["gk"], zz["gv"], la1, True, n_ctx)
    o_gla = gla_norm("gla_norm", o0[lo:], o1[lo:], zz["gr"][lo:], p["gla_norm_w"][None, :])

    su = zz["su"]
    bur = mm("s5_in_re", su, _s5_in_blocks(p["s5_b_re"]))
    bui = mm("s5_in_im", su, _s5_in_blocks(p["s5_b_im"]))
    ys = []
    for d in range(2):
        row = lambda t: t.reshape(1, -1)
        ldt = jnp.repeat(p["s5_log_dt"][d], S5_STATE)
        sr, si = s5_scan("s5_scan_rev" if d else "s5_scan", bur, bui, row(p["s5_lam_re"][d]),
                         row(p["s5_lam_im"][d]), row(ldt), d == 1, n_ctx)
        ys.append(mm("s5_out_re", sr[lo:], _s5_out_blocks(p["s5_c_re"][d])))
        ys.append(mm("s5_out_im", si[lo:], _s5_out_blocks(p["s5_c_im"][d])))
    T2 = T - lo
    tm = _tile(T2, ROW_TILE, SUBLANE)
    post = block_op("s5_post", _f_s5_post, [("row", S), ("bc", 1, S)] + [("row", S)] * 4, [("row", S)], 1, T2, tm,
                    [True] * 6)
    yg = post(su[lo:], p["s5_d"][None, :], *ys)[0]
    o_s5 = rowwise("s5_glu", _f_s5_glu, [yg, mm("s5_glu_proj", yg, p["w_s5_glu"])])[0]

    qn = qk_norm_rope("q_norm_rope", zz["aq"], cos2, sin2, p["q_norm_w"][None, :])
    kn = qk_norm_rope("k_norm_rope", zz["ak"], cos2, sin2, p["k_norm_w"][None, :])
    o_attn = attention("attn_lat", qn[n_ctx:], kn, zz["av"])
    if keep_ctx:
        o_c = attention("attn_ctx", qn[:n_ctx], kn[:n_ctx], zz["av"][:n_ctx])
        o_attn = jnp.concatenate([o_c, o_attn], 0)

    merged = rowwise("merge", _f_merge, [zz["ga"][lo:], zz["gb"][lo:], zz["gc"][lo:],
                                         mm("proj_gla", o_gla, p["w_proj_gla"]),
                                         mm("proj_s5", o_s5, p["w_proj_s5"]),
                                         mm("proj_attn", o_attn, p["w_proj_attn"])])[0]
    mix = mm("out_proj", merged, p["w_out"])
    x1 = postnorm("postnorm1", xa[lo:], mix, mod["g1"], p["ln1_w"][None, :], p["ln1_b"][None, :], ctx_rows)
    h2 = modulate("modulate2", x1, mod["sh2"], mod["sc2"], ctx_rows)
    u = mm("ffn_in", h2, p["w_ffn_in"])
    F = u.shape[1] // 2
    act = rowwise("swiglu", _f_swiglu, [u[:, :F], u[:, F:]])[0]
    f = mm("ffn_out", act, p["w_ffn_out"])
    return postnorm("postnorm2", x1, f, mod["g2"], p["ln2_w"][None, :], p["ln2_b"][None, :], ctx_rows)


def local_loss(x, c, ctx, target, w):
    n_lat, D = x.shape
    n_ctx = ctx.shape[0]
    cos2, sin2 = _rope_tables(n_ctx, n_lat)
    cc = jnp.concatenate([c[None, :], w["c_ctx"][None, :], jnp.zeros((ADA_ROWS - 2, D), F32)], 0)
    silu_cc = rowwise("silu_cond", _f_silu, [cc])[0]
    xa = jnp.concatenate([ctx, x], 0)
    depth = len(w["w_in"])
    for l in range(depth):
        p = {n: v[l] for n, v in w.items() if n != "c_ctx"}
        m = mm("ada_proj", silu_cc, p["w_ada"])
        m = block_op("ada_bias", _f_add_bias, [("row", 6 * D), ("bc", 1, 6 * D)], [("row", 6 * D)], 1, ADA_ROWS,
                     ADA_ROWS, [True, True])(m, p["b_ada"][None, :])[0]
        names = ["sh1", "sc1", "g1", "sh2", "sc2", "g2"]
        mod = {n: (m[0:1, i * D:(i + 1) * D], m[1:2, i * D:(i + 1) * D]) for i, n in enumerate(names)}
        xa = _layer(l < depth - 1, xa, n_ctx, mod, p, cos2, sin2)
    return sq_loss("loss", xa, target)


MESH = pl.DeviceIdType.MESH
ANY = pl.BlockSpec(memory_space=pl.ANY)


def _place():
    x, y, c = lax.axis_index("x"), lax.axis_index("y"), lax.axis_index("c")
    chips = [(1 - x, y), (x, 1 - y), (1 - x, 1 - y)]
    return x, y, c, chips


def _rcopy(src, dst, ssem, rsem, to):
    return pltpu.make_async_remote_copy(src_ref=src, dst_ref=dst, send_sem=ssem, recv_sem=rsem, device_id=to,
                                        device_id_type=MESH)


def gather_shards(bufs):
    n = len(bufs)
    L = bufs[0].shape[0]
    half = L // 2

    def body(*refs):
        dst = refs[n:2 * n]
        isend, irecv, dsend, drecv = refs[2 * n:]
        x, y, c, chips = _place()
        j = 2 * x + y
        mine, other = pl.ds(c * half, half), pl.ds((1 - c) * half, half)
        sends = [_rcopy(dst[i].at[mine, j], dst[i].at[mine, j], isend.at[i, r], irecv.at[i, r], (kx, ky, c))
                 for i in range(n) for r, (kx, ky) in enumerate(chips)]
        for cp in sends:
            cp.start()
        passed = []
        for i in range(n):
            for r, (kx, ky) in enumerate(chips):
                part = dst[i].at[mine, 2 * kx + ky]
                _rcopy(part, part, isend.at[i, r], irecv.at[i, r], (kx, ky, c)).wait_recv()
                fw = _rcopy(part, part, dsend.at[i, r], drecv.at[i, r], (x, y, 1 - c))
                fw.start()
                passed.append(fw)
        for i in range(n):
            for r, (kx, ky) in enumerate(chips):
                part = dst[i].at[other, 2 * kx + ky]
                _rcopy(part, part, dsend.at[i, r], drecv.at[i, r], (x, y, 1 - c)).wait_recv()
        for cp in sends + passed:
            cp.wait_send()

    out_shape = [jax.ShapeDtypeStruct(b.shape, b.dtype) for b in bufs]
    return pl.pallas_call(body, name="gather_shards", in_specs=[ANY] * n, out_specs=[ANY] * n, out_shape=out_shape,
                          scratch_shapes=[pltpu.SemaphoreType.DMA((n, 3))] * 4,
                          input_output_aliases={i: i for i in range(n)})(*bufs)


def swap_halves(grads):
    n = len(grads)

    def body(*refs):
        src, dst = refs[:n], refs[n:2 * n]
        ssem, rsem = refs[2 * n:]
        x, y, c, _ = _place()
        cps = []
        for i in range(n):
            hr = grads[i].shape[1] // 2
            cps.append(_rcopy(src[i].at[:, pl.ds((1 - c) * hr, hr)], dst[i], ssem.at[i], rsem.at[i], (x, y, 1 - c)))
        for cp in cps:
            cp.start()
        for cp in cps:
            cp.wait()

    out_shape = [jax.ShapeDtypeStruct((g.shape[0], g.shape[1] // 2, g.shape[2]), g.dtype) for g in grads]
    return pl.pallas_call(body, name="swap_halves", in_specs=[ANY] * n, out_specs=[ANY] * n, out_shape=out_shape,
                          scratch_shapes=[pltpu.SemaphoreType.DMA((n,))] * 2)(*grads)


def scatter_shards(parts):
    n = len(parts)

    def body(*refs):
        src, dst = refs[:n], refs[n:2 * n]
        ssem, rsem = refs[2 * n:]
        x, y, c, chips = _place()
        cps = [_rcopy(src[i].at[2 * kx + ky], dst[i].at[r], ssem.at[i, r], rsem.at[i, r], (kx, ky, c))
               for i in range(n) for r, (kx, ky) in enumerate(chips)]
        for cp in cps:
            cp.start()
        for cp in cps:
            cp.wait()

    out_shape = [jax.ShapeDtypeStruct((3,) + p.shape[1:], p.dtype) for p in parts]
    return pl.pallas_call(body, name="scatter_shards", in_specs=[ANY] * n, out_specs=[ANY] * n, out_shape=out_shape,
                          scratch_shapes=[pltpu.SemaphoreType.DMA((n, 3))] * 2)(*parts)


def join_halves(bufs):
    n = len(bufs)

    def body(*refs):
        dst = refs[n:2 * n]
        ssem, rsem = refs[2 * n:]
        x, y, c, _ = _place()
        cps = []
        for i in range(n):
            hr = bufs[i].shape[1] // 2
            mine = dst[i].at[:, pl.ds(c * hr, hr)]
            cps.append(_rcopy(mine, mine, ssem.at[i], rsem.at[i], (x, y, 1 - c)))
        for cp in cps:
            cp.start()
        for cp in cps:
            cp.wait()

    out_shape = [jax.ShapeDtypeStruct(b.shape, b.dtype) for b in bufs]
    return pl.pallas_call(body, name="join_halves", in_specs=[ANY] * n, out_specs=[ANY] * n, out_shape=out_shape,
                          scratch_shapes=[pltpu.SemaphoreType.DMA((n,))] * 2,
                          input_output_aliases={i: i for i in range(n)})(*bufs)


def gather_blocks(v):
    def body(v_ref, out_ref, send_sems, recv_sems, local_sem):
        x, y, c, chips = _place()
        me, sibling = (x, y, c), (x, y, 1 - c)

        def blk(px, py, pc):
            return out_ref.at[4 * px + 2 * py + pc]

        def copy(k, block, to, src=None):
            return _rcopy(blk(*block) if src is None else src, blk(*block), send_sems.at[k], recv_sems.at[k], to)

        own = pltpu.make_async_copy(v_ref, blk(*me), local_sem)
        own.start()
        first = [copy(0, me, sibling, src=v_ref)]
        first += [copy(1 + r, me, (*chip, c), src=v_ref) for r, chip in enumerate(chips)]
        for cp in first:
            cp.start()
        passed = [copy(4 + r, (*chip, c), sibling) for r, chip in enumerate(chips)]
        for r, chip in enumerate(chips):
            copy(1 + r, (*chip, c), me).wait_recv()
            passed[r].start()
        copy(0, sibling, me).wait_recv()
        for r, chip in enumerate(chips):
            copy(4 + r, (*chip, 1 - c), me).wait_recv()
        for cp in first + passed:
            cp.wait_send()
        own.wait()

    return pl.pallas_call(body, name="gather_blocks", in_specs=[ANY], out_specs=ANY,
                          out_shape=jax.ShapeDtypeStruct((8,) + v.shape, v.dtype),
                          scratch_shapes=[pltpu.SemaphoreType.DMA((7,)), pltpu.SemaphoreType.DMA((7,)),
                                          pltpu.SemaphoreType.DMA])(v)


STREAM_BLOCK = 256 * 1024


def _stream_rows(rows, cols):
    base = 2 * SUBLANE if rows % (2 * SUBLANE) == 0 else SUBLANE
    return _tile(rows, max(base, STREAM_BLOCK // cols // base * base), base)


def _view3(a, lead):
    shape = a.shape[:lead] + (-1, a.shape[-1])
    return a.reshape(shape)


def sum_parts(name, terms, out, grid, where, into=None):
    n_skip = 0 if into is None else 1

    def body(w_ref, *refs):
        refs = refs[n_skip:]
        acc = refs[0][...].astype(F32)
        for t in refs[1:-1]:
            acc = acc + t[...].astype(F32)
        refs[-1][...] = acc.astype(refs[-1].dtype)

    grid_spec = pltpu.PrefetchScalarGridSpec(
        num_scalar_prefetch=1, grid=grid, in_specs=[ANY] * n_skip + [pl.BlockSpec(b, f) for _, b, f in terms],
        out_specs=pl.BlockSpec(out[2], out[3]))
    operands = ([] if into is None else [into]) + [t[0] for t in terms]
    return pl.pallas_call(body, name=name, grid_spec=grid_spec, out_shape=jax.ShapeDtypeStruct(out[0], out[1]),
                          input_output_aliases={} if into is None else {1: 0},
                          compiler_params=_params(("arbitrary",) * len(grid)))(where, *operands)


def cast_place(w, where):
    w3 = _view3(w, 1)
    L, rows, cols = w3.shape
    tr = _stream_rows(rows, cols)

    def body(w_ref, src, dst):
        dst[...] = src[...].astype(BF16)

    grid_spec = pltpu.PrefetchScalarGridSpec(
        num_scalar_prefetch=1, grid=(L, rows // tr),
        in_specs=[pl.BlockSpec((None, tr, cols), lambda l, r, wh: (l, r, 0))],
        out_specs=pl.BlockSpec((None, None, tr, cols), lambda l, r, wh: (l, wh[1], r, 0)))
    out = pl.pallas_call(body, name="cast_place", grid_spec=grid_spec,
                         out_shape=jax.ShapeDtypeStruct((L, N_CHIPS, rows, cols), BF16),
                         compiler_params=_params(("arbitrary", "arbitrary")))(where, w3)
    return out.reshape((L, N_CHIPS) + w.shape[1:])


def adamw(name, w, g, m, v):
    n, rows, cols = w.shape
    tr = _stream_rows(rows, cols)

    def body(w_ref, g_ref, m_ref, v_ref, d_ref, nm_ref, nv_ref):
        gv = g_ref[...]
        nm = ADAM_B1 * m_ref[...] + (1.0 - ADAM_B1) * gv
        nv = ADAM_B2 * v_ref[...] + (1.0 - ADAM_B2) * (gv * gv)
        m_hat = nm / (1.0 - ADAM_B1 ** ADAM_STEP)
        v_hat = nv / (1.0 - ADAM_B2 ** ADAM_STEP)
        d_ref[...] = -ADAM_LR * (m_hat / (jnp.sqrt(v_hat) + ADAM_EPS) + ADAM_WD * w_ref[...])
        nm_ref[...] = nm
        nv_ref[...] = nv

    blk = pl.BlockSpec((None, tr, cols), lambda l, r: (l, r, 0))
    shp = jax.ShapeDtypeStruct(w.shape, F32)
    return pl.pallas_call(body, name=name, grid=(n, rows // tr), in_specs=[blk] * 4, out_specs=[blk] * 3,
                          out_shape=[shp] * 3, compiler_params=_params(("arbitrary", "arbitrary")))(w, g, m, v)


COL_SHARDED = ("w_ada", "w_in", "w_proj_gla", "w_proj_s5", "w_proj_attn", "w_ffn_in")
ROW_SHARDED = ("w_s5_glu", "w_out", "w_ffn_out")
SHARDED = COL_SHARDED + ROW_SHARDED
GATE = ("w_gla_gate", "b_gla_gate")
WEIGHTS = ("c_ctx", "w_ada", "b_ada", "w_in", "w_gla_gate", "b_gla_gate", "gla_norm_w", "s5_lam_re", "s5_lam_im",
           "s5_log_dt", "s5_b_re", "s5_b_im", "s5_c_re", "s5_c_im", "s5_d", "w_s5_glu", "q_norm_w", "k_norm_w",
           "w_proj_gla", "w_proj_s5", "w_proj_attn", "w_out", "ln1_w", "ln1_b", "ln2_w", "ln2_b", "w_ffn_in",
           "w_ffn_out")
REPLICATED = tuple(n for n in WEIGHTS if n not in SHARDED + GATE)
SMALL = REPLICATED + GATE


def _full_weight(name, g):
    if name in ROW_SHARDED:
        return g.reshape(-1, g.shape[-1])
    return g.transpose(1, 0, 2).reshape(g.shape[1], -1)


def _pack(arrays):
    flat = jnp.concatenate([a.reshape(-1) for a in arrays])
    pad = (-flat.shape[0]) % (PACK_ROWS * LANE)
    return jnp.pad(flat, (0, pad)).reshape(-1, LANE)


def _unpack(packed, like):
    flat, out, off = packed.reshape(-1), [], 0
    for a in like:
        out.append(flat[off:off + a.size].reshape(a.shape))
        off += a.size
    return out


def kernel(x, c, ctx, c_ctx, w_ada, b_ada, w_in, w_gla_gate, b_gla_gate, gla_norm_w, s5_lam_re, s5_lam_im, s5_log_dt, s5_b_re, s5_b_im, s5_c_re, s5_c_im, s5_d, w_s5_glu, q_norm_w, k_norm_w, w_proj_gla, w_proj_s5, w_proj_attn, w_out, ln1_w, ln1_b, ln2_w, ln2_b, w_ffn_in, w_ffn_out, loss_target, m_c_ctx, m_w_ada, m_b_ada, m_w_in, m_w_gla_gate, m_b_gla_gate, m_gla_norm_w, m_s5_lam_re, m_s5_lam_im, m_s5_log_dt, m_s5_b_re, m_s5_b_im, m_s5_c_re, m_s5_c_im, m_s5_d, m_w_s5_glu, m_q_norm_w, m_k_norm_w, m_w_proj_gla, m_w_proj_s5, m_w_proj_attn, m_w_out, m_ln1_w, m_ln1_b, m_ln2_w, m_ln2_b, m_w_ffn_in, m_w_ffn_out, v_c_ctx, v_w_ada, v_b_ada, v_w_in, v_w_gla_gate, v_b_gla_gate, v_gla_norm_w, v_s5_lam_re, v_s5_lam_im, v_s5_log_dt, v_s5_b_re, v_s5_b_im, v_s5_c_re, v_s5_c_im, v_s5_d, v_w_s5_glu, v_q_norm_w, v_k_norm_w, v_w_proj_gla, v_w_proj_s5, v_w_proj_attn, v_w_out, v_ln1_w, v_ln1_b, v_ln2_w, v_ln2_b, v_w_ffn_in, v_w_ffn_out):
    args = dict(locals())
    w = {n: args[n] for n in WEIGHTS}
    m = {n: args["m_" + n] for n in WEIGHTS}
    v = {n: args["v_" + n] for n in WEIGHTS}
    L = w_in.shape[0]
    half = L // 2
    core = lax.axis_index("c").astype(jnp.int32)
    place = (2 * lax.axis_index("x") + lax.axis_index("y")).astype(jnp.int32)
    zero = jnp.zeros((), jnp.int32)
    where, by_core, by_place = jnp.stack([core, place]), jnp.stack([core, zero]), jnp.stack([zero, place])

    gathered = gather_shards([cast_place(w[n], by_place) for n in SHARDED])
    layers = {n: [g[l].astype(F32) for l in range(L)] for n, g in zip(SHARDED, gathered)}
    gate_shards = [w[n] for n in GATE]
    gate_blocks = gather_blocks(_pack(gate_shards))
    per_chip = [_unpack(gate_blocks[2 * j], gate_shards) for j in range(N_CHIPS)]
    small = {n: w[n] for n in REPLICATED}
    small.update({n: jnp.concatenate([per_chip[j][i] for j in range(N_CHIPS)], -1) for i, n in enumerate(GATE)})

    def loss_fn(x1, layers, small):
        full = {n: [_full_weight(n, g) for g in gs] for n, gs in layers.items()}
        return local_loss(x1, c[0], ctx[0], loss_target[0], {**full, **small})

    loss, (gx, g_sh, g_small) = jax.value_and_grad(loss_fn, argnums=(0, 1, 2))(x[0], layers, small)
    loss = lax.psum(loss, ("x", "y", "c"))

    parts = [g_sh[n][l] for n in SHARDED for l in range(L)]
    theirs = swap_halves(parts)
    chip_sums, tiles = [], []
    for p, t in zip(parts, theirs):
        hr, cols = t.shape[1:]
        tr = _stream_rows(hr, cols)
        tiles.append(tr)
        blk = (None, tr, cols)
        chip_sums.append(sum_parts(
            "sum_cores", [(p, blk, functools.partial(lambda nb, s, r, wh: (s, wh[0] * nb + r, 0), hr // tr)),
                          (t, blk, lambda s, r, wh: (s, r, 0))],
            (t.shape, BF16, blk, lambda s, r, wh: (s, r, 0)), (N_CHIPS, hr // tr), by_core))
    recv = scatter_shards(chip_sums)
    finals = []
    for i, n in enumerate(SHARDED):
        rows, cols = parts[i * L].shape[1:]
        buf = jnp.zeros((L, rows, cols), F32)
        for l in range(L):
            k = i * L + l
            tr = tiles[k]
            nb = rows // 2 // tr
            blk = (None, tr, cols)
            terms = [(parts[k], blk, functools.partial(lambda nb, r, wh: (wh[1], wh[0] * nb + r, 0), nb)),
                     (theirs[k], blk, lambda r, wh: (wh[1], r, 0))]
            terms += [(recv[k], blk, functools.partial(lambda j, r, wh: (j, r, 0), j)) for j in range(3)]
            buf = sum_parts("sum_chips", terms,
                            ((L, rows, cols), F32, blk, functools.partial(lambda l, nb, r, wh: (l, wh[0] * nb + r, 0), l, nb)),
                            (nb,), where, into=buf)
        finals.append(buf)
    grads = {n: g.reshape(w[n].shape) for n, g in zip(SHARDED, join_halves(finals))}

    small_parts = [g_small[n] for n in SMALL]
    blocks = gather_blocks(_pack(small_parts))
    prow = (None, PACK_ROWS, LANE)
    total = sum_parts("sum_devices",
                      [(blocks, prow, functools.partial(lambda k, r, wh: (k, r, 0), k)) for k in range(8)],
                      (blocks.shape[1:], F32, prow[1:], lambda r, wh: (r, 0)), (blocks.shape[1] // PACK_ROWS,),
                      jnp.zeros((2,), jnp.int32))
    grads.update(dict(zip(SMALL, _unpack(total, small_parts))))
    for n in GATE:
        width = w[n].shape[-1]
        grads[n] = lax.dynamic_slice_in_dim(grads[n], place * width, width, axis=-1)

    delta, new_m, new_v = {}, {}, {}
    for n in SHARDED:
        d3, m3, v3 = adamw("adamw", _view3(w[n], 1), _view3(grads[n], 1), _view3(m[n], 1), _view3(v[n], 1))
        delta[n], new_m[n], new_v[n] = (t.reshape(w[n].shape) for t in (d3, m3, v3))
    packed = [_pack([d[n] for n in SMALL])[None] for d in (w, grads, m, v)]
    d3, m3, v3 = adamw("adamw_small", *packed)
    like = [w[n] for n in SMALL]
    for dst, src in ((delta, d3), (new_m, m3), (new_v, v3)):
        dst.update(dict(zip(SMALL, _unpack(src[0], like))))

    return (loss, gx[None], *[grads[n] for n in WEIGHTS], *[delta[n] for n in WEIGHTS],
            *[new_m[n] for n in WEIGHTS], *[new_v[n] for n in WEIGHTS])
```

```python
import functools

import jax
import jax.numpy as jnp
from jax import lax
from jax.experimental import pallas as pl
from jax.experimental.pallas import tpu as pltpu

F32 = jnp.float32
BF16 = jnp.bfloat16

GRID_W = 64
GLA_HEADS = 4
GLA_DK = 128
GLA_DV = 256
GLA_GATE_RANK = 16
GLA_GATE_TAU = 16.0
GLA_CHUNK = 64
S5_GROUP = 16
S5_STATE = 64
ATTN_Q_HEADS = 8
ATTN_KV_HEADS = 2
ATTN_HEAD_DIM = 128
ROPE_THETA = 10000.0
DEPTH = 4
DN_ALPHA = (2 * DEPTH) ** 0.25
EPS = 1e-6
ADAM_LR = 0.001
ADAM_B1 = 0.9
ADAM_B2 = 0.999
ADAM_EPS = 1e-08
ADAM_WD = 0.01
ADAM_STEP = 10

LANE = 128
SUBLANE = 8
VMEM_LIMIT = 56 * 1024 * 1024
ADA_ROWS = 16
S5_CHUNK = 128
S5_BLOCK_GROUPS = 8
ROW_TILE = 256
COL_TILE = 512
PACK_ROWS = 512
SHARD_PAD = 1024
N_CHIPS = 4


def _params(sem, **kw):
    return pltpu.CompilerParams(dimension_semantics=sem, vmem_limit_bytes=VMEM_LIMIT, **kw)


def _tile(n, target, base):
    if n <= target:
        return n
    best = None
    for t in range(base, target + 1, base):
        if n % t == 0:
            best = t
    assert best is not None, (n, target, base)
    return best


_DIMS = {"nn": ((1,), (0,)), "nt": ((1,), (1,)), "tn": ((0,), (0,))}


def _dg(a, b, mode):
    return lax.dot_general(a.astype(BF16), b.astype(BF16), (_DIMS[mode], ((), ())), preferred_element_type=F32)


@functools.partial(jax.custom_vjp, nondiff_argnums=(2,))
def bdot(a, b, mode):
    return _dg(a, b, mode)


def _bdot_fwd(a, b, mode):
    return _dg(a, b, mode), (a, b)


def _bdot_bwd(mode, res, g):
    a, b = res
    if mode == "nn":
        return bdot(g, b, "nt").astype(a.dtype), bdot(a, g, "tn").astype(b.dtype)
    if mode == "nt":
        return bdot(g, b, "nn").astype(a.dtype), bdot(g, a, "tn").astype(b.dtype)
    return bdot(b, g, "nt").astype(a.dtype), bdot(a, g, "nn").astype(b.dtype)


bdot.defvjp(_bdot_fwd, _bdot_bwd)


MM_TILES = {"nn": (1152, 1024, 1664), "nt": (1152, 1024, 1664), "tn": (768, 2048, 1408)}


def _mm_tiles(mode, M, Kb, Nb):
    tm, tk, tn = MM_TILES[mode]
    return _tile(M, tm, SUBLANE), _tile(Kb, tk, LANE), _tile(Nb, tn, LANE)


def _mm_body(mode, reduce_axes):
    def body(p_ref, q_ref, o_ref, acc):
        first = functools.reduce(jnp.logical_and, [pl.program_id(ax) == 0 for ax, _ in reduce_axes])
        last = functools.reduce(jnp.logical_and, [pl.program_id(ax) == n - 1 for ax, n in reduce_axes])

        @pl.when(first)
        def _():
            acc[...] = jnp.zeros_like(acc)

        acc[...] += _dg(p_ref[...], q_ref[...], mode)

        @pl.when(last)
        def _():
            o_ref[...] = acc[...].astype(o_ref.dtype)

    return body


def _mm_nn(name, a, w, l, share):
    M = a.shape[0]
    _, B, Kb, Nb = w.shape
    tm, tk, tn = _mm_tiles("nn", M, Kb, Nb)
    nk, nn = Kb // tk, Nb // tn
    a_map = (lambda b, i, j, k: (i, k)) if share else (lambda b, i, j, k: (i, b * nk + k))
    return pl.pallas_call(
        _mm_body("nn", [(3, nk)]), name=name, grid=(B, M // tm, nn, nk),
        in_specs=[pl.BlockSpec((tm, tk), a_map),
                  pl.BlockSpec((None, None, tk, tn), lambda b, i, j, k: (l, b, k, j))],
        out_specs=pl.BlockSpec((tm, tn), lambda b, i, j, k: (i, b * nn + j)),
        out_shape=jax.ShapeDtypeStruct((M, B * Nb), F32),
        scratch_shapes=[pltpu.VMEM((tm, tn), F32)],
        compiler_params=_params(("arbitrary",) * 4))(a, w)


def _mm_nt(name, g, w, l, share):
    M = g.shape[0]
    _, B, Kb, Nb = w.shape
    tm, tk, tn = _mm_tiles("nt", M, Kb, Nb)
    nk, nn = Kb // tk, Nb // tn
    if share:
        grid, red = (M // tm, nk, B, nn), [(2, B), (3, nn)]
        g_map, w_map = (lambda i, k, b, n: (i, b * nn + n)), (lambda i, k, b, n: (l, b, k, n))
        o_map, width = (lambda i, k, b, n: (i, k)), Kb
    else:
        grid, red = (B, M // tm, nk, nn), [(3, nn)]
        g_map, w_map = (lambda b, i, k, n: (i, b * nn + n)), (lambda b, i, k, n: (l, b, k, n))
        o_map, width = (lambda b, i, k, n: (i, b * nk + k)), B * Kb
    return pl.pallas_call(
        _mm_body("nt", red), name=name, grid=grid,
        in_specs=[pl.BlockSpec((tm, tn), g_map), pl.BlockSpec((None, None, tk, tn), w_map)],
        out_specs=pl.BlockSpec((tm, tk), o_map),
        out_shape=jax.ShapeDtypeStruct((M, width), F32),
        scratch_shapes=[pltpu.VMEM((tm, tk), F32)],
        compiler_params=_params(("arbitrary",) * 4))(g, w)


def _mm_tn(name, a, g, B, share, dtype):
    M = a.shape[0]
    Kb, Nb = a.shape[1] // (1 if share else B), g.shape[1] // B
    tm, tk, tn = _mm_tiles("tn", M, Kb, Nb)
    nk, nn = Kb // tk, Nb // tn
    a_map = (lambda b, k, j, m: (m, k)) if share else (lambda b, k, j, m: (m, b * nk + k))
    return pl.pallas_call(
        _mm_body("tn", [(3, M // tm)]), name=name, grid=(B, nk, nn, M // tm),
        in_specs=[pl.BlockSpec((tm, tk), a_map),
                  pl.BlockSpec((tm, tn), lambda b, k, j, m: (m, b * nn + j))],
        out_specs=pl.BlockSpec((None, tk, tn), lambda b, k, j, m: (b, k, j)),
        out_shape=jax.ShapeDtypeStruct((B, Kb, Nb), dtype),
        scratch_shapes=[pltpu.VMEM((tk, tn), F32)],
        compiler_params=_params(("arbitrary",) * 4))(a, g)


def mm(name, a, w):
    w3 = w if w.ndim == 3 else w[None]

    @jax.custom_vjp
    def op(a, w3):
        return _mm_nn(name + "_fwd", a, w3.astype(BF16)[None], 0, False)

    def fwd(a, w3):
        wb = w3.astype(BF16)[None]
        return _mm_nn(name + "_fwd", a, wb, 0, False), (a, wb)

    def bwd(res, g):
        a, wb = res
        return _mm_nt(name + "_dx", g, wb, 0, False), _mm_tn(name + "_dw", a, g, wb.shape[1], False, F32)

    op.defvjp(fwd, bwd)
    return op(a, w3)


def mm_gathered(name, a, w, l, grads, key):
    @jax.custom_vjp
    def op(a):
        return _mm_nn(name + "_fwd", a, w, l, True)

    def fwd(a):
        return _mm_nn(name + "_fwd", a, w, l, True), (a,)

    def bwd(res, g):
        (a,) = res
        grads[key] = _mm_tn(name + "_dw", a, g, w.shape[1], True, BF16)
        return (_mm_nt(name + "_dx", g, w, l, True),)

    op.defvjp(fwd, bwd)
    return op(a)


def _spec_shape(spec, G, T):
    k = spec[0]
    if k == "row":
        return (T, spec[1])
    if k == "rowg":
        return (T, (G // spec[2]) * spec[1])
    if k == "bc":
        return (spec[1], spec[2])
    return (spec[1], (G // spec[3]) * spec[2])


def _spec_block(spec, tm, rmap):
    k = spec[0]
    if k == "row":
        return pl.BlockSpec((tm, spec[1]), lambda g, r: (rmap(r), 0))
    if k == "rowg":
        d = spec[2]
        return pl.BlockSpec((tm, spec[1]), lambda g, r: (rmap(r), g // d))
    if k == "bc":
        return pl.BlockSpec((spec[1], spec[2]), lambda g, r: (0, 0))
    d = spec[3]
    return pl.BlockSpec((spec[1], spec[2]), lambda g, r: (0, g // d))


def block_op(name, f, in_specs, out_specs, G, T, tm, diff, carry=(), row0=False, order=None):
    n_in, n_out, n_c = len(in_specs), len(out_specs), len(carry)
    n_steps = T // tm
    assert T % tm == 0
    order = order or (lambda s: s)
    diff_idx = [i for i in range(n_in) if diff[i]]
    for i in diff_idx:
        assert in_specs[i][0] != "row" or G == 1
        assert in_specs[i][0] != "rowg" or in_specs[i][2] == 1
    out_shapes = [jax.ShapeDtypeStruct(_spec_shape(s, G, T), F32) for s in out_specs]
    save_shapes = [jax.ShapeDtypeStruct((n_steps, a, G * b), F32) for a, b in carry]
    sem = ("arbitrary", "arbitrary")

    def call_f(r_idx, cvals, vals):
        args = list(vals)
        if n_c:
            args = [tuple(cvals)] + args
        if row0:
            args = [r_idx * tm] + args
        return f(*args)

    def fwd_body(*refs):
        in_refs = refs[:n_in]
        out_refs = refs[n_in:n_in + n_out]
        save_refs = refs[n_in + n_out:n_in + n_out + n_c]
        c_refs = refs[n_in + n_out + n_c:]
        r = pl.program_id(1)
        if n_c:
            @pl.when(r == 0)
            def _():
                for c in c_refs:
                    c[...] = jnp.zeros_like(c)

            cvals = [c[...] for c in c_refs]
            for s, v in zip(save_refs, cvals):
                s[...] = v
            new_c, outs = call_f(r, cvals, [x[...] for x in in_refs])
            for c, v in zip(c_refs, new_c):
                c[...] = v
        else:
            outs = call_f(r, (), [x[...] for x in in_refs])
        for o, v in zip(out_refs, outs):
            o[...] = v.astype(F32)

    def fwd_call(*arrays):
        res = pl.pallas_call(
            fwd_body, name=name + "_fwd", grid=(G, n_steps),
            in_specs=[_spec_block(s, tm, order) for s in in_specs],
            out_specs=[_spec_block(s, tm, order) for s in out_specs]
            + [pl.BlockSpec((None, a, b), lambda g, r: (r, 0, g)) for a, b in carry],
            out_shape=out_shapes + save_shapes,
            scratch_shapes=[pltpu.VMEM((a, b), F32) for a, b in carry],
            compiler_params=_params(sem))(*arrays)
        return tuple(res)

    def bwd_body(*refs):
        in_refs = refs[:n_in]
        save_refs = refs[n_in:n_in + n_c]
        ct_refs = refs[n_in + n_c:n_in + n_c + n_out]
        g_refs = refs[n_in + n_c + n_out:n_in + n_c + n_out + len(diff_idx)]
        dc_refs = refs[n_in + n_c + n_out + len(diff_idx):]
        g = pl.program_id(0)
        r = pl.program_id(1)
        vals = [x[...] for x in in_refs]
        if n_c:
            @pl.when(r == 0)
            def _():
                for d in dc_refs:
                    d[...] = jnp.zeros_like(d)

        def fun(cvals, dvals):
            full = list(vals)
            for i, v in zip(diff_idx, dvals):
                full[i] = v
            return call_f(n_steps - 1 - r if n_c else r, cvals, full)

        _, vjp = jax.vjp(fun, tuple(s[...] for s in save_refs), tuple(vals[i] for i in diff_idx))
        cts = tuple(c[...] for c in ct_refs)
        if n_c:
            cts = (tuple(d[...] for d in dc_refs), cts)
        dcin, dvals = vjp(cts)
        for d, v in zip(dc_refs, dcin):
            d[...] = v
        for gref, i, v in zip(g_refs, diff_idx, dvals):
            spec = in_specs[i]
            if spec[0] in ("row", "rowg"):
                gref[...] = v
            else:
                first = (r == 0) & ((g == 0) if spec[0] == "bc" else (g % spec[3] == 0))

                @pl.when(first)
                def _(gref=gref, v=v):
                    gref[...] = v

                @pl.when(jnp.logical_not(first))
                def _(gref=gref, v=v):
                    gref[...] += v

    def bwd_call(arrays, saved, cts):
        rmap = (lambda r: order(n_steps - 1 - r)) if n_c else (lambda r: r)
        res = pl.pallas_call(
            bwd_body, name=name + "_bwd", grid=(G, n_steps),
            in_specs=[_spec_block(s, tm, rmap) for s in in_specs]
            + [pl.BlockSpec((None, a, b), lambda g, r: (n_steps - 1 - r, 0, g)) for a, b in carry]
            + [_spec_block(s, tm, rmap) for s in out_specs],
            out_specs=[_spec_block(in_specs[i], tm, rmap) for i in diff_idx],
            out_shape=[jax.ShapeDtypeStruct(_spec_shape(in_specs[i], G, T), F32) for i in diff_idx],
            scratch_shapes=[pltpu.VMEM((a, b), F32) for a, b in carry],
            compiler_params=_params(sem))(*arrays, *saved, *cts)
        return tuple(res)

    @jax.custom_vjp
    def op(*arrays):
        return fwd_call(*arrays)[:n_out]

    def op_fwd(*arrays):
        res = fwd_call(*arrays)
        return res[:n_out], (arrays, res[n_out:])

    def op_bwd(res, cts):
        arrays, saved = res
        grads = bwd_call(arrays, saved, cts)
        out = [jnp.zeros_like(a) for a in arrays]
        for i, gval in zip(diff_idx, grads):
            out[i] = gval
        return tuple(out)

    op.defvjp(op_fwd, op_bwd)
    return op


def _rows(n, m):
    return lax.broadcasted_iota(jnp.int32, (n, m), 0)


def _ctx_select(row0, tm, n_ctx, v_lat, v_ctx):
    if n_ctx == 0:
        return v_lat
    is_ctx = (row0 + _rows(tm, 1)) < n_ctx
    return jnp.where(is_ctx, v_ctx, v_lat)


def _silu(x):
    return x * jax.nn.sigmoid(x)


def _f_modulate(tm, n_ctx):
    def f(row0, x, sh_l, sh_c, sc_l, sc_c):
        sh = _ctx_select(row0, tm, n_ctx, sh_l, sh_c)
        sc = _ctx_select(row0, tm, n_ctx, sc_l, sc_c)
        return (x * (1 + sc) + sh,)
    return f


def _f_postnorm(tm, n_ctx):
    def f(row0, x, y, g_l, g_c, w, b):
        z = DN_ALPHA * x + _ctx_select(row0, tm, n_ctx, g_l, g_c) * y
        mu = jnp.mean(z, -1, keepdims=True)
        zc = z - mu
        var = jnp.mean(zc * zc, -1, keepdims=True)
        return (zc * lax.rsqrt(var + EPS) * w + b,)
    return f


def _log_sigmoid(x):
    return -(jnp.maximum(-x, 0.0) + jnp.log1p(jnp.exp(-jnp.abs(x))))


def _f_gla_prep(glr, wg0, wg1, b0, b1):
    return (_log_sigmoid(bdot(glr, wg0, "nn") + b0) / GLA_GATE_TAU,
            _log_sigmoid(bdot(glr, wg1, "nn") + b1) / GLA_GATE_TAU)


def _f_gla_step(rev):
    def f(carry, q, k, v, la):
        (st,) = carry
        n = q.shape[0]
        cols = lax.broadcasted_iota(jnp.int32, (n, n), 1)
        tri = (_rows(n, n) <= cols) if rev else (_rows(n, n) >= cols)
        b = jnp.dot(tri.astype(F32), la, precision=lax.Precision.HIGHEST)
        qe = q * (GLA_DK ** -0.5) * jnp.exp(b)
        ke = k * jnp.exp(-b)
        att = jnp.where(tri, bdot(qe, ke, "nt"), 0.0)
        o = bdot(att, v, "nn") + bdot(qe, st, "nt")
        end = 0 if rev else n - 1
        b_last = jnp.sum(jnp.where(_rows(n, 1) == end, b, 0.0), axis=0, keepdims=True)
        kd = k * jnp.exp(b_last - b)
        st = st * jnp.exp(b_last) + bdot(v, kd, "tn")
        return (st,), (o,)
    return f


def _f_gla_norm(o0, o1, gr, w):
    o = o0 + o1
    mu = jnp.mean(o, -1, keepdims=True)
    oc = o - mu
    var = jnp.mean(oc * oc, -1, keepdims=True)
    return (oc * lax.rsqrt(var + EPS) * w * _silu(gr),)


@functools.partial(jax.custom_vjp, nondiff_argnums=(1, 2))
def _shift_rows(x, d, up):
    n = x.shape[0]
    rows = _rows(n, 1)
    if up:
        return jnp.where(rows < n - d, pltpu.roll(x, n - d, 0), 0.0)
    return jnp.where(rows >= d, pltpu.roll(x, d, 0), 0.0)


def _shift_fwd(x, d, up):
    return _shift_rows(x, d, up), None


def _shift_bwd(d, up, _, g):
    return (_shift_rows(g, d, not up),)


_shift_rows.defvjp(_shift_fwd, _shift_bwd)


def _f_s5_step(rev):
    def f(carry, bur, bui, lam_re, lam_im, log_dt):
        cr, ci = carry
        n = bur.shape[0]
        dt = jnp.exp(log_dt)
        mag = jnp.exp(lam_re * dt)
        ar, ai = mag * jnp.cos(lam_im * dt), mag * jnp.sin(lam_im * dt)
        den = lam_re * lam_re + lam_im * lam_im
        nr, ni = ar - 1, ai
        kr = (nr * lam_re + ni * lam_im) / den
        ki = (ni * lam_re - nr * lam_im) / den
        first = _rows(n, 1) == (n - 1 if rev else 0)
        xr = kr * bur - ki * bui + jnp.where(first, ar * cr - ai * ci, 0.0)
        xi = kr * bui + ki * bur + jnp.where(first, ar * ci + ai * cr, 0.0)
        pr, pi = ar, ai
        d = 1
        while d < n:
            sr, si = _shift_rows(xr, d, rev), _shift_rows(xi, d, rev)
            xr, xi = xr + pr * sr - pi * si, xi + pr * si + pi * sr
            pr, pi = pr * pr - pi * pi, 2 * pr * pi
            d *= 2
        last = _rows(n, 1) == (0 if rev else n - 1)
        cr = jnp.sum(jnp.where(last, xr, 0.0), axis=0, keepdims=True)
        ci = jnp.sum(jnp.where(last, xi, 0.0), axis=0, keepdims=True)
        return (cr, ci), (xr, xi)
    return f


def _f_s5_post(su, dskip, y0r, y0i, y1r, y1i):
    return (jax.nn.gelu(su * dskip + y0r - y0i + y1r - y1i),)


def _f_s5_glu(y, t):
    return (y * jax.nn.sigmoid(t),)


def _swap_pairs(x):
    lane = lax.broadcasted_iota(jnp.int32, x.shape, 1)
    return jnp.where(lane % 2 == 0, pltpu.roll(x, x.shape[1] - 1, 1), pltpu.roll(x, 1, 1))


@jax.custom_vjp
def _rope(x, cos2, sin2):
    return x * cos2 + _swap_pairs(x) * sin2


def _rope_fwd(x, cos2, sin2):
    return _rope(x, cos2, sin2), (cos2, sin2)


def _rope_bwd(res, g):
    cos2, sin2 = res
    return g * cos2 + _swap_pairs(g * sin2), jnp.zeros_like(cos2), jnp.zeros_like(sin2)


_rope.defvjp(_rope_fwd, _rope_bwd)


def _f_qk_norm_rope(x, cos2, sin2, w):
    xn = x * lax.rsqrt(jnp.mean(x * x, -1, keepdims=True) + EPS) * w
    return (_rope(xn, cos2, sin2),)


def _f_attn(q, k, v):
    s = bdot(q, k, "nt") * (ATTN_HEAD_DIM ** -0.5)
    e = jnp.exp(s - jnp.max(s, -1, keepdims=True))
    p = e / jnp.sum(e, -1, keepdims=True)
    return (bdot(p, v, "nn"),)


def _f_merge(ga, gb, gc, pa, pb, pc):
    return (jax.nn.sigmoid(ga) * pa + jax.nn.sigmoid(gb) * pb + jax.nn.sigmoid(gc) * pc,)


def _f_swiglu(a, b):
    return (_silu(a) * b,)


def _f_silu(x):
    return (_silu(x),)


def _f_add_bias(x, b):
    return (x + b,)


def _scan_order(rev, n_ctx, T, tm):
    if not rev:
        return None
    nc, n = n_ctx // tm, T // tm
    return lambda s: jnp.where(s < nc, nc - 1 - s, n - 1 - (s - nc))


def modulate(name, x, sh, sc, n_ctx):
    T, D = x.shape
    tm = _tile(T, ROW_TILE, SUBLANE)
    cw = _tile(D, COL_TILE, LANE)
    col, vec = ("rowg", cw, 1), ("bcg", 1, cw, 1)
    op = block_op(name, _f_modulate(tm, n_ctx), [col, vec, vec, vec, vec], [col], D // cw, T, tm,
                  [True] * 5, row0=True)
    return op(x, sh[0], sh[1], sc[0], sc[1])[0]


def postnorm(name, x, y, g, w, b, n_ctx):
    T, D = x.shape
    tm = _tile(T, ROW_TILE, SUBLANE)
    vec = ("bc", 1, D)
    op = block_op(name, _f_postnorm(tm, n_ctx), [("row", D), ("row", D), vec, vec, vec, vec], [("row", D)], 1, T,
                  tm, [True] * 6, row0=True)
    return op(x, y, g[0], g[1], w, b)[0]


def rowwise(name, f, arrays, n_out=1):
    T, w = arrays[0].shape
    tm = _tile(T, ROW_TILE, SUBLANE)
    cw = _tile(w, COL_TILE, LANE)
    col = ("rowg", cw, 1)
    op = block_op(name, f, [col] * len(arrays), [col] * n_out, w // cw, T, tm, [True] * len(arrays))
    return op(*arrays)


def gla_prep(name, glr, wg, bg):
    T = glr.shape[0]
    qk = wg.shape[-1]
    tm = _tile(T, ROW_TILE, SUBLANE)
    op = block_op(name, _f_gla_prep, [("row", LANE), ("bc", LANE, qk), ("bc", LANE, qk), ("bc", 1, qk), ("bc", 1, qk)],
                  [("row", qk), ("row", qk)], 1, T, tm, [True] * 5)
    return op(glr, wg[0], wg[1], bg[0], bg[1])


def gla_scan(name, q, k, v, la, rev, n_ctx):
    T = q.shape[0]
    op = block_op(name, _f_gla_step(rev), [("rowg", GLA_DK, 1), ("rowg", GLA_DK, 1), ("rowg", GLA_DV, 1), ("rowg", GLA_DK, 1)],
                  [("rowg", GLA_DV, 1)], GLA_HEADS, T, GLA_CHUNK, [True] * 4, carry=[(GLA_DV, GLA_DK)],
                  order=_scan_order(rev, n_ctx, T, GLA_CHUNK))
    return op(q, k, v, la)[0]


def gla_norm(name, o0, o1, gr, w):
    T = o0.shape[0]
    tm = _tile(T, ROW_TILE, SUBLANE)
    hd = ("rowg", GLA_DV, 1)
    op = block_op(name, _f_gla_norm, [hd, hd, hd, ("bcg", 1, GLA_DV, 1)], [hd], GLA_HEADS, T, tm, [True] * 4)
    return op(o0, o1, gr, w)[0]


def s5_scan(name, bur, bui, lam_re, lam_im, log_dt, rev, n_ctx):
    T, S = bur.shape
    cols = _tile(S, 768, LANE)
    G = S // cols
    col, par = ("rowg", cols, 1), ("bcg", 1, cols, 1)
    op = block_op(name, _f_s5_step(rev), [col, col, par, par, par], [col, col], G, T, S5_CHUNK, [True] * 5,
                  carry=[(1, cols), (1, cols)], order=_scan_order(rev, n_ctx, T, S5_CHUNK))
    return op(bur, bui, lam_re, lam_im, log_dt)


def qk_norm_rope(name, x, cos2, sin2, w):
    T = x.shape[0]
    G = x.shape[1] // ATTN_HEAD_DIM
    tm = _tile(T, ROW_TILE, SUBLANE)
    hd = ("rowg", ATTN_HEAD_DIM, 1)
    op = block_op(name, _f_qk_norm_rope, [hd, ("row", ATTN_HEAD_DIM), ("row", ATTN_HEAD_DIM), ("bc", 1, ATTN_HEAD_DIM)],
                  [hd], G, T, tm, [True, False, False, True])
    return op(x, cos2, sin2, w)[0]


def attention(name, q, k, v):
    T, Tk = q.shape[0], k.shape[0]
    tm = _tile(T, ROW_TILE, SUBLANE)
    grp = ATTN_Q_HEADS // ATTN_KV_HEADS
    kv = ("bcg", Tk, ATTN_HEAD_DIM, grp)
    op = block_op(name, _f_attn, [("rowg", ATTN_HEAD_DIM, 1), kv, kv], [("rowg", ATTN_HEAD_DIM, 1)], ATTN_Q_HEADS, T,
                  tm, [True] * 3)
    return op(q, k, v)[0]


def sq_loss(name, y, t):
    T, D = y.shape
    tm = _tile(T, ROW_TILE, SUBLANE)

    def fwd_body(y_ref, t_ref, o_ref):
        e = y_ref[...] - t_ref[...]
        part = jnp.sum(jnp.sum(e * e, -1, keepdims=True), 0, keepdims=True) * (0.5 / D)

        @pl.when(pl.program_id(0) == 0)
        def _():
            o_ref[...] = jnp.zeros_like(o_ref)

        o_ref[...] += part * jnp.ones((1, LANE), F32)

    def bwd_body(y_ref, t_ref, g_ref, o_ref):
        o_ref[...] = (y_ref[...] - t_ref[...]) * (g_ref[:, 0:1] / D)

    row = pl.BlockSpec((tm, D), lambda r: (r, 0))
    one = pl.BlockSpec((1, LANE), lambda r: (0, 0))

    def fwd_call(y, t):
        return pl.pallas_call(fwd_body, name=name + "_fwd", grid=(T // tm,), in_specs=[row, row], out_specs=one,
                              out_shape=jax.ShapeDtypeStruct((1, LANE), F32), compiler_params=_params(("arbitrary",)))(y, t)

    @jax.custom_vjp
    def op(y, t):
        return fwd_call(y, t)[0, 0]

    def op_fwd(y, t):
        return fwd_call(y, t)[0, 0], (y, t)

    def op_bwd(res, g):
        y, t = res
        gy = pl.pallas_call(bwd_body, name=name + "_bwd", grid=(T // tm,), in_specs=[row, row, one], out_specs=row,
                            out_shape=jax.ShapeDtypeStruct((T, D), F32), compiler_params=_params(("arbitrary",)))(
                                y, t, jnp.full((1, LANE), g, F32))
        return gy, jnp.zeros_like(t)

    op.defvjp(op_fwd, op_bwd)
    return op(y, t)


def _rope_tables(n_ctx, n_lat):
    n_rows = n_lat // GRID_W
    rows = jnp.repeat(jnp.arange(n_rows), GRID_W).astype(F32)
    cols = jnp.tile(jnp.arange(GRID_W), n_rows).astype(F32)
    n_freq = ATTN_HEAD_DIM // 4
    inv = ROPE_THETA ** (-jnp.arange(n_freq, dtype=F32) / n_freq)
    ang = jnp.concatenate([rows[:, None] * inv, cols[:, None] * inv], -1)
    cos2 = jnp.repeat(jnp.cos(ang), 2, axis=-1)
    sin2 = jnp.stack([-jnp.sin(ang), jnp.sin(ang)], -1).reshape(n_lat, ATTN_HEAD_DIM)
    cos2 = jnp.concatenate([jnp.ones((n_ctx, ATTN_HEAD_DIM), F32), cos2], 0)
    sin2 = jnp.concatenate([jnp.zeros((n_ctx, ATTN_HEAD_DIM), F32), sin2], 0)
    return cos2, sin2


def _in_layout(D, s5_width):
    qk, gv = GLA_HEADS * GLA_DK, GLA_HEADS * GLA_DV
    aq, akv = ATTN_Q_HEADS * ATTN_HEAD_DIM, ATTN_KV_HEADS * ATTN_HEAD_DIM
    widths = [("gq", qk), ("gk", qk), ("gv", gv), ("gr", gv), ("glr", GLA_GATE_RANK), ("su", s5_width), ("aq", aq),
              ("ak", akv), ("av", akv), ("ga", D), ("gb", D), ("gc", D)]
    off, out = 0, {}
    for n, w in widths:
        out[n] = (off, w)
        off += w
    return out, off


def _padded_width(width):
    return width if width % LANE == 0 else -(-width // SHARD_PAD) * SHARD_PAD


def _shard_cols(z, off, width, shard, padded):
    parts = []
    for j in range(N_CHIPS):
        lo, hi = max(off, j * shard), min(off + width, (j + 1) * shard)
        if lo < hi:
            parts.append(z[:, j * padded + lo - j * shard:j * padded + hi - j * shard])
    return parts[0] if len(parts) == 1 else jnp.concatenate(parts, 1)


def _s5_in_blocks(b):
    G, P, C = b.shape
    nb, bg = G // S5_BLOCK_GROUPS, S5_BLOCK_GROUPS
    t = b.reshape(nb, bg, P, C).transpose(0, 1, 3, 2)
    return jnp.einsum("bgcp,gh->bgchp", t, jnp.eye(bg, dtype=F32)).reshape(nb, bg * C, bg * P)


def _s5_out_blocks(c):
    G, C, P = c.shape
    nb, bg = G // S5_BLOCK_GROUPS, S5_BLOCK_GROUPS
    t = c.reshape(nb, bg, C, P).transpose(0, 1, 3, 2)
    return jnp.einsum("bgpc,gh->bgphc", t, jnp.eye(bg, dtype=F32)).reshape(nb, bg * P, bg * C)


def _layer(keep_ctx, xa, n_ctx, mod, p, big, cos2, sin2):
    T, D = xa.shape
    S = p["s5_d"].shape[-1]
    lay, width = _in_layout(D, S)
    lo = 0 if keep_ctx else n_ctx
    ctx_rows = n_ctx if keep_ctx else 0

    h = modulate("modulate1", xa, mod["sh1"], mod["sc1"], n_ctx)
    z = big["w_in"]("in_proj", h)
    shard = width // N_CHIPS
    zz = {n: _shard_cols(z, o, w, shard, z.shape[1] // N_CHIPS) for n, (o, w) in lay.items()}

    wg = jnp.pad(p["w_gla_gate"], ((0, 0), (0, LANE - GLA_GATE_RANK), (0, 0)))
    glr = jnp.pad(zz["glr"], ((0, 0), (0, LANE - GLA_GATE_RANK)))
    la0, la1 = gla_prep("gla_prep", glr, wg, p["b_gla_gate"][:, None, :])
    o0 = gla_scan("gla_scan", zz["gq"], zz["gk"], zz["gv"], la0, False, n_ctx)
    o1 = gla_scan("gla_scan_rev", zz["gq"], zz["gk"], zz["gv"], la1, True, n_ctx)
    o_gla = gla_norm("gla_norm", o0[lo:], o1[lo:], zz["gr"][lo:], p["gla_norm_w"][None, :])

    su = zz["su"]
    bur = mm("s5_in_re", su, _s5_in_blocks(p["s5_b_re"]))
    bui = mm("s5_in_im", su, _s5_in_blocks(p["s5_b_im"]))
    ys = []
    for d in range(2):
        row = lambda t: t.reshape(1, -1)
        ldt = jnp.repeat(p["s5_log_dt"][d], S5_STATE)
        sr, si = s5_scan("s5_scan_rev" if d else "s5_scan", bur, bui, row(p["s5_lam_re"][d]),
                         row(p["s5_lam_im"][d]), row(ldt), d == 1, n_ctx)
        ys.append(mm("s5_out_re", sr[lo:], _s5_out_blocks(p["s5_c_re"][d])))
        ys.append(mm("s5_out_im", si[lo:], _s5_out_blocks(p["s5_c_im"][d])))
    T2 = T - lo
    tm = _tile(T2, ROW_TILE, SUBLANE)
    post = block_op("s5_post", _f_s5_post, [("row", S), ("bc", 1, S)] + [("row", S)] * 4, [("row", S)], 1, T2, tm,
                    [True] * 6)
    yg = post(su[lo:], p["s5_d"][None, :], *ys)[0]
    o_s5 = rowwise("s5_glu", _f_s5_glu, [yg, big["w_s5_glu"]("s5_glu_proj", yg)])[0]

    qn = qk_norm_rope("q_norm_rope", zz["aq"], cos2, sin2, p["q_norm_w"][None, :])
    kn = qk_norm_rope("k_norm_rope", zz["ak"], cos2, sin2, p["k_norm_w"][None, :])
    o_attn = attention("attn_lat", qn[n_ctx:], kn, zz["av"])
    if keep_ctx:
        o_c = attention("attn_ctx", qn[:n_ctx], kn[:n_ctx], zz["av"][:n_ctx])
        o_attn = jnp.concatenate([o_c, o_attn], 0)

    merged = rowwise("merge", _f_merge, [zz["ga"][lo:], zz["gb"][lo:], zz["gc"][lo:],
                                         big["w_proj_gla"]("proj_gla", o_gla),
                                         big["w_proj_s5"]("proj_s5", o_s5),
                                         big["w_proj_attn"]("proj_attn", o_attn)])[0]
    mix = big["w_out"]("out_proj", merged)
    x1 = postnorm("postnorm1", xa[lo:], mix, mod["g1"], p["ln1_w"][None, :], p["ln1_b"][None, :], ctx_rows)
    h2 = modulate("modulate2", x1, mod["sh2"], mod["sc2"], ctx_rows)
    u = big["w_ffn_in"]("ffn_in", h2)
    F = u.shape[1] // 2
    act = rowwise("swiglu", _f_swiglu, [u[:, :F], u[:, F:]])[0]
    f = big["w_ffn_out"]("ffn_out", act)
    return postnorm("postnorm2", x1, f, mod["g2"], p["ln2_w"][None, :], p["ln2_b"][None, :], ctx_rows)


def local_loss(x, c, ctx, target, small, gathered, grads):
    n_lat, D = x.shape
    n_ctx = ctx.shape[0]
    cos2, sin2 = _rope_tables(n_ctx, n_lat)
    cc = jnp.concatenate([c[None, :], small["c_ctx"][None, :], jnp.zeros((ADA_ROWS - 2, D), F32)], 0)
    silu_cc = rowwise("silu_cond", _f_silu, [cc])[0]
    xa = jnp.concatenate([ctx, x], 0)
    depth = gathered["w_in"].shape[0]
    for l in range(depth):
        p = {n: v[l] for n, v in small.items() if n != "c_ctx"}
        big = {n: functools.partial(lambda n, l, name, a: mm_gathered(name, a, gathered[n], l, grads, (n, l)), n, l)
               for n in gathered}
        m = big["w_ada"]("ada_proj", silu_cc)
        m = block_op("ada_bias", _f_add_bias, [("row", 6 * D), ("bc", 1, 6 * D)], [("row", 6 * D)], 1, ADA_ROWS,
                     ADA_ROWS, [True, True])(m, p["b_ada"][None, :])[0]
        names = ["sh1", "sc1", "g1", "sh2", "sc2", "g2"]
        mod = {n: (m[0:1, i * D:(i + 1) * D], m[1:2, i * D:(i + 1) * D]) for i, n in enumerate(names)}
        xa = _layer(l < depth - 1, xa, n_ctx, mod, p, big, cos2, sin2)
    return sq_loss("loss", xa, target)


MESH = pl.DeviceIdType.MESH
ANY = pl.BlockSpec(memory_space=pl.ANY)


def _place():
    x, y, c = lax.axis_index("x"), lax.axis_index("y"), lax.axis_index("c")
    chips = [(1 - x, y), (x, 1 - y), (1 - x, 1 - y)]
    return x, y, c, chips


def _rcopy(src, dst, ssem, rsem, to):
    return pltpu.make_async_remote_copy(src_ref=src, dst_ref=dst, send_sem=ssem, recv_sem=rsem, device_id=to,
                                        device_id_type=MESH)


def gather_shards(bufs):
    n = len(bufs)
    L = bufs[0].shape[0]
    half = L // 2

    def body(*refs):
        dst = refs[n:2 * n]
        isend, irecv, dsend, drecv = refs[2 * n:]
        x, y, c, chips = _place()
        j = 2 * x + y
        mine, other = pl.ds(c * half, half), pl.ds((1 - c) * half, half)
        sends = [_rcopy(dst[i].at[mine, j], dst[i].at[mine, j], isend.at[i, r], irecv.at[i, r], (kx, ky, c))
                 for i in range(n) for r, (kx, ky) in enumerate(chips)]
        for cp in sends:
            cp.start()
        passed = []
        for i in range(n):
            for r, (kx, ky) in enumerate(chips):
                part = dst[i].at[mine, 2 * kx + ky]
                _rcopy(part, part, isend.at[i, r], irecv.at[i, r], (kx, ky, c)).wait_recv()
                fw = _rcopy(part, part, dsend.at[i, r], drecv.at[i, r], (x, y, 1 - c))
                fw.start()
                passed.append(fw)
        for i in range(n):
            for r, (kx, ky) in enumerate(chips):
                part = dst[i].at[other, 2 * kx + ky]
                _rcopy(part, part, dsend.at[i, r], drecv.at[i, r], (x, y, 1 - c)).wait_recv()
        for cp in sends + passed:
            cp.wait_send()

    out_shape = [jax.ShapeDtypeStruct(b.shape, b.dtype) for b in bufs]
    return pl.pallas_call(body, name="gather_shards", in_specs=[ANY] * n, out_specs=[ANY] * n, out_shape=out_shape,
                          scratch_shapes=[pltpu.SemaphoreType.DMA((n, 3))] * 4,
                          input_output_aliases={i: i for i in range(n)})(*bufs)


def swap_halves(grads):
    n = len(grads)

    def body(*refs):
        src, dst = refs[:n], refs[n:2 * n]
        ssem, rsem = refs[2 * n:]
        x, y, c, _ = _place()
        cps = []
        for i in range(n):
            hr = grads[i].shape[1] // 2
            cps.append(_rcopy(src[i].at[:, pl.ds((1 - c) * hr, hr)], dst[i], ssem.at[i], rsem.at[i], (x, y, 1 - c)))
        for cp in cps:
            cp.start()
        for cp in cps:
            cp.wait()

    out_shape = [jax.ShapeDtypeStruct((g.shape[0], g.shape[1] // 2, g.shape[2]), g.dtype) for g in grads]
    return pl.pallas_call(body, name="swap_halves", in_specs=[ANY] * n, out_specs=[ANY] * n, out_shape=out_shape,
                          scratch_shapes=[pltpu.SemaphoreType.DMA((n,))] * 2)(*grads)


def scatter_shards(parts):
    n = len(parts)

    def body(*refs):
        src, dst = refs[:n], refs[n:2 * n]
        ssem, rsem = refs[2 * n:]
        x, y, c, chips = _place()
        cps = [_rcopy(src[i].at[2 * kx + ky], dst[i].at[r], ssem.at[i, r], rsem.at[i, r], (kx, ky, c))
               for i in range(n) for r, (kx, ky) in enumerate(chips)]
        for cp in cps:
            cp.start()
        for cp in cps:
            cp.wait()

    out_shape = [jax.ShapeDtypeStruct((3,) + p.shape[1:], p.dtype) for p in parts]
    return pl.pallas_call(body, name="scatter_shards", in_specs=[ANY] * n, out_specs=[ANY] * n, out_shape=out_shape,
                          scratch_shapes=[pltpu.SemaphoreType.DMA((n, 3))] * 2)(*parts)


def join_halves(bufs):
    n = len(bufs)

    def body(*refs):
        dst = refs[n:2 * n]
        ssem, rsem = refs[2 * n:]
        x, y, c, _ = _place()
        cps = []
        for i in range(n):
            hr = bufs[i].shape[1] // 2
            mine = dst[i].at[:, pl.ds(c * hr, hr)]
            cps.append(_rcopy(mine, mine, ssem.at[i], rsem.at[i], (x, y, 1 - c)))
        for cp in cps:
            cp.start()
        for cp in cps:
            cp.wait()

    out_shape = [jax.ShapeDtypeStruct(b.shape, b.dtype) for b in bufs]
    return pl.pallas_call(body, name="join_halves", in_specs=[ANY] * n, out_specs=[ANY] * n, out_shape=out_shape,
                          scratch_shapes=[pltpu.SemaphoreType.DMA((n,))] * 2,
                          input_output_aliases={i: i for i in range(n)})(*bufs)


def gather_blocks(v):
    def body(v_ref, out_ref, send_sems, recv_sems, local_sem):
        x, y, c, chips = _place()
        me, sibling = (x, y, c), (x, y, 1 - c)

        def blk(px, py, pc):
            return out_ref.at[4 * px + 2 * py + pc]

        def copy(k, block, to, src=None):
            return _rcopy(blk(*block) if src is None else src, blk(*block), send_sems.at[k], recv_sems.at[k], to)

        own = pltpu.make_async_copy(v_ref, blk(*me), local_sem)
        own.start()
        first = [copy(0, me, sibling, src=v_ref)]
        first += [copy(1 + r, me, (*chip, c), src=v_ref) for r, chip in enumerate(chips)]
        for cp in first:
            cp.start()
        passed = [copy(4 + r, (*chip, c), sibling) for r, chip in enumerate(chips)]
        for r, chip in enumerate(chips):
            copy(1 + r, (*chip, c), me).wait_recv()
            passed[r].start()
        copy(0, sibling, me).wait_recv()
        for r, chip in enumerate(chips):
            copy(4 + r, (*chip, 1 - c), me).wait_recv()
        for cp in first + passed:
            cp.wait_send()
        own.wait()

    return pl.pallas_call(body, name="gather_blocks", in_specs=[ANY], out_specs=ANY,
                          out_shape=jax.ShapeDtypeStruct((8,) + v.shape, v.dtype),
                          scratch_shapes=[pltpu.SemaphoreType.DMA((7,)), pltpu.SemaphoreType.DMA((7,)),
                                          pltpu.SemaphoreType.DMA])(v)


STREAM_BLOCK = 256 * 1024


def _stream_rows(rows, cols):
    base = 2 * SUBLANE if rows % (2 * SUBLANE) == 0 else SUBLANE
    return _tile(rows, max(base, STREAM_BLOCK // cols // base * base), base)


def _view3(a, lead):
    shape = a.shape[:lead] + (-1, a.shape[-1])
    return a.reshape(shape)


def sum_parts(name, terms, out, grid, where, into=None):
    n_skip = 0 if into is None else 1

    def body(w_ref, *refs):
        refs = refs[n_skip:]
        acc = refs[0][...].astype(F32)
        for t in refs[1:-1]:
            acc = acc + t[...].astype(F32)
        refs[-1][...] = acc.astype(refs[-1].dtype)

    grid_spec = pltpu.PrefetchScalarGridSpec(
        num_scalar_prefetch=1, grid=grid, in_specs=[ANY] * n_skip + [pl.BlockSpec(b, f) for _, b, f in terms],
        out_specs=pl.BlockSpec(out[2], out[3]))
    operands = ([] if into is None else [into]) + [t[0] for t in terms]
    return pl.pallas_call(body, name=name, grid_spec=grid_spec, out_shape=jax.ShapeDtypeStruct(out[0], out[1]),
                          input_output_aliases={} if into is None else {1: 0},
                          compiler_params=_params(("arbitrary",) * len(grid)))(where, *operands)


def cast_place(w, where):
    w3 = _view3(w, 1)
    L, rows, cols = w3.shape
    tr = _stream_rows(rows, cols)

    def body(w_ref, src, dst):
        dst[...] = src[...].astype(BF16)

    grid_spec = pltpu.PrefetchScalarGridSpec(
        num_scalar_prefetch=1, grid=(L, rows // tr),
        in_specs=[pl.BlockSpec((None, tr, cols), lambda l, r, wh: (l, r, 0))],
        out_specs=pl.BlockSpec((None, None, tr, cols), lambda l, r, wh: (l, wh[1], r, 0)))
    out = pl.pallas_call(body, name="cast_place", grid_spec=grid_spec,
                         out_shape=jax.ShapeDtypeStruct((L, N_CHIPS, rows, cols), BF16),
                         compiler_params=_params(("arbitrary", "arbitrary")))(where, w3)
    return out.reshape((L, N_CHIPS) + w.shape[1:])


def adamw(name, w, g, m, v):
    n, rows, cols = w.shape
    tr = _stream_rows(rows, cols)

    def body(w_ref, g_ref, m_ref, v_ref, d_ref, nm_ref, nv_ref):
        gv = g_ref[...]
        nm = ADAM_B1 * m_ref[...] + (1.0 - ADAM_B1) * gv
        nv = ADAM_B2 * v_ref[...] + (1.0 - ADAM_B2) * (gv * gv)
        m_hat = nm / (1.0 - ADAM_B1 ** ADAM_STEP)
        v_hat = nv / (1.0 - ADAM_B2 ** ADAM_STEP)
        d_ref[...] = -ADAM_LR * (m_hat / (jnp.sqrt(v_hat) + ADAM_EPS) + ADAM_WD * w_ref[...])
        nm_ref[...] = nm
        nv_ref[...] = nv

    blk = pl.BlockSpec((None, tr, cols), lambda l, r: (l, r, 0))
    shp = jax.ShapeDtypeStruct(w.shape, F32)
    return pl.pallas_call(body, name=name, grid=(n, rows // tr), in_specs=[blk] * 4, out_specs=[blk] * 3,
                          out_shape=[shp] * 3, compiler_params=_params(("arbitrary", "arbitrary")))(w, g, m, v)


COL_SHARDED = ("w_ada", "w_in", "w_proj_gla", "w_proj_s5", "w_proj_attn", "w_ffn_in")
ROW_SHARDED = ("w_s5_glu", "w_out", "w_ffn_out")
SHARDED = COL_SHARDED + ROW_SHARDED
GATE = ("w_gla_gate", "b_gla_gate")
WEIGHTS = ("c_ctx", "w_ada", "b_ada", "w_in", "w_gla_gate", "b_gla_gate", "gla_norm_w", "s5_lam_re", "s5_lam_im",
           "s5_log_dt", "s5_b_re", "s5_b_im", "s5_c_re", "s5_c_im", "s5_d", "w_s5_glu", "q_norm_w", "k_norm_w",
           "w_proj_gla", "w_proj_s5", "w_proj_attn", "w_out", "ln1_w", "ln1_b", "ln2_w", "ln2_b", "w_ffn_in",
           "w_ffn_out")
REPLICATED = tuple(n for n in WEIGHTS if n not in SHARDED + GATE)
SMALL = REPLICATED + GATE


def _pack(arrays):
    flat = jnp.concatenate([a.reshape(-1) for a in arrays])
    pad = (-flat.shape[0]) % (PACK_ROWS * LANE)
    return jnp.pad(flat, (0, pad)).reshape(-1, LANE)


def _unpack(packed, like):
    flat, out, off = packed.reshape(-1), [], 0
    for a in like:
        out.append(flat[off:off + a.size].reshape(a.shape))
        off += a.size
    return out


def kernel(x, c, ctx, c_ctx, w_ada, b_ada, w_in, w_gla_gate, b_gla_gate, gla_norm_w, s5_lam_re, s5_lam_im, s5_log_dt, s5_b_re, s5_b_im, s5_c_re, s5_c_im, s5_d, w_s5_glu, q_norm_w, k_norm_w, w_proj_gla, w_proj_s5, w_proj_attn, w_out, ln1_w, ln1_b, ln2_w, ln2_b, w_ffn_in, w_ffn_out, loss_target, m_c_ctx, m_w_ada, m_b_ada, m_w_in, m_w_gla_gate, m_b_gla_gate, m_gla_norm_w, m_s5_lam_re, m_s5_lam_im, m_s5_log_dt, m_s5_b_re, m_s5_b_im, m_s5_c_re, m_s5_c_im, m_s5_d, m_w_s5_glu, m_q_norm_w, m_k_norm_w, m_w_proj_gla, m_w_proj_s5, m_w_proj_attn, m_w_out, m_ln1_w, m_ln1_b, m_ln2_w, m_ln2_b, m_w_ffn_in, m_w_ffn_out, v_c_ctx, v_w_ada, v_b_ada, v_w_in, v_w_gla_gate, v_b_gla_gate, v_gla_norm_w, v_s5_lam_re, v_s5_lam_im, v_s5_log_dt, v_s5_b_re, v_s5_b_im, v_s5_c_re, v_s5_c_im, v_s5_d, v_w_s5_glu, v_q_norm_w, v_k_norm_w, v_w_proj_gla, v_w_proj_s5, v_w_proj_attn, v_w_out, v_ln1_w, v_ln1_b, v_ln2_w, v_ln2_b, v_w_ffn_in, v_w_ffn_out):
    args = dict(locals())
    w = {n: args[n] for n in WEIGHTS}
    m = {n: args["m_" + n] for n in WEIGHTS}
    v = {n: args["v_" + n] for n in WEIGHTS}
    L = w_in.shape[0]
    half = L // 2
    core = lax.axis_index("c").astype(jnp.int32)
    place = (2 * lax.axis_index("x") + lax.axis_index("y")).astype(jnp.int32)
    zero = jnp.zeros((), jnp.int32)
    where, by_core, by_place = jnp.stack([core, place]), jnp.stack([core, zero]), jnp.stack([zero, place])

    def padded(n):
        cols = w[n].shape[-1]
        extra = _padded_width(cols) - cols if n in COL_SHARDED else 0
        return jnp.pad(w[n], ((0, 0), (0, 0), (0, extra))) if extra else w[n]

    gathered = gather_shards([cast_place(padded(n), by_place) for n in SHARDED])
    gathered = {n: g if n in COL_SHARDED else g.reshape(L, 1, -1, g.shape[-1]) for n, g in zip(SHARDED, gathered)}
    gate_shards = [w[n] for n in GATE]
    gate_blocks = gather_blocks(_pack(gate_shards))
    per_chip = [_unpack(gate_blocks[2 * j], gate_shards) for j in range(N_CHIPS)]
    small = {n: w[n] for n in REPLICATED}
    small.update({n: jnp.concatenate([per_chip[j][i] for j in range(N_CHIPS)], -1) for i, n in enumerate(GATE)})

    g_sh = {}

    def loss_fn(x1, small):
        return local_loss(x1, c[0], ctx[0], loss_target[0], small, gathered, g_sh)

    loss, (gx, g_small) = jax.value_and_grad(loss_fn, argnums=(0, 1))(x[0], small)
    loss = lax.psum(loss, ("x", "y", "c"))

    parts = [g_sh[(n, l)].reshape((N_CHIPS, -1, g_sh[(n, l)].shape[-1])) for n in SHARDED for l in range(L)]
    theirs = swap_halves(parts)
    chip_sums, tiles = [], []
    for p, t in zip(parts, theirs):
        hr, cols = t.shape[1:]
        tr = _stream_rows(hr, cols)
        tiles.append(tr)
        blk = (None, tr, cols)
        chip_sums.append(sum_parts(
            "sum_cores", [(p, blk, functools.partial(lambda nb, s, r, wh: (s, wh[0] * nb + r, 0), hr // tr)),
                          (t, blk, lambda s, r, wh: (s, r, 0))],
            (t.shape, BF16, blk, lambda s, r, wh: (s, r, 0)), (N_CHIPS, hr // tr), by_core))
    recv = scatter_shards(chip_sums)
    finals = []
    for i, n in enumerate(SHARDED):
        rows, cols = parts[i * L].shape[1:]
        buf = jnp.zeros((L, rows, cols), F32)
        for l in range(L):
            k = i * L + l
            tr = tiles[k]
            nb = rows // 2 // tr
            blk = (None, tr, cols)
            terms = [(parts[k], blk, functools.partial(lambda nb, r, wh: (wh[1], wh[0] * nb + r, 0), nb)),
                     (theirs[k], blk, lambda r, wh: (wh[1], r, 0))]
            terms += [(recv[k], blk, functools.partial(lambda j, r, wh: (j, r, 0), j)) for j in range(3)]
            buf = sum_parts("sum_chips", terms,
                            ((L, rows, cols), F32, blk, functools.partial(lambda l, nb, r, wh: (l, wh[0] * nb + r, 0), l, nb)),
                            (nb,), where, into=buf)
        finals.append(buf)
    grads = {n: g[..., :w[n].shape[-1]].reshape(w[n].shape) for n, g in zip(SHARDED, join_halves(finals))}

    small_parts = [g_small[n] for n in SMALL]
    blocks = gather_blocks(_pack(small_parts))
    prow = (None, PACK_ROWS, LANE)
    total = sum_parts("sum_devices",
                      [(blocks, prow, functools.partial(lambda k, r, wh: (k, r, 0), k)) for k in range(8)],
                      (blocks.shape[1:], F32, prow[1:], lambda r, wh: (r, 0)), (blocks.shape[1] // PACK_ROWS,),
                      jnp.zeros((2,), jnp.int32))
    grads.update(dict(zip(SMALL, _unpack(total, small_parts))))
    for n in GATE:
        width = w[n].shape[-1]
        grads[n] = lax.dynamic_slice_in_dim(grads[n], place * width, width, axis=-1)

    delta, new_m, new_v = {}, {}, {}
    for n in SHARDED:
        d3, m3, v3 = adamw("adamw", _view3(w[n], 1), _view3(grads[n], 1), _view3(m[n], 1), _view3(v[n], 1))
        delta[n], new_m[n], new_v[n] = (t.reshape(w[n].shape) for t in (d3, m3, v3))
    packed = [_pack([d[n] for n in SMALL])[None] for d in (w, grads, m, v)]
    d3, m3, v3 = adamw("adamw_small", *packed)
    like = [w[n] for n in SMALL]
    for dst, src in ((delta, d3), (new_m, m3), (new_v, v3)):
        dst.update(dict(zip(SMALL, _unpack(src[0], like))))

    return (loss, gx[None], *[grads[n] for n in WEIGHTS], *[delta[n] for n in WEIGHTS],
            *[new_m[n] for n in WEIGHTS], *[new_v[n] for n in WEIGHTS])
```

```python
import functools

import jax
import jax.numpy as jnp
from jax import lax
from jax.experimental import pallas as pl
from jax.experimental.pallas import tpu as pltpu

F32 = jnp.float32
BF16 = jnp.bfloat16

GRID_W = 64
GLA_HEADS = 4
GLA_DK = 128
GLA_DV = 256
GLA_GATE_RANK = 16
GLA_GATE_TAU = 16.0
GLA_CHUNK = 64
S5_GROUP = 16
S5_STATE = 64
ATTN_Q_HEADS = 8
ATTN_KV_HEADS = 2
ATTN_HEAD_DIM = 128
ROPE_THETA = 10000.0
DEPTH = 4
DN_ALPHA = (2 * DEPTH) ** 0.25
EPS = 1e-6
ADAM_LR = 0.001
ADAM_B1 = 0.9
ADAM_B2 = 0.999
ADAM_EPS = 1e-08
ADAM_WD = 0.01
ADAM_STEP = 10

LANE = 128
SUBLANE = 8
VMEM_LIMIT = 56 * 1024 * 1024
ADA_ROWS = 16
S5_CHUNK = 128
S5_BLOCK_GROUPS = 8
ROW_TILE = 256
COL_TILE = 512
PACK_ROWS = 512
SHARD_PAD = 1024
N_CHIPS = 4


def _params(sem, **kw):
    return pltpu.CompilerParams(dimension_semantics=sem, vmem_limit_bytes=VMEM_LIMIT, **kw)


def _tile(n, target, base):
    if n <= target:
        return n
    best = None
    for t in range(base, target + 1, base):
        if n % t == 0:
            best = t
    assert best is not None, (n, target, base)
    return best


_DIMS = {"nn": ((1,), (0,)), "nt": ((1,), (1,)), "tn": ((0,), (0,))}


def _dg(a, b, mode):
    return lax.dot_general(a.astype(BF16), b.astype(BF16), (_DIMS[mode], ((), ())), preferred_element_type=F32)


@functools.partial(jax.custom_vjp, nondiff_argnums=(2,))
def bdot(a, b, mode):
    return _dg(a, b, mode)


def _bdot_fwd(a, b, mode):
    return _dg(a, b, mode), (a, b)


def _bdot_bwd(mode, res, g):
    a, b = res
    if mode == "nn":
        return bdot(g, b, "nt").astype(a.dtype), bdot(a, g, "tn").astype(b.dtype)
    if mode == "nt":
        return bdot(g, b, "nn").astype(a.dtype), bdot(g, a, "tn").astype(b.dtype)
    return bdot(b, g, "nt").astype(a.dtype), bdot(a, g, "nn").astype(b.dtype)


bdot.defvjp(_bdot_fwd, _bdot_bwd)


MM_TILES = {"nn": (1152, 1024, 1664), "nt": (1152, 1024, 1664), "tn": (768, 2048, 1408)}


def _mm_tiles(mode, M, Kb, Nb):
    tm, tk, tn = MM_TILES[mode]
    return _tile(M, tm, SUBLANE), _tile(Kb, tk, LANE), _tile(Nb, tn, LANE)


def _mm_body(mode, reduce_axes):
    def body(p_ref, q_ref, o_ref, acc):
        first = functools.reduce(jnp.logical_and, [pl.program_id(ax) == 0 for ax, _ in reduce_axes])
        last = functools.reduce(jnp.logical_and, [pl.program_id(ax) == n - 1 for ax, n in reduce_axes])

        @pl.when(first)
        def _():
            acc[...] = jnp.zeros_like(acc)

        acc[...] += _dg(p_ref[...], q_ref[...], mode)

        @pl.when(last)
        def _():
            o_ref[...] = acc[...].astype(o_ref.dtype)

    return body


def _mm_nn(name, a, w, l, share):
    M = a.shape[0]
    _, B, Kb, Nb = w.shape
    tm, tk, tn = _mm_tiles("nn", M, Kb, Nb)
    nk, nn = Kb // tk, Nb // tn
    a_map = (lambda b, i, j, k: (i, k)) if share else (lambda b, i, j, k: (i, b * nk + k))
    return pl.pallas_call(
        _mm_body("nn", [(3, nk)]), name=name, grid=(B, M // tm, nn, nk),
        in_specs=[pl.BlockSpec((tm, tk), a_map),
                  pl.BlockSpec((None, None, tk, tn), lambda b, i, j, k: (l, b, k, j))],
        out_specs=pl.BlockSpec((tm, tn), lambda b, i, j, k: (i, b * nn + j)),
        out_shape=jax.ShapeDtypeStruct((M, B * Nb), F32),
        scratch_shapes=[pltpu.VMEM((tm, tn), F32)],
        compiler_params=_params(("arbitrary",) * 4))(a, w)


def _mm_nt(name, g, w, l, share):
    M = g.shape[0]
    _, B, Kb, Nb = w.shape
    tm, tk, tn = _mm_tiles("nt", M, Kb, Nb)
    nk, nn = Kb // tk, Nb // tn
    if share:
        grid, red = (M // tm, nk, B, nn), [(2, B), (3, nn)]
        g_map, w_map = (lambda i, k, b, n: (i, b * nn + n)), (lambda i, k, b, n: (l, b, k, n))
        o_map, width = (lambda i, k, b, n: (i, k)), Kb
    else:
        grid, red = (B, M // tm, nk, nn), [(3, nn)]
        g_map, w_map = (lambda b, i, k, n: (i, b * nn + n)), (lambda b, i, k, n: (l, b, k, n))
        o_map, width = (lambda b, i, k, n: (i, b * nk + k)), B * Kb
    return pl.pallas_call(
        _mm_body("nt", red), name=name, grid=grid,
        in_specs=[pl.BlockSpec((tm, tn), g_map), pl.BlockSpec((None, None, tk, tn), w_map)],
        out_specs=pl.BlockSpec((tm, tk), o_map),
        out_shape=jax.ShapeDtypeStruct((M, width), F32),
        scratch_shapes=[pltpu.VMEM((tm, tk), F32)],
        compiler_params=_params(("arbitrary",) * 4))(g, w)


def _mm_tn(name, a, g, B, share, dtype):
    M = a.shape[0]
    Kb, Nb = a.shape[1] // (1 if share else B), g.shape[1] // B
    tm, tk, tn = _mm_tiles("tn", M, Kb, Nb)
    nk, nn = Kb // tk, Nb // tn
    a_map = (lambda b, k, j, m: (m, k)) if share else (lambda b, k, j, m: (m, b * nk + k))
    return pl.pallas_call(
        _mm_body("tn", [(3, M // tm)]), name=name, grid=(B, nk, nn, M // tm),
        in_specs=[pl.BlockSpec((tm, tk), a_map),
                  pl.BlockSpec((tm, tn), lambda b, k, j, m: (m, b * nn + j))],
        out_specs=pl.BlockSpec((None, tk, tn), lambda b, k, j, m: (b, k, j)),
        out_shape=jax.ShapeDtypeStruct((B, Kb, Nb), dtype),
        scratch_shapes=[pltpu.VMEM((tk, tn), F32)],
        compiler_params=_params(("arbitrary",) * 4))(a, g)


def mm(name, a, w):
    w3 = w if w.ndim == 3 else w[None]

    @jax.custom_vjp
    def op(a, w3):
        return _mm_nn(name + "_fwd", a, w3.astype(BF16)[None], 0, False)

    def fwd(a, w3):
        wb = w3.astype(BF16)[None]
        return _mm_nn(name + "_fwd", a, wb, 0, False), (a, wb)

    def bwd(res, g):
        a, wb = res
        return _mm_nt(name + "_dx", g, wb, 0, False), _mm_tn(name + "_dw", a, g, wb.shape[1], False, F32)

    op.defvjp(fwd, bwd)
    return op(a, w3)


def mm_gathered(name, a, w, l, grads, key, defer=False):
    @jax.custom_vjp
    def op(a):
        return _mm_nn(name + "_fwd", a, w, l, True)

    def fwd(a):
        return _mm_nn(name + "_fwd", a, w, l, True), (a,)

    def bwd(res, g):
        (a,) = res
        grads[key] = g if defer else _mm_tn(name + "_dw", a, g, w.shape[1], True, BF16)
        return (_mm_nt(name + "_dx", g, w, l, True),)

    op.defvjp(fwd, bwd)
    return op(a)


def _spec_shape(spec, G, T):
    k = spec[0]
    if k == "row":
        return (T, spec[1])
    if k == "rowg":
        return (T, (G // spec[2]) * spec[1])
    if k == "bc":
        return (spec[1], spec[2])
    return (spec[1], (G // spec[3]) * spec[2])


def _spec_block(spec, tm, rmap):
    k = spec[0]
    if k == "row":
        return pl.BlockSpec((tm, spec[1]), lambda g, r: (rmap(r), 0))
    if k == "rowg":
        d = spec[2]
        return pl.BlockSpec((tm, spec[1]), lambda g, r: (rmap(r), g // d))
    if k == "bc":
        return pl.BlockSpec((spec[1], spec[2]), lambda g, r: (0, 0))
    d = spec[3]
    return pl.BlockSpec((spec[1], spec[2]), lambda g, r: (0, g // d))


def block_op(name, f, in_specs, out_specs, G, T, tm, diff, carry=(), row0=False, order=None):
    n_in, n_out, n_c = len(in_specs), len(out_specs), len(carry)
    n_steps = T // tm
    assert T % tm == 0
    order = order or (lambda s: s)
    diff_idx = [i for i in range(n_in) if diff[i]]
    for i in diff_idx:
        assert in_specs[i][0] != "row" or G == 1
        assert in_specs[i][0] != "rowg" or in_specs[i][2] == 1
    out_shapes = [jax.ShapeDtypeStruct(_spec_shape(s, G, T), F32) for s in out_specs]
    save_shapes = [jax.ShapeDtypeStruct((n_steps, a, G * b), F32) for a, b in carry]
    sem = ("arbitrary", "arbitrary")

    def call_f(r_idx, cvals, vals):
        args = list(vals)
        if n_c:
            args = [tuple(cvals)] + args
        if row0:
            args = [r_idx * tm] + args
        return f(*args)

    def fwd_body(*refs):
        in_refs = refs[:n_in]
        out_refs = refs[n_in:n_in + n_out]
        save_refs = refs[n_in + n_out:n_in + n_out + n_c]
        c_refs = refs[n_in + n_out + n_c:]
        r = pl.program_id(1)
        if n_c:
            @pl.when(r == 0)
            def _():
                for c in c_refs:
                    c[...] = jnp.zeros_like(c)

            cvals = [c[...] for c in c_refs]
            for s, v in zip(save_refs, cvals):
                s[...] = v
            new_c, outs = call_f(r, cvals, [x[...] for x in in_refs])
            for c, v in zip(c_refs, new_c):
                c[...] = v
        else:
            outs = call_f(r, (), [x[...] for x in in_refs])
        for o, v in zip(out_refs, outs):
            o[...] = v.astype(F32)

    def fwd_call(*arrays):
        res = pl.pallas_call(
            fwd_body, name=name + "_fwd", grid=(G, n_steps),
            in_specs=[_spec_block(s, tm, order) for s in in_specs],
            out_specs=[_spec_block(s, tm, order) for s in out_specs]
            + [pl.BlockSpec((None, a, b), lambda g, r: (r, 0, g)) for a, b in carry],
            out_shape=out_shapes + save_shapes,
            scratch_shapes=[pltpu.VMEM((a, b), F32) for a, b in carry],
            compiler_params=_params(sem))(*arrays)
        return tuple(res)

    def bwd_body(*refs):
        in_refs = refs[:n_in]
        save_refs = refs[n_in:n_in + n_c]
        ct_refs = refs[n_in + n_c:n_in + n_c + n_out]
        g_refs = refs[n_in + n_c + n_out:n_in + n_c + n_out + len(diff_idx)]
        dc_refs = refs[n_in + n_c + n_out + len(diff_idx):]
        g = pl.program_id(0)
        r = pl.program_id(1)
        vals = [x[...] for x in in_refs]
        if n_c:
            @pl.when(r == 0)
            def _():
                for d in dc_refs:
                    d[...] = jnp.zeros_like(d)

        def fun(cvals, dvals):
            full = list(vals)
            for i, v in zip(diff_idx, dvals):
                full[i] = v
            return call_f(n_steps - 1 - r if n_c else r, cvals, full)

        _, vjp = jax.vjp(fun, tuple(s[...] for s in save_refs), tuple(vals[i] for i in diff_idx))
        cts = tuple(c[...] for c in ct_refs)
        if n_c:
            cts = (tuple(d[...] for d in dc_refs), cts)
        dcin, dvals = vjp(cts)
        for d, v in zip(dc_refs, dcin):
            d[...] = v
        for gref, i, v in zip(g_refs, diff_idx, dvals):
            spec = in_specs[i]
            if spec[0] in ("row", "rowg"):
                gref[...] = v
            else:
                first = (r == 0) & ((g == 0) if spec[0] == "bc" else (g % spec[3] == 0))

                @pl.when(first)
                def _(gref=gref, v=v):
                    gref[...] = v

                @pl.when(jnp.logical_not(first))
                def _(gref=gref, v=v):
                    gref[...] += v

    def bwd_call(arrays, saved, cts):
        rmap = (lambda r: order(n_steps - 1 - r)) if n_c else (lambda r: r)
        res = pl.pallas_call(
            bwd_body, name=name + "_bwd", grid=(G, n_steps),
            in_specs=[_spec_block(s, tm, rmap) for s in in_specs]
            + [pl.BlockSpec((None, a, b), lambda g, r: (n_steps - 1 - r, 0, g)) for a, b in carry]
            + [_spec_block(s, tm, rmap) for s in out_specs],
            out_specs=[_spec_block(in_specs[i], tm, rmap) for i in diff_idx],
            out_shape=[jax.ShapeDtypeStruct(_spec_shape(in_specs[i], G, T), F32) for i in diff_idx],
            scratch_shapes=[pltpu.VMEM((a, b), F32) for a, b in carry],
            compiler_params=_params(sem))(*arrays, *saved, *cts)
        return tuple(res)

    @jax.custom_vjp
    def op(*arrays):
        return fwd_call(*arrays)[:n_out]

    def op_fwd(*arrays):
        res = fwd_call(*arrays)
        return res[:n_out], (arrays, res[n_out:])

    def op_bwd(res, cts):
        arrays, saved = res
        grads = bwd_call(arrays, saved, cts)
        out = [jnp.zeros_like(a) for a in arrays]
        for i, gval in zip(diff_idx, grads):
            out[i] = gval
        return tuple(out)

    op.defvjp(op_fwd, op_bwd)
    return op


def _rows(n, m):
    return lax.broadcasted_iota(jnp.int32, (n, m), 0)


def _ctx_select(row0, tm, n_ctx, v_lat, v_ctx):
    if n_ctx == 0:
        return v_lat
    is_ctx = (row0 + _rows(tm, 1)) < n_ctx
    return jnp.where(is_ctx, v_ctx, v_lat)


def _silu(x):
    return x * jax.nn.sigmoid(x)


def _f_modulate(tm, n_ctx):
    def f(row0, x, sh_l, sh_c, sc_l, sc_c):
        sh = _ctx_select(row0, tm, n_ctx, sh_l, sh_c)
        sc = _ctx_select(row0, tm, n_ctx, sc_l, sc_c)
        return (x * (1 + sc) + sh,)
    return f


def _f_postnorm(tm, n_ctx):
    def f(row0, x, y, g_l, g_c, w, b):
        z = DN_ALPHA * x + _ctx_select(row0, tm, n_ctx, g_l, g_c) * y
        mu = jnp.mean(z, -1, keepdims=True)
        zc = z - mu
        var = jnp.mean(zc * zc, -1, keepdims=True)
        return (zc * lax.rsqrt(var + EPS) * w + b,)
    return f


def _log_sigmoid(x):
    return -(jnp.maximum(-x, 0.0) + jnp.log1p(jnp.exp(-jnp.abs(x))))


def _f_gla_prep(glr, wg0, wg1, b0, b1):
    return (_log_sigmoid(bdot(glr, wg0, "nn") + b0) / GLA_GATE_TAU,
            _log_sigmoid(bdot(glr, wg1, "nn") + b1) / GLA_GATE_TAU)


def _f_gla_step(rev):
    def f(carry, q, k, v, la):
        (st,) = carry
        n = q.shape[0]
        cols = lax.broadcasted_iota(jnp.int32, (n, n), 1)
        tri = (_rows(n, n) <= cols) if rev else (_rows(n, n) >= cols)
        b = jnp.dot(tri.astype(F32), la, precision=lax.Precision.HIGHEST)
        qe = q * (GLA_DK ** -0.5) * jnp.exp(b)
        ke = k * jnp.exp(-b)
        att = jnp.where(tri, bdot(qe, ke, "nt"), 0.0)
        o = bdot(att, v, "nn") + bdot(qe, st, "nt")
        end = 0 if rev else n - 1
        b_last = jnp.sum(jnp.where(_rows(n, 1) == end, b, 0.0), axis=0, keepdims=True)
        kd = k * jnp.exp(b_last - b)
        st = st * jnp.exp(b_last) + bdot(v, kd, "tn")
        return (st,), (o,)
    return f


def _f_gla_norm(o0, o1, gr, w):
    o = o0 + o1
    mu = jnp.mean(o, -1, keepdims=True)
    oc = o - mu
    var = jnp.mean(oc * oc, -1, keepdims=True)
    return (oc * lax.rsqrt(var + EPS) * w * _silu(gr),)


@functools.partial(jax.custom_vjp, nondiff_argnums=(1, 2))
def _shift_rows(x, d, up):
    n = x.shape[0]
    rows = _rows(n, 1)
    if up:
        return jnp.where(rows < n - d, pltpu.roll(x, n - d, 0), 0.0)
    return jnp.where(rows >= d, pltpu.roll(x, d, 0), 0.0)


def _shift_fwd(x, d, up):
    return _shift_rows(x, d, up), None


def _shift_bwd(d, up, _, g):
    return (_shift_rows(g, d, not up),)


_shift_rows.defvjp(_shift_fwd, _shift_bwd)


def _f_s5_step(rev):
    def f(carry, bur, bui, lam_re, lam_im, log_dt):
        cr, ci = carry
        n = bur.shape[0]
        dt = jnp.exp(log_dt)
        mag = jnp.exp(lam_re * dt)
        ar, ai = mag * jnp.cos(lam_im * dt), mag * jnp.sin(lam_im * dt)
        den = lam_re * lam_re + lam_im * lam_im
        nr, ni = ar - 1, ai
        kr = (nr * lam_re + ni * lam_im) / den
        ki = (ni * lam_re - nr * lam_im) / den
        first = _rows(n, 1) == (n - 1 if rev else 0)
        xr = kr * bur - ki * bui + jnp.where(first, ar * cr - ai * ci, 0.0)
        xi = kr * bui + ki * bur + jnp.where(first, ar * ci + ai * cr, 0.0)
        pr, pi = ar, ai
        d = 1
        while d < n:
            sr, si = _shift_rows(xr, d, rev), _shift_rows(xi, d, rev)
            xr, xi = xr + pr * sr - pi * si, xi + pr * si + pi * sr
            pr, pi = pr * pr - pi * pi, 2 * pr * pi
            d *= 2
        last = _rows(n, 1) == (0 if rev else n - 1)
        cr = jnp.sum(jnp.where(last, xr, 0.0), axis=0, keepdims=True)
        ci = jnp.sum(jnp.where(last, xi, 0.0), axis=0, keepdims=True)
        return (cr, ci), (xr, xi)
    return f


def _f_s5_post(su, dskip, y0r, y0i, y1r, y1i):
    return (jax.nn.gelu(su * dskip + y0r - y0i + y1r - y1i),)


def _f_s5_glu(y, t):
    return (y * jax.nn.sigmoid(t),)


def _swap_pairs(x):
    lane = lax.broadcasted_iota(jnp.int32, x.shape, 1)
    return jnp.where(lane % 2 == 0, pltpu.roll(x, x.shape[1] - 1, 1), pltpu.roll(x, 1, 1))


@jax.custom_vjp
def _rope(x, cos2, sin2):
    return x * cos2 + _swap_pairs(x) * sin2


def _rope_fwd(x, cos2, sin2):
    return _rope(x, cos2, sin2), (cos2, sin2)


def _rope_bwd(res, g):
    cos2, sin2 = res
    return g * cos2 + _swap_pairs(g * sin2), jnp.zeros_like(cos2), jnp.zeros_like(sin2)


_rope.defvjp(_rope_fwd, _rope_bwd)


def _f_qk_norm_rope(x, cos2, sin2, w):
    xn = x * lax.rsqrt(jnp.mean(x * x, -1, keepdims=True) + EPS) * w
    return (_rope(xn, cos2, sin2),)


def _f_attn(q, k, v):
    s = bdot(q, k, "nt") * (ATTN_HEAD_DIM ** -0.5)
    e = jnp.exp(s - jnp.max(s, -1, keepdims=True))
    p = e / jnp.sum(e, -1, keepdims=True)
    return (bdot(p, v, "nn"),)


def _f_merge(ga, gb, gc, pa, pb, pc):
    return (jax.nn.sigmoid(ga) * pa + jax.nn.sigmoid(gb) * pb + jax.nn.sigmoid(gc) * pc,)


def _f_swiglu(a, b):
    return (_silu(a) * b,)


def _f_silu(x):
    return (_silu(x),)


def _f_add_bias(x, b):
    return (x + b,)


def _scan_order(rev, n_ctx, T, tm):
    if not rev:
        return None
    nc, n = n_ctx // tm, T // tm
    return lambda s: jnp.where(s < nc, nc - 1 - s, n - 1 - (s - nc))


def modulate(name, x, sh, sc, n_ctx):
    T, D = x.shape
    tm = _tile(T, ROW_TILE, SUBLANE)
    cw = _tile(D, COL_TILE, LANE)
    col, vec = ("rowg", cw, 1), ("bcg", 1, cw, 1)
    op = block_op(name, _f_modulate(tm, n_ctx), [col, vec, vec, vec, vec], [col], D // cw, T, tm,
                  [True] * 5, row0=True)
    return op(x, sh[0], sh[1], sc[0], sc[1])[0]


def postnorm(name, x, y, g, w, b, n_ctx):
    T, D = x.shape
    tm = _tile(T, ROW_TILE, SUBLANE)
    vec = ("bc", 1, D)
    op = block_op(name, _f_postnorm(tm, n_ctx), [("row", D), ("row", D), vec, vec, vec, vec], [("row", D)], 1, T,
                  tm, [True] * 6, row0=True)
    return op(x, y, g[0], g[1], w, b)[0]


def rowwise(name, f, arrays, n_out=1):
    T, w = arrays[0].shape
    tm = _tile(T, ROW_TILE, SUBLANE)
    cw = _tile(w, COL_TILE, LANE)
    col = ("rowg", cw, 1)
    op = block_op(name, f, [col] * len(arrays), [col] * n_out, w // cw, T, tm, [True] * len(arrays))
    return op(*arrays)


def gla_prep(name, glr, wg, bg):
    T = glr.shape[0]
    qk = wg.shape[-1]
    tm = _tile(T, ROW_TILE, SUBLANE)
    op = block_op(name, _f_gla_prep, [("row", LANE), ("bc", LANE, qk), ("bc", LANE, qk), ("bc", 1, qk), ("bc", 1, qk)],
                  [("row", qk), ("row", qk)], 1, T, tm, [True] * 5)
    return op(glr, wg[0], wg[1], bg[0], bg[1])


def gla_scan(name, q, k, v, la, rev, n_ctx):
    T = q.shape[0]
    op = block_op(name, _f_gla_step(rev), [("rowg", GLA_DK, 1), ("rowg", GLA_DK, 1), ("rowg", GLA_DV, 1), ("rowg", GLA_DK, 1)],
                  [("rowg", GLA_DV, 1)], GLA_HEADS, T, GLA_CHUNK, [True] * 4, carry=[(GLA_DV, GLA_DK)],
                  order=_scan_order(rev, n_ctx, T, GLA_CHUNK))
    return op(q, k, v, la)[0]


def gla_norm(name, o0, o1, gr, w):
    T = o0.shape[0]
    tm = _tile(T, ROW_TILE, SUBLANE)
    hd = ("rowg", GLA_DV, 1)
    op = block_op(name, _f_gla_norm, [hd, hd, hd, ("bcg", 1, GLA_DV, 1)], [hd], GLA_HEADS, T, tm, [True] * 4)
    return op(o0, o1, gr, w)[0]


def s5_scan(name, bur, bui, lam_re, lam_im, log_dt, rev, n_ctx):
    T, S = bur.shape
    cols = _tile(S, 768, LANE)
    G = S // cols
    col, par = ("rowg", cols, 1), ("bcg", 1, cols, 1)
    op = block_op(name, _f_s5_step(rev), [col, col, par, par, par], [col, col], G, T, S5_CHUNK, [True] * 5,
                  carry=[(1, cols), (1, cols)], order=_scan_order(rev, n_ctx, T, S5_CHUNK))
    return op(bur, bui, lam_re, lam_im, log_dt)


def qk_norm_rope(name, x, cos2, sin2, w):
    T = x.shape[0]
    G = x.shape[1] // ATTN_HEAD_DIM
    tm = _tile(T, ROW_TILE, SUBLANE)
    hd = ("rowg", ATTN_HEAD_DIM, 1)
    op = block_op(name, _f_qk_norm_rope, [hd, ("row", ATTN_HEAD_DIM), ("row", ATTN_HEAD_DIM), ("bc", 1, ATTN_HEAD_DIM)],
                  [hd], G, T, tm, [True, False, False, True])
    return op(x, cos2, sin2, w)[0]


def attention(name, q, k, v):
    T, Tk = q.shape[0], k.shape[0]
    tm = _tile(T, ROW_TILE, SUBLANE)
    grp = ATTN_Q_HEADS // ATTN_KV_HEADS
    kv = ("bcg", Tk, ATTN_HEAD_DIM, grp)
    op = block_op(name, _f_attn, [("rowg", ATTN_HEAD_DIM, 1), kv, kv], [("rowg", ATTN_HEAD_DIM, 1)], ATTN_Q_HEADS, T,
                  tm, [True] * 3)
    return op(q, k, v)[0]


def sq_loss(name, y, t):
    T, D = y.shape
    tm = _tile(T, ROW_TILE, SUBLANE)

    def fwd_body(y_ref, t_ref, o_ref):
        e = y_ref[...] - t_ref[...]
        part = jnp.sum(jnp.sum(e * e, -1, keepdims=True), 0, keepdims=True) * (0.5 / D)

        @pl.when(pl.program_id(0) == 0)
        def _():
            o_ref[...] = jnp.zeros_like(o_ref)

        o_ref[...] += part * jnp.ones((1, LANE), F32)

    def bwd_body(y_ref, t_ref, g_ref, o_ref):
        o_ref[...] = (y_ref[...] - t_ref[...]) * (g_ref[:, 0:1] / D)

    row = pl.BlockSpec((tm, D), lambda r: (r, 0))
    one = pl.BlockSpec((1, LANE), lambda r: (0, 0))

    def fwd_call(y, t):
        return pl.pallas_call(fwd_body, name=name + "_fwd", grid=(T // tm,), in_specs=[row, row], out_specs=one,
                              out_shape=jax.ShapeDtypeStruct((1, LANE), F32), compiler_params=_params(("arbitrary",)))(y, t)

    @jax.custom_vjp
    def op(y, t):
        return fwd_call(y, t)[0, 0]

    def op_fwd(y, t):
        return fwd_call(y, t)[0, 0], (y, t)

    def op_bwd(res, g):
        y, t = res
        gy = pl.pallas_call(bwd_body, name=name + "_bwd", grid=(T // tm,), in_specs=[row, row, one], out_specs=row,
                            out_shape=jax.ShapeDtypeStruct((T, D), F32), compiler_params=_params(("arbitrary",)))(
                                y, t, jnp.full((1, LANE), g, F32))
        return gy, jnp.zeros_like(t)

    op.defvjp(op_fwd, op_bwd)
    return op(y, t)


def _rope_tables(n_ctx, n_lat):
    n_rows = n_lat // GRID_W
    rows = jnp.repeat(jnp.arange(n_rows), GRID_W).astype(F32)
    cols = jnp.tile(jnp.arange(GRID_W), n_rows).astype(F32)
    n_freq = ATTN_HEAD_DIM // 4
    inv = ROPE_THETA ** (-jnp.arange(n_freq, dtype=F32) / n_freq)
    ang = jnp.concatenate([rows[:, None] * inv, cols[:, None] * inv], -1)
    cos2 = jnp.repeat(jnp.cos(ang), 2, axis=-1)
    sin2 = jnp.stack([-jnp.sin(ang), jnp.sin(ang)], -1).reshape(n_lat, ATTN_HEAD_DIM)
    cos2 = jnp.concatenate([jnp.ones((n_ctx, ATTN_HEAD_DIM), F32), cos2], 0)
    sin2 = jnp.concatenate([jnp.zeros((n_ctx, ATTN_HEAD_DIM), F32), sin2], 0)
    return cos2, sin2


def _in_layout(D, s5_width):
    qk, gv = GLA_HEADS * GLA_DK, GLA_HEADS * GLA_DV
    aq, akv = ATTN_Q_HEADS * ATTN_HEAD_DIM, ATTN_KV_HEADS * ATTN_HEAD_DIM
    widths = [("gq", qk), ("gk", qk), ("gv", gv), ("gr", gv), ("glr", GLA_GATE_RANK), ("su", s5_width), ("aq", aq),
              ("ak", akv), ("av", akv), ("ga", D), ("gb", D), ("gc", D)]
    off, out = 0, {}
    for n, w in widths:
        out[n] = (off, w)
        off += w
    return out, off


def _padded_width(width):
    return width if width % LANE == 0 else -(-width // SHARD_PAD) * SHARD_PAD


def _shard_cols(z, off, width, shard, padded):
    parts = []
    for j in range(N_CHIPS):
        lo, hi = max(off, j * shard), min(off + width, (j + 1) * shard)
        if lo < hi:
            parts.append(z[:, j * padded + lo - j * shard:j * padded + hi - j * shard])
    return parts[0] if len(parts) == 1 else jnp.concatenate(parts, 1)


def _s5_in_blocks(b):
    G, P, C = b.shape
    nb, bg = G // S5_BLOCK_GROUPS, S5_BLOCK_GROUPS
    t = b.reshape(nb, bg, P, C).transpose(0, 1, 3, 2)
    return jnp.einsum("bgcp,gh->bgchp", t, jnp.eye(bg, dtype=F32)).reshape(nb, bg * C, bg * P)


def _s5_out_blocks(c):
    G, C, P = c.shape
    nb, bg = G // S5_BLOCK_GROUPS, S5_BLOCK_GROUPS
    t = c.reshape(nb, bg, C, P).transpose(0, 1, 3, 2)
    return jnp.einsum("bgpc,gh->bgphc", t, jnp.eye(bg, dtype=F32)).reshape(nb, bg * P, bg * C)


def _layer(keep_ctx, xa, n_ctx, mod, p, big, cos2, sin2):
    T, D = xa.shape
    S = p["s5_d"].shape[-1]
    lay, width = _in_layout(D, S)
    lo = 0 if keep_ctx else n_ctx
    ctx_rows = n_ctx if keep_ctx else 0

    h = modulate("modulate1", xa, mod["sh1"], mod["sc1"], n_ctx)
    z = big["w_in"]("in_proj", h)
    shard = width // N_CHIPS
    zz = {n: _shard_cols(z, o, w, shard, z.shape[1] // N_CHIPS) for n, (o, w) in lay.items()}

    wg = jnp.pad(p["w_gla_gate"], ((0, 0), (0, LANE - GLA_GATE_RANK), (0, 0)))
    glr = jnp.pad(zz["glr"], ((0, 0), (0, LANE - GLA_GATE_RANK)))
    la0, la1 = gla_prep("gla_prep", glr, wg, p["b_gla_gate"][:, None, :])
    o0 = gla_scan("gla_scan", zz["gq"], zz["gk"], zz["gv"], la0, False, n_ctx)
    o1 = gla_scan("gla_scan_rev", zz["gq"], zz["gk"], zz["gv"], la1, True, n_ctx)
    o_gla = gla_norm("gla_norm", o0[lo:], o1[lo:], zz["gr"][lo:], p["gla_norm_w"][None, :])

    su = zz["su"]
    bur = mm("s5_in_re", su, _s5_in_blocks(p["s5_b_re"]))
    bui = mm("s5_in_im", su, _s5_in_blocks(p["s5_b_im"]))
    ys = []
    for d in range(2):
        row = lambda t: t.reshape(1, -1)
        ldt = jnp.repeat(p["s5_log_dt"][d], S5_STATE)
        sr, si = s5_scan("s5_scan_rev" if d else "s5_scan", bur, bui, row(p["s5_lam_re"][d]),
                         row(p["s5_lam_im"][d]), row(ldt), d == 1, n_ctx)
        ys.append(mm("s5_out_re", sr[lo:], _s5_out_blocks(p["s5_c_re"][d])))
        ys.append(mm("s5_out_im", si[lo:], _s5_out_blocks(p["s5_c_im"][d])))
    T2 = T - lo
    tm = _tile(T2, ROW_TILE, SUBLANE)
    post = block_op("s5_post", _f_s5_post, [("row", S), ("bc", 1, S)] + [("row", S)] * 4, [("row", S)], 1, T2, tm,
                    [True] * 6)
    yg = post(su[lo:], p["s5_d"][None, :], *ys)[0]
    o_s5 = rowwise("s5_glu", _f_s5_glu, [yg, big["w_s5_glu"]("s5_glu_proj", yg)])[0]

    qn = qk_norm_rope("q_norm_rope", zz["aq"], cos2, sin2, p["q_norm_w"][None, :])
    kn = qk_norm_rope("k_norm_rope", zz["ak"], cos2, sin2, p["k_norm_w"][None, :])
    o_attn = attention("attn_lat", qn[n_ctx:], kn, zz["av"])
    if keep_ctx:
        o_c = attention("attn_ctx", qn[:n_ctx], kn[:n_ctx], zz["av"][:n_ctx])
        o_attn = jnp.concatenate([o_c, o_attn], 0)

    merged = rowwise("merge", _f_merge, [zz["ga"][lo:], zz["gb"][lo:], zz["gc"][lo:],
                                         big["w_proj_gla"]("proj_gla", o_gla),
                                         big["w_proj_s5"]("proj_s5", o_s5),
                                         big["w_proj_attn"]("proj_attn", o_attn)])[0]
    mix = big["w_out"]("out_proj", merged)
    x1 = postnorm("postnorm1", xa[lo:], mix, mod["g1"], p["ln1_w"][None, :], p["ln1_b"][None, :], ctx_rows)
    h2 = modulate("modulate2", x1, mod["sh2"], mod["sc2"], ctx_rows)
    u = big["w_ffn_in"]("ffn_in", h2)
    F = u.shape[1] // 2
    act = rowwise("swiglu", _f_swiglu, [u[:, :F], u[:, F:]])[0]
    f = big["w_ffn_out"]("ffn_out", act)
    return postnorm("postnorm2", x1, f, mod["g2"], p["ln2_w"][None, :], p["ln2_b"][None, :], ctx_rows)


def local_loss(x, c, ctx, target, small, gathered, grads):
    n_lat, D = x.shape
    n_ctx = ctx.shape[0]
    cos2, sin2 = _rope_tables(n_ctx, n_lat)
    cc = jnp.concatenate([c[None, :], small["c_ctx"][None, :], jnp.zeros((ADA_ROWS - 2, D), F32)], 0)
    silu_cc = rowwise("silu_cond", _f_silu, [cc])[0]
    xa = jnp.concatenate([ctx, x], 0)
    depth = gathered["w_in"].shape[0]
    for l in range(depth):
        p = {n: v[l] for n, v in small.items() if n != "c_ctx"}
        big = {n: functools.partial(lambda n, l, name, a: mm_gathered(name, a, gathered[n], l, grads, (n, l),
                                                                      defer=n in DEFERRED), n, l)
               for n in gathered}
        m = big["w_ada"]("ada_proj", silu_cc)
        m = block_op("ada_bias", _f_add_bias, [("row", 6 * D), ("bc", 1, 6 * D)], [("row", 6 * D)], 1, ADA_ROWS,
                     ADA_ROWS, [True, True])(m, p["b_ada"][None, :])[0]
        names = ["sh1", "sc1", "g1", "sh2", "sc2", "g2"]
        mod = {n: (m[0:1, i * D:(i + 1) * D], m[1:2, i * D:(i + 1) * D]) for i, n in enumerate(names)}
        xa = _layer(l < depth - 1, xa, n_ctx, mod, p, big, cos2, sin2)
    return sq_loss("loss", xa, target)


MESH = pl.DeviceIdType.MESH
ANY = pl.BlockSpec(memory_space=pl.ANY)


def _place():
    x, y, c = lax.axis_index("x"), lax.axis_index("y"), lax.axis_index("c")
    chips = [(1 - x, y), (x, 1 - y), (1 - x, 1 - y)]
    return x, y, c, chips


def _rcopy(src, dst, ssem, rsem, to):
    return pltpu.make_async_remote_copy(src_ref=src, dst_ref=dst, send_sem=ssem, recv_sem=rsem, device_id=to,
                                        device_id_type=MESH)


def gather_shards(bufs):
    n = len(bufs)
    L = bufs[0].shape[0]
    half = L // 2

    def body(*refs):
        dst = refs[n:2 * n]
        isend, irecv, fsend, frecv, dsend, drecv = refs[2 * n:]
        x, y, c, _ = _place()
        j = 2 * x + y
        nbr = [(1 - x, y), (x, 1 - y)]
        jn = [2 * kx + ky for kx, ky in nbr]
        jd = 2 * (1 - x) + (1 - y)
        sibling = (x, y, 1 - c)
        mine, other = pl.ds(c * half, half), pl.ds((1 - c) * half, half)

        def piece(i, layers, chip, q):
            hr = bufs[i].shape[2] // 2
            return dst[i].at[layers, chip, pl.ds(q * hr, hr)]

        direct = [_rcopy(dst[i].at[mine, j], dst[i].at[mine, j], isend.at[i, r], irecv.at[i, r], (*nbr[r], c))
                  for i in range(n) for r in range(2)]
        for cp in direct:
            cp.start()
        passed = []
        for i in range(n):
            for r in range(2):
                part = dst[i].at[mine, jn[r]]
                _rcopy(part, part, isend.at[i, r], irecv.at[i, r], (*nbr[r], c)).wait_recv()
                fwd = piece(i, mine, jn[r], 1 - r)
                passed.append(_rcopy(fwd, fwd, fsend.at[i, r], frecv.at[i, 1 - r], (*nbr[1 - r], c)))
                passed.append(_rcopy(part, part, dsend.at[i, r], drecv.at[i, r], sibling))
                passed[-2].start()
                passed[-1].start()
        for i in range(n):
            for q in range(2):
                part = piece(i, mine, jd, q)
                _rcopy(part, part, fsend.at[i, q], frecv.at[i, q], (*nbr[q], c)).wait_recv()
                passed.append(_rcopy(part, part, dsend.at[i, 2 + q], drecv.at[i, 2 + q], sibling))
                passed[-1].start()
        for i in range(n):
            for r in range(2):
                part = dst[i].at[other, jn[r]]
                _rcopy(part, part, dsend.at[i, r], drecv.at[i, r], sibling).wait_recv()
                part = piece(i, other, jd, r)
                _rcopy(part, part, dsend.at[i, 2 + r], drecv.at[i, 2 + r], sibling).wait_recv()
        for cp in direct + passed:
            cp.wait_send()

    out_shape = [jax.ShapeDtypeStruct(b.shape, b.dtype) for b in bufs]
    sems = [pltpu.SemaphoreType.DMA((n, 2))] * 4 + [pltpu.SemaphoreType.DMA((n, 4))] * 2
    return pl.pallas_call(body, name="gather_shards", in_specs=[ANY] * n, out_specs=[ANY] * n, out_shape=out_shape,
                          scratch_shapes=sems, input_output_aliases={i: i for i in range(n)})(*bufs)


def swap_halves(grads):
    n = len(grads)

    def body(*refs):
        src, dst = refs[:n], refs[n:2 * n]
        ssem, rsem = refs[2 * n:]
        x, y, c, _ = _place()
        cps = []
        for i in range(n):
            hr = grads[i].shape[1] // 2
            cps.append(_rcopy(src[i].at[:, pl.ds((1 - c) * hr, hr)], dst[i], ssem.at[i], rsem.at[i], (x, y, 1 - c)))
        for cp in cps:
            cp.start()
        for cp in cps:
            cp.wait()

    out_shape = [jax.ShapeDtypeStruct((g.shape[0], g.shape[1] // 2, g.shape[2]), g.dtype) for g in grads]
    return pl.pallas_call(body, name="swap_halves", in_specs=[ANY] * n, out_specs=[ANY] * n, out_shape=out_shape,
                          scratch_shapes=[pltpu.SemaphoreType.DMA((n,))] * 2)(*grads)


def scatter_shards(parts):
    n = len(parts)

    def body(*refs):
        src, dst = refs[:n], refs[n:2 * n]
        ssem, rsem = refs[2 * n:]
        x, y, c, chips = _place()
        cps = [_rcopy(src[i].at[2 * kx + ky], dst[i].at[r], ssem.at[i, r], rsem.at[i, r], (kx, ky, c))
               for i in range(n) for r, (kx, ky) in enumerate(chips)]
        for cp in cps:
            cp.start()
        for cp in cps:
            cp.wait()

    out_shape = [jax.ShapeDtypeStruct((3,) + p.shape[1:], p.dtype) for p in parts]
    return pl.pallas_call(body, name="scatter_shards", in_specs=[ANY] * n, out_specs=[ANY] * n, out_shape=out_shape,
                          scratch_shapes=[pltpu.SemaphoreType.DMA((n, 3))] * 2)(*parts)


def join_halves(bufs):
    n = len(bufs)

    def body(*refs):
        dst = refs[n:2 * n]
        ssem, rsem = refs[2 * n:]
        x, y, c, _ = _place()
        cps = []
        for i in range(n):
            hr = bufs[i].shape[1] // 2
            mine = dst[i].at[:, pl.ds(c * hr, hr)]
            cps.append(_rcopy(mine, mine, ssem.at[i], rsem.at[i], (x, y, 1 - c)))
        for cp in cps:
            cp.start()
        for cp in cps:
            cp.wait()

    out_shape = [jax.ShapeDtypeStruct(b.shape, b.dtype) for b in bufs]
    return pl.pallas_call(body, name="join_halves", in_specs=[ANY] * n, out_specs=[ANY] * n, out_shape=out_shape,
                          scratch_shapes=[pltpu.SemaphoreType.DMA((n,))] * 2,
                          input_output_aliases={i: i for i in range(n)})(*bufs)


def gather_blocks(v):
    def body(v_ref, out_ref, send_sems, recv_sems, local_sem):
        x, y, c, chips = _place()
        me, sibling = (x, y, c), (x, y, 1 - c)

        def blk(px, py, pc):
            return out_ref.at[4 * px + 2 * py + pc]

        def copy(k, block, to, src=None):
            return _rcopy(blk(*block) if src is None else src, blk(*block), send_sems.at[k], recv_sems.at[k], to)

        own = pltpu.make_async_copy(v_ref, blk(*me), local_sem)
        own.start()
        first = [copy(0, me, sibling, src=v_ref)]
        first += [copy(1 + r, me, (*chip, c), src=v_ref) for r, chip in enumerate(chips)]
        for cp in first:
            cp.start()
        passed = [copy(4 + r, (*chip, c), sibling) for r, chip in enumerate(chips)]
        for r, chip in enumerate(chips):
            copy(1 + r, (*chip, c), me).wait_recv()
            passed[r].start()
        copy(0, sibling, me).wait_recv()
        for r, chip in enumerate(chips):
            copy(4 + r, (*chip, 1 - c), me).wait_recv()
        for cp in first + passed:
            cp.wait_send()
        own.wait()

    return pl.pallas_call(body, name="gather_blocks", in_specs=[ANY], out_specs=ANY,
                          out_shape=jax.ShapeDtypeStruct((8,) + v.shape, v.dtype),
                          scratch_shapes=[pltpu.SemaphoreType.DMA((7,)), pltpu.SemaphoreType.DMA((7,)),
                                          pltpu.SemaphoreType.DMA])(v)


STREAM_BLOCK = 256 * 1024


def _stream_rows(rows, cols):
    base = 2 * SUBLANE if rows % (2 * SUBLANE) == 0 else SUBLANE
    return _tile(rows, max(base, STREAM_BLOCK // cols // base * base), base)


def _view3(a, lead):
    shape = a.shape[:lead] + (-1, a.shape[-1])
    return a.reshape(shape)


def sum_parts(name, terms, out, grid, where, into=None):
    n_skip = 0 if into is None else 1

    def body(w_ref, *refs):
        refs = refs[n_skip:]
        acc = refs[0][...].astype(F32)
        for t in refs[1:-1]:
            acc = acc + t[...].astype(F32)
        refs[-1][...] = acc.astype(refs[-1].dtype)

    grid_spec = pltpu.PrefetchScalarGridSpec(
        num_scalar_prefetch=1, grid=grid, in_specs=[ANY] * n_skip + [pl.BlockSpec(b, f) for _, b, f in terms],
        out_specs=pl.BlockSpec(out[2], out[3]))
    operands = ([] if into is None else [into]) + [t[0] for t in terms]
    return pl.pallas_call(body, name=name, grid_spec=grid_spec, out_shape=jax.ShapeDtypeStruct(out[0], out[1]),
                          input_output_aliases={} if into is None else {1: 0},
                          compiler_params=_params(("arbitrary",) * len(grid)))(where, *operands)


def cast_place(w, where):
    w3 = _view3(w, 1)
    L, rows, cols = w3.shape
    tr = _stream_rows(rows, cols)

    def body(w_ref, src, dst):
        dst[...] = src[...].astype(BF16)

    grid_spec = pltpu.PrefetchScalarGridSpec(
        num_scalar_prefetch=1, grid=(L, rows // tr),
        in_specs=[pl.BlockSpec((None, tr, cols), lambda l, r, wh: (l, r, 0))],
        out_specs=pl.BlockSpec((None, None, tr, cols), lambda l, r, wh: (l, wh[1], r, 0)))
    out = pl.pallas_call(body, name="cast_place", grid_spec=grid_spec,
                         out_shape=jax.ShapeDtypeStruct((L, N_CHIPS, rows, cols), BF16),
                         compiler_params=_params(("arbitrary", "arbitrary")))(where, w3)
    return out.reshape((L, N_CHIPS) + w.shape[1:])


def adamw(name, w, g, m, v):
    n, rows, cols = w.shape
    tr = _stream_rows(rows, cols)

    def body(w_ref, g_ref, m_ref, v_ref, d_ref, nm_ref, nv_ref):
        gv = g_ref[...]
        nm = ADAM_B1 * m_ref[...] + (1.0 - ADAM_B1) * gv
        nv = ADAM_B2 * v_ref[...] + (1.0 - ADAM_B2) * (gv * gv)
        m_hat = nm / (1.0 - ADAM_B1 ** ADAM_STEP)
        v_hat = nv / (1.0 - ADAM_B2 ** ADAM_STEP)
        d_ref[...] = -ADAM_LR * (m_hat / (jnp.sqrt(v_hat) + ADAM_EPS) + ADAM_WD * w_ref[...])
        nm_ref[...] = nm
        nv_ref[...] = nv

    blk = pl.BlockSpec((None, tr, cols), lambda l, r: (l, r, 0))
    shp = jax.ShapeDtypeStruct(w.shape, F32)
    return pl.pallas_call(body, name=name, grid=(n, rows // tr), in_specs=[blk] * 4, out_specs=[blk] * 3,
                          out_shape=[shp] * 3, compiler_params=_params(("arbitrary", "arbitrary")))(w, g, m, v)


COL_SHARDED = ("w_ada", "w_in", "w_proj_gla", "w_proj_s5", "w_proj_attn", "w_ffn_in")
ROW_SHARDED = ("w_s5_glu", "w_out", "w_ffn_out")
SHARDED = COL_SHARDED + ROW_SHARDED
DEFERRED = ("w_ada",)
REDUCED = tuple(n for n in SHARDED if n not in DEFERRED)
N_DEV = 8
GATE = ("w_gla_gate", "b_gla_gate")
WEIGHTS = ("c_ctx", "w_ada", "b_ada", "w_in", "w_gla_gate", "b_gla_gate", "gla_norm_w", "s5_lam_re", "s5_lam_im",
           "s5_log_dt", "s5_b_re", "s5_b_im", "s5_c_re", "s5_c_im", "s5_d", "w_s5_glu", "q_norm_w", "k_norm_w",
           "w_proj_gla", "w_proj_s5", "w_proj_attn", "w_out", "ln1_w", "ln1_b", "ln2_w", "ln2_b", "w_ffn_in",
           "w_ffn_out")
REPLICATED = tuple(n for n in WEIGHTS if n not in SHARDED + GATE)
SMALL = REPLICATED + GATE


def _pack(arrays):
    flat = jnp.concatenate([a.reshape(-1) for a in arrays])
    pad = (-flat.shape[0]) % (PACK_ROWS * LANE)
    return jnp.pad(flat, (0, pad)).reshape(-1, LANE)


def _unpack(packed, like):
    flat, out, off = packed.reshape(-1), [], 0
    for a in like:
        out.append(flat[off:off + a.size].reshape(a.shape))
        off += a.size
    return out


def kernel(x, c, ctx, c_ctx, w_ada, b_ada, w_in, w_gla_gate, b_gla_gate, gla_norm_w, s5_lam_re, s5_lam_im, s5_log_dt, s5_b_re, s5_b_im, s5_c_re, s5_c_im, s5_d, w_s5_glu, q_norm_w, k_norm_w, w_proj_gla, w_proj_s5, w_proj_attn, w_out, ln1_w, ln1_b, ln2_w, ln2_b, w_ffn_in, w_ffn_out, loss_target, m_c_ctx, m_w_ada, m_b_ada, m_w_in, m_w_gla_gate, m_b_gla_gate, m_gla_norm_w, m_s5_lam_re, m_s5_lam_im, m_s5_log_dt, m_s5_b_re, m_s5_b_im, m_s5_c_re, m_s5_c_im, m_s5_d, m_w_s5_glu, m_q_norm_w, m_k_norm_w, m_w_proj_gla, m_w_proj_s5, m_w_proj_attn, m_w_out, m_ln1_w, m_ln1_b, m_ln2_w, m_ln2_b, m_w_ffn_in, m_w_ffn_out, v_c_ctx, v_w_ada, v_b_ada, v_w_in, v_w_gla_gate, v_b_gla_gate, v_gla_norm_w, v_s5_lam_re, v_s5_lam_im, v_s5_log_dt, v_s5_b_re, v_s5_b_im, v_s5_c_re, v_s5_c_im, v_s5_d, v_w_s5_glu, v_q_norm_w, v_k_norm_w, v_w_proj_gla, v_w_proj_s5, v_w_proj_attn, v_w_out, v_ln1_w, v_ln1_b, v_ln2_w, v_ln2_b, v_w_ffn_in, v_w_ffn_out):
    args = dict(locals())
    w = {n: args[n] for n in WEIGHTS}
    m = {n: args["m_" + n] for n in WEIGHTS}
    v = {n: args["v_" + n] for n in WEIGHTS}
    L = w_in.shape[0]
    half = L // 2
    core = lax.axis_index("c").astype(jnp.int32)
    place = (2 * lax.axis_index("x") + lax.axis_index("y")).astype(jnp.int32)
    zero = jnp.zeros((), jnp.int32)
    where, by_core, by_place = jnp.stack([core, place]), jnp.stack([core, zero]), jnp.stack([zero, place])

    def padded(n):
        cols = w[n].shape[-1]
        extra = _padded_width(cols) - cols if n in COL_SHARDED else 0
        return jnp.pad(w[n], ((0, 0), (0, 0), (0, extra))) if extra else w[n]

    gathered = gather_shards([cast_place(padded(n), by_place) for n in SHARDED])
    gathered = {n: g if n in COL_SHARDED else g.reshape(L, 1, -1, g.shape[-1]) for n, g in zip(SHARDED, gathered)}
    gate_shards = [w[n] for n in GATE]
    gate_blocks = gather_blocks(_pack(gate_shards))
    per_chip = [_unpack(gate_blocks[2 * j], gate_shards) for j in range(N_CHIPS)]
    small = {n: w[n] for n in REPLICATED}
    small.update({n: jnp.concatenate([per_chip[j][i] for j in range(N_CHIPS)], -1) for i, n in enumerate(GATE)})

    g_sh = {}

    def loss_fn(x1, small):
        return local_loss(x1, c[0], ctx[0], loss_target[0], small, gathered, g_sh)

    loss, (gx, g_small) = jax.value_and_grad(loss_fn, argnums=(0, 1))(x[0], small)
    loss = lax.psum(loss, ("x", "y", "c"))

    parts = [g_sh[(n, l)].reshape((N_CHIPS, -1, g_sh[(n, l)].shape[-1])) for n in REDUCED for l in range(L)]
    theirs = swap_halves(parts)
    chip_sums, tiles = [], []
    for p, t in zip(parts, theirs):
        hr, cols = t.shape[1:]
        tr = _stream_rows(hr, cols)
        tiles.append(tr)
        blk = (None, tr, cols)
        chip_sums.append(sum_parts(
            "sum_cores", [(p, blk, functools.partial(lambda nb, s, r, wh: (s, wh[0] * nb + r, 0), hr // tr)),
                          (t, blk, lambda s, r, wh: (s, r, 0))],
            (t.shape, BF16, blk, lambda s, r, wh: (s, r, 0)), (N_CHIPS, hr // tr), by_core))
    recv = scatter_shards(chip_sums)
    finals = []
    for i, n in enumerate(REDUCED):
        rows, cols = parts[i * L].shape[1:]
        buf = jnp.zeros((L, rows, cols), F32)
        for l in range(L):
            k = i * L + l
            tr = tiles[k]
            nb = rows // 2 // tr
            blk = (None, tr, cols)
            terms = [(parts[k], blk, functools.partial(lambda nb, r, wh: (wh[1], wh[0] * nb + r, 0), nb)),
                     (theirs[k], blk, lambda r, wh: (wh[1], r, 0))]
            terms += [(recv[k], blk, functools.partial(lambda j, r, wh: (j, r, 0), j)) for j in range(3)]
            buf = sum_parts("sum_chips", terms,
                            ((L, rows, cols), F32, blk, functools.partial(lambda l, nb, r, wh: (l, wh[0] * nb + r, 0), l, nb)),
                            (nb,), where, into=buf)
        finals.append(buf)
    grads = {n: g[..., :w[n].shape[-1]].reshape(w[n].shape) for n, g in zip(REDUCED, join_halves(finals))}

    rows = jnp.stack([g_sh[("w_ada", l)][:2] for l in range(L)])
    cond = [rows, c[0]]
    cond_blocks = gather_blocks(_pack(cond))
    per_dev = [_unpack(cond_blocks[k], cond) for k in range(N_DEV)]
    vectors = jnp.stack([d[1] for d in per_dev] + [w["c_ctx"]] * N_DEV)
    left = rowwise("silu_cond", _f_silu, [vectors])[0]
    shard = w["w_ada"].shape[-1]
    ada = []
    for l in range(L):
        right = jnp.stack([d[0][l, 0] for d in per_dev] + [d[0][l, 1] for d in per_dev])
        right = lax.dynamic_slice_in_dim(right, place * shard, shard, axis=1)
        ada.append(_mm_tn("ada_dw", left, right, 1, True, F32))
    grads["w_ada"] = jnp.concatenate(ada, 0)

    small_parts = [g_small[n] for n in SMALL]
    blocks = gather_blocks(_pack(small_parts))
    prow = (None, PACK_ROWS, LANE)
    total = sum_parts("sum_devices",
                      [(blocks, prow, functools.partial(lambda k, r, wh: (k, r, 0), k)) for k in range(8)],
                      (blocks.shape[1:], F32, prow[1:], lambda r, wh: (r, 0)), (blocks.shape[1] // PACK_ROWS,),
                      jnp.zeros((2,), jnp.int32))
    grads.update(dict(zip(SMALL, _unpack(total, small_parts))))
    for n in GATE:
        width = w[n].shape[-1]
        grads[n] = lax.dynamic_slice_in_dim(grads[n], place * width, width, axis=-1)

    delta, new_m, new_v = {}, {}, {}
    for n in SHARDED:
        d3, m3, v3 = adamw("adamw", _view3(w[n], 1), _view3(grads[n], 1), _view3(m[n], 1), _view3(v[n], 1))
        delta[n], new_m[n], new_v[n] = (t.reshape(w[n].shape) for t in (d3, m3, v3))
    packed = [_pack([d[n] for n in SMALL])[None] for d in (w, grads, m, v)]
    d3, m3, v3 = adamw("adamw_small", *packed)
    like = [w[n] for n in SMALL]
    for dst, src in ((delta, d3), (new_m, m3), (new_v, v3)):
        dst.update(dict(zip(SMALL, _unpack(src[0], like))))

    return (loss, gx[None], *[grads[n] for n in WEIGHTS], *[delta[n] for n in WEIGHTS],
            *[new_m[n] for n in WEIGHTS], *[new_v[n] for n in WEIGHTS])
```

```python
import functools

import jax
import jax.numpy as jnp
from jax import lax
from jax.experimental import pallas as pl
from jax.experimental.pallas import tpu as pltpu

F32 = jnp.float32
BF16 = jnp.bfloat16

GRID_W = 64
GLA_HEADS = 4
GLA_DK = 128
GLA_DV = 256
GLA_GATE_RANK = 16
GLA_GATE_TAU = 16.0
GLA_CHUNK = 64
S5_GROUP = 16
S5_STATE = 64
ATTN_Q_HEADS = 8
ATTN_KV_HEADS = 2
ATTN_HEAD_DIM = 128
ROPE_THETA = 10000.0
DEPTH = 4
DN_ALPHA = (2 * DEPTH) ** 0.25
EPS = 1e-6
ADAM_LR = 0.001
ADAM_B1 = 0.9
ADAM_B2 = 0.999
ADAM_EPS = 1e-08
ADAM_WD = 0.01
ADAM_STEP = 10

LANE = 128
SUBLANE = 8
VMEM_LIMIT = 56 * 1024 * 1024
ADA_ROWS = 16
S5_CHUNK = 128
S5_BLOCK_GROUPS = 8
ROW_TILE = 256
COL_TILE = 512
PACK_ROWS = 512
SHARD_PAD = 1024
N_CHIPS = 4


def _params(sem, **kw):
    return pltpu.CompilerParams(dimension_semantics=sem, vmem_limit_bytes=VMEM_LIMIT, **kw)


def _tile(n, target, base):
    if n <= target:
        return n
    best = None
    for t in range(base, target + 1, base):
        if n % t == 0:
            best = t
    assert best is not None, (n, target, base)
    return best


_DIMS = {"nn": ((1,), (0,)), "nt": ((1,), (1,)), "tn": ((0,), (0,))}


def _dg(a, b, mode):
    return lax.dot_general(a.astype(BF16), b.astype(BF16), (_DIMS[mode], ((), ())), preferred_element_type=F32)


@functools.partial(jax.custom_vjp, nondiff_argnums=(2,))
def bdot(a, b, mode):
    return _dg(a, b, mode)


def _bdot_fwd(a, b, mode):
    return _dg(a, b, mode), (a, b)


def _bdot_bwd(mode, res, g):
    a, b = res
    if mode == "nn":
        return bdot(g, b, "nt").astype(a.dtype), bdot(a, g, "tn").astype(b.dtype)
    if mode == "nt":
        return bdot(g, b, "nn").astype(a.dtype), bdot(g, a, "tn").astype(b.dtype)
    return bdot(b, g, "nt").astype(a.dtype), bdot(a, g, "nn").astype(b.dtype)


bdot.defvjp(_bdot_fwd, _bdot_bwd)


MM_TILES = {"nn": (1152, 1024, 1664), "nt": (1152, 1024, 1664), "tn": (768, 2048, 1408)}


def _mm_tiles(mode, M, Kb, Nb):
    tm, tk, tn = MM_TILES[mode]
    return _tile(M, tm, SUBLANE), _tile(Kb, tk, LANE), _tile(Nb, tn, LANE)


def _mm_body(mode, reduce_axes):
    def body(p_ref, q_ref, o_ref, acc):
        first = functools.reduce(jnp.logical_and, [pl.program_id(ax) == 0 for ax, _ in reduce_axes])
        last = functools.reduce(jnp.logical_and, [pl.program_id(ax) == n - 1 for ax, n in reduce_axes])

        @pl.when(first)
        def _():
            acc[...] = jnp.zeros_like(acc)

        acc[...] += _dg(p_ref[...], q_ref[...], mode)

        @pl.when(last)
        def _():
            o_ref[...] = acc[...].astype(o_ref.dtype)

    return body


def _mm_nn(name, a, w, l, share):
    M = a.shape[0]
    _, B, Kb, Nb = w.shape
    tm, tk, tn = _mm_tiles("nn", M, Kb, Nb)
    nk, nn = Kb // tk, Nb // tn
    a_map = (lambda b, i, j, k: (i, k)) if share else (lambda b, i, j, k: (i, b * nk + k))
    return pl.pallas_call(
        _mm_body("nn", [(3, nk)]), name=name, grid=(B, M // tm, nn, nk),
        in_specs=[pl.BlockSpec((tm, tk), a_map),
                  pl.BlockSpec((None, None, tk, tn), lambda b, i, j, k: (l, b, k, j))],
        out_specs=pl.BlockSpec((tm, tn), lambda b, i, j, k: (i, b * nn + j)),
        out_shape=jax.ShapeDtypeStruct((M, B * Nb), F32),
        scratch_shapes=[pltpu.VMEM((tm, tn), F32)],
        compiler_params=_params(("arbitrary",) * 4))(a, w)


def _mm_nt(name, g, w, l, share):
    M = g.shape[0]
    _, B, Kb, Nb = w.shape
    tm, tk, tn = _mm_tiles("nt", M, Kb, Nb)
    nk, nn = Kb // tk, Nb // tn
    if share:
        grid, red = (M // tm, nk, B, nn), [(2, B), (3, nn)]
        g_map, w_map = (lambda i, k, b, n: (i, b * nn + n)), (lambda i, k, b, n: (l, b, k, n))
        o_map, width = (lambda i, k, b, n: (i, k)), Kb
    else:
        grid, red = (B, M // tm, nk, nn), [(3, nn)]
        g_map, w_map = (lambda b, i, k, n: (i, b * nn + n)), (lambda b, i, k, n: (l, b, k, n))
        o_map, width = (lambda b, i, k, n: (i, b * nk + k)), B * Kb
    return pl.pallas_call(
        _mm_body("nt", red), name=name, grid=grid,
        in_specs=[pl.BlockSpec((tm, tn), g_map), pl.BlockSpec((None, None, tk, tn), w_map)],
        out_specs=pl.BlockSpec((tm, tk), o_map),
        out_shape=jax.ShapeDtypeStruct((M, width), F32),
        scratch_shapes=[pltpu.VMEM((tm, tk), F32)],
        compiler_params=_params(("arbitrary",) * 4))(g, w)


def _mm_tn(name, a, g, B, share, dtype):
    M = a.shape[0]
    Kb, Nb = a.shape[1] // (1 if share else B), g.shape[1] // B
    tm, tk, tn = _mm_tiles("tn", M, Kb, Nb)
    nk, nn = Kb // tk, Nb // tn
    a_map = (lambda b, k, j, m: (m, k)) if share else (lambda b, k, j, m: (m, b * nk + k))
    return pl.pallas_call(
        _mm_body("tn", [(3, M // tm)]), name=name, grid=(B, nk, nn, M // tm),
        in_specs=[pl.BlockSpec((tm, tk), a_map),
                  pl.BlockSpec((tm, tn), lambda b, k, j, m: (m, b * nn + j))],
        out_specs=pl.BlockSpec((None, tk, tn), lambda b, k, j, m: (b, k, j)),
        out_shape=jax.ShapeDtypeStruct((B, Kb, Nb), dtype),
        scratch_shapes=[pltpu.VMEM((tk, tn), F32)],
        compiler_params=_params(("arbitrary",) * 4))(a, g)


def mm(name, a, w):
    w3 = w if w.ndim == 3 else w[None]

    @jax.custom_vjp
    def op(a, w3):
        return _mm_nn(name + "_fwd", a, w3.astype(BF16)[None], 0, False)

    def fwd(a, w3):
        wb = w3.astype(BF16)[None]
        return _mm_nn(name + "_fwd", a, wb, 0, False), (a, wb)

    def bwd(res, g):
        a, wb = res
        return _mm_nt(name + "_dx", g, wb, 0, False), _mm_tn(name + "_dw", a, g, wb.shape[1], False, F32)

    op.defvjp(fwd, bwd)
    return op(a, w3)


def mm_gathered(name, a, w, l, grads, key, defer=False):
    @jax.custom_vjp
    def op(a):
        return _mm_nn(name + "_fwd", a, w, l, True)

    def fwd(a):
        return _mm_nn(name + "_fwd", a, w, l, True), (a,)

    def bwd(res, g):
        (a,) = res
        grads[key] = g if defer else _mm_tn(name + "_dw", a, g, w.shape[1], True, BF16)
        return (_mm_nt(name + "_dx", g, w, l, True),)

    op.defvjp(fwd, bwd)
    return op(a)


def _spec_shape(spec, G, T):
    k = spec[0]
    if k == "row":
        return (T, spec[1])
    if k == "rowg":
        return (T, (G // spec[2]) * spec[1])
    if k == "bc":
        return (spec[1], spec[2])
    return (spec[1], (G // spec[3]) * spec[2])


def _spec_block(spec, tm, rmap):
    k = spec[0]
    if k == "row":
        return pl.BlockSpec((tm, spec[1]), lambda g, r: (rmap(r), 0))
    if k == "rowg":
        d = spec[2]
        return pl.BlockSpec((tm, spec[1]), lambda g, r: (rmap(r), g // d))
    if k == "bc":
        return pl.BlockSpec((spec[1], spec[2]), lambda g, r: (0, 0))
    d = spec[3]
    return pl.BlockSpec((spec[1], spec[2]), lambda g, r: (0, g // d))


def block_op(name, f, in_specs, out_specs, G, T, tm, diff, carry=(), row0=False, order=None, f_saved=None):
    n_in, n_out, n_c = len(in_specs), len(out_specs), len(carry)
    n_sv = n_out if f_saved is not None else 0
    n_steps = T // tm
    assert T % tm == 0
    order = order or (lambda s: s)
    diff_idx = [i for i in range(n_in) if diff[i]]
    for i in diff_idx:
        assert in_specs[i][0] != "row" or G == 1
        assert in_specs[i][0] != "rowg" or in_specs[i][2] == 1
    out_shapes = [jax.ShapeDtypeStruct(_spec_shape(s, G, T), F32) for s in out_specs]
    save_shapes = [jax.ShapeDtypeStruct((n_steps, a, G * b), F32) for a, b in carry]
    sem = ("arbitrary", "arbitrary")

    def call_f(r_idx, cvals, vals):
        args = list(vals)
        if n_c:
            args = [tuple(cvals)] + args
        if row0:
            args = [r_idx * tm] + args
        return f(*args)

    def fwd_body(*refs):
        in_refs = refs[:n_in]
        out_refs = refs[n_in:n_in + n_out]
        save_refs = refs[n_in + n_out:n_in + n_out + n_c]
        c_refs = refs[n_in + n_out + n_c:]
        r = pl.program_id(1)
        if n_c:
            @pl.when(r == 0)
            def _():
                for c in c_refs:
                    c[...] = jnp.zeros_like(c)

            cvals = [c[...] for c in c_refs]
            for s, v in zip(save_refs, cvals):
                s[...] = v
            new_c, outs = call_f(r, cvals, [x[...] for x in in_refs])
            for c, v in zip(c_refs, new_c):
                c[...] = v
        else:
            outs = call_f(r, (), [x[...] for x in in_refs])
        for o, v in zip(out_refs, outs):
            o[...] = v.astype(F32)

    def fwd_call(*arrays):
        res = pl.pallas_call(
            fwd_body, name=name + "_fwd", grid=(G, n_steps),
            in_specs=[_spec_block(s, tm, order) for s in in_specs],
            out_specs=[_spec_block(s, tm, order) for s in out_specs]
            + [pl.BlockSpec((None, a, b), lambda g, r: (r, 0, g)) for a, b in carry],
            out_shape=out_shapes + save_shapes,
            scratch_shapes=[pltpu.VMEM((a, b), F32) for a, b in carry],
            compiler_params=_params(sem))(*arrays)
        return tuple(res)

    def bwd_body(*refs):
        in_refs = refs[:n_in]
        save_refs = refs[n_in:n_in + n_c]
        ct_refs = refs[n_in + n_c:n_in + n_c + n_out]
        at = n_in + n_c + n_out
        sv_refs = refs[at:at + n_sv]
        g_refs = refs[at + n_sv:at + n_sv + len(diff_idx)]
        dc_refs = refs[at + n_sv + len(diff_idx):]
        g = pl.program_id(0)
        r = pl.program_id(1)
        vals = [x[...] for x in in_refs]
        if n_c:
            @pl.when(r == 0)
            def _():
                for d in dc_refs:
                    d[...] = jnp.zeros_like(d)

        def fun(cvals, dvals):
            full = list(vals)
            for i, v in zip(diff_idx, dvals):
                full[i] = v
            if n_sv:
                return f_saved(tuple(cvals), *full, *[s[...] for s in sv_refs])
            return call_f(n_steps - 1 - r if n_c else r, cvals, full)

        _, vjp = jax.vjp(fun, tuple(s[...] for s in save_refs), tuple(vals[i] for i in diff_idx))
        cts = tuple(c[...] for c in ct_refs)
        if n_c:
            cts = (tuple(d[...] for d in dc_refs), cts)
        dcin, dvals = vjp(cts)
        for d, v in zip(dc_refs, dcin):
            d[...] = v
        for gref, i, v in zip(g_refs, diff_idx, dvals):
            spec = in_specs[i]
            if spec[0] in ("row", "rowg"):
                gref[...] = v
            else:
                first = (r == 0) & ((g == 0) if spec[0] == "bc" else (g % spec[3] == 0))

                @pl.when(first)
                def _(gref=gref, v=v):
                    gref[...] = v

                @pl.when(jnp.logical_not(first))
                def _(gref=gref, v=v):
                    gref[...] += v

    def bwd_call(arrays, saved, cts, outs):
        rmap = (lambda r: order(n_steps - 1 - r)) if n_c else (lambda r: r)
        res = pl.pallas_call(
            bwd_body, name=name + "_bwd", grid=(G, n_steps),
            in_specs=[_spec_block(s, tm, rmap) for s in in_specs]
            + [pl.BlockSpec((None, a, b), lambda g, r: (n_steps - 1 - r, 0, g)) for a, b in carry]
            + [_spec_block(s, tm, rmap) for s in out_specs] * (2 if n_sv else 1),
            out_specs=[_spec_block(in_specs[i], tm, rmap) for i in diff_idx],
            out_shape=[jax.ShapeDtypeStruct(_spec_shape(in_specs[i], G, T), F32) for i in diff_idx],
            scratch_shapes=[pltpu.VMEM((a, b), F32) for a, b in carry],
            compiler_params=_params(sem))(*arrays, *saved, *cts, *outs)
        return tuple(res)

    @jax.custom_vjp
    def op(*arrays):
        return fwd_call(*arrays)[:n_out]

    def op_fwd(*arrays):
        res = fwd_call(*arrays)
        return res[:n_out], (arrays, res[n_out:], res[:n_out] if n_sv else ())

    def op_bwd(res, cts):
        arrays, saved, outs = res
        grads = bwd_call(arrays, saved, cts, outs)
        out = [jnp.zeros_like(a) for a in arrays]
        for i, gval in zip(diff_idx, grads):
            out[i] = gval
        return tuple(out)

    op.defvjp(op_fwd, op_bwd)
    return op


def _rows(n, m):
    return lax.broadcasted_iota(jnp.int32, (n, m), 0)


def _ctx_select(row0, tm, n_ctx, v_lat, v_ctx):
    if n_ctx == 0:
        return v_lat
    is_ctx = (row0 + _rows(tm, 1)) < n_ctx
    return jnp.where(is_ctx, v_ctx, v_lat)


def _silu(x):
    return x * jax.nn.sigmoid(x)


def _f_modulate(tm, n_ctx):
    def f(row0, x, sh_l, sh_c, sc_l, sc_c):
        sh = _ctx_select(row0, tm, n_ctx, sh_l, sh_c)
        sc = _ctx_select(row0, tm, n_ctx, sc_l, sc_c)
        return (x * (1 + sc) + sh,)
    return f


def _f_postnorm(tm, n_ctx):
    def f(row0, x, y, g_l, g_c, w, b):
        z = DN_ALPHA * x + _ctx_select(row0, tm, n_ctx, g_l, g_c) * y
        mu = jnp.mean(z, -1, keepdims=True)
        zc = z - mu
        var = jnp.mean(zc * zc, -1, keepdims=True)
        return (zc * lax.rsqrt(var + EPS) * w + b,)
    return f


def _log_sigmoid(x):
    return -(jnp.maximum(-x, 0.0) + jnp.log1p(jnp.exp(-jnp.abs(x))))


def _f_gla_prep(glr, wg0, wg1, b0, b1):
    return (_log_sigmoid(bdot(glr, wg0, "nn") + b0) / GLA_GATE_TAU,
            _log_sigmoid(bdot(glr, wg1, "nn") + b1) / GLA_GATE_TAU)


def _f_gla_step(rev):
    def f(carry, q, k, v, la):
        (st,) = carry
        n = q.shape[0]
        cols = lax.broadcasted_iota(jnp.int32, (n, n), 1)
        tri = (_rows(n, n) <= cols) if rev else (_rows(n, n) >= cols)
        b = jnp.dot(tri.astype(F32), la, precision=lax.Precision.HIGHEST)
        qe = q * (GLA_DK ** -0.5) * jnp.exp(b)
        ke = k * jnp.exp(-b)
        att = jnp.where(tri, bdot(qe, ke, "nt"), 0.0)
        o = bdot(att, v, "nn") + bdot(qe, st, "nt")
        end = 0 if rev else n - 1
        b_last = jnp.sum(jnp.where(_rows(n, 1) == end, b, 0.0), axis=0, keepdims=True)
        kd = k * jnp.exp(b_last - b)
        st = st * jnp.exp(b_last) + bdot(v, kd, "tn")
        return (st,), (o,)
    return f


def _f_gla_norm(o0, o1, gr, w):
    o = o0 + o1
    mu = jnp.mean(o, -1, keepdims=True)
    oc = o - mu
    var = jnp.mean(oc * oc, -1, keepdims=True)
    return (oc * lax.rsqrt(var + EPS) * w * _silu(gr),)


@functools.partial(jax.custom_vjp, nondiff_argnums=(1, 2))
def _shift_rows(x, d, up):
    n = x.shape[0]
    rows = _rows(n, 1)
    if up:
        return jnp.where(rows < n - d, pltpu.roll(x, n - d, 0), 0.0)
    return jnp.where(rows >= d, pltpu.roll(x, d, 0), 0.0)


def _shift_fwd(x, d, up):
    return _shift_rows(x, d, up), None


def _shift_bwd(d, up, _, g):
    return (_shift_rows(g, d, not up),)


_shift_rows.defvjp(_shift_fwd, _shift_bwd)


def _scan_doubling(ur, ui, ar, ai, rev):
    n = ur.shape[0]
    xr, xi, pr, pi = ur, ui, ar, ai
    d = 1
    while d < n:
        sr, si = _shift_rows(xr, d, rev), _shift_rows(xi, d, rev)
        xr, xi = xr + pr * sr - pi * si, xi + pr * si + pi * sr
        pr, pi = pr * pr - pi * pi, 2 * pr * pi
        d *= 2
    return xr, xi


@functools.partial(jax.custom_vjp, nondiff_argnums=(6,))
def _scan_known(ur, ui, ar, ai, xr, xi, rev):
    return xr, xi


def _scan_known_fwd(ur, ui, ar, ai, xr, xi, rev):
    return (xr, xi), (ar, ai, xr, xi)


def _scan_known_bwd(rev, res, g):
    ar, ai, xr, xi = res
    lr, li = _scan_doubling(g[0], g[1], ar, -ai, not rev)
    pr, pi = _shift_rows(xr, 1, rev), _shift_rows(xi, 1, rev)
    dar = jnp.sum(lr * pr + li * pi, axis=0, keepdims=True)
    dai = jnp.sum(li * pr - lr * pi, axis=0, keepdims=True)
    return lr, li, dar, dai, jnp.zeros_like(xr), jnp.zeros_like(xi)


_scan_known.defvjp(_scan_known_fwd, _scan_known_bwd)


def _f_s5_step(rev, known=False):
    def f(carry, bur, bui, lam_re, lam_im, log_dt, *states):
        cr, ci = carry
        n = bur.shape[0]
        dt = jnp.exp(log_dt)
        mag = jnp.exp(lam_re * dt)
        ar, ai = mag * jnp.cos(lam_im * dt), mag * jnp.sin(lam_im * dt)
        den = lam_re * lam_re + lam_im * lam_im
        nr, ni = ar - 1, ai
        kr = (nr * lam_re + ni * lam_im) / den
        ki = (ni * lam_re - nr * lam_im) / den
        first = _rows(n, 1) == (n - 1 if rev else 0)
        ur = kr * bur - ki * bui + jnp.where(first, ar * cr - ai * ci, 0.0)
        ui = kr * bui + ki * bur + jnp.where(first, ar * ci + ai * cr, 0.0)
        if known:
            xr, xi = _scan_known(ur, ui, ar, ai, states[0], states[1], rev)
        else:
            xr, xi = _scan_doubling(ur, ui, ar, ai, rev)
        last = _rows(n, 1) == (0 if rev else n - 1)
        cr = jnp.sum(jnp.where(last, xr, 0.0), axis=0, keepdims=True)
        ci = jnp.sum(jnp.where(last, xi, 0.0), axis=0, keepdims=True)
        return (cr, ci), (xr, xi)
    return f


def _f_s5_post(su, dskip, y0r, y0i, y1r, y1i):
    return (jax.nn.gelu(su * dskip + y0r - y0i + y1r - y1i),)


def _f_s5_glu(y, t):
    return (y * jax.nn.sigmoid(t),)


def _swap_pairs(x):
    lane = lax.broadcasted_iota(jnp.int32, x.shape, 1)
    return jnp.where(lane % 2 == 0, pltpu.roll(x, x.shape[1] - 1, 1), pltpu.roll(x, 1, 1))


@jax.custom_vjp
def _rope(x, cos2, sin2):
    return x * cos2 + _swap_pairs(x) * sin2


def _rope_fwd(x, cos2, sin2):
    return _rope(x, cos2, sin2), (cos2, sin2)


def _rope_bwd(res, g):
    cos2, sin2 = res
    return g * cos2 + _swap_pairs(g * sin2), jnp.zeros_like(cos2), jnp.zeros_like(sin2)


_rope.defvjp(_rope_fwd, _rope_bwd)


def _f_qk_norm_rope(x, cos2, sin2, w):
    xn = x * lax.rsqrt(jnp.mean(x * x, -1, keepdims=True) + EPS) * w
    return (_rope(xn, cos2, sin2),)


def _f_attn(q, k, v):
    s = bdot(q, k, "nt") * (ATTN_HEAD_DIM ** -0.5)
    e = jnp.exp(s - jnp.max(s, -1, keepdims=True))
    p = e / jnp.sum(e, -1, keepdims=True)
    return (bdot(p, v, "nn"),)


def _f_merge(ga, gb, gc, pa, pb, pc):
    return (jax.nn.sigmoid(ga) * pa + jax.nn.sigmoid(gb) * pb + jax.nn.sigmoid(gc) * pc,)


def _f_swiglu(a, b):
    return (_silu(a) * b,)


def _f_silu(x):
    return (_silu(x),)


def _f_add_bias(x, b):
    return (x + b,)


def _scan_order(rev, n_ctx, T, tm):
    if not rev:
        return None
    nc, n = n_ctx // tm, T // tm
    return lambda s: jnp.where(s < nc, nc - 1 - s, n - 1 - (s - nc))


def modulate(name, x, sh, sc, n_ctx):
    T, D = x.shape
    tm = _tile(T, ROW_TILE, SUBLANE)
    cw = _tile(D, COL_TILE, LANE)
    col, vec = ("rowg", cw, 1), ("bcg", 1, cw, 1)
    op = block_op(name, _f_modulate(tm, n_ctx), [col, vec, vec, vec, vec], [col], D // cw, T, tm,
                  [True] * 5, row0=True)
    return op(x, sh[0], sh[1], sc[0], sc[1])[0]


def postnorm(name, x, y, g, w, b, n_ctx):
    T, D = x.shape
    tm = _tile(T, ROW_TILE, SUBLANE)
    vec = ("bc", 1, D)
    op = block_op(name, _f_postnorm(tm, n_ctx), [("row", D), ("row", D), vec, vec, vec, vec], [("row", D)], 1, T,
                  tm, [True] * 6, row0=True)
    return op(x, y, g[0], g[1], w, b)[0]


def rowwise(name, f, arrays, n_out=1):
    T, w = arrays[0].shape
    tm = _tile(T, ROW_TILE, SUBLANE)
    cw = _tile(w, COL_TILE, LANE)
    col = ("rowg", cw, 1)
    op = block_op(name, f, [col] * len(arrays), [col] * n_out, w // cw, T, tm, [True] * len(arrays))
    return op(*arrays)


def gla_prep(name, glr, wg, bg):
    T = glr.shape[0]
    qk = wg.shape[-1]
    tm = _tile(T, ROW_TILE, SUBLANE)
    op = block_op(name, _f_gla_prep, [("row", LANE), ("bc", LANE, qk), ("bc", LANE, qk), ("bc", 1, qk), ("bc", 1, qk)],
                  [("row", qk), ("row", qk)], 1, T, tm, [True] * 5)
    return op(glr, wg[0], wg[1], bg[0], bg[1])


def gla_scan(name, q, k, v, la, rev, n_ctx):
    T = q.shape[0]
    op = block_op(name, _f_gla_step(rev), [("rowg", GLA_DK, 1), ("rowg", GLA_DK, 1), ("rowg", GLA_DV, 1), ("rowg", GLA_DK, 1)],
                  [("rowg", GLA_DV, 1)], GLA_HEADS, T, GLA_CHUNK, [True] * 4, carry=[(GLA_DV, GLA_DK)],
                  order=_scan_order(rev, n_ctx, T, GLA_CHUNK))
    return op(q, k, v, la)[0]


def gla_norm(name, o0, o1, gr, w):
    T = o0.shape[0]
    tm = _tile(T, ROW_TILE, SUBLANE)
    hd = ("rowg", GLA_DV, 1)
    op = block_op(name, _f_gla_norm, [hd, hd, hd, ("bcg", 1, GLA_DV, 1)], [hd], GLA_HEADS, T, tm, [True] * 4)
    return op(o0, o1, gr, w)[0]


def s5_scan(name, bur, bui, lam_re, lam_im, log_dt, rev, n_ctx):
    T, S = bur.shape
    cols = _tile(S, 768, LANE)
    G = S // cols
    col, par = ("rowg", cols, 1), ("bcg", 1, cols, 1)
    op = block_op(name, _f_s5_step(rev), [col, col, par, par, par], [col, col], G, T, S5_CHUNK, [True] * 5,
                  carry=[(1, cols), (1, cols)], order=_scan_order(rev, n_ctx, T, S5_CHUNK),
                  f_saved=_f_s5_step(rev, True))
    return op(bur, bui, lam_re, lam_im, log_dt)


def qk_norm_rope(name, x, cos2, sin2, w):
    T = x.shape[0]
    G = x.shape[1] // ATTN_HEAD_DIM
    tm = _tile(T, ROW_TILE, SUBLANE)
    hd = ("rowg", ATTN_HEAD_DIM, 1)
    op = block_op(name, _f_qk_norm_rope, [hd, ("row", ATTN_HEAD_DIM), ("row", ATTN_HEAD_DIM), ("bc", 1, ATTN_HEAD_DIM)],
                  [hd], G, T, tm, [True, False, False, True])
    return op(x, cos2, sin2, w)[0]


def attention(name, q, k, v):
    T, Tk = q.shape[0], k.shape[0]
    tm = _tile(T, ROW_TILE, SUBLANE)
    grp = ATTN_Q_HEADS // ATTN_KV_HEADS
    kv = ("bcg", Tk, ATTN_HEAD_DIM, grp)
    op = block_op(name, _f_attn, [("rowg", ATTN_HEAD_DIM, 1), kv, kv], [("rowg", ATTN_HEAD_DIM, 1)], ATTN_Q_HEADS, T,
                  tm, [True] * 3)
    return op(q, k, v)[0]


def sq_loss(name, y, t):
    T, D = y.shape
    tm = _tile(T, ROW_TILE, SUBLANE)

    def fwd_body(y_ref, t_ref, o_ref):
        e = y_ref[...] - t_ref[...]
        part = jnp.sum(jnp.sum(e * e, -1, keepdims=True), 0, keepdims=True) * (0.5 / D)

        @pl.when(pl.program_id(0) == 0)
        def _():
            o_ref[...] = jnp.zeros_like(o_ref)

        o_ref[...] += part * jnp.ones((1, LANE), F32)

    def bwd_body(y_ref, t_ref, g_ref, o_ref):
        o_ref[...] = (y_ref[...] - t_ref[...]) * (g_ref[:, 0:1] / D)

    row = pl.BlockSpec((tm, D), lambda r: (r, 0))
    one = pl.BlockSpec((1, LANE), lambda r: (0, 0))

    def fwd_call(y, t):
        return pl.pallas_call(fwd_body, name=name + "_fwd", grid=(T // tm,), in_specs=[row, row], out_specs=one,
                              out_shape=jax.ShapeDtypeStruct((1, LANE), F32), compiler_params=_params(("arbitrary",)))(y, t)

    @jax.custom_vjp
    def op(y, t):
        return fwd_call(y, t)[0, 0]

    def op_fwd(y, t):
        return fwd_call(y, t)[0, 0], (y, t)

    def op_bwd(res, g):
        y, t = res
        gy = pl.pallas_call(bwd_body, name=name + "_bwd", grid=(T // tm,), in_specs=[row, row, one], out_specs=row,
                            out_shape=jax.ShapeDtypeStruct((T, D), F32), compiler_params=_params(("arbitrary",)))(
                                y, t, jnp.full((1, LANE), g, F32))
        return gy, jnp.zeros_like(t)

    op.defvjp(op_fwd, op_bwd)
    return op(y, t)


def _rope_tables(n_ctx, n_lat):
    n_rows = n_lat // GRID_W
    rows = jnp.repeat(jnp.arange(n_rows), GRID_W).astype(F32)
    cols = jnp.tile(jnp.arange(GRID_W), n_rows).astype(F32)
    n_freq = ATTN_HEAD_DIM // 4
    inv = ROPE_THETA ** (-jnp.arange(n_freq, dtype=F32) / n_freq)
    ang = jnp.concatenate([rows[:, None] * inv, cols[:, None] * inv], -1)
    cos2 = jnp.repeat(jnp.cos(ang), 2, axis=-1)
    sin2 = jnp.stack([-jnp.sin(ang), jnp.sin(ang)], -1).reshape(n_lat, ATTN_HEAD_DIM)
    cos2 = jnp.concatenate([jnp.ones((n_ctx, ATTN_HEAD_DIM), F32), cos2], 0)
    sin2 = jnp.concatenate([jnp.zeros((n_ctx, ATTN_HEAD_DIM), F32), sin2], 0)
    return cos2, sin2


def _in_layout(D, s5_width):
    qk, gv = GLA_HEADS * GLA_DK, GLA_HEADS * GLA_DV
    aq, akv = ATTN_Q_HEADS * ATTN_HEAD_DIM, ATTN_KV_HEADS * ATTN_HEAD_DIM
    widths = [("gq", qk), ("gk", qk), ("gv", gv), ("gr", gv), ("glr", GLA_GATE_RANK), ("su", s5_width), ("aq", aq),
              ("ak", akv), ("av", akv), ("ga", D), ("gb", D), ("gc", D)]
    off, out = 0, {}
    for n, w in widths:
        out[n] = (off, w)
        off += w
    return out, off


def _padded_width(width):
    return width if width % LANE == 0 else -(-width // SHARD_PAD) * SHARD_PAD


def _shard_cols(z, off, width, shard, padded):
    parts = []
    for j in range(N_CHIPS):
        lo, hi = max(off, j * shard), min(off + width, (j + 1) * shard)
        if lo < hi:
            parts.append(z[:, j * padded + lo - j * shard:j * padded + hi - j * shard])
    return parts[0] if len(parts) == 1 else jnp.concatenate(parts, 1)


def _s5_in_blocks(b):
    G, P, C = b.shape
    nb, bg = G // S5_BLOCK_GROUPS, S5_BLOCK_GROUPS
    t = b.reshape(nb, bg, P, C).transpose(0, 1, 3, 2)
    return jnp.einsum("bgcp,gh->bgchp", t, jnp.eye(bg, dtype=F32)).reshape(nb, bg * C, bg * P)


def _s5_out_blocks(c):
    G, C, P = c.shape
    nb, bg = G // S5_BLOCK_GROUPS, S5_BLOCK_GROUPS
    t = c.reshape(nb, bg, C, P).transpose(0, 1, 3, 2)
    return jnp.einsum("bgpc,gh->bgphc", t, jnp.eye(bg, dtype=F32)).reshape(nb, bg * P, bg * C)


def _layer(keep_ctx, xa, n_ctx, mod, p, big, cos2, sin2):
    T, D = xa.shape
    S = p["s5_d"].shape[-1]
    lay, width = _in_layout(D, S)
    lo = 0 if keep_ctx else n_ctx
    ctx_rows = n_ctx if keep_ctx else 0

    h = modulate("modulate1", xa, mod["sh1"], mod["sc1"], n_ctx)
    z = big["w_in"]("in_proj", h)
    shard = width // N_CHIPS
    zz = {n: _shard_cols(z, o, w, shard, z.shape[1] // N_CHIPS) for n, (o, w) in lay.items()}

    wg = jnp.pad(p["w_gla_gate"], ((0, 0), (0, LANE - GLA_GATE_RANK), (0, 0)))
    glr = jnp.pad(zz["glr"], ((0, 0), (0, LANE - GLA_GATE_RANK)))
    la0, la1 = gla_prep("gla_prep", glr, wg, p["b_gla_gate"][:, None, :])
    o0 = gla_scan("gla_scan", zz["gq"], zz["gk"], zz["gv"], la0, False, n_ctx)
    o1 = gla_scan("gla_scan_rev", zz["gq"], zz["gk"], zz["gv"], la1, True, n_ctx)
    o_gla = gla_norm("gla_norm", o0[lo:], o1[lo:], zz["gr"][lo:], p["gla_norm_w"][None, :])

    su = zz["su"]
    bur = mm("s5_in_re", su, _s5_in_blocks(p["s5_b_re"]))
    bui = mm("s5_in_im", su, _s5_in_blocks(p["s5_b_im"]))
    ys = []
    for d in range(2):
        row = lambda t: t.reshape(1, -1)
        ldt = jnp.repeat(p["s5_log_dt"][d], S5_STATE)
        sr, si = s5_scan("s5_scan_rev" if d else "s5_scan", bur, bui, row(p["s5_lam_re"][d]),
                         row(p["s5_lam_im"][d]), row(ldt), d == 1, n_ctx)
        ys.append(mm("s5_out_re", sr[lo:], _s5_out_blocks(p["s5_c_re"][d])))
        ys.append(mm("s5_out_im", si[lo:], _s5_out_blocks(p["s5_c_im"][d])))
    T2 = T - lo
    tm = _tile(T2, ROW_TILE, SUBLANE)
    post = block_op("s5_post", _f_s5_post, [("row", S), ("bc", 1, S)] + [("row", S)] * 4, [("row", S)], 1, T2, tm,
                    [True] * 6)
    yg = post(su[lo:], p["s5_d"][None, :], *ys)[0]
    o_s5 = rowwise("s5_glu", _f_s5_glu, [yg, big["w_s5_glu"]("s5_glu_proj", yg)])[0]

    qn = qk_norm_rope("q_norm_rope", zz["aq"], cos2, sin2, p["q_norm_w"][None, :])
    kn = qk_norm_rope("k_norm_rope", zz["ak"], cos2, sin2, p["k_norm_w"][None, :])
    o_attn = attention("attn_lat", qn[n_ctx:], kn, zz["av"])
    if keep_ctx:
        o_c = attention("attn_ctx", qn[:n_ctx], kn[:n_ctx], zz["av"][:n_ctx])
        o_attn = jnp.concatenate([o_c, o_attn], 0)

    merged = rowwise("merge", _f_merge, [zz["ga"][lo:], zz["gb"][lo:], zz["gc"][lo:],
                                         big["w_proj_gla"]("proj_gla", o_gla),
                                         big["w_proj_s5"]("proj_s5", o_s5),
                                         big["w_proj_attn"]("proj_attn", o_attn)])[0]
    mix = big["w_out"]("out_proj", merged)
    x1 = postnorm("postnorm1", xa[lo:], mix, mod["g1"], p["ln1_w"][None, :], p["ln1_b"][None, :], ctx_rows)
    h2 = modulate("modulate2", x1, mod["sh2"], mod["sc2"], ctx_rows)
    u = big["w_ffn_in"]("ffn_in", h2)
    F = u.shape[1] // 2
    act = rowwise("swiglu", _f_swiglu, [u[:, :F], u[:, F:]])[0]
    f = big["w_ffn_out"]("ffn_out", act)
    return postnorm("postnorm2", x1, f, mod["g2"], p["ln2_w"][None, :], p["ln2_b"][None, :], ctx_rows)


def local_loss(x, c, ctx, target, small, gathered, grads):
    n_lat, D = x.shape
    n_ctx = ctx.shape[0]
    cos2, sin2 = _rope_tables(n_ctx, n_lat)
    cc = jnp.concatenate([c[None, :], small["c_ctx"][None, :], jnp.zeros((ADA_ROWS - 2, D), F32)], 0)
    silu_cc = rowwise("silu_cond", _f_silu, [cc])[0]
    xa = jnp.concatenate([ctx, x], 0)
    depth = gathered["w_in"].shape[0]
    for l in range(depth):
        p = {n: v[l] for n, v in small.items() if n != "c_ctx"}
        big = {n: functools.partial(lambda n, l, name, a: mm_gathered(name, a, gathered[n], l, grads, (n, l),
                                                                      defer=n in DEFERRED), n, l)
               for n in gathered}
        m = big["w_ada"]("ada_proj", silu_cc)
        m = block_op("ada_bias", _f_add_bias, [("row", 6 * D), ("bc", 1, 6 * D)], [("row", 6 * D)], 1, ADA_ROWS,
                     ADA_ROWS, [True, True])(m, p["b_ada"][None, :])[0]
        names = ["sh1", "sc1", "g1", "sh2", "sc2", "g2"]
        mod = {n: (m[0:1, i * D:(i + 1) * D], m[1:2, i * D:(i + 1) * D]) for i, n in enumerate(names)}
        xa = _layer(l < depth - 1, xa, n_ctx, mod, p, big, cos2, sin2)
    return sq_loss("loss", xa, target)


MESH = pl.DeviceIdType.MESH
ANY = pl.BlockSpec(memory_space=pl.ANY)


def _place():
    x, y, c = lax.axis_index("x"), lax.axis_index("y"), lax.axis_index("c")
    chips = [(1 - x, y), (x, 1 - y), (1 - x, 1 - y)]
    return x, y, c, chips


def _rcopy(src, dst, ssem, rsem, to):
    return pltpu.make_async_remote_copy(src_ref=src, dst_ref=dst, send_sem=ssem, recv_sem=rsem, device_id=to,
                                        device_id_type=MESH)


def gather_shards(bufs):
    n = len(bufs)
    L = bufs[0].shape[0]
    half = L // 2

    def body(*refs):
        dst = refs[n:2 * n]
        isend, irecv, fsend, frecv, dsend, drecv = refs[2 * n:]
        x, y, c, _ = _place()
        j = 2 * x + y
        nbr = [(1 - x, y), (x, 1 - y)]
        jn = [2 * kx + ky for kx, ky in nbr]
        jd = 2 * (1 - x) + (1 - y)
        sibling = (x, y, 1 - c)
        mine, other = pl.ds(c * half, half), pl.ds((1 - c) * half, half)

        def piece(i, layers, chip, q):
            hr = bufs[i].shape[2] // 2
            return dst[i].at[layers, chip, pl.ds(q * hr, hr)]

        direct = [_rcopy(dst[i].at[mine, j], dst[i].at[mine, j], isend.at[i, r], irecv.at[i, r], (*nbr[r], c))
                  for i in range(n) for r in range(2)]
        for cp in direct:
            cp.start()
        passed = []
        for i in range(n):
            for r in range(2):
                part = dst[i].at[mine, jn[r]]
                _rcopy(part, part, isend.at[i, r], irecv.at[i, r], (*nbr[r], c)).wait_recv()
                fwd = piece(i, mine, jn[r], 1 - r)
                passed.append(_rcopy(fwd, fwd, fsend.at[i, r], frecv.at[i, 1 - r], (*nbr[1 - r], c)))
                passed.append(_rcopy(part, part, dsend.at[i, r], drecv.at[i, r], sibling))
                passed[-2].start()
                passed[-1].start()
        for i in range(n):
            for q in range(2):
                part = piece(i, mine, jd, q)
                _rcopy(part, part, fsend.at[i, q], frecv.at[i, q], (*nbr[q], c)).wait_recv()
                passed.append(_rcopy(part, part, dsend.at[i, 2 + q], drecv.at[i, 2 + q], sibling))
                passed[-1].start()
        for i in range(n):
            for r in range(2):
                part = dst[i].at[other, jn[r]]
                _rcopy(part, part, dsend.at[i, r], drecv.at[i, r], sibling).wait_recv()
                part = piece(i, other, jd, r)
                _rcopy(part, part, dsend.at[i, 2 + r], drecv.at[i, 2 + r], sibling).wait_recv()
        for cp in direct + passed:
            cp.wait_send()

    out_shape = [jax.ShapeDtypeStruct(b.shape, b.dtype) for b in bufs]
    sems = [pltpu.SemaphoreType.DMA((n, 2))] * 4 + [pltpu.SemaphoreType.DMA((n, 4))] * 2
    return pl.pallas_call(body, name="gather_shards", in_specs=[ANY] * n, out_specs=[ANY] * n, out_shape=out_shape,
                          scratch_shapes=sems, input_output_aliases={i: i for i in range(n)})(*bufs)


def swap_halves(grads):
    n = len(grads)

    def body(*refs):
        src, dst = refs[:n], refs[n:2 * n]
        ssem, rsem = refs[2 * n:]
        x, y, c, _ = _place()
        cps = []
        for i in range(n):
            hr = grads[i].shape[1] // 2
            cps.append(_rcopy(src[i].at[:, pl.ds((1 - c) * hr, hr)], dst[i], ssem.at[i], rsem.at[i], (x, y, 1 - c)))
        for cp in cps:
            cp.start()
        for cp in cps:
            cp.wait()

    out_shape = [jax.ShapeDtypeStruct((g.shape[0], g.shape[1] // 2, g.shape[2]), g.dtype) for g in grads]
    return pl.pallas_call(body, name="swap_halves", in_specs=[ANY] * n, out_specs=[ANY] * n, out_shape=out_shape,
                          scratch_shapes=[pltpu.SemaphoreType.DMA((n,))] * 2)(*grads)


def scatter_shards(parts):
    n = len(parts)

    def body(*refs):
        src, dst = refs[:n], refs[n:2 * n]
        ssem, rsem = refs[2 * n:]
        x, y, c, chips = _place()
        cps = [_rcopy(src[i].at[2 * kx + ky], dst[i].at[r], ssem.at[i, r], rsem.at[i, r], (kx, ky, c))
               for i in range(n) for r, (kx, ky) in enumerate(chips)]
        for cp in cps:
            cp.start()
        for cp in cps:
            cp.wait()

    out_shape = [jax.ShapeDtypeStruct((3,) + p.shape[1:], p.dtype) for p in parts]
    return pl.pallas_call(body, name="scatter_shards", in_specs=[ANY] * n, out_specs=[ANY] * n, out_shape=out_shape,
                          scratch_shapes=[pltpu.SemaphoreType.DMA((n, 3))] * 2)(*parts)


def join_halves(bufs):
    n = len(bufs)

    def body(*refs):
        dst = refs[n:2 * n]
        ssem, rsem = refs[2 * n:]
        x, y, c, _ = _place()
        cps = []
        for i in range(n):
            hr = bufs[i].shape[1] // 2
            mine = dst[i].at[:, pl.ds(c * hr, hr)]
            cps.append(_rcopy(mine, mine, ssem.at[i], rsem.at[i], (x, y, 1 - c)))
        for cp in cps:
            cp.start()
        for cp in cps:
            cp.wait()

    out_shape = [jax.ShapeDtypeStruct(b.shape, b.dtype) for b in bufs]
    return pl.pallas_call(body, name="join_halves", in_specs=[ANY] * n, out_specs=[ANY] * n, out_shape=out_shape,
                          scratch_shapes=[pltpu.SemaphoreType.DMA((n,))] * 2,
                          input_output_aliases={i: i for i in range(n)})(*bufs)


def gather_blocks(v):
    def body(v_ref, out_ref, send_sems, recv_sems, local_sem):
        x, y, c, chips = _place()
        me, sibling = (x, y, c), (x, y, 1 - c)

        def blk(px, py, pc):
            return out_ref.at[4 * px + 2 * py + pc]

        def copy(k, block, to, src=None):
            return _rcopy(blk(*block) if src is None else src, blk(*block), send_sems.at[k], recv_sems.at[k], to)

        own = pltpu.make_async_copy(v_ref, blk(*me), local_sem)
        own.start()
        first = [copy(0, me, sibling, src=v_ref)]
        first += [copy(1 + r, me, (*chip, c), src=v_ref) for r, chip in enumerate(chips)]
        for cp in first:
            cp.start()
        passed = [copy(4 + r, (*chip, c), sibling) for r, chip in enumerate(chips)]
        for r, chip in enumerate(chips):
            copy(1 + r, (*chip, c), me).wait_recv()
            passed[r].start()
        copy(0, sibling, me).wait_recv()
        for r, chip in enumerate(chips):
            copy(4 + r, (*chip, 1 - c), me).wait_recv()
        for cp in first + passed:
            cp.wait_send()
        own.wait()

    return pl.pallas_call(body, name="gather_blocks", in_specs=[ANY], out_specs=ANY,
                          out_shape=jax.ShapeDtypeStruct((8,) + v.shape, v.dtype),
                          scratch_shapes=[pltpu.SemaphoreType.DMA((7,)), pltpu.SemaphoreType.DMA((7,)),
                                          pltpu.SemaphoreType.DMA])(v)


STREAM_BLOCK = 256 * 1024


def _stream_rows(rows, cols):
    base = 2 * SUBLANE if rows % (2 * SUBLANE) == 0 else SUBLANE
    return _tile(rows, max(base, STREAM_BLOCK // cols // base * base), base)


def _view3(a, lead):
    shape = a.shape[:lead] + (-1, a.shape[-1])
    return a.reshape(shape)


def sum_parts(name, terms, out, grid, where, into=None):
    n_skip = 0 if into is None else 1

    def body(w_ref, *refs):
        refs = refs[n_skip:]
        acc = refs[0][...].astype(F32)
        for t in refs[1:-1]:
            acc = acc + t[...].astype(F32)
        refs[-1][...] = acc.astype(refs[-1].dtype)

    grid_spec = pltpu.PrefetchScalarGridSpec(
        num_scalar_prefetch=1, grid=grid, in_specs=[ANY] * n_skip + [pl.BlockSpec(b, f) for _, b, f in terms],
        out_specs=pl.BlockSpec(out[2], out[3]))
    operands = ([] if into is None else [into]) + [t[0] for t in terms]
    return pl.pallas_call(body, name=name, grid_spec=grid_spec, out_shape=jax.ShapeDtypeStruct(out[0], out[1]),
                          input_output_aliases={} if into is None else {1: 0},
                          compiler_params=_params(("arbitrary",) * len(grid)))(where, *operands)


def cast_place(w, where):
    w3 = _view3(w, 1)
    L, rows, cols = w3.shape
    tr = _stream_rows(rows, cols)

    def body(w_ref, src, dst):
        dst[...] = src[...].astype(BF16)

    grid_spec = pltpu.PrefetchScalarGridSpec(
        num_scalar_prefetch=1, grid=(L, rows // tr),
        in_specs=[pl.BlockSpec((None, tr, cols), lambda l, r, wh: (l, r, 0))],
        out_specs=pl.BlockSpec((None, None, tr, cols), lambda l, r, wh: (l, wh[1], r, 0)))
    out = pl.pallas_call(body, name="cast_place", grid_spec=grid_spec,
                         out_shape=jax.ShapeDtypeStruct((L, N_CHIPS, rows, cols), BF16),
                         compiler_params=_params(("arbitrary", "arbitrary")))(where, w3)
    return out.reshape((L, N_CHIPS) + w.shape[1:])


def adamw(name, w, g, m, v):
    n, rows, cols = w.shape
    tr = _stream_rows(rows, cols)

    def body(w_ref, g_ref, m_ref, v_ref, d_ref, nm_ref, nv_ref):
        gv = g_ref[...]
        nm = ADAM_B1 * m_ref[...] + (1.0 - ADAM_B1) * gv
        nv = ADAM_B2 * v_ref[...] + (1.0 - ADAM_B2) * (gv * gv)
        m_hat = nm / (1.0 - ADAM_B1 ** ADAM_STEP)
        v_hat = nv / (1.0 - ADAM_B2 ** ADAM_STEP)
        d_ref[...] = -ADAM_LR * (m_hat / (jnp.sqrt(v_hat) + ADAM_EPS) + ADAM_WD * w_ref[...])
        nm_ref[...] = nm
        nv_ref[...] = nv

    blk = pl.BlockSpec((None, tr, cols), lambda l, r: (l, r, 0))
    shp = jax.ShapeDtypeStruct(w.shape, F32)
    return pl.pallas_call(body, name=name, grid=(n, rows // tr), in_specs=[blk] * 4, out_specs=[blk] * 3,
                          out_shape=[shp] * 3, compiler_params=_params(("arbitrary", "arbitrary")))(w, g, m, v)


COL_SHARDED = ("w_ada", "w_in", "w_proj_gla", "w_proj_s5", "w_proj_attn", "w_ffn_in")
ROW_SHARDED = ("w_s5_glu", "w_out", "w_ffn_out")
SHARDED = COL_SHARDED + ROW_SHARDED
DEFERRED = ("w_ada",)
REDUCED = tuple(n for n in SHARDED if n not in DEFERRED)
N_DEV = 8
GATE = ("w_gla_gate", "b_gla_gate")
WEIGHTS = ("c_ctx", "w_ada", "b_ada", "w_in", "w_gla_gate", "b_gla_gate", "gla_norm_w", "s5_lam_re", "s5_lam_im",
           "s5_log_dt", "s5_b_re", "s5_b_im", "s5_c_re", "s5_c_im", "s5_d", "w_s5_glu", "q_norm_w", "k_norm_w",
           "w_proj_gla", "w_proj_s5", "w_proj_attn", "w_out", "ln1_w", "ln1_b", "ln2_w", "ln2_b", "w_ffn_in",
           "w_ffn_out")
REPLICATED = tuple(n for n in WEIGHTS if n not in SHARDED + GATE)
SMALL = REPLICATED + GATE


def _pack(arrays):
    flat = jnp.concatenate([a.reshape(-1) for a in arrays])
    pad = (-flat.shape[0]) % (PACK_ROWS * LANE)
    return jnp.pad(flat, (0, pad)).reshape(-1, LANE)


def _unpack(packed, like):
    flat, out, off = packed.reshape(-1), [], 0
    for a in like:
        out.append(flat[off:off + a.size].reshape(a.shape))
        off += a.size
    return out


def kernel(x, c, ctx, c_ctx, w_ada, b_ada, w_in, w_gla_gate, b_gla_gate, gla_norm_w, s5_lam_re, s5_lam_im, s5_log_dt, s5_b_re, s5_b_im, s5_c_re, s5_c_im, s5_d, w_s5_glu, q_norm_w, k_norm_w, w_proj_gla, w_proj_s5, w_proj_attn, w_out, ln1_w, ln1_b, ln2_w, ln2_b, w_ffn_in, w_ffn_out, loss_target, m_c_ctx, m_w_ada, m_b_ada, m_w_in, m_w_gla_gate, m_b_gla_gate, m_gla_norm_w, m_s5_lam_re, m_s5_lam_im, m_s5_log_dt, m_s5_b_re, m_s5_b_im, m_s5_c_re, m_s5_c_im, m_s5_d, m_w_s5_glu, m_q_norm_w, m_k_norm_w, m_w_proj_gla, m_w_proj_s5, m_w_proj_attn, m_w_out, m_ln1_w, m_ln1_b, m_ln2_w, m_ln2_b, m_w_ffn_in, m_w_ffn_out, v_c_ctx, v_w_ada, v_b_ada, v_w_in, v_w_gla_gate, v_b_gla_gate, v_gla_norm_w, v_s5_lam_re, v_s5_lam_im, v_s5_log_dt, v_s5_b_re, v_s5_b_im, v_s5_c_re, v_s5_c_im, v_s5_d, v_w_s5_glu, v_q_norm_w, v_k_norm_w, v_w_proj_gla, v_w_proj_s5, v_w_proj_attn, v_w_out, v_ln1_w, v_ln1_b, v_ln2_w, v_ln2_b, v_w_ffn_in, v_w_ffn_out):
    args = dict(locals())
    w = {n: args[n] for n in WEIGHTS}
    m = {n: args["m_" + n] for n in WEIGHTS}
    v = {n: args["v_" + n] for n in WEIGHTS}
    L = w_in.shape[0]
    half = L // 2
    core = lax.axis_index("c").astype(jnp.int32)
    place = (2 * lax.axis_index("x") + lax.axis_index("y")).astype(jnp.int32)
    zero = jnp.zeros((), jnp.int32)
    where, by_core, by_place = jnp.stack([core, place]), jnp.stack([core, zero]), jnp.stack([zero, place])

    def padded(n):
        cols = w[n].shape[-1]
        extra = _padded_width(cols) - cols if n in COL_SHARDED else 0
        return jnp.pad(w[n], ((0, 0), (0, 0), (0, extra))) if extra else w[n]

    gathered = gather_shards([cast_place(padded(n), by_place) for n in SHARDED])
    gathered = {n: g if n in COL_SHARDED else g.reshape(L, 1, -1, g.shape[-1]) for n, g in zip(SHARDED, gathered)}
    gate_shards = [w[n] for n in GATE]
    gate_blocks = gather_blocks(_pack(gate_shards))
    per_chip = [_unpack(gate_blocks[2 * j], gate_shards) for j in range(N_CHIPS)]
    small = {n: w[n] for n in REPLICATED}
    small.update({n: jnp.concatenate([per_chip[j][i] for j in range(N_CHIPS)], -1) for i, n in enumerate(GATE)})

    g_sh = {}

    def loss_fn(x1, small):
        return local_loss(x1, c[0], ctx[0], loss_target[0], small, gathered, g_sh)

    loss, (gx, g_small) = jax.value_and_grad(loss_fn, argnums=(0, 1))(x[0], small)
    loss = lax.psum(loss, ("x", "y", "c"))

    parts = [g_sh[(n, l)].reshape((N_CHIPS, -1, g_sh[(n, l)].shape[-1])) for n in REDUCED for l in range(L)]
    theirs = swap_halves(parts)
    chip_sums, tiles = [], []
    for p, t in zip(parts, theirs):
        hr, cols = t.shape[1:]
        tr = _stream_rows(hr, cols)
        tiles.append(tr)
        blk = (None, tr, cols)
        chip_sums.append(sum_parts(
            "sum_cores", [(p, blk, functools.partial(lambda nb, s, r, wh: (s, wh[0] * nb + r, 0), hr // tr)),
                          (t, blk, lambda s, r, wh: (s, r, 0))],
            (t.shape, BF16, blk, lambda s, r, wh: (s, r, 0)), (N_CHIPS, hr // tr), by_core))
    recv = scatter_shards(chip_sums)
    finals = []
    for i, n in enumerate(REDUCED):
        rows, cols = parts[i * L].shape[1:]
        buf = jnp.zeros((L, rows, cols), F32)
        for l in range(L):
            k = i * L + l
            tr = tiles[k]
            nb = rows // 2 // tr
            blk = (None, tr, cols)
            terms = [(parts[k], blk, functools.partial(lambda nb, r, wh: (wh[1], wh[0] * nb + r, 0), nb)),
                     (theirs[k], blk, lambda r, wh: (wh[1], r, 0))]
            terms += [(recv[k], blk, functools.partial(lambda j, r, wh: (j, r, 0), j)) for j in range(3)]
            buf = sum_parts("sum_chips", terms,
                            ((L, rows, cols), F32, blk, functools.partial(lambda l, nb, r, wh: (l, wh[0] * nb + r, 0), l, nb)),
                            (nb,), where, into=buf)
        finals.append(buf)
    grads = {n: g[..., :w[n].shape[-1]].reshape(w[n].shape) for n, g in zip(REDUCED, join_halves(finals))}

    rows = jnp.stack([g_sh[("w_ada", l)][:2] for l in range(L)])
    cond = [rows, c[0]]
    cond_blocks = gather_blocks(_pack(cond))
    per_dev = [_unpack(cond_blocks[k], cond) for k in range(N_DEV)]
    vectors = jnp.stack([d[1] for d in per_dev] + [w["c_ctx"]] * N_DEV)
    left = rowwise("silu_cond", _f_silu, [vectors])[0]
    shard = w["w_ada"].shape[-1]
    ada = []
    for l in range(L):
        right = jnp.stack([d[0][l, 0] for d in per_dev] + [d[0][l, 1] for d in per_dev])
        right = lax.dynamic_slice_in_dim(right, place * shard, shard, axis=1)
        ada.append(_mm_tn("ada_dw", left, right, 1, True, F32))
    grads["w_ada"] = jnp.concatenate(ada, 0)

    small_parts = [g_small[n] for n in SMALL]
    blocks = gather_blocks(_pack(small_parts))
    prow = (None, PACK_ROWS, LANE)
    total = sum_parts("sum_devices",
                      [(blocks, prow, functools.partial(lambda k, r, wh: (k, r, 0), k)) for k in range(8)],
                      (blocks.shape[1:], F32, prow[1:], lambda r, wh: (r, 0)), (blocks.shape[1] // PACK_ROWS,),
                      jnp.zeros((2,), jnp.int32))
    grads.update(dict(zip(SMALL, _unpack(total, small_parts))))
    for n in GATE:
        width = w[n].shape[-1]
        grads[n] = lax.dynamic_slice_in_dim(grads[n], place * width, width, axis=-1)

    delta, new_m, new_v = {}, {}, {}
    for n in SHARDED:
        d3, m3, v3 = adamw("adamw", _view3(w[n], 1), _view3(grads[n], 1), _view3(m[n], 1), _view3(v[n], 1))
        delta[n], new_m[n], new_v[n] = (t.reshape(w[n].shape) for t in (d3, m3, v3))
    packed = [_pack([d[n] for n in SMALL])[None] for d in (w, grads, m, v)]
    d3, m3, v3 = adamw("adamw_small", *packed)
    like = [w[n] for n in SMALL]
    for dst, src in ((delta, d3), (new_m, m3), (new_v, v3)):
        dst.update(dict(zip(SMALL, _unpack(src[0], like))))

    return (loss, gx[None], *[grads[n] for n in WEIGHTS], *[delta[n] for n in WEIGHTS],
            *[new_m[n] for n in WEIGHTS], *[new_v[n] for n in WEIGHTS])
```

```python
import functools

import jax
import jax.numpy as jnp
from jax import lax
from jax.experimental import pallas as pl
from jax.experimental.pallas import tpu as pltpu

F32 = jnp.float32
BF16 = jnp.bfloat16

GRID_W = 64
GLA_HEADS = 4
GLA_DK = 128
GLA_DV = 256
GLA_GATE_RANK = 16
GLA_GATE_TAU = 16.0
GLA_CHUNK = 64
S5_GROUP = 16
S5_STATE = 64
ATTN_Q_HEADS = 8
ATTN_KV_HEADS = 2
ATTN_HEAD_DIM = 128
ROPE_THETA = 10000.0
DEPTH = 4
DN_ALPHA = (2 * DEPTH) ** 0.25
EPS = 1e-6
ADAM_LR = 0.001
ADAM_B1 = 0.9
ADAM_B2 = 0.999
ADAM_EPS = 1e-08
ADAM_WD = 0.01
ADAM_STEP = 10

LANE = 128
SUBLANE = 8
VMEM_LIMIT = 56 * 1024 * 1024
ADA_ROWS = 16
S5_CHUNK = 128
S5_BLOCK_GROUPS = 8
ROW_TILE = 256
COL_TILE = 512
PACK_ROWS = 512
SHARD_PAD = 1024
N_CHIPS = 4


def _params(sem, **kw):
    return pltpu.CompilerParams(dimension_semantics=sem, vmem_limit_bytes=VMEM_LIMIT, **kw)


def _tile(n, target, base):
    if n <= target:
        return n
    best = None
    for t in range(base, target + 1, base):
        if n % t == 0:
            best = t
    assert best is not None, (n, target, base)
    return best


_DIMS = {"nn": ((1,), (0,)), "nt": ((1,), (1,)), "tn": ((0,), (0,))}


def _dg(a, b, mode):
    return lax.dot_general(a.astype(BF16), b.astype(BF16), (_DIMS[mode], ((), ())), preferred_element_type=F32)


@functools.partial(jax.custom_vjp, nondiff_argnums=(2,))
def bdot(a, b, mode):
    return _dg(a, b, mode)


def _bdot_fwd(a, b, mode):
    return _dg(a, b, mode), (a, b)


def _bdot_bwd(mode, res, g):
    a, b = res
    if mode == "nn":
        return bdot(g, b, "nt").astype(a.dtype), bdot(a, g, "tn").astype(b.dtype)
    if mode == "nt":
        return bdot(g, b, "nn").astype(a.dtype), bdot(g, a, "tn").astype(b.dtype)
    return bdot(b, g, "nt").astype(a.dtype), bdot(a, g, "nn").astype(b.dtype)


bdot.defvjp(_bdot_fwd, _bdot_bwd)


MM_TILES = {"nn": (1152, 1024, 1664), "nt": (1152, 1024, 1664), "tn": (768, 2048, 1408)}


def _mm_tiles(mode, M, Kb, Nb):
    tm, tk, tn = MM_TILES[mode]
    return _tile(M, tm, SUBLANE), _tile(Kb, tk, LANE), _tile(Nb, tn, LANE)


def _mm_body(mode, reduce_axes):
    def body(p_ref, q_ref, o_ref, acc):
        first = functools.reduce(jnp.logical_and, [pl.program_id(ax) == 0 for ax, _ in reduce_axes])
        last = functools.reduce(jnp.logical_and, [pl.program_id(ax) == n - 1 for ax, n in reduce_axes])

        @pl.when(first)
        def _():
            acc[...] = jnp.zeros_like(acc)

        acc[...] += _dg(p_ref[...], q_ref[...], mode)

        @pl.when(last)
        def _():
            o_ref[...] = acc[...].astype(o_ref.dtype)

    return body


def _mm_nn(name, a, w, l, share):
    M = a.shape[0]
    _, B, Kb, Nb = w.shape
    tm, tk, tn = _mm_tiles("nn", M, Kb, Nb)
    nk, nn = Kb // tk, Nb // tn
    a_map = (lambda b, i, j, k: (i, k)) if share else (lambda b, i, j, k: (i, b * nk + k))
    return pl.pallas_call(
        _mm_body("nn", [(3, nk)]), name=name, grid=(B, M // tm, nn, nk),
        in_specs=[pl.BlockSpec((tm, tk), a_map),
                  pl.BlockSpec((None, None, tk, tn), lambda b, i, j, k: (l, b, k, j))],
        out_specs=pl.BlockSpec((tm, tn), lambda b, i, j, k: (i, b * nn + j)),
        out_shape=jax.ShapeDtypeStruct((M, B * Nb), F32),
        scratch_shapes=[pltpu.VMEM((tm, tn), F32)],
        compiler_params=_params(("arbitrary",) * 4))(a, w)


def _mm_nt(name, g, w, l, share):
    M = g.shape[0]
    _, B, Kb, Nb = w.shape
    tm, tk, tn = _mm_tiles("nt", M, Kb, Nb)
    nk, nn = Kb // tk, Nb // tn
    if share:
        grid, red = (M // tm, nk, B, nn), [(2, B), (3, nn)]
        g_map, w_map = (lambda i, k, b, n: (i, b * nn + n)), (lambda i, k, b, n: (l, b, k, n))
        o_map, width = (lambda i, k, b, n: (i, k)), Kb
    else:
        grid, red = (B, M // tm, nk, nn), [(3, nn)]
        g_map, w_map = (lambda b, i, k, n: (i, b * nn + n)), (lambda b, i, k, n: (l, b, k, n))
        o_map, width = (lambda b, i, k, n: (i, b * nk + k)), B * Kb
    return pl.pallas_call(
        _mm_body("nt", red), name=name, grid=grid,
        in_specs=[pl.BlockSpec((tm, tn), g_map), pl.BlockSpec((None, None, tk, tn), w_map)],
        out_specs=pl.BlockSpec((tm, tk), o_map),
        out_shape=jax.ShapeDtypeStruct((M, width), F32),
        scratch_shapes=[pltpu.VMEM((tm, tk), F32)],
        compiler_params=_params(("arbitrary",) * 4))(g, w)


def _mm_tn(name, a, g, B, share, dtype):
    M = a.shape[0]
    Kb, Nb = a.shape[1] // (1 if share else B), g.shape[1] // B
    tm, tk, tn = _mm_tiles("tn", M, Kb, Nb)
    nk, nn = Kb // tk, Nb // tn
    a_map = (lambda b, k, j, m: (m, k)) if share else (lambda b, k, j, m: (m, b * nk + k))
    return pl.pallas_call(
        _mm_body("tn", [(3, M // tm)]), name=name, grid=(B, nk, nn, M // tm),
        in_specs=[pl.BlockSpec((tm, tk), a_map),
                  pl.BlockSpec((tm, tn), lambda b, k, j, m: (m, b * nn + j))],
        out_specs=pl.BlockSpec((None, tk, tn), lambda b, k, j, m: (b, k, j)),
        out_shape=jax.ShapeDtypeStruct((B, Kb, Nb), dtype),
        scratch_shapes=[pltpu.VMEM((tk, tn), F32)],
        compiler_params=_params(("arbitrary",) * 4))(a, g)


def mm(name, a, w):
    w3 = w if w.ndim == 3 else w[None]

    @jax.custom_vjp
    def op(a, w3):
        return _mm_nn(name + "_fwd", a, w3.astype(BF16)[None], 0, False)

    def fwd(a, w3):
        wb = w3.astype(BF16)[None]
        return _mm_nn(name + "_fwd", a, wb, 0, False), (a, wb)

    def bwd(res, g):
        a, wb = res
        return _mm_nt(name + "_dx", g, wb, 0, False), _mm_tn(name + "_dw", a, g, wb.shape[1], False, F32)

    op.defvjp(fwd, bwd)
    return op(a, w3)


def mm_gathered(name, a, w, l, grads, key):
    @jax.custom_vjp
    def op(a):
        return _mm_nn(name + "_fwd", a, w, l, True)

    def fwd(a):
        return _mm_nn(name + "_fwd", a, w, l, True), (a,)

    def bwd(res, g):
        (a,) = res
        grads[key] = _mm_tn(name + "_dw", a, g, w.shape[1], True, BF16)
        return (_mm_nt(name + "_dx", g, w, l, True),)

    op.defvjp(fwd, bwd)
    return op(a)


def _spec_shape(spec, G, T):
    k = spec[0]
    if k == "row":
        return (T, spec[1])
    if k == "rowg":
        return (T, (G // spec[2]) * spec[1])
    if k == "bc":
        return (spec[1], spec[2])
    return (spec[1], (G // spec[3]) * spec[2])


def _spec_block(spec, tm, rmap):
    k = spec[0]
    if k == "row":
        return pl.BlockSpec((tm, spec[1]), lambda g, r: (rmap(r), 0))
    if k == "rowg":
        d = spec[2]
        return pl.BlockSpec((tm, spec[1]), lambda g, r: (rmap(r), g // d))
    if k == "bc":
        return pl.BlockSpec((spec[1], spec[2]), lambda g, r: (0, 0))
    d = spec[3]
    return pl.BlockSpec((spec[1], spec[2]), lambda g, r: (0, g // d))


def block_op(name, f, in_specs, out_specs, G, T, tm, diff, carry=(), row0=False, order=None, f_saved=None):
    n_in, n_out, n_c = len(in_specs), len(out_specs), len(carry)
    n_sv = n_out if f_saved is not None else 0
    n_steps = T // tm
    assert T % tm == 0
    order = order or (lambda s: s)
    diff_idx = [i for i in range(n_in) if diff[i]]
    for i in diff_idx:
        assert in_specs[i][0] != "row" or G == 1
        assert in_specs[i][0] != "rowg" or in_specs[i][2] == 1
    out_shapes = [jax.ShapeDtypeStruct(_spec_shape(s, G, T), F32) for s in out_specs]
    save_shapes = [jax.ShapeDtypeStruct((n_steps, a, G * b), F32) for a, b in carry]
    sem = ("arbitrary", "arbitrary")

    def call_f(r_idx, cvals, vals):
        args = list(vals)
        if n_c:
            args = [tuple(cvals)] + args
        if row0:
            args = [r_idx * tm] + args
        return f(*args)

    def fwd_body(*refs):
        in_refs = refs[:n_in]
        out_refs = refs[n_in:n_in + n_out]
        save_refs = refs[n_in + n_out:n_in + n_out + n_c]
        c_refs = refs[n_in + n_out + n_c:]
        r = pl.program_id(1)
        if n_c:
            @pl.when(r == 0)
            def _():
                for c in c_refs:
                    c[...] = jnp.zeros_like(c)

            cvals = [c[...] for c in c_refs]
            for s, v in zip(save_refs, cvals):
                s[...] = v
            new_c, outs = call_f(r, cvals, [x[...] for x in in_refs])
            for c, v in zip(c_refs, new_c):
                c[...] = v
        else:
            outs = call_f(r, (), [x[...] for x in in_refs])
        for o, v in zip(out_refs, outs):
            o[...] = v.astype(F32)

    def fwd_call(*arrays):
        res = pl.pallas_call(
            fwd_body, name=name + "_fwd", grid=(G, n_steps),
            in_specs=[_spec_block(s, tm, order) for s in in_specs],
            out_specs=[_spec_block(s, tm, order) for s in out_specs]
            + [pl.BlockSpec((None, a, b), lambda g, r: (r, 0, g)) for a, b in carry],
            out_shape=out_shapes + save_shapes,
            scratch_shapes=[pltpu.VMEM((a, b), F32) for a, b in carry],
            compiler_params=_params(sem))(*arrays)
        return tuple(res)

    def bwd_body(*refs):
        in_refs = refs[:n_in]
        save_refs = refs[n_in:n_in + n_c]
        ct_refs = refs[n_in + n_c:n_in + n_c + n_out]
        at = n_in + n_c + n_out
        sv_refs = refs[at:at + n_sv]
        g_refs = refs[at + n_sv:at + n_sv + len(diff_idx)]
        dc_refs = refs[at + n_sv + len(diff_idx):]
        g = pl.program_id(0)
        r = pl.program_id(1)
        vals = [x[...] for x in in_refs]
        if n_c:
            @pl.when(r == 0)
            def _():
                for d in dc_refs:
                    d[...] = jnp.zeros_like(d)

        def fun(cvals, dvals):
            full = list(vals)
            for i, v in zip(diff_idx, dvals):
                full[i] = v
            if n_sv:
                return f_saved(tuple(cvals), *full, *[s[...] for s in sv_refs])
            return call_f(n_steps - 1 - r if n_c else r, cvals, full)

        _, vjp = jax.vjp(fun, tuple(s[...] for s in save_refs), tuple(vals[i] for i in diff_idx))
        cts = tuple(c[...] for c in ct_refs)
        if n_c:
            cts = (tuple(d[...] for d in dc_refs), cts)
        dcin, dvals = vjp(cts)
        for d, v in zip(dc_refs, dcin):
            d[...] = v
        for gref, i, v in zip(g_refs, diff_idx, dvals):
            spec = in_specs[i]
            if spec[0] in ("row", "rowg"):
                gref[...] = v
            else:
                first = (r == 0) & ((g == 0) if spec[0] == "bc" else (g % spec[3] == 0))

                @pl.when(first)
                def _(gref=gref, v=v):
                    gref[...] = v

                @pl.when(jnp.logical_not(first))
                def _(gref=gref, v=v):
                    gref[...] += v

    def bwd_call(arrays, saved, cts, outs):
        rmap = (lambda r: order(n_steps - 1 - r)) if n_c else (lambda r: r)
        res = pl.pallas_call(
            bwd_body, name=name + "_bwd", grid=(G, n_steps),
            in_specs=[_spec_block(s, tm, rmap) for s in in_specs]
            + [pl.BlockSpec((None, a, b), lambda g, r: (n_steps - 1 - r, 0, g)) for a, b in carry]
            + [_spec_block(s, tm, rmap) for s in out_specs] * (2 if n_sv else 1),
            out_specs=[_spec_block(in_specs[i], tm, rmap) for i in diff_idx],
            out_shape=[jax.ShapeDtypeStruct(_spec_shape(in_specs[i], G, T), F32) for i in diff_idx],
            scratch_shapes=[pltpu.VMEM((a, b), F32) for a, b in carry],
            compiler_params=_params(sem))(*arrays, *saved, *cts, *outs)
        return tuple(res)

    @jax.custom_vjp
    def op(*arrays):
        return fwd_call(*arrays)[:n_out]

    def op_fwd(*arrays):
        res = fwd_call(*arrays)
        return res[:n_out], (arrays, res[n_out:], res[:n_out] if n_sv else ())

    def op_bwd(res, cts):
        arrays, saved, outs = res
        grads = bwd_call(arrays, saved, cts, outs)
        out = [jnp.zeros_like(a) for a in arrays]
        for i, gval in zip(diff_idx, grads):
            out[i] = gval
        return tuple(out)

    op.defvjp(op_fwd, op_bwd)
    return op


def _rows(n, m):
    return lax.broadcasted_iota(jnp.int32, (n, m), 0)


def _ctx_select(row0, tm, n_ctx, v_lat, v_ctx):
    if n_ctx == 0:
        return v_lat
    is_ctx = (row0 + _rows(tm, 1)) < n_ctx
    return jnp.where(is_ctx, v_ctx, v_lat)


def _silu(x):
    return x * jax.nn.sigmoid(x)


def _f_modulate(tm, n_ctx):
    def f(row0, x, sh_l, sh_c, sc_l, sc_c):
        sh = _ctx_select(row0, tm, n_ctx, sh_l, sh_c)
        sc = _ctx_select(row0, tm, n_ctx, sc_l, sc_c)
        return (x * (1 + sc) + sh,)
    return f


def _f_postnorm(tm, n_ctx):
    def f(row0, x, y, g_l, g_c, w, b):
        z = DN_ALPHA * x + _ctx_select(row0, tm, n_ctx, g_l, g_c) * y
        mu = jnp.mean(z, -1, keepdims=True)
        zc = z - mu
        var = jnp.mean(zc * zc, -1, keepdims=True)
        return (zc * lax.rsqrt(var + EPS) * w + b,)
    return f


def _log_sigmoid(x):
    return -(jnp.maximum(-x, 0.0) + jnp.log1p(jnp.exp(-jnp.abs(x))))


def _f_gla_prep(glr, wg0, wg1, b0, b1):
    return (_log_sigmoid(bdot(glr, wg0, "nn") + b0) / GLA_GATE_TAU,
            _log_sigmoid(bdot(glr, wg1, "nn") + b1) / GLA_GATE_TAU)


def _f_gla_step(rev):
    def f(carry, q, k, v, la):
        (st,) = carry
        n = q.shape[0]
        cols = lax.broadcasted_iota(jnp.int32, (n, n), 1)
        tri = (_rows(n, n) <= cols) if rev else (_rows(n, n) >= cols)
        b = jnp.dot(tri.astype(F32), la, precision=lax.Precision.HIGHEST)
        qe = q * (GLA_DK ** -0.5) * jnp.exp(b)
        ke = k * jnp.exp(-b)
        att = jnp.where(tri, bdot(qe, ke, "nt"), 0.0)
        o = bdot(att, v, "nn") + bdot(qe, st, "nt")
        end = 0 if rev else n - 1
        b_last = jnp.sum(jnp.where(_rows(n, 1) == end, b, 0.0), axis=0, keepdims=True)
        kd = k * jnp.exp(b_last - b)
        st = st * jnp.exp(b_last) + bdot(v, kd, "tn")
        return (st,), (o,)
    return f


def _f_gla_norm(o0, o1, gr, w):
    o = o0 + o1
    mu = jnp.mean(o, -1, keepdims=True)
    oc = o - mu
    var = jnp.mean(oc * oc, -1, keepdims=True)
    return (oc * lax.rsqrt(var + EPS) * w * _silu(gr),)


@functools.partial(jax.custom_vjp, nondiff_argnums=(1, 2))
def _shift_rows(x, d, up):
    n = x.shape[0]
    rows = _rows(n, 1)
    if up:
        return jnp.where(rows < n - d, pltpu.roll(x, n - d, 0), 0.0)
    return jnp.where(rows >= d, pltpu.roll(x, d, 0), 0.0)


def _shift_fwd(x, d, up):
    return _shift_rows(x, d, up), None


def _shift_bwd(d, up, _, g):
    return (_shift_rows(g, d, not up),)


_shift_rows.defvjp(_shift_fwd, _shift_bwd)


def _scan_doubling(ur, ui, ar, ai, rev):
    n = ur.shape[0]
    xr, xi, pr, pi = ur, ui, ar, ai
    d = 1
    while d < n:
        sr, si = _shift_rows(xr, d, rev), _shift_rows(xi, d, rev)
        xr, xi = xr + pr * sr - pi * si, xi + pr * si + pi * sr
        pr, pi = pr * pr - pi * pi, 2 * pr * pi
        d *= 2
    return xr, xi


@functools.partial(jax.custom_vjp, nondiff_argnums=(6,))
def _scan_known(ur, ui, ar, ai, xr, xi, rev):
    return xr, xi


def _scan_known_fwd(ur, ui, ar, ai, xr, xi, rev):
    return (xr, xi), (ar, ai, xr, xi)


def _scan_known_bwd(rev, res, g):
    ar, ai, xr, xi = res
    lr, li = _scan_doubling(g[0], g[1], ar, -ai, not rev)
    pr, pi = _shift_rows(xr, 1, rev), _shift_rows(xi, 1, rev)
    dar = jnp.sum(lr * pr + li * pi, axis=0, keepdims=True)
    dai = jnp.sum(li * pr - lr * pi, axis=0, keepdims=True)
    return lr, li, dar, dai, jnp.zeros_like(xr), jnp.zeros_like(xi)


_scan_known.defvjp(_scan_known_fwd, _scan_known_bwd)


def _f_s5_step(rev, known=False):
    def f(carry, bur, bui, lam_re, lam_im, log_dt, *states):
        cr, ci = carry
        n = bur.shape[0]
        dt = jnp.exp(log_dt)
        mag = jnp.exp(lam_re * dt)
        ar, ai = mag * jnp.cos(lam_im * dt), mag * jnp.sin(lam_im * dt)
        den = lam_re * lam_re + lam_im * lam_im
        nr, ni = ar - 1, ai
        kr = (nr * lam_re + ni * lam_im) / den
        ki = (ni * lam_re - nr * lam_im) / den
        first = _rows(n, 1) == (n - 1 if rev else 0)
        ur = kr * bur - ki * bui + jnp.where(first, ar * cr - ai * ci, 0.0)
        ui = kr * bui + ki * bur + jnp.where(first, ar * ci + ai * cr, 0.0)
        if known:
            xr, xi = _scan_known(ur, ui, ar, ai, states[0], states[1], rev)
        else:
            xr, xi = _scan_doubling(ur, ui, ar, ai, rev)
        last = _rows(n, 1) == (0 if rev else n - 1)
        cr = jnp.sum(jnp.where(last, xr, 0.0), axis=0, keepdims=True)
        ci = jnp.sum(jnp.where(last, xi, 0.0), axis=0, keepdims=True)
        return (cr, ci), (xr, xi)
    return f


def _f_s5_post(su, dskip, y0r, y0i, y1r, y1i):
    return (jax.nn.gelu(su * dskip + y0r - y0i + y1r - y1i),)


def _f_s5_glu(y, t):
    return (y * jax.nn.sigmoid(t),)


def _swap_pairs(x):
    lane = lax.broadcasted_iota(jnp.int32, x.shape, 1)
    return jnp.where(lane % 2 == 0, pltpu.roll(x, x.shape[1] - 1, 1), pltpu.roll(x, 1, 1))


@jax.custom_vjp
def _rope(x, cos2, sin2):
    return x * cos2 + _swap_pairs(x) * sin2


def _rope_fwd(x, cos2, sin2):
    return _rope(x, cos2, sin2), (cos2, sin2)


def _rope_bwd(res, g):
    cos2, sin2 = res
    return g * cos2 + _swap_pairs(g * sin2), jnp.zeros_like(cos2), jnp.zeros_like(sin2)


_rope.defvjp(_rope_fwd, _rope_bwd)


def _f_qk_norm_rope(x, cos2, sin2, w):
    xn = x * lax.rsqrt(jnp.mean(x * x, -1, keepdims=True) + EPS) * w
    return (_rope(xn, cos2, sin2),)


def _f_attn(q, k, v):
    s = bdot(q, k, "nt") * (ATTN_HEAD_DIM ** -0.5)
    e = jnp.exp(s - jnp.max(s, -1, keepdims=True))
    p = e / jnp.sum(e, -1, keepdims=True)
    return (bdot(p, v, "nn"),)


def _f_merge(ga, gb, gc, pa, pb, pc):
    return (jax.nn.sigmoid(ga) * pa + jax.nn.sigmoid(gb) * pb + jax.nn.sigmoid(gc) * pc,)


def _f_swiglu(a, b):
    return (_silu(a) * b,)


def _f_silu(x):
    return (_silu(x),)


def _f_add_bias(x, b):
    return (x + b,)


def _scan_order(rev, n_ctx, T, tm):
    if not rev:
        return None
    nc, n = n_ctx // tm, T // tm
    return lambda s: jnp.where(s < nc, nc - 1 - s, n - 1 - (s - nc))


def modulate(name, x, sh, sc, n_ctx):
    T, D = x.shape
    tm = _tile(T, ROW_TILE, SUBLANE)
    cw = _tile(D, COL_TILE, LANE)
    col, vec = ("rowg", cw, 1), ("bcg", 1, cw, 1)
    op = block_op(name, _f_modulate(tm, n_ctx), [col, vec, vec, vec, vec], [col], D // cw, T, tm,
                  [True] * 5, row0=True)
    return op(x, sh[0], sh[1], sc[0], sc[1])[0]


def postnorm(name, x, y, g, w, b, n_ctx):
    T, D = x.shape
    tm = _tile(T, ROW_TILE, SUBLANE)
    vec = ("bc", 1, D)
    op = block_op(name, _f_postnorm(tm, n_ctx), [("row", D), ("row", D), vec, vec, vec, vec], [("row", D)], 1, T,
                  tm, [True] * 6, row0=True)
    return op(x, y, g[0], g[1], w, b)[0]


def rowwise(name, f, arrays, n_out=1):
    T, w = arrays[0].shape
    tm = _tile(T, ROW_TILE, SUBLANE)
    cw = _tile(w, COL_TILE, LANE)
    col = ("rowg", cw, 1)
    op = block_op(name, f, [col] * len(arrays), [col] * n_out, w // cw, T, tm, [True] * len(arrays))
    return op(*arrays)


def gla_prep(name, glr, wg, bg):
    T = glr.shape[0]
    qk = wg.shape[-1]
    tm = _tile(T, ROW_TILE, SUBLANE)
    op = block_op(name, _f_gla_prep, [("row", LANE), ("bc", LANE, qk), ("bc", LANE, qk), ("bc", 1, qk), ("bc", 1, qk)],
                  [("row", qk), ("row", qk)], 1, T, tm, [True] * 5)
    return op(glr, wg[0], wg[1], bg[0], bg[1])


def gla_scan(name, q, k, v, la, rev, n_ctx):
    T = q.shape[0]
    op = block_op(name, _f_gla_step(rev), [("rowg", GLA_DK, 1), ("rowg", GLA_DK, 1), ("rowg", GLA_DV, 1), ("rowg", GLA_DK, 1)],
                  [("rowg", GLA_DV, 1)], GLA_HEADS, T, GLA_CHUNK, [True] * 4, carry=[(GLA_DV, GLA_DK)],
                  order=_scan_order(rev, n_ctx, T, GLA_CHUNK))
    return op(q, k, v, la)[0]


def gla_norm(name, o0, o1, gr, w):
    T = o0.shape[0]
    tm = _tile(T, ROW_TILE, SUBLANE)
    hd = ("rowg", GLA_DV, 1)
    op = block_op(name, _f_gla_norm, [hd, hd, hd, ("bcg", 1, GLA_DV, 1)], [hd], GLA_HEADS, T, tm, [True] * 4)
    return op(o0, o1, gr, w)[0]


def s5_scan(name, bur, bui, lam_re, lam_im, log_dt, rev, n_ctx):
    T, S = bur.shape
    cols = _tile(S, 768, LANE)
    G = S // cols
    col, par = ("rowg", cols, 1), ("bcg", 1, cols, 1)
    op = block_op(name, _f_s5_step(rev), [col, col, par, par, par], [col, col], G, T, S5_CHUNK, [True] * 5,
                  carry=[(1, cols), (1, cols)], order=_scan_order(rev, n_ctx, T, S5_CHUNK),
                  f_saved=_f_s5_step(rev, True))
    return op(bur, bui, lam_re, lam_im, log_dt)


def qk_norm_rope(name, x, cos2, sin2, w):
    T = x.shape[0]
    G = x.shape[1] // ATTN_HEAD_DIM
    tm = _tile(T, ROW_TILE, SUBLANE)
    hd = ("rowg", ATTN_HEAD_DIM, 1)
    op = block_op(name, _f_qk_norm_rope, [hd, ("row", ATTN_HEAD_DIM), ("row", ATTN_HEAD_DIM), ("bc", 1, ATTN_HEAD_DIM)],
                  [hd], G, T, tm, [True, False, False, True])
    return op(x, cos2, sin2, w)[0]


def attention(name, q, k, v):
    T, Tk = q.shape[0], k.shape[0]
    tm = _tile(T, ROW_TILE, SUBLANE)
    grp = ATTN_Q_HEADS // ATTN_KV_HEADS
    kv = ("bcg", Tk, ATTN_HEAD_DIM, grp)
    op = block_op(name, _f_attn, [("rowg", ATTN_HEAD_DIM, 1), kv, kv], [("rowg", ATTN_HEAD_DIM, 1)], ATTN_Q_HEADS, T,
                  tm, [True] * 3)
    return op(q, k, v)[0]


def sq_loss(name, y, t):
    T, D = y.shape
    tm = _tile(T, ROW_TILE, SUBLANE)

    def fwd_body(y_ref, t_ref, o_ref):
        e = y_ref[...] - t_ref[...]
        part = jnp.sum(jnp.sum(e * e, -1, keepdims=True), 0, keepdims=True) * (0.5 / D)

        @pl.when(pl.program_id(0) == 0)
        def _():
            o_ref[...] = jnp.zeros_like(o_ref)

        o_ref[...] += part * jnp.ones((1, LANE), F32)

    def bwd_body(y_ref, t_ref, g_ref, o_ref):
        o_ref[...] = (y_ref[...] - t_ref[...]) * (g_ref[:, 0:1] / D)

    row = pl.BlockSpec((tm, D), lambda r: (r, 0))
    one = pl.BlockSpec((1, LANE), lambda r: (0, 0))

    def fwd_call(y, t):
        return pl.pallas_call(fwd_body, name=name + "_fwd", grid=(T // tm,), in_specs=[row, row], out_specs=one,
                              out_shape=jax.ShapeDtypeStruct((1, LANE), F32), compiler_params=_params(("arbitrary",)))(y, t)

    @jax.custom_vjp
    def op(y, t):
        return fwd_call(y, t)[0, 0]

    def op_fwd(y, t):
        return fwd_call(y, t)[0, 0], (y, t)

    def op_bwd(res, g):
        y, t = res
        gy = pl.pallas_call(bwd_body, name=name + "_bwd", grid=(T // tm,), in_specs=[row, row, one], out_specs=row,
                            out_shape=jax.ShapeDtypeStruct((T, D), F32), compiler_params=_params(("arbitrary",)))(
                                y, t, jnp.full((1, LANE), g, F32))
        return gy, jnp.zeros_like(t)

    op.defvjp(op_fwd, op_bwd)
    return op(y, t)


def _rope_tables(n_ctx, n_lat):
    n_rows = n_lat // GRID_W
    rows = jnp.repeat(jnp.arange(n_rows), GRID_W).astype(F32)
    cols = jnp.tile(jnp.arange(GRID_W), n_rows).astype(F32)
    n_freq = ATTN_HEAD_DIM // 4
    inv = ROPE_THETA ** (-jnp.arange(n_freq, dtype=F32) / n_freq)
    ang = jnp.concatenate([rows[:, None] * inv, cols[:, None] * inv], -1)
    cos2 = jnp.repeat(jnp.cos(ang), 2, axis=-1)
    sin2 = jnp.stack([-jnp.sin(ang), jnp.sin(ang)], -1).reshape(n_lat, ATTN_HEAD_DIM)
    cos2 = jnp.concatenate([jnp.ones((n_ctx, ATTN_HEAD_DIM), F32), cos2], 0)
    sin2 = jnp.concatenate([jnp.zeros((n_ctx, ATTN_HEAD_DIM), F32), sin2], 0)
    return cos2, sin2


def _in_layout(D, s5_width):
    qk, gv = GLA_HEADS * GLA_DK, GLA_HEADS * GLA_DV
    aq, akv = ATTN_Q_HEADS * ATTN_HEAD_DIM, ATTN_KV_HEADS * ATTN_HEAD_DIM
    widths = [("gq", qk), ("gk", qk), ("gv", gv), ("gr", gv), ("glr", GLA_GATE_RANK), ("su", s5_width), ("aq", aq),
              ("ak", akv), ("av", akv), ("ga", D), ("gb", D), ("gc", D)]
    off, out = 0, {}
    for n, w in widths:
        out[n] = (off, w)
        off += w
    return out, off


def _padded_width(width):
    return width if width % LANE == 0 else -(-width // SHARD_PAD) * SHARD_PAD


def _shard_cols(z, off, width, shard, padded):
    parts = []
    for j in range(N_CHIPS):
        lo, hi = max(off, j * shard), min(off + width, (j + 1) * shard)
        if lo < hi:
            parts.append(z[:, j * padded + lo - j * shard:j * padded + hi - j * shard])
    return parts[0] if len(parts) == 1 else jnp.concatenate(parts, 1)


def _s5_in_blocks(b):
    G, P, C = b.shape
    nb, bg = G // S5_BLOCK_GROUPS, S5_BLOCK_GROUPS
    t = b.reshape(nb, bg, P, C).transpose(0, 1, 3, 2)
    return jnp.einsum("bgcp,gh->bgchp", t, jnp.eye(bg, dtype=F32)).reshape(nb, bg * C, bg * P)


def _s5_out_blocks(c):
    G, C, P = c.shape
    nb, bg = G // S5_BLOCK_GROUPS, S5_BLOCK_GROUPS
    t = c.reshape(nb, bg, C, P).transpose(0, 1, 3, 2)
    return jnp.einsum("bgpc,gh->bgphc", t, jnp.eye(bg, dtype=F32)).reshape(nb, bg * P, bg * C)


def _layer(keep_ctx, xa, n_ctx, mod, p, big, cos2, sin2):
    T, D = xa.shape
    S = p["s5_d"].shape[-1]
    lay, width = _in_layout(D, S)
    lo = 0 if keep_ctx else n_ctx
    ctx_rows = n_ctx if keep_ctx else 0

    h = modulate("modulate1", xa, mod["sh1"], mod["sc1"], n_ctx)
    z = big["w_in"]("in_proj", h)
    shard = width // N_CHIPS
    zz = {n: _shard_cols(z, o, w, shard, z.shape[1] // N_CHIPS) for n, (o, w) in lay.items()}

    wg = jnp.pad(p["w_gla_gate"], ((0, 0), (0, LANE - GLA_GATE_RANK), (0, 0)))
    glr = jnp.pad(zz["glr"], ((0, 0), (0, LANE - GLA_GATE_RANK)))
    la0, la1 = gla_prep("gla_prep", glr, wg, p["b_gla_gate"][:, None, :])
    o0 = gla_scan("gla_scan", zz["gq"], zz["gk"], zz["gv"], la0, False, n_ctx)
    o1 = gla_scan("gla_scan_rev", zz["gq"], zz["gk"], zz["gv"], la1, True, n_ctx)
    o_gla = gla_norm("gla_norm", o0[lo:], o1[lo:], zz["gr"][lo:], p["gla_norm_w"][None, :])

    su = zz["su"]
    bur = mm("s5_in_re", su, _s5_in_blocks(p["s5_b_re"]))
    bui = mm("s5_in_im", su, _s5_in_blocks(p["s5_b_im"]))
    ys = []
    for d in range(2):
        row = lambda t: t.reshape(1, -1)
        ldt = jnp.repeat(p["s5_log_dt"][d], S5_STATE)
        sr, si = s5_scan("s5_scan_rev" if d else "s5_scan", bur, bui, row(p["s5_lam_re"][d]),
                         row(p["s5_lam_im"][d]), row(ldt), d == 1, n_ctx)
        ys.append(mm("s5_out_re", sr[lo:], _s5_out_blocks(p["s5_c_re"][d])))
        ys.append(mm("s5_out_im", si[lo:], _s5_out_blocks(p["s5_c_im"][d])))
    T2 = T - lo
    tm = _tile(T2, ROW_TILE, SUBLANE)
    post = block_op("s5_post", _f_s5_post, [("row", S), ("bc", 1, S)] + [("row", S)] * 4, [("row", S)], 1, T2, tm,
                    [True] * 6)
    yg = post(su[lo:], p["s5_d"][None, :], *ys)[0]
    o_s5 = rowwise("s5_glu", _f_s5_glu, [yg, big["w_s5_glu"]("s5_glu_proj", yg)])[0]

    qn = qk_norm_rope("q_norm_rope", zz["aq"], cos2, sin2, p["q_norm_w"][None, :])
    kn = qk_norm_rope("k_norm_rope", zz["ak"], cos2, sin2, p["k_norm_w"][None, :])
    o_attn = attention("attn_lat", qn[n_ctx:], kn, zz["av"])
    if keep_ctx:
        o_c = attention("attn_ctx", qn[:n_ctx], kn[:n_ctx], zz["av"][:n_ctx])
        o_attn = jnp.concatenate([o_c, o_attn], 0)

    merged = rowwise("merge", _f_merge, [zz["ga"][lo:], zz["gb"][lo:], zz["gc"][lo:],
                                         big["w_proj_gla"]("proj_gla", o_gla),
                                         big["w_proj_s5"]("proj_s5", o_s5),
                                         big["w_proj_attn"]("proj_attn", o_attn)])[0]
    mix = big["w_out"]("out_proj", merged)
    x1 = postnorm("postnorm1", xa[lo:], mix, mod["g1"], p["ln1_w"][None, :], p["ln1_b"][None, :], ctx_rows)
    h2 = modulate("modulate2", x1, mod["sh2"], mod["sc2"], ctx_rows)
    u = big["w_ffn_in"]("ffn_in", h2)
    F = u.shape[1] // 2
    act = rowwise("swiglu", _f_swiglu, [u[:, :F], u[:, F:]])[0]
    f = big["w_ffn_out"]("ffn_out", act)
    return postnorm("postnorm2", x1, f, mod["g2"], p["ln2_w"][None, :], p["ln2_b"][None, :], ctx_rows)


def local_loss(x, ctx, target, small, cond, gathered, grads):
    n_lat, D = x.shape
    n_ctx = ctx.shape[0]
    cos2, sin2 = _rope_tables(n_ctx, n_lat)
    xa = jnp.concatenate([ctx, x], 0)
    depth = gathered["w_in"].shape[0]
    for l in range(depth):
        p = {n: v[l] for n, v in small.items() if n != "c_ctx"}
        big = {n: functools.partial(lambda n, l, name, a: mm_gathered(name, a, gathered[n], l, grads, (n, l)), n, l)
               for n in gathered}
        m = block_op("ada_bias", _f_add_bias, [("row", 6 * D), ("bc", 1, 6 * D)], [("row", 6 * D)], 1, 2, 2,
                     [True, True])(cond[l], p["b_ada"][None, :])[0]
        names = ["sh1", "sc1", "g1", "sh2", "sc2", "g2"]
        mod = {n: (m[0:1, i * D:(i + 1) * D], m[1:2, i * D:(i + 1) * D]) for i, n in enumerate(names)}
        xa = _layer(l < depth - 1, xa, n_ctx, mod, p, big, cos2, sin2)
    return sq_loss("loss", xa, target)


MESH = pl.DeviceIdType.MESH
ANY = pl.BlockSpec(memory_space=pl.ANY)


def _place():
    x, y, c = lax.axis_index("x"), lax.axis_index("y"), lax.axis_index("c")
    chips = [(1 - x, y), (x, 1 - y), (1 - x, 1 - y)]
    return x, y, c, chips


def _rcopy(src, dst, ssem, rsem, to):
    return pltpu.make_async_remote_copy(src_ref=src, dst_ref=dst, send_sem=ssem, recv_sem=rsem, device_id=to,
                                        device_id_type=MESH)


def gather_shards(bufs):
    n = len(bufs)
    L = bufs[0].shape[0]
    half = L // 2

    def body(*refs):
        dst = refs[n:2 * n]
        isend, irecv, fsend, frecv, dsend, drecv = refs[2 * n:]
        x, y, c, _ = _place()
        j = 2 * x + y
        nbr = [(1 - x, y), (x, 1 - y)]
        jn = [2 * kx + ky for kx, ky in nbr]
        jd = 2 * (1 - x) + (1 - y)
        sibling = (x, y, 1 - c)
        mine, other = pl.ds(c * half, half), pl.ds((1 - c) * half, half)

        def piece(i, layers, chip, q):
            hr = bufs[i].shape[2] // 2
            return dst[i].at[layers, chip, pl.ds(q * hr, hr)]

        direct = [_rcopy(dst[i].at[mine, j], dst[i].at[mine, j], isend.at[i, r], irecv.at[i, r], (*nbr[r], c))
                  for i in range(n) for r in range(2)]
        for cp in direct:
            cp.start()
        passed = []
        for i in range(n):
            for r in range(2):
                part = dst[i].at[mine, jn[r]]
                _rcopy(part, part, isend.at[i, r], irecv.at[i, r], (*nbr[r], c)).wait_recv()
                fwd = piece(i, mine, jn[r], 1 - r)
                passed.append(_rcopy(fwd, fwd, fsend.at[i, r], frecv.at[i, 1 - r], (*nbr[1 - r], c)))
                passed.append(_rcopy(part, part, dsend.at[i, r], drecv.at[i, r], sibling))
                passed[-2].start()
                passed[-1].start()
        for i in range(n):
            for q in range(2):
                part = piece(i, mine, jd, q)
                _rcopy(part, part, fsend.at[i, q], frecv.at[i, q], (*nbr[q], c)).wait_recv()
                passed.append(_rcopy(part, part, dsend.at[i, 2 + q], drecv.at[i, 2 + q], sibling))
                passed[-1].start()
        for i in range(n):
            for r in range(2):
                part = dst[i].at[other, jn[r]]
                _rcopy(part, part, dsend.at[i, r], drecv.at[i, r], sibling).wait_recv()
                part = piece(i, other, jd, r)
                _rcopy(part, part, dsend.at[i, 2 + r], drecv.at[i, 2 + r], sibling).wait_recv()
        for cp in direct + passed:
            cp.wait_send()

    out_shape = [jax.ShapeDtypeStruct(b.shape, b.dtype) for b in bufs]
    sems = [pltpu.SemaphoreType.DMA((n, 2))] * 4 + [pltpu.SemaphoreType.DMA((n, 4))] * 2
    return pl.pallas_call(body, name="gather_shards", in_specs=[ANY] * n, out_specs=[ANY] * n, out_shape=out_shape,
                          scratch_shapes=sems, input_output_aliases={i: i for i in range(n)})(*bufs)


def swap_halves(grads):
    n = len(grads)

    def body(*refs):
        src, dst = refs[:n], refs[n:2 * n]
        ssem, rsem = refs[2 * n:]
        x, y, c, _ = _place()
        cps = []
        for i in range(n):
            hr = grads[i].shape[1] // 2
            cps.append(_rcopy(src[i].at[:, pl.ds((1 - c) * hr, hr)], dst[i], ssem.at[i], rsem.at[i], (x, y, 1 - c)))
        for cp in cps:
            cp.start()
        for cp in cps:
            cp.wait()

    out_shape = [jax.ShapeDtypeStruct((g.shape[0], g.shape[1] // 2, g.shape[2]), g.dtype) for g in grads]
    return pl.pallas_call(body, name="swap_halves", in_specs=[ANY] * n, out_specs=[ANY] * n, out_shape=out_shape,
                          scratch_shapes=[pltpu.SemaphoreType.DMA((n,))] * 2)(*grads)


def scatter_shards(parts):
    n = len(parts)

    def body(*refs):
        src, dst = refs[:n], refs[n:2 * n]
        ssem, rsem = refs[2 * n:]
        x, y, c, chips = _place()
        cps = [_rcopy(src[i].at[2 * kx + ky], dst[i].at[r], ssem.at[i, r], rsem.at[i, r], (kx, ky, c))
               for i in range(n) for r, (kx, ky) in enumerate(chips)]
        for cp in cps:
            cp.start()
        for cp in cps:
            cp.wait()

    out_shape = [jax.ShapeDtypeStruct((3,) + p.shape[1:], p.dtype) for p in parts]
    return pl.pallas_call(body, name="scatter_shards", in_specs=[ANY] * n, out_specs=[ANY] * n, out_shape=out_shape,
                          scratch_shapes=[pltpu.SemaphoreType.DMA((n, 3))] * 2)(*parts)


def join_halves(bufs):
    n = len(bufs)

    def body(*refs):
        dst = refs[n:2 * n]
        ssem, rsem = refs[2 * n:]
        x, y, c, _ = _place()
        cps = []
        for i in range(n):
            hr = bufs[i].shape[1] // 2
            mine = dst[i].at[:, pl.ds(c * hr, hr)]
            cps.append(_rcopy(mine, mine, ssem.at[i], rsem.at[i], (x, y, 1 - c)))
        for cp in cps:
            cp.start()
        for cp in cps:
            cp.wait()

    out_shape = [jax.ShapeDtypeStruct(b.shape, b.dtype) for b in bufs]
    return pl.pallas_call(body, name="join_halves", in_specs=[ANY] * n, out_specs=[ANY] * n, out_shape=out_shape,
                          scratch_shapes=[pltpu.SemaphoreType.DMA((n,))] * 2,
                          input_output_aliases={i: i for i in range(n)})(*bufs)


def gather_blocks(v):
    def body(v_ref, out_ref, send_sems, recv_sems, local_sem):
        x, y, c, chips = _place()
        me, sibling = (x, y, c), (x, y, 1 - c)

        def blk(px, py, pc):
            return out_ref.at[4 * px + 2 * py + pc]

        def copy(k, block, to, src=None):
            return _rcopy(blk(*block) if src is None else src, blk(*block), send_sems.at[k], recv_sems.at[k], to)

        own = pltpu.make_async_copy(v_ref, blk(*me), local_sem)
        own.start()
        first = [copy(0, me, sibling, src=v_ref)]
        first += [copy(1 + r, me, (*chip, c), src=v_ref) for r, chip in enumerate(chips)]
        for cp in first:
            cp.start()
        passed = [copy(4 + r, (*chip, c), sibling) for r, chip in enumerate(chips)]
        for r, chip in enumerate(chips):
            copy(1 + r, (*chip, c), me).wait_recv()
            passed[r].start()
        copy(0, sibling, me).wait_recv()
        for r, chip in enumerate(chips):
            copy(4 + r, (*chip, 1 - c), me).wait_recv()
        for cp in first + passed:
            cp.wait_send()
        own.wait()

    return pl.pallas_call(body, name="gather_blocks", in_specs=[ANY], out_specs=ANY,
                          out_shape=jax.ShapeDtypeStruct((8,) + v.shape, v.dtype),
                          scratch_shapes=[pltpu.SemaphoreType.DMA((7,)), pltpu.SemaphoreType.DMA((7,)),
                                          pltpu.SemaphoreType.DMA])(v)


STREAM_BLOCK = 256 * 1024


def _stream_rows(rows, cols):
    base = 2 * SUBLANE if rows % (2 * SUBLANE) == 0 else SUBLANE
    return _tile(rows, max(base, STREAM_BLOCK // cols // base * base), base)


def _view3(a, lead):
    shape = a.shape[:lead] + (-1, a.shape[-1])
    return a.reshape(shape)


def sum_parts(name, terms, out, grid, where, into=None):
    n_skip = 0 if into is None else 1

    def body(w_ref, *refs):
        refs = refs[n_skip:]
        acc = refs[0][...].astype(F32)
        for t in refs[1:-1]:
            acc = acc + t[...].astype(F32)
        refs[-1][...] = acc.astype(refs[-1].dtype)

    grid_spec = pltpu.PrefetchScalarGridSpec(
        num_scalar_prefetch=1, grid=grid, in_specs=[ANY] * n_skip + [pl.BlockSpec(b, f) for _, b, f in terms],
        out_specs=pl.BlockSpec(out[2], out[3]))
    operands = ([] if into is None else [into]) + [t[0] for t in terms]
    return pl.pallas_call(body, name=name, grid_spec=grid_spec, out_shape=jax.ShapeDtypeStruct(out[0], out[1]),
                          input_output_aliases={} if into is None else {1: 0},
                          compiler_params=_params(("arbitrary",) * len(grid)))(where, *operands)


def cast_place(w, where):
    w3 = _view3(w, 1)
    L, rows, cols = w3.shape
    tr = _stream_rows(rows, cols)

    def body(w_ref, src, dst):
        dst[...] = src[...].astype(BF16)

    grid_spec = pltpu.PrefetchScalarGridSpec(
        num_scalar_prefetch=1, grid=(L, rows // tr),
        in_specs=[pl.BlockSpec((None, tr, cols), lambda l, r, wh: (l, r, 0))],
        out_specs=pl.BlockSpec((None, None, tr, cols), lambda l, r, wh: (l, wh[1], r, 0)))
    out = pl.pallas_call(body, name="cast_place", grid_spec=grid_spec,
                         out_shape=jax.ShapeDtypeStruct((L, N_CHIPS, rows, cols), BF16),
                         compiler_params=_params(("arbitrary", "arbitrary")))(where, w3)
    return out.reshape((L, N_CHIPS) + w.shape[1:])


def silu_vjp_rows(name, rows, x):
    def body(rows_ref, x_ref, o_ref):
        total = jnp.sum(rows_ref[...], axis=0, keepdims=True)
        s = jax.nn.sigmoid(x_ref[...])
        o_ref[...] = total * (s * (1.0 + x_ref[...] * (1.0 - s)))

    return pl.pallas_call(body, name=name, out_shape=jax.ShapeDtypeStruct(x.shape, F32))(rows, x)


def adamw(name, w, g, m, v):
    n, rows, cols = w.shape
    tr = _stream_rows(rows, cols)

    def body(w_ref, g_ref, m_ref, v_ref, d_ref, nm_ref, nv_ref):
        gv = g_ref[...]
        nm = ADAM_B1 * m_ref[...] + (1.0 - ADAM_B1) * gv
        nv = ADAM_B2 * v_ref[...] + (1.0 - ADAM_B2) * (gv * gv)
        m_hat = nm / (1.0 - ADAM_B1 ** ADAM_STEP)
        v_hat = nv / (1.0 - ADAM_B2 ** ADAM_STEP)
        d_ref[...] = -ADAM_LR * (m_hat / (jnp.sqrt(v_hat) + ADAM_EPS) + ADAM_WD * w_ref[...])
        nm_ref[...] = nm
        nv_ref[...] = nv

    blk = pl.BlockSpec((None, tr, cols), lambda l, r: (l, r, 0))
    shp = jax.ShapeDtypeStruct(w.shape, F32)
    return pl.pallas_call(body, name=name, grid=(n, rows // tr), in_specs=[blk] * 4, out_specs=[blk] * 3,
                          out_shape=[shp] * 3, compiler_params=_params(("arbitrary", "arbitrary")))(w, g, m, v)


COL_SHARDED = ("w_ada", "w_in", "w_proj_gla", "w_proj_s5", "w_proj_attn", "w_ffn_in")
ROW_SHARDED = ("w_s5_glu", "w_out", "w_ffn_out")
SHARDED = COL_SHARDED + ROW_SHARDED
DEFERRED = ("w_ada",)
REDUCED = tuple(n for n in SHARDED if n not in DEFERRED)
N_DEV = 8
GATE = ("w_gla_gate", "b_gla_gate")
WEIGHTS = ("c_ctx", "w_ada", "b_ada", "w_in", "w_gla_gate", "b_gla_gate", "gla_norm_w", "s5_lam_re", "s5_lam_im",
           "s5_log_dt", "s5_b_re", "s5_b_im", "s5_c_re", "s5_c_im", "s5_d", "w_s5_glu", "q_norm_w", "k_norm_w",
           "w_proj_gla", "w_proj_s5", "w_proj_attn", "w_out", "ln1_w", "ln1_b", "ln2_w", "ln2_b", "w_ffn_in",
           "w_ffn_out")
REPLICATED = tuple(n for n in WEIGHTS if n not in SHARDED + GATE)
SMALL = REPLICATED + GATE


def _pack(arrays):
    flat = jnp.concatenate([a.reshape(-1) for a in arrays])
    pad = (-flat.shape[0]) % (PACK_ROWS * LANE)
    return jnp.pad(flat, (0, pad)).reshape(-1, LANE)


def _unpack(packed, like):
    flat, out, off = packed.reshape(-1), [], 0
    for a in like:
        out.append(flat[off:off + a.size].reshape(a.shape))
        off += a.size
    return out


def kernel(x, c, ctx, c_ctx, w_ada, b_ada, w_in, w_gla_gate, b_gla_gate, gla_norm_w, s5_lam_re, s5_lam_im, s5_log_dt, s5_b_re, s5_b_im, s5_c_re, s5_c_im, s5_d, w_s5_glu, q_norm_w, k_norm_w, w_proj_gla, w_proj_s5, w_proj_attn, w_out, ln1_w, ln1_b, ln2_w, ln2_b, w_ffn_in, w_ffn_out, loss_target, m_c_ctx, m_w_ada, m_b_ada, m_w_in, m_w_gla_gate, m_b_gla_gate, m_gla_norm_w, m_s5_lam_re, m_s5_lam_im, m_s5_log_dt, m_s5_b_re, m_s5_b_im, m_s5_c_re, m_s5_c_im, m_s5_d, m_w_s5_glu, m_q_norm_w, m_k_norm_w, m_w_proj_gla, m_w_proj_s5, m_w_proj_attn, m_w_out, m_ln1_w, m_ln1_b, m_ln2_w, m_ln2_b, m_w_ffn_in, m_w_ffn_out, v_c_ctx, v_w_ada, v_b_ada, v_w_in, v_w_gla_gate, v_b_gla_gate, v_gla_norm_w, v_s5_lam_re, v_s5_lam_im, v_s5_log_dt, v_s5_b_re, v_s5_b_im, v_s5_c_re, v_s5_c_im, v_s5_d, v_w_s5_glu, v_q_norm_w, v_k_norm_w, v_w_proj_gla, v_w_proj_s5, v_w_proj_attn, v_w_out, v_ln1_w, v_ln1_b, v_ln2_w, v_ln2_b, v_w_ffn_in, v_w_ffn_out):
    args = dict(locals())
    w = {n: args[n] for n in WEIGHTS}
    m = {n: args["m_" + n] for n in WEIGHTS}
    v = {n: args["v_" + n] for n in WEIGHTS}
    L = w_in.shape[0]
    half = L // 2
    core = lax.axis_index("c").astype(jnp.int32)
    place = (2 * lax.axis_index("x") + lax.axis_index("y")).astype(jnp.int32)
    zero = jnp.zeros((), jnp.int32)
    where, by_core, by_place = jnp.stack([core, place]), jnp.stack([core, zero]), jnp.stack([zero, place])

    def padded(n):
        cols = w[n].shape[-1]
        extra = _padded_width(cols) - cols if n in COL_SHARDED else 0
        return jnp.pad(w[n], ((0, 0), (0, 0), (0, extra))) if extra else w[n]

    gathered = gather_shards([cast_place(padded(n), by_place) for n in REDUCED])
    gathered = {n: g if n in COL_SHARDED else g.reshape(L, 1, -1, g.shape[-1]) for n, g in zip(REDUCED, gathered)}
    first = [w[n] for n in GATE] + [c[0]]
    first_blocks = gather_blocks(_pack(first))
    per_dev = [_unpack(first_blocks[k], first) for k in range(N_DEV)]
    small = {n: w[n] for n in REPLICATED}
    small.update({n: jnp.concatenate([per_dev[2 * j][i] for j in range(N_CHIPS)], -1) for i, n in enumerate(GATE)})

    vectors = jnp.stack([d[-1] for d in per_dev] + [w["c_ctx"]] * N_DEV)
    left = rowwise("silu_cond", _f_silu, [vectors])[0]
    ada_w = w["w_ada"].astype(BF16)[:, None]
    mine = [jnp.stack([_mm_nn("ada_proj", left, ada_w, l, True) for l in range(L)])]
    ada_blocks = gather_blocks(_pack(mine))
    full = jnp.concatenate([_unpack(ada_blocks[2 * j], mine)[0] for j in range(N_CHIPS)], -1)
    me = 2 * place + core
    cond = jnp.stack([lax.dynamic_index_in_dim(full, me, axis=1, keepdims=False), full[:, N_DEV]], 1)

    g_sh = {}

    def loss_fn(x1, small, cond):
        return local_loss(x1, ctx[0], loss_target[0], small, cond, gathered, g_sh)

    loss, (gx, g_small, g_cond) = jax.value_and_grad(loss_fn, argnums=(0, 1, 2))(x[0], small, cond)
    loss = lax.psum(loss, ("x", "y", "c"))

    parts = [g_sh[(n, l)].reshape((N_CHIPS, -1, g_sh[(n, l)].shape[-1])) for n in REDUCED for l in range(L)]
    theirs = swap_halves(parts)
    chip_sums, tiles = [], []
    for p, t in zip(parts, theirs):
        hr, cols = t.shape[1:]
        tr = _stream_rows(hr, cols)
        tiles.append(tr)
        blk = (None, tr, cols)
        chip_sums.append(sum_parts(
            "sum_cores", [(p, blk, functools.partial(lambda nb, s, r, wh: (s, wh[0] * nb + r, 0), hr // tr)),
                          (t, blk, lambda s, r, wh: (s, r, 0))],
            (t.shape, BF16, blk, lambda s, r, wh: (s, r, 0)), (N_CHIPS, hr // tr), by_core))
    recv = scatter_shards(chip_sums)
    finals = []
    for i, n in enumerate(REDUCED):
        rows, cols = parts[i * L].shape[1:]
        buf = None
        for l in range(L):
            k = i * L + l
            tr = tiles[k]
            nb = rows // 2 // tr
            blk = (None, tr, cols)
            terms = [(parts[k], blk, functools.partial(lambda nb, r, wh: (wh[1], wh[0] * nb + r, 0), nb)),
                     (theirs[k], blk, lambda r, wh: (wh[1], r, 0))]
            terms += [(recv[k], blk, functools.partial(lambda j, r, wh: (j, r, 0), j)) for j in range(3)]
            buf = sum_parts("sum_chips", terms,
                            ((L, rows, cols), F32, blk, functools.partial(lambda l, nb, r, wh: (l, wh[0] * nb + r, 0), l, nb)),
                            (nb,), where, into=buf)
        finals.append(buf)
    grads = {n: g[..., :w[n].shape[-1]].reshape(w[n].shape) for n, g in zip(REDUCED, join_halves(finals))}

    row_blocks = gather_blocks(_pack([g_cond]))
    rows = [_unpack(row_blocks[k], [g_cond])[0] for k in range(N_DEV)]
    shard = w["w_ada"].shape[-1]
    ada, rights = [], []
    for l in range(L):
        right = jnp.stack([r[l, 0] for r in rows] + [r[l, 1] for r in rows])
        rights.append(lax.dynamic_slice_in_dim(right, place * shard, shard, axis=1))
        ada.append(_mm_tn("ada_dw", left, rights[l], 1, True, F32))
    grads["w_ada"] = jnp.concatenate(ada, 0)
    back = _mm_nt("ada_dx", jnp.concatenate(rights, 1), ada_w.reshape((1, L) + ada_w.shape[2:]), 0, True)
    share = [back[N_DEV:]]
    share_blocks = gather_blocks(_pack(share))
    shares = jnp.concatenate([_unpack(share_blocks[2 * j], share)[0] for j in range(N_CHIPS)], 0)
    grad_c_ctx = silu_vjp_rows("ctx_vector_grad", shares, w["c_ctx"][None, :])[0]

    small_parts = [g_small[n] for n in SMALL]
    blocks = gather_blocks(_pack(small_parts))
    prow = (None, PACK_ROWS, LANE)
    total = sum_parts("sum_devices",
                      [(blocks, prow, functools.partial(lambda k, r, wh: (k, r, 0), k)) for k in range(8)],
                      (blocks.shape[1:], F32, prow[1:], lambda r, wh: (r, 0)), (blocks.shape[1] // PACK_ROWS,),
                      jnp.zeros((2,), jnp.int32))
    grads.update(dict(zip(SMALL, _unpack(total, small_parts))))
    grads["c_ctx"] = grad_c_ctx
    for n in GATE:
        width = w[n].shape[-1]
        grads[n] = lax.dynamic_slice_in_dim(grads[n], place * width, width, axis=-1)

    delta, new_m, new_v = {}, {}, {}
    for n in SHARDED:
        d3, m3, v3 = adamw("adamw", _view3(w[n], 1), _view3(grads[n], 1), _view3(m[n], 1), _view3(v[n], 1))
        delta[n], new_m[n], new_v[n] = (t.reshape(w[n].shape) for t in (d3, m3, v3))
    packed = [_pack([d[n] for n in SMALL])[None] for d in (w, grads, m, v)]
    d3, m3, v3 = adamw("adamw_small", *packed)
    like = [w[n] for n in SMALL]
    for dst, src in ((delta, d3), (new_m, m3), (new_v, v3)):
        dst.update(dict(zip(SMALL, _unpack(src[0], like))))

    return (loss, gx[None], *[grads[n] for n in WEIGHTS], *[delta[n] for n in WEIGHTS],
            *[new_m[n] for n in WEIGHTS], *[new_v[n] for n in WEIGHTS])
```

```python
import functools

import jax
import jax.numpy as jnp
from jax import lax
from jax.experimental import pallas as pl
from jax.experimental.pallas import tpu as pltpu

F32 = jnp.float32
BF16 = jnp.bfloat16

GRID_W = 64
GLA_HEADS = 4
GLA_DK = 128
GLA_DV = 256
GLA_GATE_RANK = 16
GLA_GATE_TAU = 16.0
GLA_CHUNK = 64
S5_GROUP = 16
S5_STATE = 64
ATTN_Q_HEADS = 8
ATTN_KV_HEADS = 2
ATTN_HEAD_DIM = 128
ROPE_THETA = 10000.0
DEPTH = 4
DN_ALPHA = (2 * DEPTH) ** 0.25
EPS = 1e-6
ADAM_LR = 0.001
ADAM_B1 = 0.9
ADAM_B2 = 0.999
ADAM_EPS = 1e-08
ADAM_WD = 0.01
ADAM_STEP = 10

LANE = 128
SUBLANE = 8
VMEM_LIMIT = 56 * 1024 * 1024
ADA_ROWS = 16
S5_CHUNK = 128
S5_BLOCK_GROUPS = 8
ROW_TILE = 256
COL_TILE = 512
PACK_ROWS = 512
SHARD_PAD = 1024
N_CHIPS = 4


def _params(sem, **kw):
    return pltpu.CompilerParams(dimension_semantics=sem, vmem_limit_bytes=VMEM_LIMIT, **kw)


def _tile(n, target, base):
    if n <= target:
        return n
    best = None
    for t in range(base, target + 1, base):
        if n % t == 0:
            best = t
    assert best is not None, (n, target, base)
    return best


_DIMS = {"nn": ((1,), (0,)), "nt": ((1,), (1,)), "tn": ((0,), (0,))}


def _dg(a, b, mode):
    return lax.dot_general(a.astype(BF16), b.astype(BF16), (_DIMS[mode], ((), ())), preferred_element_type=F32)


@functools.partial(jax.custom_vjp, nondiff_argnums=(2,))
def bdot(a, b, mode):
    return _dg(a, b, mode)


def _bdot_fwd(a, b, mode):
    return _dg(a, b, mode), (a, b)


def _bdot_bwd(mode, res, g):
    a, b = res
    if mode == "nn":
        return bdot(g, b, "nt").astype(a.dtype), bdot(a, g, "tn").astype(b.dtype)
    if mode == "nt":
        return bdot(g, b, "nn").astype(a.dtype), bdot(g, a, "tn").astype(b.dtype)
    return bdot(b, g, "nt").astype(a.dtype), bdot(a, g, "nn").astype(b.dtype)


bdot.defvjp(_bdot_fwd, _bdot_bwd)


MM_TILES = {"nn": (1152, 1024, 1664), "nt": (1152, 1024, 1664), "tn": (768, 2048, 1408)}


def _mm_tiles(mode, M, Kb, Nb):
    tm, tk, tn = MM_TILES[mode]
    return _tile(M, tm, SUBLANE), _tile(Kb, tk, LANE), _tile(Nb, tn, LANE)


def _mm_body(mode, reduce_axes):
    def body(p_ref, q_ref, o_ref, acc):
        first = functools.reduce(jnp.logical_and, [pl.program_id(ax) == 0 for ax, _ in reduce_axes])
        last = functools.reduce(jnp.logical_and, [pl.program_id(ax) == n - 1 for ax, n in reduce_axes])

        @pl.when(first)
        def _():
            acc[...] = jnp.zeros_like(acc)

        acc[...] += _dg(p_ref[...], q_ref[...], mode)

        @pl.when(last)
        def _():
            o_ref[...] = acc[...].astype(o_ref.dtype)

    return body


def _mm_nn(name, a, w, l, share):
    M = a.shape[0]
    _, B, Kb, Nb = w.shape
    tm, tk, tn = _mm_tiles("nn", M, Kb, Nb)
    nk, nn = Kb // tk, Nb // tn
    a_map = (lambda b, i, j, k: (i, k)) if share else (lambda b, i, j, k: (i, b * nk + k))
    return pl.pallas_call(
        _mm_body("nn", [(3, nk)]), name=name, grid=(B, M // tm, nn, nk),
        in_specs=[pl.BlockSpec((tm, tk), a_map),
                  pl.BlockSpec((None, None, tk, tn), lambda b, i, j, k: (l, b, k, j))],
        out_specs=pl.BlockSpec((tm, tn), lambda b, i, j, k: (i, b * nn + j)),
        out_shape=jax.ShapeDtypeStruct((M, B * Nb), F32),
        scratch_shapes=[pltpu.VMEM((tm, tn), F32)],
        compiler_params=_params(("arbitrary",) * 4))(a, w)


def _mm_nt(name, g, w, l, share):
    M = g.shape[0]
    _, B, Kb, Nb = w.shape
    tm, tk, tn = _mm_tiles("nt", M, Kb, Nb)
    nk, nn = Kb // tk, Nb // tn
    if share:
        grid, red = (M // tm, nk, B, nn), [(2, B), (3, nn)]
        g_map, w_map = (lambda i, k, b, n: (i, b * nn + n)), (lambda i, k, b, n: (l, b, k, n))
        o_map, width = (lambda i, k, b, n: (i, k)), Kb
    else:
        grid, red = (B, M // tm, nk, nn), [(3, nn)]
        g_map, w_map = (lambda b, i, k, n: (i, b * nn + n)), (lambda b, i, k, n: (l, b, k, n))
        o_map, width = (lambda b, i, k, n: (i, b * nk + k)), B * Kb
    return pl.pallas_call(
        _mm_body("nt", red), name=name, grid=grid,
        in_specs=[pl.BlockSpec((tm, tn), g_map), pl.BlockSpec((None, None, tk, tn), w_map)],
        out_specs=pl.BlockSpec((tm, tk), o_map),
        out_shape=jax.ShapeDtypeStruct((M, width), F32),
        scratch_shapes=[pltpu.VMEM((tm, tk), F32)],
        compiler_params=_params(("arbitrary",) * 4))(g, w)


def _mm_tn(name, a, g, B, share, dtype):
    M = a.shape[0]
    Kb, Nb = a.shape[1] // (1 if share else B), g.shape[1] // B
    tm, tk, tn = _mm_tiles("tn", M, Kb, Nb)
    nk, nn = Kb // tk, Nb // tn
    a_map = (lambda b, k, j, m: (m, k)) if share else (lambda b, k, j, m: (m, b * nk + k))
    return pl.pallas_call(
        _mm_body("tn", [(3, M // tm)]), name=name, grid=(B, nk, nn, M // tm),
        in_specs=[pl.BlockSpec((tm, tk), a_map),
                  pl.BlockSpec((tm, tn), lambda b, k, j, m: (m, b * nn + j))],
        out_specs=pl.BlockSpec((None, tk, tn), lambda b, k, j, m: (b, k, j)),
        out_shape=jax.ShapeDtypeStruct((B, Kb, Nb), dtype),
        scratch_shapes=[pltpu.VMEM((tk, tn), F32)],
        compiler_params=_params(("arbitrary",) * 4))(a, g)


def mm(name, a, w):
    w3 = w if w.ndim == 3 else w[None]

    @jax.custom_vjp
    def op(a, w3):
        return _mm_nn(name + "_fwd", a, w3.astype(BF16)[None], 0, False)

    def fwd(a, w3):
        wb = w3.astype(BF16)[None]
        return _mm_nn(name + "_fwd", a, wb, 0, False), (a, wb)

    def bwd(res, g):
        a, wb = res
        return _mm_nt(name + "_dx", g, wb, 0, False), _mm_tn(name + "_dw", a, g, wb.shape[1], False, F32)

    op.defvjp(fwd, bwd)
    return op(a, w3)


def mm_gathered(name, a, w, l, grads, key):
    @jax.custom_vjp
    def op(a):
        return _mm_nn(name + "_fwd", a, w, l, True)

    def fwd(a):
        return _mm_nn(name + "_fwd", a, w, l, True), (a,)

    def bwd(res, g):
        (a,) = res
        grads[key] = _mm_tn(name + "_dw", a, g, w.shape[1], True, BF16)
        return (_mm_nt(name + "_dx", g, w, l, True),)

    op.defvjp(fwd, bwd)
    return op(a)


def _spec_shape(spec, G, T):
    k = spec[0]
    if k == "row":
        return (T, spec[1])
    if k == "rowg":
        return (T, (G // spec[2]) * spec[1])
    if k == "bc":
        return (spec[1], spec[2])
    return (spec[1], (G // spec[3]) * spec[2])


def _spec_block(spec, tm, rmap):
    k = spec[0]
    if k == "row":
        return pl.BlockSpec((tm, spec[1]), lambda g, r: (rmap(r), 0))
    if k == "rowg":
        d = spec[2]
        return pl.BlockSpec((tm, spec[1]), lambda g, r: (rmap(r), g // d))
    if k == "bc":
        return pl.BlockSpec((spec[1], spec[2]), lambda g, r: (0, 0))
    d = spec[3]
    return pl.BlockSpec((spec[1], spec[2]), lambda g, r: (0, g // d))


def block_op(name, f, in_specs, out_specs, G, T, tm, diff, carry=(), row0=False, order=None, f_saved=None):
    n_in, n_out, n_c = len(in_specs), len(out_specs), len(carry)
    n_sv = n_out if f_saved is not None else 0
    n_steps = T // tm
    assert T % tm == 0
    order = order or (lambda s: s)
    diff_idx = [i for i in range(n_in) if diff[i]]
    for i in diff_idx:
        assert in_specs[i][0] != "row" or G == 1
        assert in_specs[i][0] != "rowg" or in_specs[i][2] == 1
    out_shapes = [jax.ShapeDtypeStruct(_spec_shape(s, G, T), F32) for s in out_specs]
    save_shapes = [jax.ShapeDtypeStruct((n_steps, a, G * b), F32) for a, b in carry]
    sem = ("arbitrary", "arbitrary")

    def call_f(r_idx, cvals, vals):
        args = list(vals)
        if n_c:
            args = [tuple(cvals)] + args
        if row0:
            args = [r_idx * tm] + args
        return f(*args)

    def fwd_body(*refs):
        in_refs = refs[:n_in]
        out_refs = refs[n_in:n_in + n_out]
        save_refs = refs[n_in + n_out:n_in + n_out + n_c]
        c_refs = refs[n_in + n_out + n_c:]
        r = pl.program_id(1)
        if n_c:
            @pl.when(r == 0)
            def _():
                for c in c_refs:
                    c[...] = jnp.zeros_like(c)

            cvals = [c[...] for c in c_refs]
            for s, v in zip(save_refs, cvals):
                s[...] = v
            new_c, outs = call_f(r, cvals, [x[...] for x in in_refs])
            for c, v in zip(c_refs, new_c):
                c[...] = v
        else:
            outs = call_f(r, (), [x[...] for x in in_refs])
        for o, v in zip(out_refs, outs):
            o[...] = v.astype(F32)

    def fwd_call(*arrays):
        res = pl.pallas_call(
            fwd_body, name=name + "_fwd", grid=(G, n_steps),
            in_specs=[_spec_block(s, tm, order) for s in in_specs],
            out_specs=[_spec_block(s, tm, order) for s in out_specs]
            + [pl.BlockSpec((None, a, b), lambda g, r: (r, 0, g)) for a, b in carry],
            out_shape=out_shapes + save_shapes,
            scratch_shapes=[pltpu.VMEM((a, b), F32) for a, b in carry],
            compiler_params=_params(sem))(*arrays)
        return tuple(res)

    def bwd_body(*refs):
        in_refs = refs[:n_in]
        save_refs = refs[n_in:n_in + n_c]
        ct_refs = refs[n_in + n_c:n_in + n_c + n_out]
        at = n_in + n_c + n_out
        sv_refs = refs[at:at + n_sv]
        g_refs = refs[at + n_sv:at + n_sv + len(diff_idx)]
        dc_refs = refs[at + n_sv + len(diff_idx):]
        g = pl.program_id(0)
        r = pl.program_id(1)
        vals = [x[...] for x in in_refs]
        if n_c:
            @pl.when(r == 0)
            def _():
                for d in dc_refs:
                    d[...] = jnp.zeros_like(d)

        def fun(cvals, dvals):
            full = list(vals)
            for i, v in zip(diff_idx, dvals):
                full[i] = v
            if n_sv:
                return f_saved(tuple(cvals), *full, *[s[...] for s in sv_refs])
            return call_f(n_steps - 1 - r if n_c else r, cvals, full)

        _, vjp = jax.vjp(fun, tuple(s[...] for s in save_refs), tuple(vals[i] for i in diff_idx))
        cts = tuple(c[...] for c in ct_refs)
        if n_c:
            cts = (tuple(d[...] for d in dc_refs), cts)
        dcin, dvals = vjp(cts)
        for d, v in zip(dc_refs, dcin):
            d[...] = v
        for gref, i, v in zip(g_refs, diff_idx, dvals):
            spec = in_specs[i]
            if spec[0] in ("row", "rowg"):
                gref[...] = v
            else:
                first = (r == 0) & ((g == 0) if spec[0] == "bc" else (g % spec[3] == 0))

                @pl.when(first)
                def _(gref=gref, v=v):
                    gref[...] = v

                @pl.when(jnp.logical_not(first))
                def _(gref=gref, v=v):
                    gref[...] += v

    def bwd_call(arrays, saved, cts, outs):
        rmap = (lambda r: order(n_steps - 1 - r)) if n_c else (lambda r: r)
        res = pl.pallas_call(
            bwd_body, name=name + "_bwd", grid=(G, n_steps),
            in_specs=[_spec_block(s, tm, rmap) for s in in_specs]
            + [pl.BlockSpec((None, a, b), lambda g, r: (n_steps - 1 - r, 0, g)) for a, b in carry]
            + [_spec_block(s, tm, rmap) for s in out_specs] * (2 if n_sv else 1),
            out_specs=[_spec_block(in_specs[i], tm, rmap) for i in diff_idx],
            out_shape=[jax.ShapeDtypeStruct(_spec_shape(in_specs[i], G, T), F32) for i in diff_idx],
            scratch_shapes=[pltpu.VMEM((a, b), F32) for a, b in carry],
            compiler_params=_params(sem))(*arrays, *saved, *cts, *outs)
        return tuple(res)

    @jax.custom_vjp
    def op(*arrays):
        return fwd_call(*arrays)[:n_out]

    def op_fwd(*arrays):
        res = fwd_call(*arrays)
        return res[:n_out], (arrays, res[n_out:], res[:n_out] if n_sv else ())

    def op_bwd(res, cts):
        arrays, saved, outs = res
        grads = bwd_call(arrays, saved, cts, outs)
        out = [jnp.zeros_like(a) for a in arrays]
        for i, gval in zip(diff_idx, grads):
            out[i] = gval
        return tuple(out)

    op.defvjp(op_fwd, op_bwd)
    return op


def _rows(n, m):
    return lax.broadcasted_iota(jnp.int32, (n, m), 0)


def _ctx_select(row0, tm, n_ctx, v_lat, v_ctx):
    if n_ctx == 0:
        return v_lat
    is_ctx = (row0 + _rows(tm, 1)) < n_ctx
    return jnp.where(is_ctx, v_ctx, v_lat)


def _silu(x):
    return x * jax.nn.sigmoid(x)


def _f_modulate(tm, n_ctx):
    def f(row0, x, sh_l, sh_c, sc_l, sc_c):
        sh = _ctx_select(row0, tm, n_ctx, sh_l, sh_c)
        sc = _ctx_select(row0, tm, n_ctx, sc_l, sc_c)
        return (x * (1 + sc) + sh,)
    return f


def _f_postnorm(tm, n_ctx):
    def f(row0, x, y, g_l, g_c, w, b):
        z = DN_ALPHA * x + _ctx_select(row0, tm, n_ctx, g_l, g_c) * y
        mu = jnp.mean(z, -1, keepdims=True)
        zc = z - mu
        var = jnp.mean(zc * zc, -1, keepdims=True)
        return (zc * lax.rsqrt(var + EPS) * w + b,)
    return f


def _log_sigmoid(x):
    return -(jnp.maximum(-x, 0.0) + jnp.log1p(jnp.exp(-jnp.abs(x))))


def _f_gla_prep(glr, wg0, wg1, b0, b1):
    return (_log_sigmoid(bdot(glr, wg0, "nn") + b0) / GLA_GATE_TAU,
            _log_sigmoid(bdot(glr, wg1, "nn") + b1) / GLA_GATE_TAU)


def _f_gla_step(rev):
    def f(carry, q, k, v, la):
        (st,) = carry
        n = q.shape[0]
        cols = lax.broadcasted_iota(jnp.int32, (n, n), 1)
        tri = (_rows(n, n) <= cols) if rev else (_rows(n, n) >= cols)
        b = jnp.dot(tri.astype(F32), la, precision=lax.Precision.HIGHEST)
        qe = q * (GLA_DK ** -0.5) * jnp.exp(b)
        ke = k * jnp.exp(-b)
        att = jnp.where(tri, bdot(qe, ke, "nt"), 0.0)
        o = bdot(att, v, "nn") + bdot(qe, st, "nt")
        end = 0 if rev else n - 1
        b_last = jnp.sum(jnp.where(_rows(n, 1) == end, b, 0.0), axis=0, keepdims=True)
        kd = k * jnp.exp(b_last - b)
        st = st * jnp.exp(b_last) + bdot(v, kd, "tn")
        return (st,), (o,)
    return f


def _f_gla_norm(o0, o1, gr, w):
    o = o0 + o1
    mu = jnp.mean(o, -1, keepdims=True)
    oc = o - mu
    var = jnp.mean(oc * oc, -1, keepdims=True)
    return (oc * lax.rsqrt(var + EPS) * w * _silu(gr),)


@functools.partial(jax.custom_vjp, nondiff_argnums=(1, 2))
def _shift_rows(x, d, up):
    n = x.shape[0]
    rows = _rows(n, 1)
    if up:
        return jnp.where(rows < n - d, pltpu.roll(x, n - d, 0), 0.0)
    return jnp.where(rows >= d, pltpu.roll(x, d, 0), 0.0)


def _shift_fwd(x, d, up):
    return _shift_rows(x, d, up), None


def _shift_bwd(d, up, _, g):
    return (_shift_rows(g, d, not up),)


_shift_rows.defvjp(_shift_fwd, _shift_bwd)


def _scan_doubling(ur, ui, ar, ai, rev):
    n = ur.shape[0]
    xr, xi, pr, pi = ur, ui, ar, ai
    d = 1
    while d < n:
        sr, si = _shift_rows(xr, d, rev), _shift_rows(xi, d, rev)
        xr, xi = xr + pr * sr - pi * si, xi + pr * si + pi * sr
        pr, pi = pr * pr - pi * pi, 2 * pr * pi
        d *= 2
    return xr, xi


@functools.partial(jax.custom_vjp, nondiff_argnums=(6,))
def _scan_known(ur, ui, ar, ai, xr, xi, rev):
    return xr, xi


def _scan_known_fwd(ur, ui, ar, ai, xr, xi, rev):
    return (xr, xi), (ar, ai, xr, xi)


def _scan_known_bwd(rev, res, g):
    ar, ai, xr, xi = res
    lr, li = _scan_doubling(g[0], g[1], ar, -ai, not rev)
    pr, pi = _shift_rows(xr, 1, rev), _shift_rows(xi, 1, rev)
    dar = jnp.sum(lr * pr + li * pi, axis=0, keepdims=True)
    dai = jnp.sum(li * pr - lr * pi, axis=0, keepdims=True)
    return lr, li, dar, dai, jnp.zeros_like(xr), jnp.zeros_like(xi)


_scan_known.defvjp(_scan_known_fwd, _scan_known_bwd)


def _f_s5_step(rev, known=False):
    def f(carry, bur, bui, lam_re, lam_im, log_dt, *states):
        cr, ci = carry
        n = bur.shape[0]
        dt = jnp.exp(log_dt)
        mag = jnp.exp(lam_re * dt)
        ar, ai = mag * jnp.cos(lam_im * dt), mag * jnp.sin(lam_im * dt)
        den = lam_re * lam_re + lam_im * lam_im
        nr, ni = ar - 1, ai
        kr = (nr * lam_re + ni * lam_im) / den
        ki = (ni * lam_re - nr * lam_im) / den
        first = _rows(n, 1) == (n - 1 if rev else 0)
        ur = kr * bur - ki * bui + jnp.where(first, ar * cr - ai * ci, 0.0)
        ui = kr * bui + ki * bur + jnp.where(first, ar * ci + ai * cr, 0.0)
        if known:
            xr, xi = _scan_known(ur, ui, ar, ai, states[0], states[1], rev)
        else:
            xr, xi = _scan_doubling(ur, ui, ar, ai, rev)
        last = _rows(n, 1) == (0 if rev else n - 1)
        cr = jnp.sum(jnp.where(last, xr, 0.0), axis=0, keepdims=True)
        ci = jnp.sum(jnp.where(last, xi, 0.0), axis=0, keepdims=True)
        return (cr, ci), (xr, xi)
    return f


def _f_s5_post(su, dskip, y0r, y0i, y1r, y1i):
    return (jax.nn.gelu(su * dskip + y0r - y0i + y1r - y1i),)


def _f_s5_glu(y, t):
    return (y * jax.nn.sigmoid(t),)


def _swap_pairs(x):
    lane = lax.broadcasted_iota(jnp.int32, x.shape, 1)
    return jnp.where(lane % 2 == 0, pltpu.roll(x, x.shape[1] - 1, 1), pltpu.roll(x, 1, 1))


@jax.custom_vjp
def _rope(x, cos2, sin2):
    return x * cos2 + _swap_pairs(x) * sin2


def _rope_fwd(x, cos2, sin2):
    return _rope(x, cos2, sin2), (cos2, sin2)


def _rope_bwd(res, g):
    cos2, sin2 = res
    return g * cos2 + _swap_pairs(g * sin2), jnp.zeros_like(cos2), jnp.zeros_like(sin2)


_rope.defvjp(_rope_fwd, _rope_bwd)


def _f_qk_norm_rope(x, cos2, sin2, w):
    xn = x * lax.rsqrt(jnp.mean(x * x, -1, keepdims=True) + EPS) * w
    return (_rope(xn, cos2, sin2),)


def _f_attn(q, k, v):
    s = bdot(q, k, "nt") * (ATTN_HEAD_DIM ** -0.5)
    e = jnp.exp(s - jnp.max(s, -1, keepdims=True))
    p = e / jnp.sum(e, -1, keepdims=True)
    return (bdot(p, v, "nn"),)


def _f_merge(ga, gb, gc, pa, pb, pc):
    return (jax.nn.sigmoid(ga) * pa + jax.nn.sigmoid(gb) * pb + jax.nn.sigmoid(gc) * pc,)


def _f_swiglu(a, b):
    return (_silu(a) * b,)


def _f_silu(x):
    return (_silu(x),)


def _f_add_bias(x, b):
    return (x + b,)


def _scan_order(rev, n_ctx, T, tm):
    if not rev:
        return None
    nc, n = n_ctx // tm, T // tm
    return lambda s: jnp.where(s < nc, nc - 1 - s, n - 1 - (s - nc))


def modulate(name, x, sh, sc, n_ctx):
    T, D = x.shape
    tm = _tile(T, ROW_TILE, SUBLANE)
    cw = _tile(D, COL_TILE, LANE)
    col, vec = ("rowg", cw, 1), ("bcg", 1, cw, 1)
    op = block_op(name, _f_modulate(tm, n_ctx), [col, vec, vec, vec, vec], [col], D // cw, T, tm,
                  [True] * 5, row0=True)
    return op(x, sh[0], sh[1], sc[0], sc[1])[0]


def postnorm(name, x, y, g, w, b, n_ctx):
    T, D = x.shape
    tm = _tile(T, ROW_TILE, SUBLANE)
    vec = ("bc", 1, D)
    op = block_op(name, _f_postnorm(tm, n_ctx), [("row", D), ("row", D), vec, vec, vec, vec], [("row", D)], 1, T,
                  tm, [True] * 6, row0=True)
    return op(x, y, g[0], g[1], w, b)[0]


def rowwise(name, f, arrays, n_out=1):
    T, w = arrays[0].shape
    tm = _tile(T, ROW_TILE, SUBLANE)
    cw = _tile(w, COL_TILE, LANE)
    col = ("rowg", cw, 1)
    op = block_op(name, f, [col] * len(arrays), [col] * n_out, w // cw, T, tm, [True] * len(arrays))
    return op(*arrays)


def gla_prep(name, glr, wg, bg):
    T = glr.shape[0]
    qk = wg.shape[-1]
    tm = _tile(T, ROW_TILE, SUBLANE)
    op = block_op(name, _f_gla_prep, [("row", LANE), ("bc", LANE, qk), ("bc", LANE, qk), ("bc", 1, qk), ("bc", 1, qk)],
                  [("row", qk), ("row", qk)], 1, T, tm, [True] * 5)
    return op(glr, wg[0], wg[1], bg[0], bg[1])


def gla_scan(name, q, k, v, la, rev, n_ctx):
    T = q.shape[0]
    op = block_op(name, _f_gla_step(rev), [("rowg", GLA_DK, 1), ("rowg", GLA_DK, 1), ("rowg", GLA_DV, 1), ("rowg", GLA_DK, 1)],
                  [("rowg", GLA_DV, 1)], GLA_HEADS, T, GLA_CHUNK, [True] * 4, carry=[(GLA_DV, GLA_DK)],
                  order=_scan_order(rev, n_ctx, T, GLA_CHUNK))
    return op(q, k, v, la)[0]


def gla_norm(name, o0, o1, gr, w):
    T = o0.shape[0]
    tm = _tile(T, ROW_TILE, SUBLANE)
    hd = ("rowg", GLA_DV, 1)
    op = block_op(name, _f_gla_norm, [hd, hd, hd, ("bcg", 1, GLA_DV, 1)], [hd], GLA_HEADS, T, tm, [True] * 4)
    return op(o0, o1, gr, w)[0]


def s5_scan(name, bur, bui, lam_re, lam_im, log_dt, rev, n_ctx):
    T, S = bur.shape
    cols = _tile(S, 768, LANE)
    G = S // cols
    col, par = ("rowg", cols, 1), ("bcg", 1, cols, 1)
    op = block_op(name, _f_s5_step(rev), [col, col, par, par, par], [col, col], G, T, S5_CHUNK, [True] * 5,
                  carry=[(1, cols), (1, cols)], order=_scan_order(rev, n_ctx, T, S5_CHUNK),
                  f_saved=_f_s5_step(rev, True))
    return op(bur, bui, lam_re, lam_im, log_dt)


def qk_norm_rope(name, x, cos2, sin2, w):
    T = x.shape[0]
    G = x.shape[1] // ATTN_HEAD_DIM
    tm = _tile(T, ROW_TILE, SUBLANE)
    hd = ("rowg", ATTN_HEAD_DIM, 1)
    op = block_op(name, _f_qk_norm_rope, [hd, ("row", ATTN_HEAD_DIM), ("row", ATTN_HEAD_DIM), ("bc", 1, ATTN_HEAD_DIM)],
                  [hd], G, T, tm, [True, False, False, True])
    return op(x, cos2, sin2, w)[0]


def attention(name, q, k, v):
    T, Tk = q.shape[0], k.shape[0]
    tm = _tile(T, ROW_TILE, SUBLANE)
    grp = ATTN_Q_HEADS // ATTN_KV_HEADS
    kv = ("bcg", Tk, ATTN_HEAD_DIM, grp)
    op = block_op(name, _f_attn, [("rowg", ATTN_HEAD_DIM, 1), kv, kv], [("rowg", ATTN_HEAD_DIM, 1)], ATTN_Q_HEADS, T,
                  tm, [True] * 3)
    return op(q, k, v)[0]


def sq_loss(name, y, t):
    T, D = y.shape
    tm = _tile(T, ROW_TILE, SUBLANE)

    def fwd_body(y_ref, t_ref, o_ref):
        e = y_ref[...] - t_ref[...]
        part = jnp.sum(jnp.sum(e * e, -1, keepdims=True), 0, keepdims=True) * (0.5 / D)

        @pl.when(pl.program_id(0) == 0)
        def _():
            o_ref[...] = jnp.zeros_like(o_ref)

        o_ref[...] += part * jnp.ones((1, LANE), F32)

    def bwd_body(y_ref, t_ref, g_ref, o_ref):
        o_ref[...] = (y_ref[...] - t_ref[...]) * (g_ref[:, 0:1] / D)

    row = pl.BlockSpec((tm, D), lambda r: (r, 0))
    one = pl.BlockSpec((1, LANE), lambda r: (0, 0))

    def fwd_call(y, t):
        return pl.pallas_call(fwd_body, name=name + "_fwd", grid=(T // tm,), in_specs=[row, row], out_specs=one,
                              out_shape=jax.ShapeDtypeStruct((1, LANE), F32), compiler_params=_params(("arbitrary",)))(y, t)

    @jax.custom_vjp
    def op(y, t):
        return fwd_call(y, t)[0, 0]

    def op_fwd(y, t):
        return fwd_call(y, t)[0, 0], (y, t)

    def op_bwd(res, g):
        y, t = res
        gy = pl.pallas_call(bwd_body, name=name + "_bwd", grid=(T // tm,), in_specs=[row, row, one], out_specs=row,
                            out_shape=jax.ShapeDtypeStruct((T, D), F32), compiler_params=_params(("arbitrary",)))(
                                y, t, jnp.full((1, LANE), g, F32))
        return gy, jnp.zeros_like(t)

    op.defvjp(op_fwd, op_bwd)
    return op(y, t)


def _rope_tables(n_ctx, n_lat):
    n_rows = n_lat // GRID_W
    rows = jnp.repeat(jnp.arange(n_rows), GRID_W).astype(F32)
    cols = jnp.tile(jnp.arange(GRID_W), n_rows).astype(F32)
    n_freq = ATTN_HEAD_DIM // 4
    inv = ROPE_THETA ** (-jnp.arange(n_freq, dtype=F32) / n_freq)
    ang = jnp.concatenate([rows[:, None] * inv, cols[:, None] * inv], -1)
    cos2 = jnp.repeat(jnp.cos(ang), 2, axis=-1)
    sin2 = jnp.stack([-jnp.sin(ang), jnp.sin(ang)], -1).reshape(n_lat, ATTN_HEAD_DIM)
    cos2 = jnp.concatenate([jnp.ones((n_ctx, ATTN_HEAD_DIM), F32), cos2], 0)
    sin2 = jnp.concatenate([jnp.zeros((n_ctx, ATTN_HEAD_DIM), F32), sin2], 0)
    return cos2, sin2


def _in_layout(D, s5_width):
    qk, gv = GLA_HEADS * GLA_DK, GLA_HEADS * GLA_DV
    aq, akv = ATTN_Q_HEADS * ATTN_HEAD_DIM, ATTN_KV_HEADS * ATTN_HEAD_DIM
    widths = [("gq", qk), ("gk", qk), ("gv", gv), ("gr", gv), ("glr", GLA_GATE_RANK), ("su", s5_width), ("aq", aq),
              ("ak", akv), ("av", akv), ("ga", D), ("gb", D), ("gc", D)]
    off, out = 0, {}
    for n, w in widths:
        out[n] = (off, w)
        off += w
    return out, off


def _padded_width(width):
    return width if width % LANE == 0 else -(-width // SHARD_PAD) * SHARD_PAD


def _shard_cols(z, off, width, shard, padded):
    parts = []
    for j in range(N_CHIPS):
        lo, hi = max(off, j * shard), min(off + width, (j + 1) * shard)
        if lo < hi:
            parts.append(z[:, j * padded + lo - j * shard:j * padded + hi - j * shard])
    return parts[0] if len(parts) == 1 else jnp.concatenate(parts, 1)


def _s5_in_blocks(b):
    G, P, C = b.shape
    nb, bg = G // S5_BLOCK_GROUPS, S5_BLOCK_GROUPS
    t = b.reshape(nb, bg, P, C).transpose(0, 1, 3, 2)
    return jnp.einsum("bgcp,gh->bgchp", t, jnp.eye(bg, dtype=F32)).reshape(nb, bg * C, bg * P)


def _s5_out_blocks(c):
    G, C, P = c.shape
    nb, bg = G // S5_BLOCK_GROUPS, S5_BLOCK_GROUPS
    t = c.reshape(nb, bg, C, P).transpose(0, 1, 3, 2)
    return jnp.einsum("bgpc,gh->bgphc", t, jnp.eye(bg, dtype=F32)).reshape(nb, bg * P, bg * C)


def _layer(keep_ctx, xa, n_ctx, mod, p, big, cos2, sin2):
    T, D = xa.shape
    S = p["s5_d"].shape[-1]
    lay, width = _in_layout(D, S)
    lo = 0 if keep_ctx else n_ctx
    ctx_rows = n_ctx if keep_ctx else 0

    h = modulate("modulate1", xa, mod["sh1"], mod["sc1"], n_ctx)
    z = big["w_in"]("in_proj", h)
    shard = width // N_CHIPS
    zz = {n: _shard_cols(z, o, w, shard, z.shape[1] // N_CHIPS) for n, (o, w) in lay.items()}

    wg = jnp.pad(p["w_gla_gate"], ((0, 0), (0, LANE - GLA_GATE_RANK), (0, 0)))
    glr = jnp.pad(zz["glr"], ((0, 0), (0, LANE - GLA_GATE_RANK)))
    la0, la1 = gla_prep("gla_prep", glr, wg, p["b_gla_gate"][:, None, :])
    o0 = gla_scan("gla_scan", zz["gq"], zz["gk"], zz["gv"], la0, False, n_ctx)
    o1 = gla_scan("gla_scan_rev", zz["gq"], zz["gk"], zz["gv"], la1, True, n_ctx)
    o_gla = gla_norm("gla_norm", o0[lo:], o1[lo:], zz["gr"][lo:], p["gla_norm_w"][None, :])

    su = zz["su"]
    bur = mm("s5_in_re", su, _s5_in_blocks(p["s5_b_re"]))
    bui = mm("s5_in_im", su, _s5_in_blocks(p["s5_b_im"]))
    ys = []
    for d in range(2):
        row = lambda t: t.reshape(1, -1)
        ldt = jnp.repeat(p["s5_log_dt"][d], S5_STATE)
        sr, si = s5_scan("s5_scan_rev" if d else "s5_scan", bur, bui, row(p["s5_lam_re"][d]),
                         row(p["s5_lam_im"][d]), row(ldt), d == 1, n_ctx)
        ys.append(mm("s5_out_re", sr[lo:], _s5_out_blocks(p["s5_c_re"][d])))
        ys.append(mm("s5_out_im", si[lo:], _s5_out_blocks(p["s5_c_im"][d])))
    T2 = T - lo
    tm = _tile(T2, ROW_TILE, SUBLANE)
    post = block_op("s5_post", _f_s5_post, [("row", S), ("bc", 1, S)] + [("row", S)] * 4, [("row", S)], 1, T2, tm,
                    [True] * 6)
    yg = post(su[lo:], p["s5_d"][None, :], *ys)[0]
    o_s5 = rowwise("s5_glu", _f_s5_glu, [yg, big["w_s5_glu"]("s5_glu_proj", yg)])[0]

    qn = qk_norm_rope("q_norm_rope", zz["aq"], cos2, sin2, p["q_norm_w"][None, :])
    kn = qk_norm_rope("k_norm_rope", zz["ak"], cos2, sin2, p["k_norm_w"][None, :])
    o_attn = attention("attn_lat", qn[n_ctx:], kn, zz["av"])
    if keep_ctx:
        o_c = attention("attn_ctx", qn[:n_ctx], kn[:n_ctx], zz["av"][:n_ctx])
        o_attn = jnp.concatenate([o_c, o_attn], 0)

    merged = rowwise("merge", _f_merge, [zz["ga"][lo:], zz["gb"][lo:], zz["gc"][lo:],
                                         big["w_proj_gla"]("proj_gla", o_gla),
                                         big["w_proj_s5"]("proj_s5", o_s5),
                                         big["w_proj_attn"]("proj_attn", o_attn)])[0]
    mix = big["w_out"]("out_proj", merged)
    x1 = postnorm("postnorm1", xa[lo:], mix, mod["g1"], p["ln1_w"][None, :], p["ln1_b"][None, :], ctx_rows)
    h2 = modulate("modulate2", x1, mod["sh2"], mod["sc2"], ctx_rows)
    u = big["w_ffn_in"]("ffn_in", h2)
    F = u.shape[1] // 2
    act = rowwise("swiglu", _f_swiglu, [u[:, :F], u[:, F:]])[0]
    f = big["w_ffn_out"]("ffn_out", act)
    return postnorm("postnorm2", x1, f, mod["g2"], p["ln2_w"][None, :], p["ln2_b"][None, :], ctx_rows)


def local_loss(x, ctx, target, small, cond, gathered, grads):
    n_lat, D = x.shape
    n_ctx = ctx.shape[0]
    cos2, sin2 = _rope_tables(n_ctx, n_lat)
    xa = jnp.concatenate([ctx, x], 0)
    depth = gathered["w_in"].shape[0]
    for l in range(depth):
        p = {n: v[l] for n, v in small.items() if n != "c_ctx"}
        big = {n: functools.partial(lambda n, l, name, a: mm_gathered(name, a, gathered[n], l, grads, (n, l)), n, l)
               for n in gathered}
        m = block_op("ada_bias", _f_add_bias, [("row", 6 * D), ("bc", 1, 6 * D)], [("row", 6 * D)], 1, 2, 2,
                     [True, True])(cond[l], p["b_ada"][None, :])[0]
        names = ["sh1", "sc1", "g1", "sh2", "sc2", "g2"]
        mod = {n: (m[0:1, i * D:(i + 1) * D], m[1:2, i * D:(i + 1) * D]) for i, n in enumerate(names)}
        xa = _layer(l < depth - 1, xa, n_ctx, mod, p, big, cos2, sin2)
    return sq_loss("loss", xa, target)


MESH = pl.DeviceIdType.MESH
ANY = pl.BlockSpec(memory_space=pl.ANY)


def _place():
    x, y, c = lax.axis_index("x"), lax.axis_index("y"), lax.axis_index("c")
    chips = [(1 - x, y), (x, 1 - y), (1 - x, 1 - y)]
    return x, y, c, chips


def _rcopy(src, dst, ssem, rsem, to):
    return pltpu.make_async_remote_copy(src_ref=src, dst_ref=dst, send_sem=ssem, recv_sem=rsem, device_id=to,
                                        device_id_type=MESH)


def gather_shards(bufs):
    n = len(bufs)
    L = bufs[0].shape[0]
    half = L // 2

    def body(*refs):
        dst = refs[n:2 * n]
        isend, irecv, fsend, frecv, dsend, drecv = refs[2 * n:]
        x, y, c, _ = _place()
        j = 2 * x + y
        nbr = [(1 - x, y), (x, 1 - y)]
        jn = [2 * kx + ky for kx, ky in nbr]
        jd = 2 * (1 - x) + (1 - y)
        sibling = (x, y, 1 - c)
        mine, other = pl.ds(c * half, half), pl.ds((1 - c) * half, half)

        def piece(i, layers, chip, q):
            hr = bufs[i].shape[2] // 2
            return dst[i].at[layers, chip, pl.ds(q * hr, hr)]

        direct = [_rcopy(dst[i].at[mine, j], dst[i].at[mine, j], isend.at[i, r], irecv.at[i, r], (*nbr[r], c))
                  for i in range(n) for r in range(2)]
        for cp in direct:
            cp.start()
        passed = []
        for i in range(n):
            for r in range(2):
                part = dst[i].at[mine, jn[r]]
                _rcopy(part, part, isend.at[i, r], irecv.at[i, r], (*nbr[r], c)).wait_recv()
                fwd = piece(i, mine, jn[r], 1 - r)
                passed.append(_rcopy(fwd, fwd, fsend.at[i, r], frecv.at[i, 1 - r], (*nbr[1 - r], c)))
                passed.append(_rcopy(part, part, dsend.at[i, r], drecv.at[i, r], sibling))
                passed[-2].start()
                passed[-1].start()
        for i in range(n):
            for q in range(2):
                part = piece(i, mine, jd, q)
                _rcopy(part, part, fsend.at[i, q], frecv.at[i, q], (*nbr[q], c)).wait_recv()
                passed.append(_rcopy(part, part, dsend.at[i, 2 + q], drecv.at[i, 2 + q], sibling))
                passed[-1].start()
        for i in range(n):
            for r in range(2):
                part = dst[i].at[other, jn[r]]
                _rcopy(part, part, dsend.at[i, r], drecv.at[i, r], sibling).wait_recv()
                part = piece(i, other, jd, r)
                _rcopy(part, part, dsend.at[i, 2 + r], drecv.at[i, 2 + r], sibling).wait_recv()
        for cp in direct + passed:
            cp.wait_send()

    out_shape = [jax.ShapeDtypeStruct(b.shape, b.dtype) for b in bufs]
    sems = [pltpu.SemaphoreType.DMA((n, 2))] * 4 + [pltpu.SemaphoreType.DMA((n, 4))] * 2
    return pl.pallas_call(body, name="gather_shards", in_specs=[ANY] * n, out_specs=[ANY] * n, out_shape=out_shape,
                          scratch_shapes=sems, input_output_aliases={i: i for i in range(n)})(*bufs)


def swap_halves(grads):
    n = len(grads)

    def body(*refs):
        src, dst = refs[:n], refs[n:2 * n]
        ssem, rsem = refs[2 * n:]
        x, y, c, _ = _place()
        cps = []
        for i in range(n):
            hr = grads[i].shape[1] // 2
            cps.append(_rcopy(src[i].at[:, pl.ds((1 - c) * hr, hr)], dst[i], ssem.at[i], rsem.at[i], (x, y, 1 - c)))
        for cp in cps:
            cp.start()
        for cp in cps:
            cp.wait()

    out_shape = [jax.ShapeDtypeStruct((g.shape[0], g.shape[1] // 2, g.shape[2]), g.dtype) for g in grads]
    return pl.pallas_call(body, name="swap_halves", in_specs=[ANY] * n, out_specs=[ANY] * n, out_shape=out_shape,
                          scratch_shapes=[pltpu.SemaphoreType.DMA((n,))] * 2)(*grads)


def scatter_diagonal(parts):
    n = len(parts)

    def body(*refs):
        src, dst = refs[:n], refs[n:2 * n]
        ssem, rsem = refs[2 * n:]
        x, y, c, _ = _place()
        nbr = [(1 - x, y), (x, 1 - y)]
        jd = 2 * (1 - x) + (1 - y)
        cps = []
        for i in range(n):
            h2 = parts[i].shape[1] // 2
            for q in range(2):
                cps.append(_rcopy(src[i].at[jd, pl.ds(q * h2, h2)], dst[i].at[q], ssem.at[i, q], rsem.at[i, q],
                                  (*nbr[q], c)))
        for cp in cps:
            cp.start()
        for cp in cps:
            cp.wait()

    out_shape = [jax.ShapeDtypeStruct((2, p.shape[1] // 2, p.shape[2]), p.dtype) for p in parts]
    return pl.pallas_call(body, name="scatter_diagonal", in_specs=[ANY] * n, out_specs=[ANY] * n, out_shape=out_shape,
                          scratch_shapes=[pltpu.SemaphoreType.DMA((n, 2))] * 2)(*parts)


def scatter_shards(parts, via):
    n = len(parts)

    def body(*refs):
        src, add, dst = refs[:n], refs[n:2 * n], refs[2 * n:3 * n]
        ssem, rsem = refs[3 * n:]
        x, y, c, _ = _place()
        nbr = [(1 - x, y), (x, 1 - y)]
        cps = []
        for i in range(n):
            h2 = parts[i].shape[1] // 2
            for t, (kx, ky) in enumerate(nbr):
                plain = pl.ds(t * h2, h2)
                cps.append(_rcopy(src[i].at[2 * kx + ky, plain], dst[i].at[t, plain], ssem.at[i, 2 * t],
                                  rsem.at[i, 2 * t], (kx, ky, c)))
                cps.append(_rcopy(add[i].at[t], dst[i].at[t, pl.ds((1 - t) * h2, h2)], ssem.at[i, 2 * t + 1],
                                  rsem.at[i, 2 * t + 1], (kx, ky, c)))
        for cp in cps:
            cp.start()
        for cp in cps:
            cp.wait()

    out_shape = [jax.ShapeDtypeStruct((2,) + p.shape[1:], p.dtype) for p in parts]
    return pl.pallas_call(body, name="scatter_shards", in_specs=[ANY] * (2 * n), out_specs=[ANY] * n,
                          out_shape=out_shape, scratch_shapes=[pltpu.SemaphoreType.DMA((n, 4))] * 2)(*parts, *via)


def join_halves(bufs):
    n = len(bufs)

    def body(*refs):
        dst = refs[n:2 * n]
        ssem, rsem = refs[2 * n:]
        x, y, c, _ = _place()
        cps = []
        for i in range(n):
            hr = bufs[i].shape[1] // 2
            mine = dst[i].at[:, pl.ds(c * hr, hr)]
            cps.append(_rcopy(mine, mine, ssem.at[i], rsem.at[i], (x, y, 1 - c)))
        for cp in cps:
            cp.start()
        for cp in cps:
            cp.wait()

    out_shape = [jax.ShapeDtypeStruct(b.shape, b.dtype) for b in bufs]
    return pl.pallas_call(body, name="join_halves", in_specs=[ANY] * n, out_specs=[ANY] * n, out_shape=out_shape,
                          scratch_shapes=[pltpu.SemaphoreType.DMA((n,))] * 2,
                          input_output_aliases={i: i for i in range(n)})(*bufs)


def gather_blocks(v):
    def body(v_ref, out_ref, send_sems, recv_sems, local_sem):
        x, y, c, chips = _place()
        me, sibling = (x, y, c), (x, y, 1 - c)

        def blk(px, py, pc):
            return out_ref.at[4 * px + 2 * py + pc]

        def copy(k, block, to, src=None):
            return _rcopy(blk(*block) if src is None else src, blk(*block), send_sems.at[k], recv_sems.at[k], to)

        own = pltpu.make_async_copy(v_ref, blk(*me), local_sem)
        own.start()
        first = [copy(0, me, sibling, src=v_ref)]
        first += [copy(1 + r, me, (*chip, c), src=v_ref) for r, chip in enumerate(chips)]
        for cp in first:
            cp.start()
        passed = [copy(4 + r, (*chip, c), sibling) for r, chip in enumerate(chips)]
        for r, chip in enumerate(chips):
            copy(1 + r, (*chip, c), me).wait_recv()
            passed[r].start()
        copy(0, sibling, me).wait_recv()
        for r, chip in enumerate(chips):
            copy(4 + r, (*chip, 1 - c), me).wait_recv()
        for cp in first + passed:
            cp.wait_send()
        own.wait()

    return pl.pallas_call(body, name="gather_blocks", in_specs=[ANY], out_specs=ANY,
                          out_shape=jax.ShapeDtypeStruct((8,) + v.shape, v.dtype),
                          scratch_shapes=[pltpu.SemaphoreType.DMA((7,)), pltpu.SemaphoreType.DMA((7,)),
                                          pltpu.SemaphoreType.DMA])(v)


STREAM_BLOCK = 256 * 1024


def _stream_rows(rows, cols):
    base = 2 * SUBLANE if rows % (2 * SUBLANE) == 0 else SUBLANE
    return _tile(rows, max(base, STREAM_BLOCK // cols // base * base), base)


def _view3(a, lead):
    shape = a.shape[:lead] + (-1, a.shape[-1])
    return a.reshape(shape)


def sum_parts(name, terms, out, grid, where, into=None):
    n_skip = 0 if into is None else 1

    def body(w_ref, *refs):
        refs = refs[n_skip:]
        acc = refs[0][...].astype(F32)
        for t in refs[1:-1]:
            acc = acc + t[...].astype(F32)
        refs[-1][...] = acc.astype(refs[-1].dtype)

    grid_spec = pltpu.PrefetchScalarGridSpec(
        num_scalar_prefetch=1, grid=grid, in_specs=[ANY] * n_skip + [pl.BlockSpec(b, f) for _, b, f in terms],
        out_specs=pl.BlockSpec(out[2], out[3]))
    operands = ([] if into is None else [into]) + [t[0] for t in terms]
    return pl.pallas_call(body, name=name, grid_spec=grid_spec, out_shape=jax.ShapeDtypeStruct(out[0], out[1]),
                          input_output_aliases={} if into is None else {1: 0},
                          compiler_params=_params(("arbitrary",) * len(grid)))(where, *operands)


def cast_place(w, where):
    w3 = _view3(w, 1)
    L, rows, cols = w3.shape
    tr = _stream_rows(rows, cols)

    def body(w_ref, src, dst):
        dst[...] = src[...].astype(BF16)

    grid_spec = pltpu.PrefetchScalarGridSpec(
        num_scalar_prefetch=1, grid=(L, rows // tr),
        in_specs=[pl.BlockSpec((None, tr, cols), lambda l, r, wh: (l, r, 0))],
        out_specs=pl.BlockSpec((None, None, tr, cols), lambda l, r, wh: (l, wh[1], r, 0)))
    out = pl.pallas_call(body, name="cast_place", grid_spec=grid_spec,
                         out_shape=jax.ShapeDtypeStruct((L, N_CHIPS, rows, cols), BF16),
                         compiler_params=_params(("arbitrary", "arbitrary")))(where, w3)
    return out.reshape((L, N_CHIPS) + w.shape[1:])


def silu_vjp_rows(name, rows, x):
    def body(rows_ref, x_ref, o_ref):
        total = jnp.sum(rows_ref[...], axis=0, keepdims=True)
        s = jax.nn.sigmoid(x_ref[...])
        o_ref[...] = total * (s * (1.0 + x_ref[...] * (1.0 - s)))

    return pl.pallas_call(body, name=name, out_shape=jax.ShapeDtypeStruct(x.shape, F32))(rows, x)


def adamw(name, w, g, m, v):
    n, rows, cols = w.shape
    tr = _stream_rows(rows, cols)

    def body(w_ref, g_ref, m_ref, v_ref, d_ref, nm_ref, nv_ref):
        gv = g_ref[...]
        nm = ADAM_B1 * m_ref[...] + (1.0 - ADAM_B1) * gv
        nv = ADAM_B2 * v_ref[...] + (1.0 - ADAM_B2) * (gv * gv)
        m_hat = nm / (1.0 - ADAM_B1 ** ADAM_STEP)
        v_hat = nv / (1.0 - ADAM_B2 ** ADAM_STEP)
        d_ref[...] = -ADAM_LR * (m_hat / (jnp.sqrt(v_hat) + ADAM_EPS) + ADAM_WD * w_ref[...])
        nm_ref[...] = nm
        nv_ref[...] = nv

    blk = pl.BlockSpec((None, tr, cols), lambda l, r: (l, r, 0))
    shp = jax.ShapeDtypeStruct(w.shape, F32)
    return pl.pallas_call(body, name=name, grid=(n, rows // tr), in_specs=[blk] * 4, out_specs=[blk] * 3,
                          out_shape=[shp] * 3, compiler_params=_params(("arbitrary", "arbitrary")))(w, g, m, v)


COL_SHARDED = ("w_ada", "w_in", "w_proj_gla", "w_proj_s5", "w_proj_attn", "w_ffn_in")
ROW_SHARDED = ("w_s5_glu", "w_out", "w_ffn_out")
SHARDED = COL_SHARDED + ROW_SHARDED
DEFERRED = ("w_ada",)
REDUCED = tuple(n for n in SHARDED if n not in DEFERRED)
N_DEV = 8
GATE = ("w_gla_gate", "b_gla_gate")
WEIGHTS = ("c_ctx", "w_ada", "b_ada", "w_in", "w_gla_gate", "b_gla_gate", "gla_norm_w", "s5_lam_re", "s5_lam_im",
           "s5_log_dt", "s5_b_re", "s5_b_im", "s5_c_re", "s5_c_im", "s5_d", "w_s5_glu", "q_norm_w", "k_norm_w",
           "w_proj_gla", "w_proj_s5", "w_proj_attn", "w_out", "ln1_w", "ln1_b", "ln2_w", "ln2_b", "w_ffn_in",
           "w_ffn_out")
REPLICATED = tuple(n for n in WEIGHTS if n not in SHARDED + GATE)
SMALL = REPLICATED + GATE


def _pack(arrays):
    flat = jnp.concatenate([a.reshape(-1) for a in arrays])
    pad = (-flat.shape[0]) % (PACK_ROWS * LANE)
    return jnp.pad(flat, (0, pad)).reshape(-1, LANE)


def _unpack(packed, like):
    flat, out, off = packed.reshape(-1), [], 0
    for a in like:
        out.append(flat[off:off + a.size].reshape(a.shape))
        off += a.size
    return out


def kernel(x, c, ctx, c_ctx, w_ada, b_ada, w_in, w_gla_gate, b_gla_gate, gla_norm_w, s5_lam_re, s5_lam_im, s5_log_dt, s5_b_re, s5_b_im, s5_c_re, s5_c_im, s5_d, w_s5_glu, q_norm_w, k_norm_w, w_proj_gla, w_proj_s5, w_proj_attn, w_out, ln1_w, ln1_b, ln2_w, ln2_b, w_ffn_in, w_ffn_out, loss_target, m_c_ctx, m_w_ada, m_b_ada, m_w_in, m_w_gla_gate, m_b_gla_gate, m_gla_norm_w, m_s5_lam_re, m_s5_lam_im, m_s5_log_dt, m_s5_b_re, m_s5_b_im, m_s5_c_re, m_s5_c_im, m_s5_d, m_w_s5_glu, m_q_norm_w, m_k_norm_w, m_w_proj_gla, m_w_proj_s5, m_w_proj_attn, m_w_out, m_ln1_w, m_ln1_b, m_ln2_w, m_ln2_b, m_w_ffn_in, m_w_ffn_out, v_c_ctx, v_w_ada, v_b_ada, v_w_in, v_w_gla_gate, v_b_gla_gate, v_gla_norm_w, v_s5_lam_re, v_s5_lam_im, v_s5_log_dt, v_s5_b_re, v_s5_b_im, v_s5_c_re, v_s5_c_im, v_s5_d, v_w_s5_glu, v_q_norm_w, v_k_norm_w, v_w_proj_gla, v_w_proj_s5, v_w_proj_attn, v_w_out, v_ln1_w, v_ln1_b, v_ln2_w, v_ln2_b, v_w_ffn_in, v_w_ffn_out):
    args = dict(locals())
    w = {n: args[n] for n in WEIGHTS}
    m = {n: args["m_" + n] for n in WEIGHTS}
    v = {n: args["v_" + n] for n in WEIGHTS}
    L = w_in.shape[0]
    half = L // 2
    core = lax.axis_index("c").astype(jnp.int32)
    place = (2 * lax.axis_index("x") + lax.axis_index("y")).astype(jnp.int32)
    zero = jnp.zeros((), jnp.int32)
    where, by_core, by_place = jnp.stack([core, place]), jnp.stack([core, zero]), jnp.stack([zero, place])

    def padded(n):
        cols = w[n].shape[-1]
        extra = _padded_width(cols) - cols if n in COL_SHARDED else 0
        return jnp.pad(w[n], ((0, 0), (0, 0), (0, extra))) if extra else w[n]

    gathered = gather_shards([cast_place(padded(n), by_place) for n in REDUCED])
    gathered = {n: g if n in COL_SHARDED else g.reshape(L, 1, -1, g.shape[-1]) for n, g in zip(REDUCED, gathered)}
    first = [w[n] for n in GATE] + [c[0]]
    first_blocks = gather_blocks(_pack(first))
    per_dev = [_unpack(first_blocks[k], first) for k in range(N_DEV)]
    small = {n: w[n] for n in REPLICATED}
    small.update({n: jnp.concatenate([per_dev[2 * j][i] for j in range(N_CHIPS)], -1) for i, n in enumerate(GATE)})

    vectors = jnp.stack([d[-1] for d in per_dev] + [w["c_ctx"]] * N_DEV)
    left = rowwise("silu_cond", _f_silu, [vectors])[0]
    ada_w = w["w_ada"].astype(BF16)[:, None]
    mine = [jnp.stack([_mm_nn("ada_proj", left, ada_w, l, True) for l in range(L)])]
    ada_blocks = gather_blocks(_pack(mine))
    full = jnp.concatenate([_unpack(ada_blocks[2 * j], mine)[0] for j in range(N_CHIPS)], -1)
    me = 2 * place + core
    cond = jnp.stack([lax.dynamic_index_in_dim(full, me, axis=1, keepdims=False), full[:, N_DEV]], 1)

    g_sh = {}

    def loss_fn(x1, small, cond):
        return local_loss(x1, ctx[0], loss_target[0], small, cond, gathered, g_sh)

    loss, (gx, g_small, g_cond) = jax.value_and_grad(loss_fn, argnums=(0, 1, 2))(x[0], small, cond)
    loss = lax.psum(loss, ("x", "y", "c"))

    parts = [g_sh[(n, l)].reshape((N_CHIPS, -1, g_sh[(n, l)].shape[-1])) for n in REDUCED for l in range(L)]
    theirs = swap_halves(parts)
    chip_sums, tiles = [], []
    for p, t in zip(parts, theirs):
        hr, cols = t.shape[1:]
        tr = _stream_rows(hr, cols)
        tiles.append(tr)
        blk = (None, tr, cols)
        chip_sums.append(sum_parts(
            "sum_cores", [(p, blk, functools.partial(lambda nb, s, r, wh: (s, wh[0] * nb + r, 0), hr // tr)),
                          (t, blk, lambda s, r, wh: (s, r, 0))],
            (t.shape, BF16, blk, lambda s, r, wh: (s, r, 0)), (N_CHIPS, hr // tr), by_core))
    through = scatter_diagonal(chip_sums)
    via = []
    for s, t in zip(chip_sums, through):
        h2, cols = t.shape[1:]
        tr = _stream_rows(h2, cols)
        blk = (None, tr, cols)

        def shard_of_neighbour(nb, t, r, wh):
            chip = jnp.where(t == 0, (wh[1] + 2) % N_CHIPS, wh[1] + 1 - 2 * (wh[1] % 2))
            return (chip, (1 - t) * nb + r, 0)

        via.append(sum_parts("sum_through", [(s, blk, functools.partial(shard_of_neighbour, h2 // tr)),
                                             (t, blk, lambda t, r, wh: (1 - t, r, 0))],
                             (t.shape, BF16, blk, lambda t, r, wh: (t, r, 0)), (2, h2 // tr), by_place))
    recv = scatter_shards(chip_sums, via)
    finals = []
    for i, n in enumerate(REDUCED):
        rows, cols = parts[i * L].shape[1:]
        buf = None
        for l in range(L):
            k = i * L + l
            tr = tiles[k]
            nb = rows // 2 // tr
            blk = (None, tr, cols)
            terms = [(parts[k], blk, functools.partial(lambda nb, r, wh: (wh[1], wh[0] * nb + r, 0), nb)),
                     (theirs[k], blk, lambda r, wh: (wh[1], r, 0))]
            terms += [(recv[k], blk, functools.partial(lambda j, r, wh: (j, r, 0), j)) for j in range(2)]
            buf = sum_parts("sum_chips", terms,
                            ((L, rows, cols), F32, blk, functools.partial(lambda l, nb, r, wh: (l, wh[0] * nb + r, 0), l, nb)),
                            (nb,), where, into=buf)
        finals.append(buf)
    grads = {n: g[..., :w[n].shape[-1]].reshape(w[n].shape) for n, g in zip(REDUCED, join_halves(finals))}

    row_blocks = gather_blocks(_pack([g_cond]))
    rows = [_unpack(row_blocks[k], [g_cond])[0] for k in range(N_DEV)]
    shard = w["w_ada"].shape[-1]
    ada, rights = [], []
    for l in range(L):
        right = jnp.stack([r[l, 0] for r in rows] + [r[l, 1] for r in rows])
        rights.append(lax.dynamic_slice_in_dim(right, place * shard, shard, axis=1))
        ada.append(_mm_tn("ada_dw", left, rights[l], 1, True, F32))
    grads["w_ada"] = jnp.concatenate(ada, 0)
    back = _mm_nt("ada_dx", jnp.concatenate(rights, 1), ada_w.reshape((1, L) + ada_w.shape[2:]), 0, True)
    share = [back[N_DEV:]]
    share_blocks = gather_blocks(_pack(share))
    shares = jnp.concatenate([_unpack(share_blocks[2 * j], share)[0] for j in range(N_CHIPS)], 0)
    grad_c_ctx = silu_vjp_rows("ctx_vector_grad", shares, w["c_ctx"][None, :])[0]

    small_parts = [g_small[n] for n in SMALL]
    blocks = gather_blocks(_pack(small_parts))
    prow = (None, PACK_ROWS, LANE)
    total = sum_parts("sum_devices",
                      [(blocks, prow, functools.partial(lambda k, r, wh: (k, r, 0), k)) for k in range(8)],
                      (blocks.shape[1:], F32, prow[1:], lambda r, wh: (r, 0)), (blocks.shape[1] // PACK_ROWS,),
                      jnp.zeros((2,), jnp.int32))
    grads.update(dict(zip(SMALL, _unpack(total, small_parts))))
    grads["c_ctx"] = grad_c_ctx
    for n in GATE:
        width = w[n].shape[-1]
        grads[n] = lax.dynamic_slice_in_dim(grads[n], place * width, width, axis=-1)

    delta, new_m, new_v = {}, {}, {}
    for n in SHARDED:
        d3, m3, v3 = adamw("adamw", _view3(w[n], 1), _view3(grads[n], 1), _view3(m[n], 1), _view3(v[n], 1))
        delta[n], new_m[n], new_v[n] = (t.reshape(w[n].shape) for t in (d3, m3, v3))
    packed = [_pack([d[n] for n in SMALL])[None] for d in (w, grads, m, v)]
    d3, m3, v3 = adamw("adamw_small", *packed)
    like = [w[n] for n in SMALL]
    for dst, src in ((delta, d3), (new_m, m3), (new_v, v3)):
        dst.update(dict(zip(SMALL, _unpack(src[0], like))))

    return (loss, gx[None], *[grads[n] for n in WEIGHTS], *[delta[n] for n in WEIGHTS],
            *[new_m[n] for n in WEIGHTS], *[new_v[n] for n in WEIGHTS])
```

```python
import functools

import jax
import jax.numpy as jnp
from jax import lax
from jax.experimental import pallas as pl
from jax.experimental.pallas import tpu as pltpu

F32 = jnp.float32
BF16 = jnp.bfloat16

GRID_W = 64
GLA_HEADS = 4
GLA_DK = 128
GLA_DV = 256
GLA_GATE_RANK = 16
GLA_GATE_TAU = 16.0
GLA_CHUNK = 64
S5_GROUP = 16
S5_STATE = 64
ATTN_Q_HEADS = 8
ATTN_KV_HEADS = 2
ATTN_HEAD_DIM = 128
ROPE_THETA = 10000.0
DEPTH = 4
DN_ALPHA = (2 * DEPTH) ** 0.25
EPS = 1e-6
ADAM_LR = 0.001
ADAM_B1 = 0.9
ADAM_B2 = 0.999
ADAM_EPS = 1e-08
ADAM_WD = 0.01
ADAM_STEP = 10

LANE = 128
SUBLANE = 8
VMEM_LIMIT = 56 * 1024 * 1024
ADA_ROWS = 16
S5_CHUNK = 128
S5_BLOCK_GROUPS = 8
ROW_TILE = 256
ELEMENTWISE_ROWS = 512
COL_TILE = 512
PACK_ROWS = 512
SHARD_PAD = 1024
N_CHIPS = 4


def _params(sem, **kw):
    return pltpu.CompilerParams(dimension_semantics=sem, vmem_limit_bytes=VMEM_LIMIT, **kw)


def _tile(n, target, base):
    if n <= target:
        return n
    best = None
    for t in range(base, target + 1, base):
        if n % t == 0:
            best = t
    assert best is not None, (n, target, base)
    return best


_DIMS = {"nn": ((1,), (0,)), "nt": ((1,), (1,)), "tn": ((0,), (0,))}


def _dg(a, b, mode):
    return lax.dot_general(a.astype(BF16), b.astype(BF16), (_DIMS[mode], ((), ())), preferred_element_type=F32)


@functools.partial(jax.custom_vjp, nondiff_argnums=(2,))
def bdot(a, b, mode):
    return _dg(a, b, mode)


def _bdot_fwd(a, b, mode):
    return _dg(a, b, mode), (a, b)


def _bdot_bwd(mode, res, g):
    a, b = res
    if mode == "nn":
        return bdot(g, b, "nt").astype(a.dtype), bdot(a, g, "tn").astype(b.dtype)
    if mode == "nt":
        return bdot(g, b, "nn").astype(a.dtype), bdot(g, a, "tn").astype(b.dtype)
    return bdot(b, g, "nt").astype(a.dtype), bdot(a, g, "nn").astype(b.dtype)


bdot.defvjp(_bdot_fwd, _bdot_bwd)


MM_TILES = {"nn": (1152, 1024, 1664), "nt": (1152, 1024, 1664), "tn": (768, 2048, 1408)}


def _mm_tiles(mode, M, Kb, Nb):
    tm, tk, tn = MM_TILES[mode]
    return _tile(M, tm, SUBLANE), _tile(Kb, tk, LANE), _tile(Nb, tn, LANE)


def _mm_body(mode, reduce_axes):
    def body(p_ref, q_ref, o_ref, acc):
        first = functools.reduce(jnp.logical_and, [pl.program_id(ax) == 0 for ax, _ in reduce_axes])
        last = functools.reduce(jnp.logical_and, [pl.program_id(ax) == n - 1 for ax, n in reduce_axes])

        @pl.when(first)
        def _():
            acc[...] = jnp.zeros_like(acc)

        acc[...] += _dg(p_ref[...], q_ref[...], mode)

        @pl.when(last)
        def _():
            o_ref[...] = acc[...].astype(o_ref.dtype)

    return body


def _mm_nn(name, a, w, l, share):
    M = a.shape[0]
    _, B, Kb, Nb = w.shape
    tm, tk, tn = _mm_tiles("nn", M, Kb, Nb)
    nk, nn = Kb // tk, Nb // tn
    a_map = (lambda b, i, j, k: (i, k)) if share else (lambda b, i, j, k: (i, b * nk + k))
    return pl.pallas_call(
        _mm_body("nn", [(3, nk)]), name=name, grid=(B, M // tm, nn, nk),
        in_specs=[pl.BlockSpec((tm, tk), a_map),
                  pl.BlockSpec((None, None, tk, tn), lambda b, i, j, k: (l, b, k, j))],
        out_specs=pl.BlockSpec((tm, tn), lambda b, i, j, k: (i, b * nn + j)),
        out_shape=jax.ShapeDtypeStruct((M, B * Nb), F32),
        scratch_shapes=[pltpu.VMEM((tm, tn), F32)],
        compiler_params=_params(("arbitrary",) * 4))(a, w)


def _mm_nt(name, g, w, l, share):
    M = g.shape[0]
    _, B, Kb, Nb = w.shape
    tm, tk, tn = _mm_tiles("nt", M, Kb, Nb)
    nk, nn = Kb // tk, Nb // tn
    if share:
        grid, red = (M // tm, nk, B, nn), [(2, B), (3, nn)]
        g_map, w_map = (lambda i, k, b, n: (i, b * nn + n)), (lambda i, k, b, n: (l, b, k, n))
        o_map, width = (lambda i, k, b, n: (i, k)), Kb
    else:
        grid, red = (B, M // tm, nk, nn), [(3, nn)]
        g_map, w_map = (lambda b, i, k, n: (i, b * nn + n)), (lambda b, i, k, n: (l, b, k, n))
        o_map, width = (lambda b, i, k, n: (i, b * nk + k)), B * Kb
    return pl.pallas_call(
        _mm_body("nt", red), name=name, grid=grid,
        in_specs=[pl.BlockSpec((tm, tn), g_map), pl.BlockSpec((None, None, tk, tn), w_map)],
        out_specs=pl.BlockSpec((tm, tk), o_map),
        out_shape=jax.ShapeDtypeStruct((M, width), F32),
        scratch_shapes=[pltpu.VMEM((tm, tk), F32)],
        compiler_params=_params(("arbitrary",) * 4))(g, w)


def _mm_tn(name, a, g, B, share, dtype):
    M = a.shape[0]
    Kb, Nb = a.shape[1] // (1 if share else B), g.shape[1] // B
    tm, tk, tn = _mm_tiles("tn", M, Kb, Nb)
    nk, nn = Kb // tk, Nb // tn
    a_map = (lambda b, k, j, m: (m, k)) if share else (lambda b, k, j, m: (m, b * nk + k))
    return pl.pallas_call(
        _mm_body("tn", [(3, M // tm)]), name=name, grid=(B, nk, nn, M // tm),
        in_specs=[pl.BlockSpec((tm, tk), a_map),
                  pl.BlockSpec((tm, tn), lambda b, k, j, m: (m, b * nn + j))],
        out_specs=pl.BlockSpec((None, tk, tn), lambda b, k, j, m: (b, k, j)),
        out_shape=jax.ShapeDtypeStruct((B, Kb, Nb), dtype),
        scratch_shapes=[pltpu.VMEM((tk, tn), F32)],
        compiler_params=_params(("arbitrary",) * 4))(a, g)


def mm(name, a, w):
    w3 = w if w.ndim == 3 else w[None]

    @jax.custom_vjp
    def op(a, w3):
        return _mm_nn(name + "_fwd", a, w3.astype(BF16)[None], 0, False)

    def fwd(a, w3):
        wb = w3.astype(BF16)[None]
        return _mm_nn(name + "_fwd", a, wb, 0, False), (a, wb)

    def bwd(res, g):
        a, wb = res
        return _mm_nt(name + "_dx", g, wb, 0, False), _mm_tn(name + "_dw", a, g, wb.shape[1], False, F32)

    op.defvjp(fwd, bwd)
    return op(a, w3)


def mm_gathered(name, a, w, l, grads, key):
    @jax.custom_vjp
    def op(a):
        return _mm_nn(name + "_fwd", a, w, l, True)

    def fwd(a):
        return _mm_nn(name + "_fwd", a, w, l, True), (a,)

    def bwd(res, g):
        (a,) = res
        grads[key] = _mm_tn(name + "_dw", a, g, w.shape[1], True, BF16)
        return (_mm_nt(name + "_dx", g, w, l, True),)

    op.defvjp(fwd, bwd)
    return op(a)


def _spec_shape(spec, G, T):
    k = spec[0]
    if k == "row":
        return (T, spec[1])
    if k == "rowg":
        return (T, (G // spec[2]) * spec[1])
    if k == "bc":
        return (spec[1], spec[2])
    return (spec[1], (G // spec[3]) * spec[2])


def _spec_block(spec, tm, rmap):
    k = spec[0]
    if k == "row":
        return pl.BlockSpec((tm, spec[1]), lambda g, r: (rmap(r), 0))
    if k == "rowg":
        d = spec[2]
        return pl.BlockSpec((tm, spec[1]), lambda g, r: (rmap(r), g // d))
    if k == "bc":
        return pl.BlockSpec((spec[1], spec[2]), lambda g, r: (0, 0))
    d = spec[3]
    return pl.BlockSpec((spec[1], spec[2]), lambda g, r: (0, g // d))


def block_op(name, f, in_specs, out_specs, G, T, tm, diff, carry=(), row0=False, order=None, f_saved=None):
    n_in, n_out, n_c = len(in_specs), len(out_specs), len(carry)
    n_sv = n_out if f_saved is not None else 0
    n_steps = T // tm
    assert T % tm == 0
    order = order or (lambda s: s)
    diff_idx = [i for i in range(n_in) if diff[i]]
    for i in diff_idx:
        assert in_specs[i][0] != "row" or G == 1
        assert in_specs[i][0] != "rowg" or in_specs[i][2] == 1
    out_shapes = [jax.ShapeDtypeStruct(_spec_shape(s, G, T), F32) for s in out_specs]
    save_shapes = [jax.ShapeDtypeStruct((n_steps, a, G * b), F32) for a, b in carry]
    sem = ("arbitrary", "arbitrary")

    def call_f(r_idx, cvals, vals):
        args = list(vals)
        if n_c:
            args = [tuple(cvals)] + args
        if row0:
            args = [r_idx * tm] + args
        return f(*args)

    def fwd_body(*refs):
        in_refs = refs[:n_in]
        out_refs = refs[n_in:n_in + n_out]
        save_refs = refs[n_in + n_out:n_in + n_out + n_c]
        c_refs = refs[n_in + n_out + n_c:]
        r = pl.program_id(1)
        if n_c:
            @pl.when(r == 0)
            def _():
                for c in c_refs:
                    c[...] = jnp.zeros_like(c)

            cvals = [c[...] for c in c_refs]
            for s, v in zip(save_refs, cvals):
                s[...] = v
            new_c, outs = call_f(r, cvals, [x[...] for x in in_refs])
            for c, v in zip(c_refs, new_c):
                c[...] = v
        else:
            outs = call_f(r, (), [x[...] for x in in_refs])
        for o, v in zip(out_refs, outs):
            o[...] = v.astype(F32)

    def fwd_call(*arrays):
        res = pl.pallas_call(
            fwd_body, name=name + "_fwd", grid=(G, n_steps),
            in_specs=[_spec_block(s, tm, order) for s in in_specs],
            out_specs=[_spec_block(s, tm, order) for s in out_specs]
            + [pl.BlockSpec((None, a, b), lambda g, r: (r, 0, g)) for a, b in carry],
            out_shape=out_shapes + save_shapes,
            scratch_shapes=[pltpu.VMEM((a, b), F32) for a, b in carry],
            compiler_params=_params(sem))(*arrays)
        return tuple(res)

    def bwd_body(*refs):
        in_refs = refs[:n_in]
        save_refs = refs[n_in:n_in + n_c]
        ct_refs = refs[n_in + n_c:n_in + n_c + n_out]
        at = n_in + n_c + n_out
        sv_refs = refs[at:at + n_sv]
        g_refs = refs[at + n_sv:at + n_sv + len(diff_idx)]
        dc_refs = refs[at + n_sv + len(diff_idx):]
        g = pl.program_id(0)
        r = pl.program_id(1)
        vals = [x[...] for x in in_refs]
        if n_c:
            @pl.when(r == 0)
            def _():
                for d in dc_refs:
                    d[...] = jnp.zeros_like(d)

        def fun(cvals, dvals):
            full = list(vals)
            for i, v in zip(diff_idx, dvals):
                full[i] = v
            if n_sv:
                return f_saved(tuple(cvals), *full, *[s[...] for s in sv_refs])
            return call_f(n_steps - 1 - r if n_c else r, cvals, full)

        _, vjp = jax.vjp(fun, tuple(s[...] for s in save_refs), tuple(vals[i] for i in diff_idx))
        cts = tuple(c[...] for c in ct_refs)
        if n_c:
            cts = (tuple(d[...] for d in dc_refs), cts)
        dcin, dvals = vjp(cts)
        for d, v in zip(dc_refs, dcin):
            d[...] = v
        for gref, i, v in zip(g_refs, diff_idx, dvals):
            spec = in_specs[i]
            if spec[0] in ("row", "rowg"):
                gref[...] = v
            else:
                first = (r == 0) & ((g == 0) if spec[0] == "bc" else (g % spec[3] == 0))

                @pl.when(first)
                def _(gref=gref, v=v):
                    gref[...] = v

                @pl.when(jnp.logical_not(first))
                def _(gref=gref, v=v):
                    gref[...] += v

    def bwd_call(arrays, saved, cts, outs):
        rmap = (lambda r: order(n_steps - 1 - r)) if n_c else (lambda r: r)
        res = pl.pallas_call(
            bwd_body, name=name + "_bwd", grid=(G, n_steps),
            in_specs=[_spec_block(s, tm, rmap) for s in in_specs]
            + [pl.BlockSpec((None, a, b), lambda g, r: (n_steps - 1 - r, 0, g)) for a, b in carry]
            + [_spec_block(s, tm, rmap) for s in out_specs] * (2 if n_sv else 1),
            out_specs=[_spec_block(in_specs[i], tm, rmap) for i in diff_idx],
            out_shape=[jax.ShapeDtypeStruct(_spec_shape(in_specs[i], G, T), F32) for i in diff_idx],
            scratch_shapes=[pltpu.VMEM((a, b), F32) for a, b in carry],
            compiler_params=_params(sem))(*arrays, *saved, *cts, *outs)
        return tuple(res)

    @jax.custom_vjp
    def op(*arrays):
        return fwd_call(*arrays)[:n_out]

    def op_fwd(*arrays):
        res = fwd_call(*arrays)
        return res[:n_out], (arrays, res[n_out:], res[:n_out] if n_sv else ())

    def op_bwd(res, cts):
        arrays, saved, outs = res
        grads = bwd_call(arrays, saved, cts, outs)
        out = [jnp.zeros_like(a) for a in arrays]
        for i, gval in zip(diff_idx, grads):
            out[i] = gval
        return tuple(out)

    op.defvjp(op_fwd, op_bwd)
    return op


def _rows(n, m):
    return lax.broadcasted_iota(jnp.int32, (n, m), 0)


def _ctx_select(row0, tm, n_ctx, v_lat, v_ctx):
    if n_ctx == 0:
        return v_lat
    is_ctx = (row0 + _rows(tm, 1)) < n_ctx
    return jnp.where(is_ctx, v_ctx, v_lat)


def _silu(x):
    return x * jax.nn.sigmoid(x)


def _f_modulate(tm, n_ctx):
    def f(row0, x, sh_l, sh_c, sc_l, sc_c):
        sh = _ctx_select(row0, tm, n_ctx, sh_l, sh_c)
        sc = _ctx_select(row0, tm, n_ctx, sc_l, sc_c)
        return (x * (1 + sc) + sh,)
    return f


def _f_postnorm(tm, n_ctx):
    def f(row0, x, y, g_l, g_c, w, b):
        z = DN_ALPHA * x + _ctx_select(row0, tm, n_ctx, g_l, g_c) * y
        mu = jnp.mean(z, -1, keepdims=True)
        zc = z - mu
        var = jnp.mean(zc * zc, -1, keepdims=True)
        return (zc * lax.rsqrt(var + EPS) * w + b,)
    return f


def _log_sigmoid(x):
    return -(jnp.maximum(-x, 0.0) + jnp.log1p(jnp.exp(-jnp.abs(x))))


def _f_gla_prep(glr, wg0, wg1, b0, b1):
    return (_log_sigmoid(bdot(glr, wg0, "nn") + b0) / GLA_GATE_TAU,
            _log_sigmoid(bdot(glr, wg1, "nn") + b1) / GLA_GATE_TAU)


def _f_gla_step(rev):
    def f(carry, q, k, v, la):
        (st,) = carry
        n = q.shape[0]
        cols = lax.broadcasted_iota(jnp.int32, (n, n), 1)
        tri = (_rows(n, n) <= cols) if rev else (_rows(n, n) >= cols)
        b = jnp.dot(tri.astype(F32), la, precision=lax.Precision.HIGHEST)
        qe = q * (GLA_DK ** -0.5) * jnp.exp(b)
        ke = k * jnp.exp(-b)
        att = jnp.where(tri, bdot(qe, ke, "nt"), 0.0)
        o = bdot(att, v, "nn") + bdot(qe, st, "nt")
        end = 0 if rev else n - 1
        b_last = jnp.sum(jnp.where(_rows(n, 1) == end, b, 0.0), axis=0, keepdims=True)
        kd = k * jnp.exp(b_last - b)
        st = st * jnp.exp(b_last) + bdot(v, kd, "tn")
        return (st,), (o,)
    return f


def _f_gla_norm(o0, o1, gr, w):
    o = o0 + o1
    mu = jnp.mean(o, -1, keepdims=True)
    oc = o - mu
    var = jnp.mean(oc * oc, -1, keepdims=True)
    return (oc * lax.rsqrt(var + EPS) * w * _silu(gr),)


@functools.partial(jax.custom_vjp, nondiff_argnums=(1, 2))
def _shift_rows(x, d, up):
    n = x.shape[0]
    rows = _rows(n, 1)
    if up:
        return jnp.where(rows < n - d, pltpu.roll(x, n - d, 0), 0.0)
    return jnp.where(rows >= d, pltpu.roll(x, d, 0), 0.0)


def _shift_fwd(x, d, up):
    return _shift_rows(x, d, up), None


def _shift_bwd(d, up, _, g):
    return (_shift_rows(g, d, not up),)


_shift_rows.defvjp(_shift_fwd, _shift_bwd)


def _scan_doubling(ur, ui, ar, ai, rev):
    n = ur.shape[0]
    xr, xi, pr, pi = ur, ui, ar, ai
    d = 1
    while d < n:
        sr, si = _shift_rows(xr, d, rev), _shift_rows(xi, d, rev)
        xr, xi = xr + pr * sr - pi * si, xi + pr * si + pi * sr
        pr, pi = pr * pr - pi * pi, 2 * pr * pi
        d *= 2
    return xr, xi


@functools.partial(jax.custom_vjp, nondiff_argnums=(6,))
def _scan_known(ur, ui, ar, ai, xr, xi, rev):
    return xr, xi


def _scan_known_fwd(ur, ui, ar, ai, xr, xi, rev):
    return (xr, xi), (ar, ai, xr, xi)


def _scan_known_bwd(rev, res, g):
    ar, ai, xr, xi = res
    lr, li = _scan_doubling(g[0], g[1], ar, -ai, not rev)
    pr, pi = _shift_rows(xr, 1, rev), _shift_rows(xi, 1, rev)
    dar = jnp.sum(lr * pr + li * pi, axis=0, keepdims=True)
    dai = jnp.sum(li * pr - lr * pi, axis=0, keepdims=True)
    return lr, li, dar, dai, jnp.zeros_like(xr), jnp.zeros_like(xi)


_scan_known.defvjp(_scan_known_fwd, _scan_known_bwd)


def _f_s5_step(rev, known=False):
    def f(carry, bur, bui, lam_re, lam_im, log_dt, *states):
        cr, ci = carry
        n = bur.shape[0]
        dt = jnp.exp(log_dt)
        mag = jnp.exp(lam_re * dt)
        ar, ai = mag * jnp.cos(lam_im * dt), mag * jnp.sin(lam_im * dt)
        den = lam_re * lam_re + lam_im * lam_im
        nr, ni = ar - 1, ai
        kr = (nr * lam_re + ni * lam_im) / den
        ki = (ni * lam_re - nr * lam_im) / den
        first = _rows(n, 1) == (n - 1 if rev else 0)
        ur = kr * bur - ki * bui + jnp.where(first, ar * cr - ai * ci, 0.0)
        ui = kr * bui + ki * bur + jnp.where(first, ar * ci + ai * cr, 0.0)
        if known:
            xr, xi = _scan_known(ur, ui, ar, ai, states[0], states[1], rev)
        else:
            xr, xi = _scan_doubling(ur, ui, ar, ai, rev)
        last = _rows(n, 1) == (0 if rev else n - 1)
        cr = jnp.sum(jnp.where(last, xr, 0.0), axis=0, keepdims=True)
        ci = jnp.sum(jnp.where(last, xi, 0.0), axis=0, keepdims=True)
        return (cr, ci), (xr, xi)
    return f


def _f_s5_post(su, dskip, y0r, y0i, y1r, y1i):
    return (jax.nn.gelu(su * dskip + y0r - y0i + y1r - y1i),)


def _f_s5_glu(y, t):
    return (y * jax.nn.sigmoid(t),)


def _swap_pairs(x):
    lane = lax.broadcasted_iota(jnp.int32, x.shape, 1)
    return jnp.where(lane % 2 == 0, pltpu.roll(x, x.shape[1] - 1, 1), pltpu.roll(x, 1, 1))


@jax.custom_vjp
def _rope(x, cos2, sin2):
    return x * cos2 + _swap_pairs(x) * sin2


def _rope_fwd(x, cos2, sin2):
    return _rope(x, cos2, sin2), (cos2, sin2)


def _rope_bwd(res, g):
    cos2, sin2 = res
    return g * cos2 + _swap_pairs(g * sin2), jnp.zeros_like(cos2), jnp.zeros_like(sin2)


_rope.defvjp(_rope_fwd, _rope_bwd)


def _f_qk_norm_rope(x, cos2, sin2, w):
    xn = x * lax.rsqrt(jnp.mean(x * x, -1, keepdims=True) + EPS) * w
    return (_rope(xn, cos2, sin2),)


def _f_attn(q, k, v):
    s = bdot(q, k, "nt") * (ATTN_HEAD_DIM ** -0.5)
    e = jnp.exp(s - jnp.max(s, -1, keepdims=True))
    p = e / jnp.sum(e, -1, keepdims=True)
    return (bdot(p, v, "nn"),)


def _f_merge(ga, gb, gc, pa, pb, pc):
    return (jax.nn.sigmoid(ga) * pa + jax.nn.sigmoid(gb) * pb + jax.nn.sigmoid(gc) * pc,)


def _f_swiglu(a, b):
    return (_silu(a) * b,)


def _f_silu(x):
    return (_silu(x),)


def _f_add_bias(x, b):
    return (x + b,)


def _scan_order(rev, n_ctx, T, tm):
    if not rev:
        return None
    nc, n = n_ctx // tm, T // tm
    return lambda s: jnp.where(s < nc, nc - 1 - s, n - 1 - (s - nc))


def modulate(name, x, sh, sc, n_ctx):
    T, D = x.shape
    tm = _tile(T, ELEMENTWISE_ROWS, SUBLANE)
    cw = _tile(D, COL_TILE, LANE)
    col, vec = ("rowg", cw, 1), ("bcg", 1, cw, 1)
    op = block_op(name, _f_modulate(tm, n_ctx), [col, vec, vec, vec, vec], [col], D // cw, T, tm,
                  [True] * 5, row0=True)
    return op(x, sh[0], sh[1], sc[0], sc[1])[0]


def postnorm(name, x, y, g, w, b, n_ctx):
    T, D = x.shape
    tm = _tile(T, ROW_TILE, SUBLANE)
    vec = ("bc", 1, D)
    op = block_op(name, _f_postnorm(tm, n_ctx), [("row", D), ("row", D), vec, vec, vec, vec], [("row", D)], 1, T,
                  tm, [True] * 6, row0=True)
    return op(x, y, g[0], g[1], w, b)[0]


def rowwise(name, f, arrays, n_out=1):
    T, w = arrays[0].shape
    tm = _tile(T, ELEMENTWISE_ROWS, SUBLANE)
    cw = _tile(w, COL_TILE, LANE)
    col = ("rowg", cw, 1)
    op = block_op(name, f, [col] * len(arrays), [col] * n_out, w // cw, T, tm, [True] * len(arrays))
    return op(*arrays)


def gla_prep(name, glr, wg, bg):
    T = glr.shape[0]
    qk = wg.shape[-1]
    tm = _tile(T, ROW_TILE, SUBLANE)
    op = block_op(name, _f_gla_prep, [("row", LANE), ("bc", LANE, qk), ("bc", LANE, qk), ("bc", 1, qk), ("bc", 1, qk)],
                  [("row", qk), ("row", qk)], 1, T, tm, [True] * 5)
    return op(glr, wg[0], wg[1], bg[0], bg[1])


def gla_scan(name, q, k, v, la, rev, n_ctx):
    T = q.shape[0]
    op = block_op(name, _f_gla_step(rev), [("rowg", GLA_DK, 1), ("rowg", GLA_DK, 1), ("rowg", GLA_DV, 1), ("rowg", GLA_DK, 1)],
                  [("rowg", GLA_DV, 1)], GLA_HEADS, T, GLA_CHUNK, [True] * 4, carry=[(GLA_DV, GLA_DK)],
                  order=_scan_order(rev, n_ctx, T, GLA_CHUNK))
    return op(q, k, v, la)[0]


def gla_norm(name, o0, o1, gr, w):
    T = o0.shape[0]
    tm = _tile(T, ROW_TILE, SUBLANE)
    hd = ("rowg", GLA_DV, 1)
    op = block_op(name, _f_gla_norm, [hd, hd, hd, ("bcg", 1, GLA_DV, 1)], [hd], GLA_HEADS, T, tm, [True] * 4)
    return op(o0, o1, gr, w)[0]


def s5_scan(name, bur, bui, lam_re, lam_im, log_dt, rev, n_ctx):
    T, S = bur.shape
    cols = _tile(S, 768, LANE)
    G = S // cols
    col, par = ("rowg", cols, 1), ("bcg", 1, cols, 1)
    op = block_op(name, _f_s5_step(rev), [col, col, par, par, par], [col, col], G, T, S5_CHUNK, [True] * 5,
                  carry=[(1, cols), (1, cols)], order=_scan_order(rev, n_ctx, T, S5_CHUNK),
                  f_saved=_f_s5_step(rev, True))
    return op(bur, bui, lam_re, lam_im, log_dt)


def qk_norm_rope(name, x, cos2, sin2, w):
    T = x.shape[0]
    G = x.shape[1] // ATTN_HEAD_DIM
    tm = _tile(T, ROW_TILE, SUBLANE)
    hd = ("rowg", ATTN_HEAD_DIM, 1)
    op = block_op(name, _f_qk_norm_rope, [hd, ("row", ATTN_HEAD_DIM), ("row", ATTN_HEAD_DIM), ("bc", 1, ATTN_HEAD_DIM)],
                  [hd], G, T, tm, [True, False, False, True])
    return op(x, cos2, sin2, w)[0]


def attention(name, q, k, v):
    T, Tk = q.shape[0], k.shape[0]
    tm = _tile(T, ROW_TILE, SUBLANE)
    grp = ATTN_Q_HEADS // ATTN_KV_HEADS
    kv = ("bcg", Tk, ATTN_HEAD_DIM, grp)
    op = block_op(name, _f_attn, [("rowg", ATTN_HEAD_DIM, 1), kv, kv], [("rowg", ATTN_HEAD_DIM, 1)], ATTN_Q_HEADS, T,
                  tm, [True] * 3)
    return op(q, k, v)[0]


def sq_loss(name, y, t):
    T, D = y.shape
    tm = _tile(T, ROW_TILE, SUBLANE)

    def fwd_body(y_ref, t_ref, o_ref):
        e = y_ref[...] - t_ref[...]
        part = jnp.sum(jnp.sum(e * e, -1, keepdims=True), 0, keepdims=True) * (0.5 / D)

        @pl.when(pl.program_id(0) == 0)
        def _():
            o_ref[...] = jnp.zeros_like(o_ref)

        o_ref[...] += part * jnp.ones((1, LANE), F32)

    def bwd_body(y_ref, t_ref, g_ref, o_ref):
        o_ref[...] = (y_ref[...] - t_ref[...]) * (g_ref[:, 0:1] / D)

    row = pl.BlockSpec((tm, D), lambda r: (r, 0))
    one = pl.BlockSpec((1, LANE), lambda r: (0, 0))

    def fwd_call(y, t):
        return pl.pallas_call(fwd_body, name=name + "_fwd", grid=(T // tm,), in_specs=[row, row], out_specs=one,
                              out_shape=jax.ShapeDtypeStruct((1, LANE), F32), compiler_params=_params(("arbitrary",)))(y, t)

    @jax.custom_vjp
    def op(y, t):
        return fwd_call(y, t)[0, 0]

    def op_fwd(y, t):
        return fwd_call(y, t)[0, 0], (y, t)

    def op_bwd(res, g):
        y, t = res
        gy = pl.pallas_call(bwd_body, name=name + "_bwd", grid=(T // tm,), in_specs=[row, row, one], out_specs=row,
                            out_shape=jax.ShapeDtypeStruct((T, D), F32), compiler_params=_params(("arbitrary",)))(
                                y, t, jnp.full((1, LANE), g, F32))
        return gy, jnp.zeros_like(t)

    op.defvjp(op_fwd, op_bwd)
    return op(y, t)


def _rope_tables(n_ctx, n_lat):
    n_rows = n_lat // GRID_W
    rows = jnp.repeat(jnp.arange(n_rows), GRID_W).astype(F32)
    cols = jnp.tile(jnp.arange(GRID_W), n_rows).astype(F32)
    n_freq = ATTN_HEAD_DIM // 4
    inv = ROPE_THETA ** (-jnp.arange(n_freq, dtype=F32) / n_freq)
    ang = jnp.concatenate([rows[:, None] * inv, cols[:, None] * inv], -1)
    cos2 = jnp.repeat(jnp.cos(ang), 2, axis=-1)
    sin2 = jnp.stack([-jnp.sin(ang), jnp.sin(ang)], -1).reshape(n_lat, ATTN_HEAD_DIM)
    cos2 = jnp.concatenate([jnp.ones((n_ctx, ATTN_HEAD_DIM), F32), cos2], 0)
    sin2 = jnp.concatenate([jnp.zeros((n_ctx, ATTN_HEAD_DIM), F32), sin2], 0)
    return cos2, sin2


def _in_layout(D, s5_width):
    qk, gv = GLA_HEADS * GLA_DK, GLA_HEADS * GLA_DV
    aq, akv = ATTN_Q_HEADS * ATTN_HEAD_DIM, ATTN_KV_HEADS * ATTN_HEAD_DIM
    widths = [("gq", qk), ("gk", qk), ("gv", gv), ("gr", gv), ("glr", GLA_GATE_RANK), ("su", s5_width), ("aq", aq),
              ("ak", akv), ("av", akv), ("ga", D), ("gb", D), ("gc", D)]
    off, out = 0, {}
    for n, w in widths:
        out[n] = (off, w)
        off += w
    return out, off


def _padded_width(width):
    return width if width % LANE == 0 else -(-width // SHARD_PAD) * SHARD_PAD


def _shard_cols(z, off, width, shard, padded):
    parts = []
    for j in range(N_CHIPS):
        lo, hi = max(off, j * shard), min(off + width, (j + 1) * shard)
        if lo < hi:
            parts.append(z[:, j * padded + lo - j * shard:j * padded + hi - j * shard])
    return parts[0] if len(parts) == 1 else jnp.concatenate(parts, 1)


def _s5_in_blocks(b):
    G, P, C = b.shape
    nb, bg = G // S5_BLOCK_GROUPS, S5_BLOCK_GROUPS
    t = b.reshape(nb, bg, P, C).transpose(0, 1, 3, 2)
    return jnp.einsum("bgcp,gh->bgchp", t, jnp.eye(bg, dtype=F32)).reshape(nb, bg * C, bg * P)


def _s5_out_blocks(c):
    G, C, P = c.shape
    nb, bg = G // S5_BLOCK_GROUPS, S5_BLOCK_GROUPS
    t = c.reshape(nb, bg, C, P).transpose(0, 1, 3, 2)
    return jnp.einsum("bgpc,gh->bgphc", t, jnp.eye(bg, dtype=F32)).reshape(nb, bg * P, bg * C)


def _layer(keep_ctx, xa, n_ctx, mod, p, big, cos2, sin2):
    T, D = xa.shape
    S = p["s5_d"].shape[-1]
    lay, width = _in_layout(D, S)
    lo = 0 if keep_ctx else n_ctx
    ctx_rows = n_ctx if keep_ctx else 0

    h = modulate("modulate1", xa, mod["sh1"], mod["sc1"], n_ctx)
    z = big["w_in"]("in_proj", h)
    shard = width // N_CHIPS
    zz = {n: _shard_cols(z, o, w, shard, z.shape[1] // N_CHIPS) for n, (o, w) in lay.items()}

    wg = jnp.pad(p["w_gla_gate"], ((0, 0), (0, LANE - GLA_GATE_RANK), (0, 0)))
    glr = jnp.pad(zz["glr"], ((0, 0), (0, LANE - GLA_GATE_RANK)))
    la0, la1 = gla_prep("gla_prep", glr, wg, p["b_gla_gate"][:, None, :])
    o0 = gla_scan("gla_scan", zz["gq"], zz["gk"], zz["gv"], la0, False, n_ctx)
    o1 = gla_scan("gla_scan_rev", zz["gq"], zz["gk"], zz["gv"], la1, True, n_ctx)
    o_gla = gla_norm("gla_norm", o0[lo:], o1[lo:], zz["gr"][lo:], p["gla_norm_w"][None, :])

    su = zz["su"]
    bur = mm("s5_in_re", su, _s5_in_blocks(p["s5_b_re"]))
    bui = mm("s5_in_im", su, _s5_in_blocks(p["s5_b_im"]))
    ys = []
    for d in range(2):
        row = lambda t: t.reshape(1, -1)
        ldt = jnp.repeat(p["s5_log_dt"][d], S5_STATE)
        sr, si = s5_scan("s5_scan_rev" if d else "s5_scan", bur, bui, row(p["s5_lam_re"][d]),
                         row(p["s5_lam_im"][d]), row(ldt), d == 1, n_ctx)
        ys.append(mm("s5_out_re", sr[lo:], _s5_out_blocks(p["s5_c_re"][d])))
        ys.append(mm("s5_out_im", si[lo:], _s5_out_blocks(p["s5_c_im"][d])))
    T2 = T - lo
    tm = _tile(T2, ROW_TILE, SUBLANE)
    post = block_op("s5_post", _f_s5_post, [("row", S), ("bc", 1, S)] + [("row", S)] * 4, [("row", S)], 1, T2, tm,
                    [True] * 6)
    yg = post(su[lo:], p["s5_d"][None, :], *ys)[0]
    o_s5 = rowwise("s5_glu", _f_s5_glu, [yg, big["w_s5_glu"]("s5_glu_proj", yg)])[0]

    qn = qk_norm_rope("q_norm_rope", zz["aq"], cos2, sin2, p["q_norm_w"][None, :])
    kn = qk_norm_rope("k_norm_rope", zz["ak"], cos2, sin2, p["k_norm_w"][None, :])
    o_attn = attention("attn_lat", qn[n_ctx:], kn, zz["av"])
    if keep_ctx:
        o_c = attention("attn_ctx", qn[:n_ctx], kn[:n_ctx], zz["av"][:n_ctx])
        o_attn = jnp.concatenate([o_c, o_attn], 0)

    merged = rowwise("merge", _f_merge, [zz["ga"][lo:], zz["gb"][lo:], zz["gc"][lo:],
                                         big["w_proj_gla"]("proj_gla", o_gla),
                                         big["w_proj_s5"]("proj_s5", o_s5),
                                         big["w_proj_attn"]("proj_attn", o_attn)])[0]
    mix = big["w_out"]("out_proj", merged)
    x1 = postnorm("postnorm1", xa[lo:], mix, mod["g1"], p["ln1_w"][None, :], p["ln1_b"][None, :], ctx_rows)
    h2 = modulate("modulate2", x1, mod["sh2"], mod["sc2"], ctx_rows)
    u = big["w_ffn_in"]("ffn_in", h2)
    F = u.shape[1] // 2
    act = rowwise("swiglu", _f_swiglu, [u[:, :F], u[:, F:]])[0]
    f = big["w_ffn_out"]("ffn_out", act)
    return postnorm("postnorm2", x1, f, mod["g2"], p["ln2_w"][None, :], p["ln2_b"][None, :], ctx_rows)


def local_loss(x, ctx, target, small, cond, gathered, grads):
    n_lat, D = x.shape
    n_ctx = ctx.shape[0]
    cos2, sin2 = _rope_tables(n_ctx, n_lat)
    xa = jnp.concatenate([ctx, x], 0)
    depth = gathered["w_in"].shape[0]
    for l in range(depth):
        p = {n: v[l] for n, v in small.items() if n != "c_ctx"}
        big = {n: functools.partial(lambda n, l, name, a: mm_gathered(name, a, gathered[n], l, grads, (n, l)), n, l)
               for n in gathered}
        m = block_op("ada_bias", _f_add_bias, [("row", 6 * D), ("bc", 1, 6 * D)], [("row", 6 * D)], 1, 2, 2,
                     [True, True])(cond[l], p["b_ada"][None, :])[0]
        names = ["sh1", "sc1", "g1", "sh2", "sc2", "g2"]
        mod = {n: (m[0:1, i * D:(i + 1) * D], m[1:2, i * D:(i + 1) * D]) for i, n in enumerate(names)}
        xa = _layer(l < depth - 1, xa, n_ctx, mod, p, big, cos2, sin2)
    return sq_loss("loss", xa, target)


MESH = pl.DeviceIdType.MESH
ANY = pl.BlockSpec(memory_space=pl.ANY)


def _place():
    x, y, c = lax.axis_index("x"), lax.axis_index("y"), lax.axis_index("c")
    chips = [(1 - x, y), (x, 1 - y), (1 - x, 1 - y)]
    return x, y, c, chips


def _rcopy(src, dst, ssem, rsem, to):
    return pltpu.make_async_remote_copy(src_ref=src, dst_ref=dst, send_sem=ssem, recv_sem=rsem, device_id=to,
                                        device_id_type=MESH)


def gather_shards(bufs):
    n = len(bufs)
    L = bufs[0].shape[0]
    half = L // 2

    def body(*refs):
        dst = refs[n:2 * n]
        isend, irecv, fsend, frecv, dsend, drecv = refs[2 * n:]
        x, y, c, _ = _place()
        j = 2 * x + y
        nbr = [(1 - x, y), (x, 1 - y)]
        jn = [2 * kx + ky for kx, ky in nbr]
        jd = 2 * (1 - x) + (1 - y)
        sibling = (x, y, 1 - c)
        mine, other = pl.ds(c * half, half), pl.ds((1 - c) * half, half)

        def piece(i, layers, chip, q):
            hr = bufs[i].shape[2] // 2
            return dst[i].at[layers, chip, pl.ds(q * hr, hr)]

        direct = [_rcopy(dst[i].at[mine, j], dst[i].at[mine, j], isend.at[i, r], irecv.at[i, r], (*nbr[r], c))
                  for i in range(n) for r in range(2)]
        for cp in direct:
            cp.start()
        passed = []
        for i in range(n):
            for r in range(2):
                part = dst[i].at[mine, jn[r]]
                _rcopy(part, part, isend.at[i, r], irecv.at[i, r], (*nbr[r], c)).wait_recv()
                fwd = piece(i, mine, jn[r], 1 - r)
                passed.append(_rcopy(fwd, fwd, fsend.at[i, r], frecv.at[i, 1 - r], (*nbr[1 - r], c)))
                passed.append(_rcopy(part, part, dsend.at[i, r], drecv.at[i, r], sibling))
                passed[-2].start()
                passed[-1].start()
        for i in range(n):
            for q in range(2):
                part = piece(i, mine, jd, q)
                _rcopy(part, part, fsend.at[i, q], frecv.at[i, q], (*nbr[q], c)).wait_recv()
                passed.append(_rcopy(part, part, dsend.at[i, 2 + q], drecv.at[i, 2 + q], sibling))
                passed[-1].start()
        for i in range(n):
            for r in range(2):
                part = dst[i].at[other, jn[r]]
                _rcopy(part, part, dsend.at[i, r], drecv.at[i, r], sibling).wait_recv()
                part = piece(i, other, jd, r)
                _rcopy(part, part, dsend.at[i, 2 + r], drecv.at[i, 2 + r], sibling).wait_recv()
        for cp in direct + passed:
            cp.wait_send()

    out_shape = [jax.ShapeDtypeStruct(b.shape, b.dtype) for b in bufs]
    sems = [pltpu.SemaphoreType.DMA((n, 2))] * 4 + [pltpu.SemaphoreType.DMA((n, 4))] * 2
    return pl.pallas_call(body, name="gather_shards", in_specs=[ANY] * n, out_specs=[ANY] * n, out_shape=out_shape,
                          scratch_shapes=sems, input_output_aliases={i: i for i in range(n)})(*bufs)


def swap_halves(grads):
    n = len(grads)

    def body(*refs):
        src, dst = refs[:n], refs[n:2 * n]
        ssem, rsem = refs[2 * n:]
        x, y, c, _ = _place()
        cps = []
        for i in range(n):
            hr = grads[i].shape[1] // 2
            cps.append(_rcopy(src[i].at[:, pl.ds((1 - c) * hr, hr)], dst[i], ssem.at[i], rsem.at[i], (x, y, 1 - c)))
        for cp in cps:
            cp.start()
        for cp in cps:
            cp.wait()

    out_shape = [jax.ShapeDtypeStruct((g.shape[0], g.shape[1] // 2, g.shape[2]), g.dtype) for g in grads]
    return pl.pallas_call(body, name="swap_halves", in_specs=[ANY] * n, out_specs=[ANY] * n, out_shape=out_shape,
                          scratch_shapes=[pltpu.SemaphoreType.DMA((n,))] * 2)(*grads)


def scatter_diagonal(parts):
    n = len(parts)

    def body(*refs):
        src, dst = refs[:n], refs[n:2 * n]
        ssem, rsem = refs[2 * n:]
        x, y, c, _ = _place()
        nbr = [(1 - x, y), (x, 1 - y)]
        jd = 2 * (1 - x) + (1 - y)
        cps = []
        for i in range(n):
            h2 = parts[i].shape[1] // 2
            for q in range(2):
                cps.append(_rcopy(src[i].at[jd, pl.ds(q * h2, h2)], dst[i].at[q], ssem.at[i, q], rsem.at[i, q],
                                  (*nbr[q], c)))
        for cp in cps:
            cp.start()
        for cp in cps:
            cp.wait()

    out_shape = [jax.ShapeDtypeStruct((2, p.shape[1] // 2, p.shape[2]), p.dtype) for p in parts]
    return pl.pallas_call(body, name="scatter_diagonal", in_specs=[ANY] * n, out_specs=[ANY] * n, out_shape=out_shape,
                          scratch_shapes=[pltpu.SemaphoreType.DMA((n, 2))] * 2)(*parts)


def scatter_shards(parts, via):
    n = len(parts)

    def body(*refs):
        src, add, dst = refs[:n], refs[n:2 * n], refs[2 * n:3 * n]
        ssem, rsem = refs[3 * n:]
        x, y, c, _ = _place()
        nbr = [(1 - x, y), (x, 1 - y)]
        cps = []
        for i in range(n):
            h2 = parts[i].shape[1] // 2
            for t, (kx, ky) in enumerate(nbr):
                plain = pl.ds(t * h2, h2)
                cps.append(_rcopy(src[i].at[2 * kx + ky, plain], dst[i].at[t, plain], ssem.at[i, 2 * t],
                                  rsem.at[i, 2 * t], (kx, ky, c)))
                cps.append(_rcopy(add[i].at[t], dst[i].at[t, pl.ds((1 - t) * h2, h2)], ssem.at[i, 2 * t + 1],
                                  rsem.at[i, 2 * t + 1], (kx, ky, c)))
        for cp in cps:
            cp.start()
        for cp in cps:
            cp.wait()

    out_shape = [jax.ShapeDtypeStruct((2,) + p.shape[1:], p.dtype) for p in parts]
    return pl.pallas_call(body, name="scatter_shards", in_specs=[ANY] * (2 * n), out_specs=[ANY] * n,
                          out_shape=out_shape, scratch_shapes=[pltpu.SemaphoreType.DMA((n, 4))] * 2)(*parts, *via)


def join_halves(bufs):
    n = len(bufs)

    def body(*refs):
        dst = refs[n:2 * n]
        ssem, rsem = refs[2 * n:]
        x, y, c, _ = _place()
        cps = []
        for i in range(n):
            hr = bufs[i].shape[1] // 2
            mine = dst[i].at[:, pl.ds(c * hr, hr)]
            cps.append(_rcopy(mine, mine, ssem.at[i], rsem.at[i], (x, y, 1 - c)))
        for cp in cps:
            cp.start()
        for cp in cps:
            cp.wait()

    out_shape = [jax.ShapeDtypeStruct(b.shape, b.dtype) for b in bufs]
    return pl.pallas_call(body, name="join_halves", in_specs=[ANY] * n, out_specs=[ANY] * n, out_shape=out_shape,
                          scratch_shapes=[pltpu.SemaphoreType.DMA((n,))] * 2,
                          input_output_aliases={i: i for i in range(n)})(*bufs)


def gather_blocks(v):
    def body(v_ref, out_ref, send_sems, recv_sems, local_sem):
        x, y, c, chips = _place()
        me, sibling = (x, y, c), (x, y, 1 - c)

        def blk(px, py, pc):
            return out_ref.at[4 * px + 2 * py + pc]

        def copy(k, block, to, src=None):
            return _rcopy(blk(*block) if src is None else src, blk(*block), send_sems.at[k], recv_sems.at[k], to)

        own = pltpu.make_async_copy(v_ref, blk(*me), local_sem)
        own.start()
        first = [copy(0, me, sibling, src=v_ref)]
        first += [copy(1 + r, me, (*chip, c), src=v_ref) for r, chip in enumerate(chips)]
        for cp in first:
            cp.start()
        passed = [copy(4 + r, (*chip, c), sibling) for r, chip in enumerate(chips)]
        for r, chip in enumerate(chips):
            copy(1 + r, (*chip, c), me).wait_recv()
            passed[r].start()
        copy(0, sibling, me).wait_recv()
        for r, chip in enumerate(chips):
            copy(4 + r, (*chip, 1 - c), me).wait_recv()
        for cp in first + passed:
            cp.wait_send()
        own.wait()

    return pl.pallas_call(body, name="gather_blocks", in_specs=[ANY], out_specs=ANY,
                          out_shape=jax.ShapeDtypeStruct((8,) + v.shape, v.dtype),
                          scratch_shapes=[pltpu.SemaphoreType.DMA((7,)), pltpu.SemaphoreType.DMA((7,)),
                                          pltpu.SemaphoreType.DMA])(v)


STREAM_BLOCK = 512 * 1024


def _stream_rows(rows, cols):
    base = 2 * SUBLANE if rows % (2 * SUBLANE) == 0 else SUBLANE
    return _tile(rows, max(base, STREAM_BLOCK // cols // base * base), base)


def _view3(a, lead):
    shape = a.shape[:lead] + (-1, a.shape[-1])
    return a.reshape(shape)


def sum_parts(name, terms, out, grid, where, into=None):
    n_skip = 0 if into is None else 1

    def body(w_ref, *refs):
        refs = refs[n_skip:]
        acc = refs[0][...].astype(F32)
        for t in refs[1:-1]:
            acc = acc + t[...].astype(F32)
        refs[-1][...] = acc.astype(refs[-1].dtype)

    grid_spec = pltpu.PrefetchScalarGridSpec(
        num_scalar_prefetch=1, grid=grid, in_specs=[ANY] * n_skip + [pl.BlockSpec(b, f) for _, b, f in terms],
        out_specs=pl.BlockSpec(out[2], out[3]))
    operands = ([] if into is None else [into]) + [t[0] for t in terms]
    return pl.pallas_call(body, name=name, grid_spec=grid_spec, out_shape=jax.ShapeDtypeStruct(out[0], out[1]),
                          input_output_aliases={} if into is None else {1: 0},
                          compiler_params=_params(("arbitrary",) * len(grid)))(where, *operands)


def cast_place(w, where):
    w3 = _view3(w, 1)
    L, rows, cols = w3.shape
    tr = _stream_rows(rows, cols)

    def body(w_ref, src, dst):
        dst[...] = src[...].astype(BF16)

    grid_spec = pltpu.PrefetchScalarGridSpec(
        num_scalar_prefetch=1, grid=(L, rows // tr),
        in_specs=[pl.BlockSpec((None, tr, cols), lambda l, r, wh: (l, r, 0))],
        out_specs=pl.BlockSpec((None, None, tr, cols), lambda l, r, wh: (l, wh[1], r, 0)))
    out = pl.pallas_call(body, name="cast_place", grid_spec=grid_spec,
                         out_shape=jax.ShapeDtypeStruct((L, N_CHIPS, rows, cols), BF16),
                         compiler_params=_params(("arbitrary", "arbitrary")))(where, w3)
    return out.reshape((L, N_CHIPS) + w.shape[1:])


def silu_vjp_rows(name, rows, x):
    def body(rows_ref, x_ref, o_ref):
        total = jnp.sum(rows_ref[...], axis=0, keepdims=True)
        s = jax.nn.sigmoid(x_ref[...])
        o_ref[...] = total * (s * (1.0 + x_ref[...] * (1.0 - s)))

    return pl.pallas_call(body, name=name, out_shape=jax.ShapeDtypeStruct(x.shape, F32))(rows, x)


def adamw(name, w, g, m, v):
    n, rows, cols = w.shape
    tr = _stream_rows(rows, cols)

    def body(w_ref, g_ref, m_ref, v_ref, d_ref, nm_ref, nv_ref):
        gv = g_ref[...]
        nm = ADAM_B1 * m_ref[...] + (1.0 - ADAM_B1) * gv
        nv = ADAM_B2 * v_ref[...] + (1.0 - ADAM_B2) * (gv * gv)
        m_hat = nm / (1.0 - ADAM_B1 ** ADAM_STEP)
        v_hat = nv / (1.0 - ADAM_B2 ** ADAM_STEP)
        d_ref[...] = -ADAM_LR * (m_hat / (jnp.sqrt(v_hat) + ADAM_EPS) + ADAM_WD * w_ref[...])
        nm_ref[...] = nm
        nv_ref[...] = nv

    blk = pl.BlockSpec((None, tr, cols), lambda l, r: (l, r, 0))
    shp = jax.ShapeDtypeStruct(w.shape, F32)
    return pl.pallas_call(body, name=name, grid=(n, rows // tr), in_specs=[blk] * 4, out_specs=[blk] * 3,
                          out_shape=[shp] * 3, compiler_params=_params(("arbitrary", "arbitrary")))(w, g, m, v)


COL_SHARDED = ("w_ada", "w_in", "w_proj_gla", "w_proj_s5", "w_proj_attn", "w_ffn_in")
ROW_SHARDED = ("w_s5_glu", "w_out", "w_ffn_out")
SHARDED = COL_SHARDED + ROW_SHARDED
DEFERRED = ("w_ada",)
REDUCED = tuple(n for n in SHARDED if n not in DEFERRED)
N_DEV = 8
GATE = ("w_gla_gate", "b_gla_gate")
WEIGHTS = ("c_ctx", "w_ada", "b_ada", "w_in", "w_gla_gate", "b_gla_gate", "gla_norm_w", "s5_lam_re", "s5_lam_im",
           "s5_log_dt", "s5_b_re", "s5_b_im", "s5_c_re", "s5_c_im", "s5_d", "w_s5_glu", "q_norm_w", "k_norm_w",
           "w_proj_gla", "w_proj_s5", "w_proj_attn", "w_out", "ln1_w", "ln1_b", "ln2_w", "ln2_b", "w_ffn_in",
           "w_ffn_out")
REPLICATED = tuple(n for n in WEIGHTS if n not in SHARDED + GATE)
SMALL = REPLICATED + GATE


def _pack(arrays):
    flat = jnp.concatenate([a.reshape(-1) for a in arrays])
    pad = (-flat.shape[0]) % (PACK_ROWS * LANE)
    return jnp.pad(flat, (0, pad)).reshape(-1, LANE)


def _unpack(packed, like):
    flat, out, off = packed.reshape(-1), [], 0
    for a in like:
        out.append(flat[off:off + a.size].reshape(a.shape))
        off += a.size
    return out


def kernel(x, c, ctx, c_ctx, w_ada, b_ada, w_in, w_gla_gate, b_gla_gate, gla_norm_w, s5_lam_re, s5_lam_im, s5_log_dt, s5_b_re, s5_b_im, s5_c_re, s5_c_im, s5_d, w_s5_glu, q_norm_w, k_norm_w, w_proj_gla, w_proj_s5, w_proj_attn, w_out, ln1_w, ln1_b, ln2_w, ln2_b, w_ffn_in, w_ffn_out, loss_target, m_c_ctx, m_w_ada, m_b_ada, m_w_in, m_w_gla_gate, m_b_gla_gate, m_gla_norm_w, m_s5_lam_re, m_s5_lam_im, m_s5_log_dt, m_s5_b_re, m_s5_b_im, m_s5_c_re, m_s5_c_im, m_s5_d, m_w_s5_glu, m_q_norm_w, m_k_norm_w, m_w_proj_gla, m_w_proj_s5, m_w_proj_attn, m_w_out, m_ln1_w, m_ln1_b, m_ln2_w, m_ln2_b, m_w_ffn_in, m_w_ffn_out, v_c_ctx, v_w_ada, v_b_ada, v_w_in, v_w_gla_gate, v_b_gla_gate, v_gla_norm_w, v_s5_lam_re, v_s5_lam_im, v_s5_log_dt, v_s5_b_re, v_s5_b_im, v_s5_c_re, v_s5_c_im, v_s5_d, v_w_s5_glu, v_q_norm_w, v_k_norm_w, v_w_proj_gla, v_w_proj_s5, v_w_proj_attn, v_w_out, v_ln1_w, v_ln1_b, v_ln2_w, v_ln2_b, v_w_ffn_in, v_w_ffn_out):
    args = dict(locals())
    w = {n: args[n] for n in WEIGHTS}
    m = {n: args["m_" + n] for n in WEIGHTS}
    v = {n: args["v_" + n] for n in WEIGHTS}
    L = w_in.shape[0]
    half = L // 2
    core = lax.axis_index("c").astype(jnp.int32)
    place = (2 * lax.axis_index("x") + lax.axis_index("y")).astype(jnp.int32)
    zero = jnp.zeros((), jnp.int32)
    where, by_core, by_place = jnp.stack([core, place]), jnp.stack([core, zero]), jnp.stack([zero, place])

    def padded(n):
        cols = w[n].shape[-1]
        extra = _padded_width(cols) - cols if n in COL_SHARDED else 0
        return jnp.pad(w[n], ((0, 0), (0, 0), (0, extra))) if extra else w[n]

    gathered = gather_shards([cast_place(padded(n), by_place) for n in REDUCED])
    gathered = {n: g if n in COL_SHARDED else g.reshape(L, 1, -1, g.shape[-1]) for n, g in zip(REDUCED, gathered)}
    first = [w[n] for n in GATE] + [c[0]]
    first_blocks = gather_blocks(_pack(first))
    per_dev = [_unpack(first_blocks[k], first) for k in range(N_DEV)]
    small = {n: w[n] for n in REPLICATED}
    small.update({n: jnp.concatenate([per_dev[2 * j][i] for j in range(N_CHIPS)], -1) for i, n in enumerate(GATE)})

    vectors = jnp.stack([d[-1] for d in per_dev] + [w["c_ctx"]] * N_DEV)
    left = rowwise("silu_cond", _f_silu, [vectors])[0]
    ada_w = w["w_ada"].astype(BF16)[:, None]
    mine = [jnp.stack([_mm_nn("ada_proj", left, ada_w, l, True) for l in range(L)])]
    ada_blocks = gather_blocks(_pack(mine))
    full = jnp.concatenate([_unpack(ada_blocks[2 * j], mine)[0] for j in range(N_CHIPS)], -1)
    me = 2 * place + core
    cond = jnp.stack([lax.dynamic_index_in_dim(full, me, axis=1, keepdims=False), full[:, N_DEV]], 1)

    g_sh = {}

    def loss_fn(x1, small, cond):
        return local_loss(x1, ctx[0], loss_target[0], small, cond, gathered, g_sh)

    loss, (gx, g_small, g_cond) = jax.value_and_grad(loss_fn, argnums=(0, 1, 2))(x[0], small, cond)
    loss = lax.psum(loss, ("x", "y", "c"))

    parts = [g_sh[(n, l)].reshape((N_CHIPS, -1, g_sh[(n, l)].shape[-1])) for n in REDUCED for l in range(L)]
    theirs = swap_halves(parts)
    chip_sums, tiles = [], []
    for p, t in zip(parts, theirs):
        hr, cols = t.shape[1:]
        tr = _stream_rows(hr, cols)
        tiles.append(tr)
        blk = (None, tr, cols)
        chip_sums.append(sum_parts(
            "sum_cores", [(p, blk, functools.partial(lambda nb, s, r, wh: (s, wh[0] * nb + r, 0), hr // tr)),
                          (t, blk, lambda s, r, wh: (s, r, 0))],
            (t.shape, BF16, blk, lambda s, r, wh: (s, r, 0)), (N_CHIPS, hr // tr), by_core))
    through = scatter_diagonal(chip_sums)
    via = []
    for s, t in zip(chip_sums, through):
        h2, cols = t.shape[1:]
        tr = _stream_rows(h2, cols)
        blk = (None, tr, cols)

        def shard_of_neighbour(nb, t, r, wh):
            chip = jnp.where(t == 0, (wh[1] + 2) % N_CHIPS, wh[1] + 1 - 2 * (wh[1] % 2))
            return (chip, (1 - t) * nb + r, 0)

        via.append(sum_parts("sum_through", [(s, blk, functools.partial(shard_of_neighbour, h2 // tr)),
                                             (t, blk, lambda t, r, wh: (1 - t, r, 0))],
                             (t.shape, BF16, blk, lambda t, r, wh: (t, r, 0)), (2, h2 // tr), by_place))
    recv = scatter_shards(chip_sums, via)
    finals = []
    for i, n in enumerate(REDUCED):
        rows, cols = parts[i * L].shape[1:]
        buf = None
        for l in range(L):
            k = i * L + l
            tr = tiles[k]
            nb = rows // 2 // tr
            blk = (None, tr, cols)
            terms = [(parts[k], blk, functools.partial(lambda nb, r, wh: (wh[1], wh[0] * nb + r, 0), nb)),
                     (theirs[k], blk, lambda r, wh: (wh[1], r, 0))]
            terms += [(recv[k], blk, functools.partial(lambda j, r, wh: (j, r, 0), j)) for j in range(2)]
            buf = sum_parts("sum_chips", terms,
                            ((L, rows, cols), F32, blk, functools.partial(lambda l, nb, r, wh: (l, wh[0] * nb + r, 0), l, nb)),
                            (nb,), where, into=buf)
        finals.append(buf)
    grads = {n: g[..., :w[n].shape[-1]].reshape(w[n].shape) for n, g in zip(REDUCED, join_halves(finals))}

    row_blocks = gather_blocks(_pack([g_cond]))
    rows = [_unpack(row_blocks[k], [g_cond])[0] for k in range(N_DEV)]
    shard = w["w_ada"].shape[-1]
    ada, rights = [], []
    for l in range(L):
        right = jnp.stack([r[l, 0] for r in rows] + [r[l, 1] for r in rows])
        rights.append(lax.dynamic_slice_in_dim(right, place * shard, shard, axis=1))
        ada.append(_mm_tn("ada_dw", left, rights[l], 1, True, F32))
    grads["w_ada"] = jnp.concatenate(ada, 0)
    back = _mm_nt("ada_dx", jnp.concatenate(rights, 1), ada_w.reshape((1, L) + ada_w.shape[2:]), 0, True)
    share = [back[N_DEV:]]
    share_blocks = gather_blocks(_pack(share))
    shares = jnp.concatenate([_unpack(share_blocks[2 * j], share)[0] for j in range(N_CHIPS)], 0)
    grad_c_ctx = silu_vjp_rows("ctx_vector_grad", shares, w["c_ctx"][None, :])[0]

    small_parts = [g_small[n] for n in SMALL]
    blocks = gather_blocks(_pack(small_parts))
    prow = (None, PACK_ROWS, LANE)
    total = sum_parts("sum_devices",
                      [(blocks, prow, functools.partial(lambda k, r, wh: (k, r, 0), k)) for k in range(8)],
                      (blocks.shape[1:], F32, prow[1:], lambda r, wh: (r, 0)), (blocks.shape[1] // PACK_ROWS,),
                      jnp.zeros((2,), jnp.int32))
    grads.update(dict(zip(SMALL, _unpack(total, small_parts))))
    grads["c_ctx"] = grad_c_ctx
    for n in GATE:
        width = w[n].shape[-1]
        grads[n] = lax.dynamic_slice_in_dim(grads[n], place * width, width, axis=-1)

    delta, new_m, new_v = {}, {}, {}
    for n in SHARDED:
        d3, m3, v3 = adamw("adamw", _view3(w[n], 1), _view3(grads[n], 1), _view3(m[n], 1), _view3(v[n], 1))
        delta[n], new_m[n], new_v[n] = (t.reshape(w[n].shape) for t in (d3, m3, v3))
    packed = [_pack([d[n] for n in SMALL])[None] for d in (w, grads, m, v)]
    d3, m3, v3 = adamw("adamw_small", *packed)
    like = [w[n] for n in SMALL]
    for dst, src in ((delta, d3), (new_m, m3), (new_v, v3)):
        dst.update(dict(zip(SMALL, _unpack(src[0], like))))

    return (loss, gx[None], *[grads[n] for n in WEIGHTS], *[delta[n] for n in WEIGHTS],
            *[new_m[n] for n in WEIGHTS], *[new_v[n] for n in WEIGHTS])
```

```python
import functools

import jax
import jax.numpy as jnp
from jax import lax
from jax.experimental import pallas as pl
from jax.experimental.pallas import tpu as pltpu

F32 = jnp.float32
BF16 = jnp.bfloat16

GRID_W = 64
GLA_HEADS = 4
GLA_DK = 128
GLA_DV = 256
GLA_GATE_RANK = 16
GLA_GATE_TAU = 16.0
GLA_CHUNK = 64
S5_GROUP = 16
S5_STATE = 64
ATTN_Q_HEADS = 8
ATTN_KV_HEADS = 2
ATTN_HEAD_DIM = 128
ROPE_THETA = 10000.0
DEPTH = 4
DN_ALPHA = (2 * DEPTH) ** 0.25
EPS = 1e-6
ADAM_LR = 0.001
ADAM_B1 = 0.9
ADAM_B2 = 0.999
ADAM_EPS = 1e-08
ADAM_WD = 0.01
ADAM_STEP = 10

LANE = 128
SUBLANE = 8
VMEM_LIMIT = 56 * 1024 * 1024
ADA_ROWS = 16
S5_CHUNK = 128
S5_BLOCK_GROUPS = 8
ROW_TILE = 256
ELEMENTWISE_ROWS = 512
COL_TILE = 512
PACK_ROWS = 512
SHARD_PAD = 1024
N_CHIPS = 4


def _params(sem, **kw):
    return pltpu.CompilerParams(dimension_semantics=sem, vmem_limit_bytes=VMEM_LIMIT, **kw)


def _tile(n, target, base):
    if n <= target:
        return n
    best = None
    for t in range(base, target + 1, base):
        if n % t == 0:
            best = t
    assert best is not None, (n, target, base)
    return best


_DIMS = {"nn": ((1,), (0,)), "nt": ((1,), (1,)), "tn": ((0,), (0,))}


def _dg(a, b, mode):
    return lax.dot_general(a.astype(BF16), b.astype(BF16), (_DIMS[mode], ((), ())), preferred_element_type=F32)


@functools.partial(jax.custom_vjp, nondiff_argnums=(2,))
def bdot(a, b, mode):
    return _dg(a, b, mode)


def _bdot_fwd(a, b, mode):
    return _dg(a, b, mode), (a, b)


def _bdot_bwd(mode, res, g):
    a, b = res
    if mode == "nn":
        return bdot(g, b, "nt").astype(a.dtype), bdot(a, g, "tn").astype(b.dtype)
    if mode == "nt":
        return bdot(g, b, "nn").astype(a.dtype), bdot(g, a, "tn").astype(b.dtype)
    return bdot(b, g, "nt").astype(a.dtype), bdot(a, g, "nn").astype(b.dtype)


bdot.defvjp(_bdot_fwd, _bdot_bwd)


MM_TILES = {"nn": (1152, 1024, 1664), "nt": (1152, 1024, 1664), "tn": (768, 2048, 1408)}


def _mm_tiles(mode, M, Kb, Nb):
    tm, tk, tn = MM_TILES[mode]
    return _tile(M, tm, SUBLANE), _tile(Kb, tk, LANE), _tile(Nb, tn, LANE)


def _mm_body(mode, reduce_axes):
    def body(p_ref, q_ref, o_ref, acc):
        first = functools.reduce(jnp.logical_and, [pl.program_id(ax) == 0 for ax, _ in reduce_axes])
        last = functools.reduce(jnp.logical_and, [pl.program_id(ax) == n - 1 for ax, n in reduce_axes])

        @pl.when(first)
        def _():
            acc[...] = jnp.zeros_like(acc)

        acc[...] += _dg(p_ref[...], q_ref[...], mode)

        @pl.when(last)
        def _():
            o_ref[...] = acc[...].astype(o_ref.dtype)

    return body


def _mm_nn(name, a, w, l, share):
    M = a.shape[0]
    _, B, Kb, Nb = w.shape
    tm, tk, tn = _mm_tiles("nn", M, Kb, Nb)
    nk, nn = Kb // tk, Nb // tn
    a_map = (lambda b, i, j, k: (i, k)) if share else (lambda b, i, j, k: (i, b * nk + k))
    return pl.pallas_call(
        _mm_body("nn", [(3, nk)]), name=name, grid=(B, M // tm, nn, nk),
        in_specs=[pl.BlockSpec((tm, tk), a_map),
                  pl.BlockSpec((None, None, tk, tn), lambda b, i, j, k: (l, b, k, j))],
        out_specs=pl.BlockSpec((tm, tn), lambda b, i, j, k: (i, b * nn + j)),
        out_shape=jax.ShapeDtypeStruct((M, B * Nb), F32),
        scratch_shapes=[pltpu.VMEM((tm, tn), F32)],
        compiler_params=_params(("arbitrary",) * 4))(a, w)


def _mm_nt(name, g, w, l, share):
    M = g.shape[0]
    _, B, Kb, Nb = w.shape
    tm, tk, tn = _mm_tiles("nt", M, Kb, Nb)
    nk, nn = Kb // tk, Nb // tn
    if share:
        grid, red = (M // tm, nk, B, nn), [(2, B), (3, nn)]
        g_map, w_map = (lambda i, k, b, n: (i, b * nn + n)), (lambda i, k, b, n: (l, b, k, n))
        o_map, width = (lambda i, k, b, n: (i, k)), Kb
    else:
        grid, red = (B, M // tm, nk, nn), [(3, nn)]
        g_map, w_map = (lambda b, i, k, n: (i, b * nn + n)), (lambda b, i, k, n: (l, b, k, n))
        o_map, width = (lambda b, i, k, n: (i, b * nk + k)), B * Kb
    return pl.pallas_call(
        _mm_body("nt", red), name=name, grid=grid,
        in_specs=[pl.BlockSpec((tm, tn), g_map), pl.BlockSpec((None, None, tk, tn), w_map)],
        out_specs=pl.BlockSpec((tm, tk), o_map),
        out_shape=jax.ShapeDtypeStruct((M, width), F32),
        scratch_shapes=[pltpu.VMEM((tm, tk), F32)],
        compiler_params=_params(("arbitrary",) * 4))(g, w)


def _mm_tn(name, a, g, B, share, dtype):
    M = a.shape[0]
    Kb, Nb = a.shape[1] // (1 if share else B), g.shape[1] // B
    tm, tk, tn = _mm_tiles("tn", M, Kb, Nb)
    nk, nn = Kb // tk, Nb // tn
    a_map = (lambda b, k, j, m: (m, k)) if share else (lambda b, k, j, m: (m, b * nk + k))
    return pl.pallas_call(
        _mm_body("tn", [(3, M // tm)]), name=name, grid=(B, nk, nn, M // tm),
        in_specs=[pl.BlockSpec((tm, tk), a_map),
                  pl.BlockSpec((tm, tn), lambda b, k, j, m: (m, b * nn + j))],
        out_specs=pl.BlockSpec((None, tk, tn), lambda b, k, j, m: (b, k, j)),
        out_shape=jax.ShapeDtypeStruct((B, Kb, Nb), dtype),
        scratch_shapes=[pltpu.VMEM((tk, tn), F32)],
        compiler_params=_params(("arbitrary",) * 4))(a, g)


def mm(name, a, w):
    w3 = w if w.ndim == 3 else w[None]

    @jax.custom_vjp
    def op(a, w3):
        return _mm_nn(name + "_fwd", a, w3.astype(BF16)[None], 0, False)

    def fwd(a, w3):
        wb = w3.astype(BF16)[None]
        return _mm_nn(name + "_fwd", a, wb, 0, False), (a, wb)

    def bwd(res, g):
        a, wb = res
        return _mm_nt(name + "_dx", g, wb, 0, False), _mm_tn(name + "_dw", a, g, wb.shape[1], False, F32)

    op.defvjp(fwd, bwd)
    return op(a, w3)


def mm_gathered(name, a, w, l, grads, key):
    @jax.custom_vjp
    def op(a):
        return _mm_nn(name + "_fwd", a.astype(BF16), w, l, True)

    def fwd(a):
        ab = a.astype(BF16)
        return _mm_nn(name + "_fwd", ab, w, l, True), (ab,)

    def bwd(res, g):
        (a,) = res
        grads[key] = _mm_tn(name + "_dw", a, g, w.shape[1], True, BF16)
        return (_mm_nt(name + "_dx", g, w, l, True),)

    op.defvjp(fwd, bwd)
    return op(a)


def _spec_shape(spec, G, T):
    k = spec[0]
    if k == "row":
        return (T, spec[1])
    if k == "rowg":
        return (T, (G // spec[2]) * spec[1])
    if k == "bc":
        return (spec[1], spec[2])
    return (spec[1], (G // spec[3]) * spec[2])


def _spec_block(spec, tm, rmap):
    k = spec[0]
    if k == "row":
        return pl.BlockSpec((tm, spec[1]), lambda g, r: (rmap(r), 0))
    if k == "rowg":
        d = spec[2]
        return pl.BlockSpec((tm, spec[1]), lambda g, r: (rmap(r), g // d))
    if k == "bc":
        return pl.BlockSpec((spec[1], spec[2]), lambda g, r: (0, 0))
    d = spec[3]
    return pl.BlockSpec((spec[1], spec[2]), lambda g, r: (0, g // d))


def block_op(name, f, in_specs, out_specs, G, T, tm, diff, carry=(), row0=False, order=None, f_saved=None):
    n_in, n_out, n_c = len(in_specs), len(out_specs), len(carry)
    n_sv = n_out if f_saved is not None else 0
    n_steps = T // tm
    assert T % tm == 0
    order = order or (lambda s: s)
    diff_idx = [i for i in range(n_in) if diff[i]]
    for i in diff_idx:
        assert in_specs[i][0] != "row" or G == 1
        assert in_specs[i][0] != "rowg" or in_specs[i][2] == 1
    out_shapes = [jax.ShapeDtypeStruct(_spec_shape(s, G, T), F32) for s in out_specs]
    save_shapes = [jax.ShapeDtypeStruct((n_steps, a, G * b), F32) for a, b in carry]
    sem = ("arbitrary", "arbitrary")

    def call_f(r_idx, cvals, vals):
        args = list(vals)
        if n_c:
            args = [tuple(cvals)] + args
        if row0:
            args = [r_idx * tm] + args
        return f(*args)

    def fwd_body(*refs):
        in_refs = refs[:n_in]
        out_refs = refs[n_in:n_in + n_out]
        save_refs = refs[n_in + n_out:n_in + n_out + n_c]
        c_refs = refs[n_in + n_out + n_c:]
        r = pl.program_id(1)
        if n_c:
            @pl.when(r == 0)
            def _():
                for c in c_refs:
                    c[...] = jnp.zeros_like(c)

            cvals = [c[...] for c in c_refs]
            for s, v in zip(save_refs, cvals):
                s[...] = v
            new_c, outs = call_f(r, cvals, [x[...] for x in in_refs])
            for c, v in zip(c_refs, new_c):
                c[...] = v
        else:
            outs = call_f(r, (), [x[...] for x in in_refs])
        for o, v in zip(out_refs, outs):
            o[...] = v.astype(F32)

    def fwd_call(*arrays):
        res = pl.pallas_call(
            fwd_body, name=name + "_fwd", grid=(G, n_steps),
            in_specs=[_spec_block(s, tm, order) for s in in_specs],
            out_specs=[_spec_block(s, tm, order) for s in out_specs]
            + [pl.BlockSpec((None, a, b), lambda g, r: (r, 0, g)) for a, b in carry],
            out_shape=out_shapes + save_shapes,
            scratch_shapes=[pltpu.VMEM((a, b), F32) for a, b in carry],
            compiler_params=_params(sem))(*arrays)
        return tuple(res)

    def bwd_body(*refs):
        in_refs = refs[:n_in]
        save_refs = refs[n_in:n_in + n_c]
        ct_refs = refs[n_in + n_c:n_in + n_c + n_out]
        at = n_in + n_c + n_out
        sv_refs = refs[at:at + n_sv]
        g_refs = refs[at + n_sv:at + n_sv + len(diff_idx)]
        dc_refs = refs[at + n_sv + len(diff_idx):]
        g = pl.program_id(0)
        r = pl.program_id(1)
        vals = [x[...] for x in in_refs]
        if n_c:
            @pl.when(r == 0)
            def _():
                for d in dc_refs:
                    d[...] = jnp.zeros_like(d)

        def fun(cvals, dvals):
            full = list(vals)
            for i, v in zip(diff_idx, dvals):
                full[i] = v
            if n_sv:
                return f_saved(tuple(cvals), *full, *[s[...] for s in sv_refs])
            return call_f(n_steps - 1 - r if n_c else r, cvals, full)

        _, vjp = jax.vjp(fun, tuple(s[...] for s in save_refs), tuple(vals[i] for i in diff_idx))
        cts = tuple(c[...] for c in ct_refs)
        if n_c:
            cts = (tuple(d[...] for d in dc_refs), cts)
        dcin, dvals = vjp(cts)
        for d, v in zip(dc_refs, dcin):
            d[...] = v
        for gref, i, v in zip(g_refs, diff_idx, dvals):
            spec = in_specs[i]
            if spec[0] in ("row", "rowg"):
                gref[...] = v
            else:
                first = (r == 0) & ((g == 0) if spec[0] == "bc" else (g % spec[3] == 0))

                @pl.when(first)
                def _(gref=gref, v=v):
                    gref[...] = v

                @pl.when(jnp.logical_not(first))
                def _(gref=gref, v=v):
                    gref[...] += v

    def bwd_call(arrays, saved, cts, outs):
        rmap = (lambda r: order(n_steps - 1 - r)) if n_c else (lambda r: r)
        res = pl.pallas_call(
            bwd_body, name=name + "_bwd", grid=(G, n_steps),
            in_specs=[_spec_block(s, tm, rmap) for s in in_specs]
            + [pl.BlockSpec((None, a, b), lambda g, r: (n_steps - 1 - r, 0, g)) for a, b in carry]
            + [_spec_block(s, tm, rmap) for s in out_specs] * (2 if n_sv else 1),
            out_specs=[_spec_block(in_specs[i], tm, rmap) for i in diff_idx],
            out_shape=[jax.ShapeDtypeStruct(_spec_shape(in_specs[i], G, T), F32) for i in diff_idx],
            scratch_shapes=[pltpu.VMEM((a, b), F32) for a, b in carry],
            compiler_params=_params(sem))(*arrays, *saved, *cts, *outs)
        return tuple(res)

    @jax.custom_vjp
    def op(*arrays):
        return fwd_call(*arrays)[:n_out]

    def op_fwd(*arrays):
        res = fwd_call(*arrays)
        return res[:n_out], (arrays, res[n_out:], res[:n_out] if n_sv else ())

    def op_bwd(res, cts):
        arrays, saved, outs = res
        grads = bwd_call(arrays, saved, cts, outs)
        out = [jnp.zeros_like(a) for a in arrays]
        for i, gval in zip(diff_idx, grads):
            out[i] = gval
        return tuple(out)

    op.defvjp(op_fwd, op_bwd)
    return op


def _rows(n, m):
    return lax.broadcasted_iota(jnp.int32, (n, m), 0)


def _ctx_select(row0, tm, n_ctx, v_lat, v_ctx):
    if n_ctx == 0:
        return v_lat
    is_ctx = (row0 + _rows(tm, 1)) < n_ctx
    return jnp.where(is_ctx, v_ctx, v_lat)


def _silu(x):
    return x * jax.nn.sigmoid(x)


def _f_modulate(tm, n_ctx):
    def f(row0, x, sh_l, sh_c, sc_l, sc_c):
        sh = _ctx_select(row0, tm, n_ctx, sh_l, sh_c)
        sc = _ctx_select(row0, tm, n_ctx, sc_l, sc_c)
        return (x * (1 + sc) + sh,)
    return f


def _f_postnorm(tm, n_ctx):
    def f(row0, x, y, g_l, g_c, w, b):
        z = DN_ALPHA * x + _ctx_select(row0, tm, n_ctx, g_l, g_c) * y
        mu = jnp.mean(z, -1, keepdims=True)
        zc = z - mu
        var = jnp.mean(zc * zc, -1, keepdims=True)
        return (zc * lax.rsqrt(var + EPS) * w + b,)
    return f


def _log_sigmoid(x):
    return -(jnp.maximum(-x, 0.0) + jnp.log1p(jnp.exp(-jnp.abs(x))))


def _f_gla_prep(glr, wg0, wg1, b0, b1):
    return (_log_sigmoid(bdot(glr, wg0, "nn") + b0) / GLA_GATE_TAU,
            _log_sigmoid(bdot(glr, wg1, "nn") + b1) / GLA_GATE_TAU)


def _f_gla_step(rev):
    def f(carry, q, k, v, la):
        (st,) = carry
        n = q.shape[0]
        cols = lax.broadcasted_iota(jnp.int32, (n, n), 1)
        tri = (_rows(n, n) <= cols) if rev else (_rows(n, n) >= cols)
        b = jnp.dot(tri.astype(F32), la, precision=lax.Precision.HIGHEST)
        qe = q * (GLA_DK ** -0.5) * jnp.exp(b)
        ke = k * jnp.exp(-b)
        att = jnp.where(tri, bdot(qe, ke, "nt"), 0.0)
        o = bdot(att, v, "nn") + bdot(qe, st, "nt")
        end = 0 if rev else n - 1
        b_last = jnp.sum(jnp.where(_rows(n, 1) == end, b, 0.0), axis=0, keepdims=True)
        kd = k * jnp.exp(b_last - b)
        st = st * jnp.exp(b_last) + bdot(v, kd, "tn")
        return (st,), (o,)
    return f


def _f_gla_norm(o0, o1, gr, w):
    o = o0 + o1
    mu = jnp.mean(o, -1, keepdims=True)
    oc = o - mu
    var = jnp.mean(oc * oc, -1, keepdims=True)
    return (oc * lax.rsqrt(var + EPS) * w * _silu(gr),)


@functools.partial(jax.custom_vjp, nondiff_argnums=(1, 2))
def _shift_rows(x, d, up):
    n = x.shape[0]
    rows = _rows(n, 1)
    if up:
        return jnp.where(rows < n - d, pltpu.roll(x, n - d, 0), 0.0)
    return jnp.where(rows >= d, pltpu.roll(x, d, 0), 0.0)


def _shift_fwd(x, d, up):
    return _shift_rows(x, d, up), None


def _shift_bwd(d, up, _, g):
    return (_shift_rows(g, d, not up),)


_shift_rows.defvjp(_shift_fwd, _shift_bwd)


def _scan_doubling(ur, ui, ar, ai, rev):
    n = ur.shape[0]
    xr, xi, pr, pi = ur, ui, ar, ai
    d = 1
    while d < n:
        sr, si = _shift_rows(xr, d, rev), _shift_rows(xi, d, rev)
        xr, xi = xr + pr * sr - pi * si, xi + pr * si + pi * sr
        pr, pi = pr * pr - pi * pi, 2 * pr * pi
        d *= 2
    return xr, xi


@functools.partial(jax.custom_vjp, nondiff_argnums=(6,))
def _scan_known(ur, ui, ar, ai, xr, xi, rev):
    return xr, xi


def _scan_known_fwd(ur, ui, ar, ai, xr, xi, rev):
    return (xr, xi), (ar, ai, xr, xi)


def _scan_known_bwd(rev, res, g):
    ar, ai, xr, xi = res
    lr, li = _scan_doubling(g[0], g[1], ar, -ai, not rev)
    pr, pi = _shift_rows(xr, 1, rev), _shift_rows(xi, 1, rev)
    dar = jnp.sum(lr * pr + li * pi, axis=0, keepdims=True)
    dai = jnp.sum(li * pr - lr * pi, axis=0, keepdims=True)
    return lr, li, dar, dai, jnp.zeros_like(xr), jnp.zeros_like(xi)


_scan_known.defvjp(_scan_known_fwd, _scan_known_bwd)


def _f_s5_step(rev, known=False):
    def f(carry, bur, bui, lam_re, lam_im, log_dt, *states):
        cr, ci = carry
        n = bur.shape[0]
        dt = jnp.exp(log_dt)
        mag = jnp.exp(lam_re * dt)
        ar, ai = mag * jnp.cos(lam_im * dt), mag * jnp.sin(lam_im * dt)
        den = lam_re * lam_re + lam_im * lam_im
        nr, ni = ar - 1, ai
        kr = (nr * lam_re + ni * lam_im) / den
        ki = (ni * lam_re - nr * lam_im) / den
        first = _rows(n, 1) == (n - 1 if rev else 0)
        ur = kr * bur - ki * bui + jnp.where(first, ar * cr - ai * ci, 0.0)
        ui = kr * bui + ki * bur + jnp.where(first, ar * ci + ai * cr, 0.0)
        if known:
            xr, xi = _scan_known(ur, ui, ar, ai, states[0], states[1], rev)
        else:
            xr, xi = _scan_doubling(ur, ui, ar, ai, rev)
        last = _rows(n, 1) == (0 if rev else n - 1)
        cr = jnp.sum(jnp.where(last, xr, 0.0), axis=0, keepdims=True)
        ci = jnp.sum(jnp.where(last, xi, 0.0), axis=0, keepdims=True)
        return (cr, ci), (xr, xi)
    return f


def _f_s5_post(su, dskip, y0r, y0i, y1r, y1i):
    return (jax.nn.gelu(su * dskip + y0r - y0i + y1r - y1i),)


def _f_s5_glu(y, t):
    return (y * jax.nn.sigmoid(t),)


def _swap_pairs(x):
    lane = lax.broadcasted_iota(jnp.int32, x.shape, 1)
    return jnp.where(lane % 2 == 0, pltpu.roll(x, x.shape[1] - 1, 1), pltpu.roll(x, 1, 1))


@jax.custom_vjp
def _rope(x, cos2, sin2):
    return x * cos2 + _swap_pairs(x) * sin2


def _rope_fwd(x, cos2, sin2):
    return _rope(x, cos2, sin2), (cos2, sin2)


def _rope_bwd(res, g):
    cos2, sin2 = res
    return g * cos2 + _swap_pairs(g * sin2), jnp.zeros_like(cos2), jnp.zeros_like(sin2)


_rope.defvjp(_rope_fwd, _rope_bwd)


def _f_qk_norm_rope(x, cos2, sin2, w):
    xn = x * lax.rsqrt(jnp.mean(x * x, -1, keepdims=True) + EPS) * w
    return (_rope(xn, cos2, sin2),)


def _f_attn(q, k, v):
    s = bdot(q, k, "nt") * (ATTN_HEAD_DIM ** -0.5)
    e = jnp.exp(s - jnp.max(s, -1, keepdims=True))
    p = e / jnp.sum(e, -1, keepdims=True)
    return (bdot(p, v, "nn"),)


def _f_merge(ga, gb, gc, pa, pb, pc):
    return (jax.nn.sigmoid(ga) * pa + jax.nn.sigmoid(gb) * pb + jax.nn.sigmoid(gc) * pc,)


def _f_swiglu(a, b):
    return (_silu(a) * b,)


def _f_silu(x):
    return (_silu(x),)


def _f_add_bias(x, b):
    return (x + b,)


def _scan_order(rev, n_ctx, T, tm):
    if not rev:
        return None
    nc, n = n_ctx // tm, T // tm
    return lambda s: jnp.where(s < nc, nc - 1 - s, n - 1 - (s - nc))


def modulate(name, x, sh, sc, n_ctx):
    T, D = x.shape
    tm = _tile(T, ELEMENTWISE_ROWS, SUBLANE)
    cw = _tile(D, COL_TILE, LANE)
    col, vec = ("rowg", cw, 1), ("bcg", 1, cw, 1)
    op = block_op(name, _f_modulate(tm, n_ctx), [col, vec, vec, vec, vec], [col], D // cw, T, tm,
                  [True] * 5, row0=True)
    return op(x, sh[0], sh[1], sc[0], sc[1])[0]


def postnorm(name, x, y, g, w, b, n_ctx):
    T, D = x.shape
    tm = _tile(T, ROW_TILE, SUBLANE)
    vec = ("bc", 1, D)
    op = block_op(name, _f_postnorm(tm, n_ctx), [("row", D), ("row", D), vec, vec, vec, vec], [("row", D)], 1, T,
                  tm, [True] * 6, row0=True)
    return op(x, y, g[0], g[1], w, b)[0]


def rowwise(name, f, arrays, n_out=1):
    T, w = arrays[0].shape
    tm = _tile(T, ELEMENTWISE_ROWS, SUBLANE)
    cw = _tile(w, COL_TILE, LANE)
    col = ("rowg", cw, 1)
    op = block_op(name, f, [col] * len(arrays), [col] * n_out, w // cw, T, tm, [True] * len(arrays))
    return op(*arrays)


def gla_prep(name, glr, wg, bg):
    T = glr.shape[0]
    qk = wg.shape[-1]
    tm = _tile(T, ROW_TILE, SUBLANE)
    op = block_op(name, _f_gla_prep, [("row", LANE), ("bc", LANE, qk), ("bc", LANE, qk), ("bc", 1, qk), ("bc", 1, qk)],
                  [("row", qk), ("row", qk)], 1, T, tm, [True] * 5)
    return op(glr, wg[0], wg[1], bg[0], bg[1])


def gla_scan(name, q, k, v, la, rev, n_ctx):
    T = q.shape[0]
    op = block_op(name, _f_gla_step(rev), [("rowg", GLA_DK, 1), ("rowg", GLA_DK, 1), ("rowg", GLA_DV, 1), ("rowg", GLA_DK, 1)],
                  [("rowg", GLA_DV, 1)], GLA_HEADS, T, GLA_CHUNK, [True] * 4, carry=[(GLA_DV, GLA_DK)],
                  order=_scan_order(rev, n_ctx, T, GLA_CHUNK))
    return op(q, k, v, la)[0]


def gla_norm(name, o0, o1, gr, w):
    T = o0.shape[0]
    tm = _tile(T, ROW_TILE, SUBLANE)
    hd = ("rowg", GLA_DV, 1)
    op = block_op(name, _f_gla_norm, [hd, hd, hd, ("bcg", 1, GLA_DV, 1)], [hd], GLA_HEADS, T, tm, [True] * 4)
    return op(o0, o1, gr, w)[0]


def s5_scan(name, bur, bui, lam_re, lam_im, log_dt, rev, n_ctx):
    T, S = bur.shape
    cols = _tile(S, 768, LANE)
    G = S // cols
    col, par = ("rowg", cols, 1), ("bcg", 1, cols, 1)
    op = block_op(name, _f_s5_step(rev), [col, col, par, par, par], [col, col], G, T, S5_CHUNK, [True] * 5,
                  carry=[(1, cols), (1, cols)], order=_scan_order(rev, n_ctx, T, S5_CHUNK),
                  f_saved=_f_s5_step(rev, True))
    return op(bur, bui, lam_re, lam_im, log_dt)


def qk_norm_rope(name, x, cos2, sin2, w):
    T = x.shape[0]
    G = x.shape[1] // ATTN_HEAD_DIM
    tm = _tile(T, ROW_TILE, SUBLANE)
    hd = ("rowg", ATTN_HEAD_DIM, 1)
    op = block_op(name, _f_qk_norm_rope, [hd, ("row", ATTN_HEAD_DIM), ("row", ATTN_HEAD_DIM), ("bc", 1, ATTN_HEAD_DIM)],
                  [hd], G, T, tm, [True, False, False, True])
    return op(x, cos2, sin2, w)[0]


def attention(name, q, k, v):
    T, Tk = q.shape[0], k.shape[0]
    tm = _tile(T, ROW_TILE, SUBLANE)
    grp = ATTN_Q_HEADS // ATTN_KV_HEADS
    kv = ("bcg", Tk, ATTN_HEAD_DIM, grp)
    op = block_op(name, _f_attn, [("rowg", ATTN_HEAD_DIM, 1), kv, kv], [("rowg", ATTN_HEAD_DIM, 1)], ATTN_Q_HEADS, T,
                  tm, [True] * 3)
    return op(q, k, v)[0]


def sq_loss(name, y, t):
    T, D = y.shape
    tm = _tile(T, ROW_TILE, SUBLANE)

    def fwd_body(y_ref, t_ref, o_ref):
        e = y_ref[...] - t_ref[...]
        part = jnp.sum(jnp.sum(e * e, -1, keepdims=True), 0, keepdims=True) * (0.5 / D)

        @pl.when(pl.program_id(0) == 0)
        def _():
            o_ref[...] = jnp.zeros_like(o_ref)

        o_ref[...] += part * jnp.ones((1, LANE), F32)

    def bwd_body(y_ref, t_ref, g_ref, o_ref):
        o_ref[...] = (y_ref[...] - t_ref[...]) * (g_ref[:, 0:1] / D)

    row = pl.BlockSpec((tm, D), lambda r: (r, 0))
    one = pl.BlockSpec((1, LANE), lambda r: (0, 0))

    def fwd_call(y, t):
        return pl.pallas_call(fwd_body, name=name + "_fwd", grid=(T // tm,), in_specs=[row, row], out_specs=one,
                              out_shape=jax.ShapeDtypeStruct((1, LANE), F32), compiler_params=_params(("arbitrary",)))(y, t)

    @jax.custom_vjp
    def op(y, t):
        return fwd_call(y, t)[0, 0]

    def op_fwd(y, t):
        return fwd_call(y, t)[0, 0], (y, t)

    def op_bwd(res, g):
        y, t = res
        gy = pl.pallas_call(bwd_body, name=name + "_bwd", grid=(T // tm,), in_specs=[row, row, one], out_specs=row,
                            out_shape=jax.ShapeDtypeStruct((T, D), F32), compiler_params=_params(("arbitrary",)))(
                                y, t, jnp.full((1, LANE), g, F32))
        return gy, jnp.zeros_like(t)

    op.defvjp(op_fwd, op_bwd)
    return op(y, t)


def _rope_tables(n_ctx, n_lat):
    n_rows = n_lat // GRID_W
    rows = jnp.repeat(jnp.arange(n_rows), GRID_W).astype(F32)
    cols = jnp.tile(jnp.arange(GRID_W), n_rows).astype(F32)
    n_freq = ATTN_HEAD_DIM // 4
    inv = ROPE_THETA ** (-jnp.arange(n_freq, dtype=F32) / n_freq)
    ang = jnp.concatenate([rows[:, None] * inv, cols[:, None] * inv], -1)
    cos2 = jnp.repeat(jnp.cos(ang), 2, axis=-1)
    sin2 = jnp.stack([-jnp.sin(ang), jnp.sin(ang)], -1).reshape(n_lat, ATTN_HEAD_DIM)
    cos2 = jnp.concatenate([jnp.ones((n_ctx, ATTN_HEAD_DIM), F32), cos2], 0)
    sin2 = jnp.concatenate([jnp.zeros((n_ctx, ATTN_HEAD_DIM), F32), sin2], 0)
    return cos2, sin2


def _in_layout(D, s5_width):
    qk, gv = GLA_HEADS * GLA_DK, GLA_HEADS * GLA_DV
    aq, akv = ATTN_Q_HEADS * ATTN_HEAD_DIM, ATTN_KV_HEADS * ATTN_HEAD_DIM
    widths = [("gq", qk), ("gk", qk), ("gv", gv), ("gr", gv), ("glr", GLA_GATE_RANK), ("su", s5_width), ("aq", aq),
              ("ak", akv), ("av", akv), ("ga", D), ("gb", D), ("gc", D)]
    off, out = 0, {}
    for n, w in widths:
        out[n] = (off, w)
        off += w
    return out, off


def _padded_width(width):
    return width if width % LANE == 0 else -(-width // SHARD_PAD) * SHARD_PAD


def _shard_cols(z, off, width, shard, padded):
    parts = []
    for j in range(N_CHIPS):
        lo, hi = max(off, j * shard), min(off + width, (j + 1) * shard)
        if lo < hi:
            parts.append(z[:, j * padded + lo - j * shard:j * padded + hi - j * shard])
    return parts[0] if len(parts) == 1 else jnp.concatenate(parts, 1)


def _s5_in_blocks(b):
    G, P, C = b.shape
    nb, bg = G // S5_BLOCK_GROUPS, S5_BLOCK_GROUPS
    t = b.reshape(nb, bg, P, C).transpose(0, 1, 3, 2)
    return jnp.einsum("bgcp,gh->bgchp", t, jnp.eye(bg, dtype=F32)).reshape(nb, bg * C, bg * P)


def _s5_out_blocks(c):
    G, C, P = c.shape
    nb, bg = G // S5_BLOCK_GROUPS, S5_BLOCK_GROUPS
    t = c.reshape(nb, bg, C, P).transpose(0, 1, 3, 2)
    return jnp.einsum("bgpc,gh->bgphc", t, jnp.eye(bg, dtype=F32)).reshape(nb, bg * P, bg * C)


def _layer(keep_ctx, xa, n_ctx, mod, p, big, cos2, sin2):
    T, D = xa.shape
    S = p["s5_d"].shape[-1]
    lay, width = _in_layout(D, S)
    lo = 0 if keep_ctx else n_ctx
    ctx_rows = n_ctx if keep_ctx else 0

    h = modulate("modulate1", xa, mod["sh1"], mod["sc1"], n_ctx)
    z = big["w_in"]("in_proj", h)
    shard = width // N_CHIPS
    zz = {n: _shard_cols(z, o, w, shard, z.shape[1] // N_CHIPS) for n, (o, w) in lay.items()}

    wg = jnp.pad(p["w_gla_gate"], ((0, 0), (0, LANE - GLA_GATE_RANK), (0, 0)))
    glr = jnp.pad(zz["glr"], ((0, 0), (0, LANE - GLA_GATE_RANK)))
    la0, la1 = gla_prep("gla_prep", glr, wg, p["b_gla_gate"][:, None, :])
    o0 = gla_scan("gla_scan", zz["gq"], zz["gk"], zz["gv"], la0, False, n_ctx)
    o1 = gla_scan("gla_scan_rev", zz["gq"], zz["gk"], zz["gv"], la1, True, n_ctx)
    o_gla = gla_norm("gla_norm", o0[lo:], o1[lo:], zz["gr"][lo:], p["gla_norm_w"][None, :])

    su = zz["su"]
    bur = mm("s5_in_re", su, _s5_in_blocks(p["s5_b_re"]))
    bui = mm("s5_in_im", su, _s5_in_blocks(p["s5_b_im"]))
    ys = []
    for d in range(2):
        row = lambda t: t.reshape(1, -1)
        ldt = jnp.repeat(p["s5_log_dt"][d], S5_STATE)
        sr, si = s5_scan("s5_scan_rev" if d else "s5_scan", bur, bui, row(p["s5_lam_re"][d]),
                         row(p["s5_lam_im"][d]), row(ldt), d == 1, n_ctx)
        ys.append(mm("s5_out_re", sr[lo:], _s5_out_blocks(p["s5_c_re"][d])))
        ys.append(mm("s5_out_im", si[lo:], _s5_out_blocks(p["s5_c_im"][d])))
    T2 = T - lo
    tm = _tile(T2, ROW_TILE, SUBLANE)
    post = block_op("s5_post", _f_s5_post, [("row", S), ("bc", 1, S)] + [("row", S)] * 4, [("row", S)], 1, T2, tm,
                    [True] * 6)
    yg = post(su[lo:], p["s5_d"][None, :], *ys)[0]
    o_s5 = rowwise("s5_glu", _f_s5_glu, [yg, big["w_s5_glu"]("s5_glu_proj", yg)])[0]

    qn = qk_norm_rope("q_norm_rope", zz["aq"], cos2, sin2, p["q_norm_w"][None, :])
    kn = qk_norm_rope("k_norm_rope", zz["ak"], cos2, sin2, p["k_norm_w"][None, :])
    o_attn = attention("attn_lat", qn[n_ctx:], kn, zz["av"])
    if keep_ctx:
        o_c = attention("attn_ctx", qn[:n_ctx], kn[:n_ctx], zz["av"][:n_ctx])
        o_attn = jnp.concatenate([o_c, o_attn], 0)

    merged = rowwise("merge", _f_merge, [zz["ga"][lo:], zz["gb"][lo:], zz["gc"][lo:],
                                         big["w_proj_gla"]("proj_gla", o_gla),
                                         big["w_proj_s5"]("proj_s5", o_s5),
                                         big["w_proj_attn"]("proj_attn", o_attn)])[0]
    mix = big["w_out"]("out_proj", merged)
    x1 = postnorm("postnorm1", xa[lo:], mix, mod["g1"], p["ln1_w"][None, :], p["ln1_b"][None, :], ctx_rows)
    h2 = modulate("modulate2", x1, mod["sh2"], mod["sc2"], ctx_rows)
    u = big["w_ffn_in"]("ffn_in", h2)
    F = u.shape[1] // 2
    act = rowwise("swiglu", _f_swiglu, [u[:, :F], u[:, F:]])[0]
    f = big["w_ffn_out"]("ffn_out", act)
    return postnorm("postnorm2", x1, f, mod["g2"], p["ln2_w"][None, :], p["ln2_b"][None, :], ctx_rows)


def local_loss(x, ctx, target, small, cond, gathered, grads):
    n_lat, D = x.shape
    n_ctx = ctx.shape[0]
    cos2, sin2 = _rope_tables(n_ctx, n_lat)
    xa = jnp.concatenate([ctx, x], 0)
    depth = gathered["w_in"].shape[0]
    for l in range(depth):
        p = {n: v[l] for n, v in small.items() if n != "c_ctx"}
        big = {n: functools.partial(lambda n, l, name, a: mm_gathered(name, a, gathered[n], l, grads, (n, l)), n, l)
               for n in gathered}
        m = block_op("ada_bias", _f_add_bias, [("row", 6 * D), ("bc", 1, 6 * D)], [("row", 6 * D)], 1, 2, 2,
                     [True, True])(cond[l], p["b_ada"][None, :])[0]
        names = ["sh1", "sc1", "g1", "sh2", "sc2", "g2"]
        mod = {n: (m[0:1, i * D:(i + 1) * D], m[1:2, i * D:(i + 1) * D]) for i, n in enumerate(names)}
        xa = _layer(l < depth - 1, xa, n_ctx, mod, p, big, cos2, sin2)
    return sq_loss("loss", xa, target)


MESH = pl.DeviceIdType.MESH
ANY = pl.BlockSpec(memory_space=pl.ANY)


def _place():
    x, y, c = lax.axis_index("x"), lax.axis_index("y"), lax.axis_index("c")
    chips = [(1 - x, y), (x, 1 - y), (1 - x, 1 - y)]
    return x, y, c, chips


def _rcopy(src, dst, ssem, rsem, to):
    return pltpu.make_async_remote_copy(src_ref=src, dst_ref=dst, send_sem=ssem, recv_sem=rsem, device_id=to,
                                        device_id_type=MESH)


def gather_shards(bufs):
    n = len(bufs)
    L = bufs[0].shape[0]
    half = L // 2

    def body(*refs):
        dst = refs[n:2 * n]
        isend, irecv, fsend, frecv, dsend, drecv = refs[2 * n:]
        x, y, c, _ = _place()
        j = 2 * x + y
        nbr = [(1 - x, y), (x, 1 - y)]
        jn = [2 * kx + ky for kx, ky in nbr]
        jd = 2 * (1 - x) + (1 - y)
        sibling = (x, y, 1 - c)
        mine, other = pl.ds(c * half, half), pl.ds((1 - c) * half, half)

        def piece(i, layers, chip, q):
            hr = bufs[i].shape[2] // 2
            return dst[i].at[layers, chip, pl.ds(q * hr, hr)]

        direct = [_rcopy(dst[i].at[mine, j], dst[i].at[mine, j], isend.at[i, r], irecv.at[i, r], (*nbr[r], c))
                  for i in range(n) for r in range(2)]
        for cp in direct:
            cp.start()
        passed = []
        for i in range(n):
            for r in range(2):
                part = dst[i].at[mine, jn[r]]
                _rcopy(part, part, isend.at[i, r], irecv.at[i, r], (*nbr[r], c)).wait_recv()
                fwd = piece(i, mine, jn[r], 1 - r)
                passed.append(_rcopy(fwd, fwd, fsend.at[i, r], frecv.at[i, 1 - r], (*nbr[1 - r], c)))
                passed.append(_rcopy(part, part, dsend.at[i, r], drecv.at[i, r], sibling))
                passed[-2].start()
                passed[-1].start()
        for i in range(n):
            for q in range(2):
                part = piece(i, mine, jd, q)
                _rcopy(part, part, fsend.at[i, q], frecv.at[i, q], (*nbr[q], c)).wait_recv()
                passed.append(_rcopy(part, part, dsend.at[i, 2 + q], drecv.at[i, 2 + q], sibling))
                passed[-1].start()
        for i in range(n):
            for r in range(2):
                part = dst[i].at[other, jn[r]]
                _rcopy(part, part, dsend.at[i, r], drecv.at[i, r], sibling).wait_recv()
                part = piece(i, other, jd, r)
                _rcopy(part, part, dsend.at[i, 2 + r], drecv.at[i, 2 + r], sibling).wait_recv()
        for cp in direct + passed:
            cp.wait_send()

    out_shape = [jax.ShapeDtypeStruct(b.shape, b.dtype) for b in bufs]
    sems = [pltpu.SemaphoreType.DMA((n, 2))] * 4 + [pltpu.SemaphoreType.DMA((n, 4))] * 2
    return pl.pallas_call(body, name="gather_shards", in_specs=[ANY] * n, out_specs=[ANY] * n, out_shape=out_shape,
                          scratch_shapes=sems, input_output_aliases={i: i for i in range(n)})(*bufs)


def swap_halves(grads):
    n = len(grads)

    def body(*refs):
        src, dst = refs[:n], refs[n:2 * n]
        ssem, rsem = refs[2 * n:]
        x, y, c, _ = _place()
        cps = []
        for i in range(n):
            hr = grads[i].shape[1] // 2
            cps.append(_rcopy(src[i].at[:, pl.ds((1 - c) * hr, hr)], dst[i], ssem.at[i], rsem.at[i], (x, y, 1 - c)))
        for cp in cps:
            cp.start()
        for cp in cps:
            cp.wait()

    out_shape = [jax.ShapeDtypeStruct((g.shape[0], g.shape[1] // 2, g.shape[2]), g.dtype) for g in grads]
    return pl.pallas_call(body, name="swap_halves", in_specs=[ANY] * n, out_specs=[ANY] * n, out_shape=out_shape,
                          scratch_shapes=[pltpu.SemaphoreType.DMA((n,))] * 2)(*grads)


def scatter_diagonal(parts):
    n = len(parts)

    def body(*refs):
        src, dst = refs[:n], refs[n:2 * n]
        ssem, rsem = refs[2 * n:]
        x, y, c, _ = _place()
        nbr = [(1 - x, y), (x, 1 - y)]
        jd = 2 * (1 - x) + (1 - y)
        cps = []
        for i in range(n):
            h2 = parts[i].shape[1] // 2
            for q in range(2):
                cps.append(_rcopy(src[i].at[jd, pl.ds(q * h2, h2)], dst[i].at[q], ssem.at[i, q], rsem.at[i, q],
                                  (*nbr[q], c)))
        for cp in cps:
            cp.start()
        for cp in cps:
            cp.wait()

    out_shape = [jax.ShapeDtypeStruct((2, p.shape[1] // 2, p.shape[2]), p.dtype) for p in parts]
    return pl.pallas_call(body, name="scatter_diagonal", in_specs=[ANY] * n, out_specs=[ANY] * n, out_shape=out_shape,
                          scratch_shapes=[pltpu.SemaphoreType.DMA((n, 2))] * 2)(*parts)


def scatter_shards(parts, via):
    n = len(parts)

    def body(*refs):
        src, add, dst = refs[:n], refs[n:2 * n], refs[2 * n:3 * n]
        ssem, rsem = refs[3 * n:]
        x, y, c, _ = _place()
        nbr = [(1 - x, y), (x, 1 - y)]
        cps = []
        for i in range(n):
            h2 = parts[i].shape[1] // 2
            for t, (kx, ky) in enumerate(nbr):
                plain = pl.ds(t * h2, h2)
                cps.append(_rcopy(src[i].at[2 * kx + ky, plain], dst[i].at[t, plain], ssem.at[i, 2 * t],
                                  rsem.at[i, 2 * t], (kx, ky, c)))
                cps.append(_rcopy(add[i].at[t], dst[i].at[t, pl.ds((1 - t) * h2, h2)], ssem.at[i, 2 * t + 1],
                                  rsem.at[i, 2 * t + 1], (kx, ky, c)))
        for cp in cps:
            cp.start()
        for cp in cps:
            cp.wait()

    out_shape = [jax.ShapeDtypeStruct((2,) + p.shape[1:], p.dtype) for p in parts]
    return pl.pallas_call(body, name="scatter_shards", in_specs=[ANY] * (2 * n), out_specs=[ANY] * n,
                          out_shape=out_shape, scratch_shapes=[pltpu.SemaphoreType.DMA((n, 4))] * 2)(*parts, *via)


def join_halves(bufs):
    n = len(bufs)

    def body(*refs):
        dst = refs[n:2 * n]
        ssem, rsem = refs[2 * n:]
        x, y, c, _ = _place()
        cps = []
        for i in range(n):
            hr = bufs[i].shape[1] // 2
            mine = dst[i].at[:, pl.ds(c * hr, hr)]
            cps.append(_rcopy(mine, mine, ssem.at[i], rsem.at[i], (x, y, 1 - c)))
        for cp in cps:
            cp.start()
        for cp in cps:
            cp.wait()

    out_shape = [jax.ShapeDtypeStruct(b.shape, b.dtype) for b in bufs]
    return pl.pallas_call(body, name="join_halves", in_specs=[ANY] * n, out_specs=[ANY] * n, out_shape=out_shape,
                          scratch_shapes=[pltpu.SemaphoreType.DMA((n,))] * 2,
                          input_output_aliases={i: i for i in range(n)})(*bufs)


def gather_blocks(v):
    def body(v_ref, out_ref, send_sems, recv_sems, local_sem):
        x, y, c, chips = _place()
        me, sibling = (x, y, c), (x, y, 1 - c)

        def blk(px, py, pc):
            return out_ref.at[4 * px + 2 * py + pc]

        def copy(k, block, to, src=None):
            return _rcopy(blk(*block) if src is None else src, blk(*block), send_sems.at[k], recv_sems.at[k], to)

        own = pltpu.make_async_copy(v_ref, blk(*me), local_sem)
        own.start()
        first = [copy(0, me, sibling, src=v_ref)]
        first += [copy(1 + r, me, (*chip, c), src=v_ref) for r, chip in enumerate(chips)]
        for cp in first:
            cp.start()
        passed = [copy(4 + r, (*chip, c), sibling) for r, chip in enumerate(chips)]
        for r, chip in enumerate(chips):
            copy(1 + r, (*chip, c), me).wait_recv()
            passed[r].start()
        copy(0, sibling, me).wait_recv()
        for r, chip in enumerate(chips):
            copy(4 + r, (*chip, 1 - c), me).wait_recv()
        for cp in first + passed:
            cp.wait_send()
        own.wait()

    return pl.pallas_call(body, name="gather_blocks", in_specs=[ANY], out_specs=ANY,
                          out_shape=jax.ShapeDtypeStruct((8,) + v.shape, v.dtype),
                          scratch_shapes=[pltpu.SemaphoreType.DMA((7,)), pltpu.SemaphoreType.DMA((7,)),
                                          pltpu.SemaphoreType.DMA])(v)


STREAM_BLOCK = 512 * 1024


def _stream_rows(rows, cols):
    base = 2 * SUBLANE if rows % (2 * SUBLANE) == 0 else SUBLANE
    return _tile(rows, max(base, STREAM_BLOCK // cols // base * base), base)


def _view3(a, lead):
    shape = a.shape[:lead] + (-1, a.shape[-1])
    return a.reshape(shape)


def sum_parts(name, terms, out, grid, where, into=None):
    n_skip = 0 if into is None else 1

    def body(w_ref, *refs):
        refs = refs[n_skip:]
        acc = refs[0][...].astype(F32)
        for t in refs[1:-1]:
            acc = acc + t[...].astype(F32)
        refs[-1][...] = acc.astype(refs[-1].dtype)

    grid_spec = pltpu.PrefetchScalarGridSpec(
        num_scalar_prefetch=1, grid=grid, in_specs=[ANY] * n_skip + [pl.BlockSpec(b, f) for _, b, f in terms],
        out_specs=pl.BlockSpec(out[2], out[3]))
    operands = ([] if into is None else [into]) + [t[0] for t in terms]
    return pl.pallas_call(body, name=name, grid_spec=grid_spec, out_shape=jax.ShapeDtypeStruct(out[0], out[1]),
                          input_output_aliases={} if into is None else {1: 0},
                          compiler_params=_params(("arbitrary",) * len(grid)))(where, *operands)


def cast_place(w, where):
    w3 = _view3(w, 1)
    L, rows, cols = w3.shape
    tr = _stream_rows(rows, cols)

    def body(w_ref, src, dst):
        dst[...] = src[...].astype(BF16)

    grid_spec = pltpu.PrefetchScalarGridSpec(
        num_scalar_prefetch=1, grid=(L, rows // tr),
        in_specs=[pl.BlockSpec((None, tr, cols), lambda l, r, wh: (l, r, 0))],
        out_specs=pl.BlockSpec((None, None, tr, cols), lambda l, r, wh: (l, wh[1], r, 0)))
    out = pl.pallas_call(body, name="cast_place", grid_spec=grid_spec,
                         out_shape=jax.ShapeDtypeStruct((L, N_CHIPS, rows, cols), BF16),
                         compiler_params=_params(("arbitrary", "arbitrary")))(where, w3)
    return out.reshape((L, N_CHIPS) + w.shape[1:])


def silu_vjp_rows(name, rows, x):
    def body(rows_ref, x_ref, o_ref):
        total = jnp.sum(rows_ref[...], axis=0, keepdims=True)
        s = jax.nn.sigmoid(x_ref[...])
        o_ref[...] = total * (s * (1.0 + x_ref[...] * (1.0 - s)))

    return pl.pallas_call(body, name=name, out_shape=jax.ShapeDtypeStruct(x.shape, F32))(rows, x)


def adamw(name, w, g, m, v):
    n, rows, cols = w.shape
    tr = _stream_rows(rows, cols)

    def body(w_ref, g_ref, m_ref, v_ref, d_ref, nm_ref, nv_ref):
        gv = g_ref[...]
        nm = ADAM_B1 * m_ref[...] + (1.0 - ADAM_B1) * gv
        nv = ADAM_B2 * v_ref[...] + (1.0 - ADAM_B2) * (gv * gv)
        m_hat = nm / (1.0 - ADAM_B1 ** ADAM_STEP)
        v_hat = nv / (1.0 - ADAM_B2 ** ADAM_STEP)
        d_ref[...] = -ADAM_LR * (m_hat / (jnp.sqrt(v_hat) + ADAM_EPS) + ADAM_WD * w_ref[...])
        nm_ref[...] = nm
        nv_ref[...] = nv

    blk = pl.BlockSpec((None, tr, cols), lambda l, r: (l, r, 0))
    shp = jax.ShapeDtypeStruct(w.shape, F32)
    return pl.pallas_call(body, name=name, grid=(n, rows // tr), in_specs=[blk] * 4, out_specs=[blk] * 3,
                          out_shape=[shp] * 3, compiler_params=_params(("arbitrary", "arbitrary")))(w, g, m, v)


COL_SHARDED = ("w_ada", "w_in", "w_proj_gla", "w_proj_s5", "w_proj_attn", "w_ffn_in")
ROW_SHARDED = ("w_s5_glu", "w_out", "w_ffn_out")
SHARDED = COL_SHARDED + ROW_SHARDED
DEFERRED = ("w_ada",)
REDUCED = tuple(n for n in SHARDED if n not in DEFERRED)
N_DEV = 8
GATE = ("w_gla_gate", "b_gla_gate")
WEIGHTS = ("c_ctx", "w_ada", "b_ada", "w_in", "w_gla_gate", "b_gla_gate", "gla_norm_w", "s5_lam_re", "s5_lam_im",
           "s5_log_dt", "s5_b_re", "s5_b_im", "s5_c_re", "s5_c_im", "s5_d", "w_s5_glu", "q_norm_w", "k_norm_w",
           "w_proj_gla", "w_proj_s5", "w_proj_attn", "w_out", "ln1_w", "ln1_b", "ln2_w", "ln2_b", "w_ffn_in",
           "w_ffn_out")
REPLICATED = tuple(n for n in WEIGHTS if n not in SHARDED + GATE)
SMALL = REPLICATED + GATE


def _pack(arrays):
    flat = jnp.concatenate([a.reshape(-1) for a in arrays])
    pad = (-flat.shape[0]) % (PACK_ROWS * LANE)
    return jnp.pad(flat, (0, pad)).reshape(-1, LANE)


def _unpack(packed, like):
    flat, out, off = packed.reshape(-1), [], 0
    for a in like:
        out.append(flat[off:off + a.size].reshape(a.shape))
        off += a.size
    return out


def kernel(x, c, ctx, c_ctx, w_ada, b_ada, w_in, w_gla_gate, b_gla_gate, gla_norm_w, s5_lam_re, s5_lam_im, s5_log_dt, s5_b_re, s5_b_im, s5_c_re, s5_c_im, s5_d, w_s5_glu, q_norm_w, k_norm_w, w_proj_gla, w_proj_s5, w_proj_attn, w_out, ln1_w, ln1_b, ln2_w, ln2_b, w_ffn_in, w_ffn_out, loss_target, m_c_ctx, m_w_ada, m_b_ada, m_w_in, m_w_gla_gate, m_b_gla_gate, m_gla_norm_w, m_s5_lam_re, m_s5_lam_im, m_s5_log_dt, m_s5_b_re, m_s5_b_im, m_s5_c_re, m_s5_c_im, m_s5_d, m_w_s5_glu, m_q_norm_w, m_k_norm_w, m_w_proj_gla, m_w_proj_s5, m_w_proj_attn, m_w_out, m_ln1_w, m_ln1_b, m_ln2_w, m_ln2_b, m_w_ffn_in, m_w_ffn_out, v_c_ctx, v_w_ada, v_b_ada, v_w_in, v_w_gla_gate, v_b_gla_gate, v_gla_norm_w, v_s5_lam_re, v_s5_lam_im, v_s5_log_dt, v_s5_b_re, v_s5_b_im, v_s5_c_re, v_s5_c_im, v_s5_d, v_w_s5_glu, v_q_norm_w, v_k_norm_w, v_w_proj_gla, v_w_proj_s5, v_w_proj_attn, v_w_out, v_ln1_w, v_ln1_b, v_ln2_w, v_ln2_b, v_w_ffn_in, v_w_ffn_out):
    args = dict(locals())
    w = {n: args[n] for n in WEIGHTS}
    m = {n: args["m_" + n] for n in WEIGHTS}
    v = {n: args["v_" + n] for n in WEIGHTS}
    L = w_in.shape[0]
    half = L // 2
    core = lax.axis_index("c").astype(jnp.int32)
    place = (2 * lax.axis_index("x") + lax.axis_index("y")).astype(jnp.int32)
    zero = jnp.zeros((), jnp.int32)
    where, by_core, by_place = jnp.stack([core, place]), jnp.stack([core, zero]), jnp.stack([zero, place])

    def padded(n):
        cols = w[n].shape[-1]
        extra = _padded_width(cols) - cols if n in COL_SHARDED else 0
        return jnp.pad(w[n], ((0, 0), (0, 0), (0, extra))) if extra else w[n]

    gathered = gather_shards([cast_place(padded(n), by_place) for n in REDUCED])
    gathered = {n: g if n in COL_SHARDED else g.reshape(L, 1, -1, g.shape[-1]) for n, g in zip(REDUCED, gathered)}
    first = [w[n] for n in GATE] + [c[0]]
    first_blocks = gather_blocks(_pack(first))
    per_dev = [_unpack(first_blocks[k], first) for k in range(N_DEV)]
    small = {n: w[n] for n in REPLICATED}
    small.update({n: jnp.concatenate([per_dev[2 * j][i] for j in range(N_CHIPS)], -1) for i, n in enumerate(GATE)})

    vectors = jnp.stack([d[-1] for d in per_dev] + [w["c_ctx"]] * N_DEV)
    left = rowwise("silu_cond", _f_silu, [vectors])[0]
    ada_w = w["w_ada"].astype(BF16)[:, None]
    mine = [jnp.stack([_mm_nn("ada_proj", left, ada_w, l, True) for l in range(L)])]
    ada_blocks = gather_blocks(_pack(mine))
    full = jnp.concatenate([_unpack(ada_blocks[2 * j], mine)[0] for j in range(N_CHIPS)], -1)
    me = 2 * place + core
    cond = jnp.stack([lax.dynamic_index_in_dim(full, me, axis=1, keepdims=False), full[:, N_DEV]], 1)

    g_sh = {}

    def loss_fn(x1, small, cond):
        return local_loss(x1, ctx[0], loss_target[0], small, cond, gathered, g_sh)

    loss, (gx, g_small, g_cond) = jax.value_and_grad(loss_fn, argnums=(0, 1, 2))(x[0], small, cond)
    loss = lax.psum(loss, ("x", "y", "c"))

    parts = [g_sh[(n, l)].reshape((N_CHIPS, -1, g_sh[(n, l)].shape[-1])) for n in REDUCED for l in range(L)]
    theirs = swap_halves(parts)
    chip_sums, tiles = [], []
    for p, t in zip(parts, theirs):
        hr, cols = t.shape[1:]
        tr = _stream_rows(hr, cols)
        tiles.append(tr)
        blk = (None, tr, cols)
        chip_sums.append(sum_parts(
            "sum_cores", [(p, blk, functools.partial(lambda nb, s, r, wh: (s, wh[0] * nb + r, 0), hr // tr)),
                          (t, blk, lambda s, r, wh: (s, r, 0))],
            (t.shape, BF16, blk, lambda s, r, wh: (s, r, 0)), (N_CHIPS, hr // tr), by_core))
    through = scatter_diagonal(chip_sums)
    via = []
    for s, t in zip(chip_sums, through):
        h2, cols = t.shape[1:]
        tr = _stream_rows(h2, cols)
        blk = (None, tr, cols)

        def shard_of_neighbour(nb, t, r, wh):
            chip = jnp.where(t == 0, (wh[1] + 2) % N_CHIPS, wh[1] + 1 - 2 * (wh[1] % 2))
            return (chip, (1 - t) * nb + r, 0)

        via.append(sum_parts("sum_through", [(s, blk, functools.partial(shard_of_neighbour, h2 // tr)),
                                             (t, blk, lambda t, r, wh: (1 - t, r, 0))],
                             (t.shape, BF16, blk, lambda t, r, wh: (t, r, 0)), (2, h2 // tr), by_place))
    recv = scatter_shards(chip_sums, via)
    finals = []
    for i, n in enumerate(REDUCED):
        rows, cols = parts[i * L].shape[1:]
        buf = None
        for l in range(L):
            k = i * L + l
            tr = tiles[k]
            nb = rows // 2 // tr
            blk = (None, tr, cols)
            terms = [(parts[k], blk, functools.partial(lambda nb, r, wh: (wh[1], wh[0] * nb + r, 0), nb)),
                     (theirs[k], blk, lambda r, wh: (wh[1], r, 0))]
            terms += [(recv[k], blk, functools.partial(lambda j, r, wh: (j, r, 0), j)) for j in range(2)]
            buf = sum_parts("sum_chips", terms,
                            ((L, rows, cols), F32, blk, functools.partial(lambda l, nb, r, wh: (l, wh[0] * nb + r, 0), l, nb)),
                            (nb,), where, into=buf)
        finals.append(buf)
    grads = {n: g[..., :w[n].shape[-1]].reshape(w[n].shape) for n, g in zip(REDUCED, join_halves(finals))}

    row_blocks = gather_blocks(_pack([g_cond]))
    rows = [_unpack(row_blocks[k], [g_cond])[0] for k in range(N_DEV)]
    shard = w["w_ada"].shape[-1]
    ada, rights = [], []
    for l in range(L):
        right = jnp.stack([r[l, 0] for r in rows] + [r[l, 1] for r in rows])
        rights.append(lax.dynamic_slice_in_dim(right, place * shard, shard, axis=1))
        ada.append(_mm_tn("ada_dw", left, rights[l], 1, True, F32))
    grads["w_ada"] = jnp.concatenate(ada, 0)
    back = _mm_nt("ada_dx", jnp.concatenate(rights, 1), ada_w.reshape((1, L) + ada_w.shape[2:]), 0, True)
    share = [back[N_DEV:]]
    share_blocks = gather_blocks(_pack(share))
    shares = jnp.concatenate([_unpack(share_blocks[2 * j], share)[0] for j in range(N_CHIPS)], 0)
    grad_c_ctx = silu_vjp_rows("ctx_vector_grad", shares, w["c_ctx"][None, :])[0]

    small_parts = [g_small[n] for n in SMALL]
    blocks = gather_blocks(_pack(small_parts))
    prow = (None, PACK_ROWS, LANE)
    total = sum_parts("sum_devices",
                      [(blocks, prow, functools.partial(lambda k, r, wh: (k, r, 0), k)) for k in range(8)],
                      (blocks.shape[1:], F32, prow[1:], lambda r, wh: (r, 0)), (blocks.shape[1] // PACK_ROWS,),
                      jnp.zeros((2,), jnp.int32))
    grads.update(dict(zip(SMALL, _unpack(total, small_parts))))
    grads["c_ctx"] = grad_c_ctx
    for n in GATE:
        width = w[n].shape[-1]
        grads[n] = lax.dynamic_slice_in_dim(grads[n], place * width, width, axis=-1)

    delta, new_m, new_v = {}, {}, {}
    for n in SHARDED:
        d3, m3, v3 = adamw("adamw", _view3(w[n], 1), _view3(grads[n], 1), _view3(m[n], 1), _view3(v[n], 1))
        delta[n], new_m[n], new_v[n] = (t.reshape(w[n].shape) for t in (d3, m3, v3))
    packed = [_pack([d[n] for n in SMALL])[None] for d in (w, grads, m, v)]
    d3, m3, v3 = adamw("adamw_small", *packed)
    like = [w[n] for n in SMALL]
    for dst, src in ((delta, d3), (new_m, m3), (new_v, v3)):
        dst.update(dict(zip(SMALL, _unpack(src[0], like))))

    return (loss, gx[None], *[grads[n] for n in WEIGHTS], *[delta[n] for n in WEIGHTS],
            *[new_m[n] for n in WEIGHTS], *[new_v[n] for n in WEIGHTS])
```
